```python
import math
import jax, jax.numpy as jnp
from jax import lax
import numpy as np

D_MODEL = 1024
BATCH = 8
SEQ = 4096
DEPTH = 2

CHUNK = 64
LOOKBACK_CHUNKS = 8
Q_BLOCK = 128
HEAD_DIM = 64
N_HEADS_A = 8
N_HEADS_B = 8
WIDTH_A = N_HEADS_A * HEAD_DIM
WIDTH_B = N_HEADS_B * HEAD_DIM
REL_CLIP = 128
N_IN = 3 * WIDTH_A + 3 * WIDTH_B + N_HEADS_B + 2 * D_MODEL
_SPLITS = list(np.cumsum([WIDTH_A, WIDTH_A, WIDTH_A, WIDTH_B, WIDTH_B, WIDTH_B,
                          N_HEADS_B, D_MODEL])[:].astype(int))
N_EXPERTS = 64
TOP_K = 8
N_GROUPS = 8
TOPK_GROUPS = 4
EXPERT_FF = 256
SHARED_FF = 256
ROUTED_SCALE = 2.5
EXPERT_BLOCK = 256
EPS = 1e-6
NEG_INF = -1e30

kernel_name = "hybrid_chunked_fox_moe_adaln_trunk"


def rms_norm(x, g):
    xf = x.astype(jnp.float32)
    y = xf * lax.rsqrt(jnp.mean(xf * xf, axis=-1, keepdims=True) + EPS)
    return (y * g.astype(jnp.float32)).astype(x.dtype)


def modulate(h, shift, scale):
    return h * (1.0 + scale[:, None, :]) + shift[:, None, :]


def chunk_band_attention(q, k, v, rel_table):
    B, H, S, dh = q.shape
    n_chunks = S // CHUNK
    pad = LOOKBACK_CHUNKS * CHUNK
    band = pad + CHUNK
    k_pad = jnp.pad(k, ((0, 0), (0, 0), (pad, 0), (0, 0)))
    v_pad = jnp.pad(v, ((0, 0), (0, 0), (pad, 0), (0, 0)))
    qi = jnp.arange(CHUNK)[:, None]
    kj = jnp.arange(band)[None, :]
    rel = qi + pad - kj
    bias = rel_table[:, jnp.clip(rel, -REL_CLIP, REL_CLIP) + REL_CLIP]
    bias = bias.astype(jnp.float32)
    scale = 1.0 / math.sqrt(dh)

    def one_chunk(n):
        start = n * CHUNK
        qn = lax.dynamic_slice_in_dim(q, start, CHUNK, axis=2)
        kn = lax.dynamic_slice_in_dim(k_pad, start, band, axis=2)
        vn = lax.dynamic_slice_in_dim(v_pad, start, band, axis=2)
        s = jnp.einsum('bhqd,bhkd->bhqk', qn, kn).astype(jnp.float32) * scale + bias
        valid = (start - pad + kj) >= 0
        p = jax.nn.softmax(jnp.where(valid, s, NEG_INF), axis=-1)
        return jnp.einsum('bhqk,bhkd->bhqd', p.astype(vn.dtype), vn)

    out = lax.map(one_chunk, jnp.arange(n_chunks))
    return out.transpose(1, 2, 0, 3, 4).reshape(B, H, S, dh)


def forgetting_attention(q, k, v, log_f):
    B, H, S, dh = q.shape
    cum = jnp.cumsum(log_f, axis=-1)
    scale = 1.0 / math.sqrt(dh)
    kpos = jnp.arange(S)[None, :]

    def one_block(i):
        start = i * Q_BLOCK
        qb = lax.dynamic_slice_in_dim(q, start, Q_BLOCK, axis=2)
        cq = lax.dynamic_slice_in_dim(cum, start, Q_BLOCK, axis=2)
        s = jnp.einsum('bhqd,bhkd->bhqk', qb, k).astype(jnp.float32) * scale
        s = s + cq[..., :, None] - cum[..., None, :]
        causal = kpos <= (start + jnp.arange(Q_BLOCK))[:, None]
        p = jax.nn.softmax(jnp.where(causal, s, NEG_INF), axis=-1)
        return jnp.einsum('bhqk,bhkd->bhqd', p.astype(v.dtype), v)

    out = lax.map(one_block, jnp.arange(S // Q_BLOCK))
    return out.transpose(1, 2, 0, 3, 4).reshape(B, H, S, dh)


def hybrid_mixer(h, w_in, b_fgate, qn_a, kn_a, qn_b, kn_b, rel_table, w_proj_a, w_proj_b, w_out):
    B, S, _ = h.shape
    z = h @ w_in
    qa, ka, va, qb, kb, vb, f_raw, ga, gb = jnp.split(z, _SPLITS, axis=-1)

    def heads(u):
        return u.reshape(B, S, -1, HEAD_DIM)

    def to_bhsd(u):
        return u.transpose(0, 2, 1, 3)

    qa = to_bhsd(rms_norm(heads(qa), qn_a))
    ka = to_bhsd(rms_norm(heads(ka), kn_a))
    qb = to_bhsd(rms_norm(heads(qb), qn_b))
    kb = to_bhsd(rms_norm(heads(kb), kn_b))
    ya = chunk_band_attention(qa, ka, to_bhsd(heads(va)), rel_table)
    log_f = jax.nn.log_sigmoid((f_raw + b_fgate).astype(jnp.float32)).transpose(0, 2, 1)
    yb = forgetting_attention(qb, kb, to_bhsd(heads(vb)), log_f)

    def merge_heads(y):
        return y.transpose(0, 2, 1, 3).reshape(B, S, -1)

    ua = merge_heads(ya) @ w_proj_a
    ub = merge_heads(yb) @ w_proj_b
    m = jax.nn.sigmoid(ga) * ua + jax.nn.sigmoid(gb) * ub
    return m @ w_out


def swiglu(t, wg, wu, wd):
    return (jax.nn.silu(t @ wg) * (t @ wu)) @ wd


def routed_experts(t, idx, w, wg, wu, wd):
    T, D = t.shape
    K = idx.shape[1]
    E = wg.shape[0]
    A = T * K
    G = EXPERT_BLOCK
    n_blocks = (A + G - 1) // G + E
    P = n_blocks * G
    flat_e = idx.reshape(-1)
    flat_tok = jnp.repeat(jnp.arange(T, dtype=jnp.int32), K)
    flat_w = w.reshape(-1)
    order = jnp.argsort(flat_e)
    se, stok, sw = flat_e[order], flat_tok[order], flat_w[order]
    counts = jnp.bincount(flat_e, length=E)
    starts = jnp.cumsum(counts) - counts
    padded = ((counts + G - 1) // G) * G
    pends = jnp.cumsum(padded)
    pstarts = pends - padded
    dest = pstarts[se] + (jnp.arange(A) - starts[se])
    buf_tok = jnp.full((P,), T, jnp.int32).at[dest].set(stok)
    buf_w = jnp.zeros((P,), flat_w.dtype).at[dest].set(sw)
    block_e = jnp.minimum(jnp.searchsorted(pends, jnp.arange(n_blocks) * G, side='right'), E - 1)
    t_pad = jnp.concatenate([t, jnp.zeros((1, D), t.dtype)], axis=0)

    def one_block(args):
        tok, e = args
        xb = t_pad[tok]
        return swiglu(xb, wg[e], wu[e], wd[e])

    out = lax.map(one_block, (buf_tok.reshape(n_blocks, G), block_e)).reshape(P, D)
    out = out * buf_w[:, None].astype(out.dtype)
    y = jnp.zeros((T + 1, D), out.dtype).at[buf_tok].add(out)[:T]
    return y.astype(t.dtype)


def moe_ffn(h, w_router, router_bias, wg, wu, wd, wgs, wus, wds):
    B, S, D = h.shape
    t = h.reshape(-1, D)
    T = t.shape[0]
    scores = jax.nn.sigmoid((t @ w_router).astype(jnp.float32))
    biased = scores + router_bias.astype(jnp.float32)
    per_group = N_EXPERTS // N_GROUPS
    grp = lax.top_k(biased.reshape(T, N_GROUPS, per_group), 2)[0].sum(-1)
    gidx = lax.top_k(grp, TOPK_GROUPS)[1]
    gmask = jnp.any(jnp.arange(N_GROUPS)[None, None, :] == gidx[:, :, None], axis=1)
    emask = jnp.repeat(gmask, per_group, axis=1)
    _, idx = lax.top_k(jnp.where(emask, biased, -jnp.inf), TOP_K)
    wsel = jnp.take_along_axis(scores, idx, axis=-1)
    wsel = wsel / jnp.sum(wsel, axis=-1, keepdims=True) * ROUTED_SCALE
    routed = routed_experts(t, idx, wsel, wg, wu, wd)
    shared = swiglu(t, wgs, wus, wds)
    return (routed + shared).reshape(B, S, D)


def setup_inputs(seed: int = 0) -> dict:
    key = jax.random.key(seed)
    ks = jax.random.split(key, 24)
    f32 = jnp.float32

    def nrm(k, shape, scale):
        return jax.random.normal(k, shape, f32) * scale

    L, D = DEPTH, D_MODEL
    return {
        "x": nrm(ks[0], (BATCH, SEQ, D), 1.0),
        "c": nrm(ks[1], (BATCH, D), 1.0),
        "ada_w": nrm(ks[2], (L, D, 6 * D), 0.5 * D ** -0.5),
        "ada_b": nrm(ks[3], (L, 6 * D), 0.02),
        "mix_norm_g": 1.0 + nrm(ks[4], (L, D), 0.05),
        "w_in": nrm(ks[5], (L, D, N_IN), D ** -0.5),
        "b_fgate": jax.random.uniform(ks[6], (L, N_HEADS_B), f32, 1.0, 4.0),
        "qn_a": 1.0 + nrm(ks[7], (L, HEAD_DIM), 0.05),
        "kn_a": 1.0 + nrm(ks[8], (L, HEAD_DIM), 0.05),
        "qn_b": 1.0 + nrm(ks[9], (L, HEAD_DIM), 0.05),
        "kn_b": 1.0 + nrm(ks[10], (L, HEAD_DIM), 0.05),
        "rel_bias": nrm(ks[11], (L, N_HEADS_A, 2 * REL_CLIP + 1), 0.5),
        "w_proj_a": nrm(ks[12], (L, WIDTH_A, D), WIDTH_A ** -0.5),
        "w_proj_b": nrm(ks[13], (L, WIDTH_B, D), WIDTH_B ** -0.5),
        "w_out": nrm(ks[14], (L, D, D), D ** -0.5),
        "ffn_norm_g": 1.0 + nrm(ks[15], (L, D), 0.05),
        "w_router": nrm(ks[16], (L, D, N_EXPERTS), D ** -0.5),
        "router_bias": nrm(ks[17], (L, N_EXPERTS), 0.01),
        "w_gate_e": nrm(ks[18], (L, N_EXPERTS, D, EXPERT_FF), D ** -0.5),
        "w_up_e": nrm(ks[19], (L, N_EXPERTS, D, EXPERT_FF), D ** -0.5),
        "w_down_e": nrm(ks[20], (L, N_EXPERTS, EXPERT_FF, D), EXPERT_FF ** -0.5),
        "w_gate_s": nrm(ks[21], (L, D, SHARED_FF), D ** -0.5),
        "w_up_s": nrm(ks[22], (L, D, SHARED_FF), D ** -0.5),
        "w_down_s": nrm(ks[23], (L, SHARED_FF, D), SHARED_FF ** -0.5),
    }


def reference(x, c, ada_w, ada_b, mix_norm_g, w_in, b_fgate, qn_a, kn_a, qn_b, kn_b, rel_bias,
              w_proj_a, w_proj_b, w_out, ffn_norm_g, w_router, router_bias,
              w_gate_e, w_up_e, w_down_e, w_gate_s, w_up_s, w_down_s):
    c_act = jax.nn.silu(c)
    for l in range(DEPTH):
        mod = c_act @ ada_w[l] + ada_b[l]
        sh1, sc1, g1, sh2, sc2, g2 = jnp.split(mod, 6, axis=-1)
        h = modulate(rms_norm(x, mix_norm_g[l]), sh1, sc1)
        x = x + g1[:, None, :] * hybrid_mixer(h, w_in[l], b_fgate[l], qn_a[l], kn_a[l], qn_b[l],
                                              kn_b[l], rel_bias[l], w_proj_a[l], w_proj_b[l],
                                              w_out[l])
        h = modulate(rms_norm(x, ffn_norm_g[l]), sh2, sc2)
        x = x + g2[:, None, :] * moe_ffn(h, w_router[l], router_bias[l], w_gate_e[l], w_up_e[l],
                                         w_down_e[l], w_gate_s[l], w_up_s[l], w_down_s[l])
    return x
```

```python
import functools
import math

import jax
import jax.numpy as jnp
from jax import lax
from jax.experimental import pallas as pl
from jax.experimental.pallas import tpu as pltpu

F32 = jnp.float32
BF16 = jnp.bfloat16

D_MODEL = 1024
HEAD_DIM = 64
N_HEADS = 8
WIDTH = N_HEADS * HEAD_DIM
CHUNK = 64
LOOKBACK_CHUNKS = 8
REL_CLIP = 128
N_EXPERTS = 64
TOP_K = 8
N_GROUPS = 8
TOPK_GROUPS = 4
EXPERT_FF = 256
ROUTED_SCALE = 2.5
EPS = 1e-6
NEG_INF = -1e30

LANES = 128
MXU_DIM = 256
VMEM_LIMIT = 56 * 1024 * 1024

TM_PRE = 512
TQ_BAND = 256
BAND_WIN = TQ_BAND + LOOKBACK_CHUNKS * CHUNK
TQ_FOX = 512
TM_MOE = 256
RUN_ALIGN = 16
K_SORT = ((TOP_K * TM_MOE + N_EXPERTS * (RUN_ALIGN - 1) + MXU_DIM - 1) // MXU_DIM) * MXU_DIM
BLOCK_ROWS = 512
RUN_SIZES = tuple(RUN_ALIGN << b for b in range(int(math.log2(TM_MOE // RUN_ALIGN)) + 1))
GAP_SIZES = tuple(RUN_ALIGN << b for b in range(int(math.log2(BLOCK_ROWS // RUN_ALIGN))))


def _dot(a, b):
    return jnp.dot(a, b, preferred_element_type=F32)


def _dot_nt(a, b):
    return lax.dot_general(a, b, (((1,), (1,)), ((), ())), preferred_element_type=F32)


def _sigmoid(v):
    return 1.0 / (1.0 + jnp.exp(-v))


def _cparams(sem):
    return pltpu.CompilerParams(dimension_semantics=sem, vmem_limit_bytes=VMEM_LIMIT)


def _adaln_kernel(c_ref, w_ref, b_ref, o_ref):
    c = c_ref[...]
    ca = c * _sigmoid(c)
    o_ref[0] = jnp.dot(ca, w_ref[0], preferred_element_type=F32,
                       precision=lax.Precision.HIGHEST) + b_ref[0]


def _adaln(c, ada_w, ada_b):
    L, D, N = ada_w.shape
    B = c.shape[0]
    tn = 1536
    return pl.pallas_call(
        _adaln_kernel,
        grid=(L, N // tn),
        in_specs=[pl.BlockSpec((B, D), lambda l, j: (0, 0)),
                  pl.BlockSpec((1, D, tn), lambda l, j: (l, 0, j)),
                  pl.BlockSpec((1, 1, tn), lambda l, j: (l, 0, j))],
        out_specs=pl.BlockSpec((1, B, tn), lambda l, j: (l, 0, j)),
        out_shape=jax.ShapeDtypeStruct((L, B, N), F32),
        compiler_params=_cparams(("arbitrary", "arbitrary")),
        name="adaln",
    )(c, ada_w, ada_b.reshape(L, 1, N))


def _modnorm(x, g, sc, sh):
    ms = jnp.mean(x * x, axis=-1, keepdims=True)
    y = x * lax.rsqrt(ms + EPS) * g
    return y * (1.0 + sc) + sh


def _premix_kernel(x_ref, sh_ref, sc_ref, g_ref, wqkv_ref, wf_ref, wg_ref, bf_ref, hn_ref, bd_ref,
                   tri_ref, qkv_ref, cum_ref, ga_ref, gb_ref, carry_ref):
    si = pl.program_id(1)
    tm = x_ref.shape[1]
    h = _modnorm(x_ref[0], g_ref[...], sc_ref[0], sh_ref[0]).astype(BF16)

    normed = {0: 0, 1: 1, 3: 2, 4: 3}
    for c in range(6):
        z = _dot(h, wqkv_ref[:, c * WIDTH:(c + 1) * WIDTH])
        if c in normed:
            sq = (z * z).astype(BF16)
            ms = jnp.concatenate(
                [_dot(sq[:, j * MXU_DIM:(j + 1) * MXU_DIM], bd_ref[...]) for j in range(WIDTH // MXU_DIM)],
                axis=1)
            r = normed[c]
            z = z * lax.rsqrt(ms + EPS) * hn_ref[r:r + 1, :]
        qkv_ref[c, 0] = z.astype(BF16)

    fr = _dot_nt(wf_ref[...], h)[:N_HEADS]
    xg = fr + bf_ref[...]
    logf = jnp.minimum(xg, 0.0) - jnp.log(1.0 + jnp.exp(-jnp.abs(xg)))
    hi = logf.astype(BF16).astype(F32)
    r1 = logf - hi
    mid = r1.astype(BF16).astype(F32)
    lo = r1 - mid
    parts = jnp.concatenate([hi, mid, lo, jnp.zeros_like(hi)], axis=0)
    cs3 = _dot(parts.astype(BF16), tri_ref[...])
    cs = cs3[0:8] + cs3[8:16] + cs3[16:24]

    @pl.when(si == 0)
    def _():
        carry_ref[...] = jnp.zeros_like(carry_ref)

    cum = cs + carry_ref[:, 0:1]
    cum_ref[0] = cum
    carry_ref[...] = jnp.broadcast_to(cum[:, tm - 1:tm], carry_ref.shape)

    for c in range(4):
        zg = _dot(h, wg_ref[:, c * WIDTH:(c + 1) * WIDTH])
        sg = _sigmoid(zg).astype(BF16)
        if c < 2:
            ga_ref[0, :, c * WIDTH:(c + 1) * WIDTH] = sg
        else:
            gb_ref[0, :, (c - 2) * WIDTH:(c - 1) * WIDTH] = sg


def _premix(x, sh, sc, g, wqkv, wf, wg, bf, hn, bd, tri):
    B, S, D = x.shape
    tm = TM_PRE
    const = lambda b, s: (0, 0)
    return pl.pallas_call(
        _premix_kernel,
        grid=(B, S // tm),
        in_specs=[pl.BlockSpec((1, tm, D), lambda b, s: (b, s, 0)),
                  pl.BlockSpec((1, 1, D), lambda b, s: (b, 0, 0)),
                  pl.BlockSpec((1, 1, D), lambda b, s: (b, 0, 0)),
                  pl.BlockSpec((1, D), const),
                  pl.BlockSpec(wqkv.shape, const),
                  pl.BlockSpec(wf.shape, const),
                  pl.BlockSpec(wg.shape, const),
                  pl.BlockSpec(bf.shape, const),
                  pl.BlockSpec(hn.shape, const),
                  pl.BlockSpec(bd.shape, const),
                  pl.BlockSpec(tri.shape, const)],
        out_specs=[pl.BlockSpec((6, 1, tm, WIDTH), lambda b, s: (0, b, s, 0)),
                   pl.BlockSpec((1, N_HEADS, tm), lambda b, s: (b, 0, s)),
                   pl.BlockSpec((1, tm, D), lambda b, s: (b, s, 0)),
                   pl.BlockSpec((1, tm, D), lambda b, s: (b, s, 0))],
        out_shape=[jax.ShapeDtypeStruct((6, B, S, WIDTH), BF16),
                   jax.ShapeDtypeStruct((B, N_HEADS, S), F32),
                   jax.ShapeDtypeStruct((B, S, D), BF16),
                   jax.ShapeDtypeStruct((B, S, D), BF16)],
        scratch_shapes=[pltpu.VMEM((N_HEADS, LANES), F32)],
        compiler_params=_cparams(("arbitrary", "arbitrary")),
        name="premix",
    )(x, sh, sc, g, wqkv, wf, wg, bf, hn, bd, tri)


def _pair_finish(acc0, acc1, lane):
    l0 = pltpu.roll(acc0, HEAD_DIM, axis=1)
    l1 = pltpu.roll(acc1, HEAD_DIM, axis=1)
    return jnp.where(lane < HEAD_DIM, acc0 / l0, acc1 / l1)


def _band_kernel(q_ref, k0_ref, k1_ref, k2_ref, v0_ref, v1_ref, v2_ref, bias_ref, o_ref):
    qi = pl.program_id(2)
    tq = q_ref.shape[0]
    q = q_ref[...]
    lane = lax.broadcasted_iota(jnp.int32, (1, LANES), 1)
    col = lax.broadcasted_iota(jnp.int32, (1, BAND_WIN), 1)
    kvalid = (col + (qi * tq - LOOKBACK_CHUNKS * CHUNK)) >= 0
    ks = (k0_ref[...], k1_ref[...], k2_ref[...])
    vs = (v0_ref[...], v1_ref[...], v2_ref[...])
    accs = []
    for hh in range(2):
        hm = (lane // HEAD_DIM) == hh
        qm = jnp.where(hm, q, jnp.zeros_like(q))
        s = jnp.concatenate([_dot_nt(qm, k) for k in ks], axis=1) + bias_ref[hh]
        s = jnp.where(kvalid, s, NEG_INF)
        m = jnp.max(s, axis=1, keepdims=True)
        p = jnp.exp(s - m).astype(BF16)
        acc = None
        for j in range(3):
            va = jnp.where(hm, vs[j], jnp.ones_like(vs[j]))
            t = _dot(p[:, j * tq:(j + 1) * tq], va)
            acc = t if acc is None else acc + t
        accs.append(acc)
    o_ref[...] = _pair_finish(accs[0], accs[1], lane).astype(o_ref.dtype)


def _band_attention(qkv, bias):
    _, B, S, _ = qkv.shape
    tq = TQ_BAND
    nq = S // tq

    def kv_spec(c, back):
        return pl.BlockSpec((None, None, tq, LANES),
                            lambda b, hp, qi: (c, b, jnp.maximum(qi - back, 0), hp))

    return pl.pallas_call(
        _band_kernel,
        grid=(B, N_HEADS // 2, nq),
        in_specs=[pl.BlockSpec((None, None, tq, LANES), lambda b, hp, qi: (0, b, qi, hp)),
                  kv_spec(1, 2), kv_spec(1, 1), kv_spec(1, 0),
                  kv_spec(2, 2), kv_spec(2, 1), kv_spec(2, 0),
                  pl.BlockSpec((2, tq, BAND_WIN), lambda b, hp, qi: (hp, 0, 0))],
        out_specs=pl.BlockSpec((None, tq, LANES), lambda b, hp, qi: (b, qi, hp)),
        out_shape=jax.ShapeDtypeStruct((B, S, WIDTH), BF16),
        compiler_params=_cparams(("arbitrary", "arbitrary", "arbitrary")),
        name="band_attn",
    )(qkv, qkv, qkv, qkv, qkv, qkv, qkv, bias)


def _band_bias(rel_table):
    pad = LOOKBACK_CHUNKS * CHUNK
    i = jnp.arange(TQ_BAND)[:, None]
    j = jnp.arange(BAND_WIN)[None, :]
    rel = i + pad - j
    bias = rel_table[:, jnp.clip(rel, -REL_CLIP, REL_CLIP) + REL_CLIP].astype(F32)
    c0 = (i // CHUNK) * CHUNK
    inband = (j >= c0) & (j < c0 + pad + CHUNK)
    return jnp.where(inband[None], bias, NEG_INF)


def _fox_kernel(qi_tab, kj_tab, q_ref, k_ref, v_ref, cum_ref, o_ref, acc_ref, m_ref):
    hp = pl.program_id(1)
    t = pl.program_id(2)
    qi = qi_tab[t]
    kj = kj_tab[t]
    tq, tk = q_ref.shape[0], k_ref.shape[0]
    lane = lax.broadcasted_iota(jnp.int32, (1, LANES), 1)

    @pl.when(kj == 0)
    def _():
        acc_ref[...] = jnp.zeros_like(acc_ref)
        m_ref[...] = jnp.full_like(m_ref, NEG_INF)

    def step(diagonal):
        q = q_ref[...]
        k = k_ref[...]
        v = v_ref[...]
        if diagonal:
            row = lax.broadcasted_iota(jnp.int32, (tq, tk), 0)
            colk = lax.broadcasted_iota(jnp.int32, (tq, tk), 1)
            causal = colk <= row
        for hh in range(2):
            hm = (lane // HEAD_DIM) == hh
            qm = jnp.where(hm, q, jnp.zeros_like(q))
            s = _dot_nt(qm, k) - cum_ref[pl.ds(2 * hp + hh, 1), :]
            if diagonal:
                s = jnp.where(causal, s, NEG_INF)
            m_old = m_ref[hh]
            m_new = jnp.maximum(m_old, jnp.max(s, axis=1, keepdims=True))
            alpha = jnp.exp(m_old - m_new)
            p = jnp.exp(s - m_new).astype(BF16)
            va = jnp.where(hm, v, jnp.ones_like(v))
            acc_ref[hh] = acc_ref[hh] * alpha + _dot(p, va)
            m_ref[hh] = m_new

    @pl.when(kj < qi)
    def _():
        step(False)

    @pl.when(kj == qi)
    def _():
        step(True)
        o_ref[...] = _pair_finish(acc_ref[0], acc_ref[1], lane).astype(o_ref.dtype)


def _fox_attention(qkv, cum):
    _, B, S, _ = qkv.shape
    tq = TQ_FOX
    nq = S // tq
    pairs = [(i, j) for i in range(nq) for j in range(i + 1)]
    qi_tab = jnp.array([p[0] for p in pairs], jnp.int32)
    kj_tab = jnp.array([p[1] for p in pairs], jnp.int32)
    grid_spec = pltpu.PrefetchScalarGridSpec(
        num_scalar_prefetch=2,
        grid=(B, N_HEADS // 2, len(pairs)),
        in_specs=[pl.BlockSpec((None, None, tq, LANES), lambda b, hp, t, qt, kt: (3, b, qt[t], hp)),
                  pl.BlockSpec((None, None, tq, LANES), lambda b, hp, t, qt, kt: (4, b, kt[t], hp)),
                  pl.BlockSpec((None, None, tq, LANES), lambda b, hp, t, qt, kt: (5, b, kt[t], hp)),
                  pl.BlockSpec((None, N_HEADS, tq), lambda b, hp, t, qt, kt: (b, 0, kt[t]))],
        out_specs=pl.BlockSpec((None, tq, LANES), lambda b, hp, t, qt, kt: (b, qt[t], hp)),
        scratch_shapes=[pltpu.VMEM((2, tq, LANES), F32), pltpu.VMEM((2, tq, 1), F32)],
    )
    return pl.pallas_call(
        _fox_kernel,
        grid_spec=grid_spec,
        out_shape=jax.ShapeDtypeStruct((B, S, WIDTH), BF16),
        compiler_params=_cparams(("arbitrary", "arbitrary", "arbitrary")),
        name="fox_attn",
    )(qi_tab, kj_tab, qkv, qkv, qkv, cum)


def _postmix_kernel(ya_ref, yb_ref, ga_ref, gb_ref, x_ref, g1_ref, wpa_ref, wpb_ref, wo_ref,
                    g_ref, sh_ref, sc_ref, x1_ref, h2_ref):
    ua = _dot(ya_ref[0], wpa_ref[...])
    ub = _dot(yb_ref[0], wpb_ref[...])
    m = ga_ref[0].astype(F32) * ua + gb_ref[0].astype(F32) * ub
    mo = _dot(m.astype(BF16), wo_ref[...])
    x1 = x_ref[0] + g1_ref[0] * mo
    x1_ref[0] = x1
    h2_ref[0] = _modnorm(x1, g_ref[...], sc_ref[0], sh_ref[0]).astype(BF16)


def _postmix(ya, yb, ga, gb, x, g1, wpa, wpb, wo, g, sh, sc):
    B, S, D = x.shape
    tm = TM_PRE
    const = lambda b, s: (0, 0)
    tok = lambda w: pl.BlockSpec((1, tm, w), lambda b, s: (b, s, 0))
    row = pl.BlockSpec((1, 1, D), lambda b, s: (b, 0, 0))
    return pl.pallas_call(
        _postmix_kernel,
        grid=(B, S // tm),
        in_specs=[tok(WIDTH), tok(WIDTH), tok(D), tok(D), tok(D), row,
                  pl.BlockSpec(wpa.shape, const), pl.BlockSpec(wpb.shape, const),
                  pl.BlockSpec(wo.shape, const), pl.BlockSpec((1, D), const), row, row],
        out_specs=[tok(D), tok(D)],
        out_shape=[jax.ShapeDtypeStruct((B, S, D), F32), jax.ShapeDtypeStruct((B, S, D), BF16)],
        compiler_params=_cparams(("arbitrary", "arbitrary")),
        name="postmix",
    )(ya, yb, ga, gb, x, g1, wpa, wpb, wo, g, sh, sc)


def _router_kernel(h_ref, wr_ref, rb_ref, ustrict_ref, lstrict_ref, posk_ref, ptok_ref, wtok_ref, cnt_ref):
    tm = h_ref.shape[0]
    per_group = N_EXPERTS // N_GROUPS
    logits = _dot_nt(wr_ref[...], h_ref[...])
    scores = _sigmoid(logits)
    biased = scores + rb_ref[...]
    sub8 = lax.broadcasted_iota(jnp.int32, (per_group, tm), 0).astype(F32)
    neg = -jnp.inf

    grp_rows = []
    for g in range(N_GROUPS):
        a = biased[g * per_group:(g + 1) * per_group]
        m1 = jnp.max(a, axis=0, keepdims=True)
        i1 = jnp.min(jnp.where(a == m1, sub8, float(per_group)), axis=0, keepdims=True)
        m2 = jnp.max(jnp.where(sub8 == i1, neg, a), axis=0, keepdims=True)
        grp_rows.append(m1 + m2)
    grp = jnp.concatenate(grp_rows, axis=0)

    gsub = lax.broadcasted_iota(jnp.int32, (N_GROUPS, tm), 0).astype(F32)
    gmask = jnp.zeros((N_GROUPS, tm), F32)
    for _ in range(TOPK_GROUPS):
        mx = jnp.max(grp, axis=0, keepdims=True)
        gi = jnp.min(jnp.where(grp == mx, gsub, float(N_GROUPS)), axis=0, keepdims=True)
        sel = gsub == gi
        gmask = jnp.where(sel, 1.0, gmask)
        grp = jnp.where(sel, neg, grp)
    masked = jnp.concatenate(
        [jnp.where(gmask[g:g + 1] > 0.5, biased[g * per_group:(g + 1) * per_group], neg)
         for g in range(N_GROUPS)], axis=0)

    esub = lax.broadcasted_iota(jnp.int32, (N_EXPERTS, tm), 0).astype(F32)
    sels, ws = [], []
    for _ in range(TOP_K):
        mx = jnp.max(masked, axis=0, keepdims=True)
        ei = jnp.min(jnp.where(masked == mx, esub, float(N_EXPERTS)), axis=0, keepdims=True)
        sel = esub == ei
        sels.append(sel)
        ws.append(jnp.sum(jnp.where(sel, scores, 0.0), axis=0, keepdims=True))
        masked = jnp.where(sel, neg, masked)
    wsum = ws[0]
    for w in ws[1:]:
        wsum = wsum + w

    selmask = jnp.zeros((N_EXPERTS, tm), F32)
    for sel in sels:
        selmask = selmask + jnp.where(sel, 1.0, 0.0)
    rank = _dot(selmask.astype(BF16), ustrict_ref[...])
    counts = jnp.sum(selmask, axis=1, keepdims=True)
    padded = jnp.ceil(counts * (1.0 / RUN_ALIGN)) * RUN_ALIGN
    padded_b = jnp.broadcast_to(padded, (N_EXPERTS, LANES))
    base = _dot(lstrict_ref[...], padded_b.astype(BF16))
    pos = base[:, 0:1] + rank

    posk = jnp.concatenate([jnp.sum(jnp.where(sel, pos, 0.0), axis=0, keepdims=True) for sel in sels], axis=0)
    wk = jnp.concatenate([(w / wsum) * ROUTED_SCALE for w in ws], axis=0)
    posk_ref[0] = posk
    fill = jnp.zeros((LANES - TOP_K, tm), F32)
    ptok_ref[...] = jnp.concatenate([posk, fill - 1.0], axis=0).T
    wtok_ref[...] = jnp.concatenate([wk, fill], axis=0).T
    cnt_ref[0] = padded_b


def _router(h2, wr_t, rb, ustrict, lstrict):
    T, D = h2.shape
    tm = TM_MOE
    nt = T // tm
    const = lambda i: (0, 0)
    return pl.pallas_call(
        _router_kernel,
        grid=(nt,),
        in_specs=[pl.BlockSpec((tm, D), lambda i: (i, 0)),
                  pl.BlockSpec(wr_t.shape, const), pl.BlockSpec(rb.shape, const),
                  pl.BlockSpec(ustrict.shape, const), pl.BlockSpec(lstrict.shape, const)],
        out_specs=[pl.BlockSpec((1, TOP_K, tm), lambda i: (i, 0, 0)),
                   pl.BlockSpec((tm, LANES), lambda i: (i, 0)),
                   pl.BlockSpec((tm, LANES), lambda i: (i, 0)),
                   pl.BlockSpec((1, N_EXPERTS, LANES), lambda i: (i, 0, 0))],
        out_shape=[jax.ShapeDtypeStruct((nt, TOP_K, tm), F32),
                   jax.ShapeDtypeStruct((T, LANES), F32),
                   jax.ShapeDtypeStruct((T, LANES), F32),
                   jax.ShapeDtypeStruct((nt, N_EXPERTS, LANES), F32)],
        compiler_params=_cparams(("arbitrary",)),
        name="router",
    )(h2, wr_t, rb, ustrict, lstrict)


def _run_copies(m, src_ref, src0, dst_ref, dst0, sem, sizes, wait, src_fixed=False):
    for bit, size in enumerate(sizes):
        @pl.when(((m >> bit) & 1) == 1)
        def _(bit=bit, size=size):
            low = (m & ((1 << bit) - 1)) * RUN_ALIGN
            src_at = src0 if src_fixed else pl.multiple_of(src0 + low, RUN_ALIGN)
            cp = pltpu.make_async_copy(
                src_ref.at[pl.ds(src_at, size)],
                dst_ref.at[pl.ds(pl.multiple_of(dst0 + low, RUN_ALIGN), size)],
                sem)
            if wait:
                cp.wait()
            else:
                cp.start()


def _dispatch_kernel(off_ref, m_ref, base_ref, gapoff_ref, gapm_ref, h_ref, posk_ref, xs_hbm,
                     xs_ref, zero_ref, sem):
    i = pl.program_id(0)
    tm = h_ref.shape[0]
    n_chunks = K_SORT // MXU_DIM

    @pl.when(i == 0)
    def _():
        zero_ref[...] = jnp.zeros_like(zero_ref)
        for wait in (False, True):
            def gap(e, carry, wait=wait):
                _run_copies(gapm_ref[e], zero_ref, 0, xs_hbm, gapoff_ref[e], sem, GAP_SIZES, wait,
                            src_fixed=True)
                return carry
            lax.fori_loop(0, N_EXPERTS, gap, 0)

    h = h_ref[...]
    posk = posk_ref[0]
    for c in range(n_chunks):
        rows = (lax.broadcasted_iota(jnp.int32, (MXU_DIM, tm), 0) + c * MXU_DIM).astype(F32)
        p = jnp.zeros((MXU_DIM, tm), F32)
        for k in range(TOP_K):
            p = p + jnp.where(rows == posk[k:k + 1, :], 1.0, 0.0)
        xs_ref[c * MXU_DIM:(c + 1) * MXU_DIM, :] = _dot(p.astype(BF16), h).astype(BF16)

    for wait in (False, True):
        def run(e, carry, wait=wait):
            idx = i * N_EXPERTS + e
            _run_copies(m_ref[idx], xs_ref, base_ref[idx], xs_hbm, off_ref[idx], sem, RUN_SIZES, wait)
            return carry
        lax.fori_loop(0, N_EXPERTS, run, 0)


def _dispatch(off, m16, base, gapoff, gapm, h2, posk, n_rows):
    T, D = h2.shape
    tm = TM_MOE
    grid_spec = pltpu.PrefetchScalarGridSpec(
        num_scalar_prefetch=5,
        grid=(T // tm,),
        in_specs=[pl.BlockSpec((tm, D), lambda i, *_: (i, 0)),
                  pl.BlockSpec((1, TOP_K, tm), lambda i, *_: (i, 0, 0))],
        out_specs=pl.BlockSpec(memory_space=pl.ANY),
        scratch_shapes=[pltpu.VMEM((K_SORT, D), BF16),
                        pltpu.VMEM((GAP_SIZES[-1], D), BF16),
                        pltpu.SemaphoreType.DMA],
    )
    return pl.pallas_call(
        _dispatch_kernel,
        grid_spec=grid_spec,
        out_shape=jax.ShapeDtypeStruct((n_rows, D), BF16),
        compiler_params=_cparams(("arbitrary",)),
        name="dispatch",
    )(off, m16, base, gapoff, gapm, h2, posk)


def _expert_kernel(bexp_ref, nvalid_ref, x_ref, wgu_ref, wd_ref, y_ref):
    @pl.when(pl.program_id(0) < nvalid_ref[0])
    def _():
        gu = _dot(x_ref[...], wgu_ref[...])
        g = gu[:, :EXPERT_FF]
        a = (g * _sigmoid(g) * gu[:, EXPERT_FF:]).astype(BF16)
        y_ref[...] = _dot(a, wd_ref[...]).astype(y_ref.dtype)


def _experts(bexp, nvalid, xs, wgu, wd):
    n_rows, D = xs.shape
    nb = n_rows // BLOCK_ROWS

    def blk(i, be, nv):
        return jnp.minimum(i, nv[0] - 1)

    grid_spec = pltpu.PrefetchScalarGridSpec(
        num_scalar_prefetch=2,
        grid=(nb,),
        in_specs=[pl.BlockSpec((BLOCK_ROWS, D), lambda i, be, nv: (blk(i, be, nv), 0)),
                  pl.BlockSpec((None, D, 2 * EXPERT_FF), lambda i, be, nv: (be[blk(i, be, nv)], 0, 0)),
                  pl.BlockSpec((None, EXPERT_FF, D), lambda i, be, nv: (be[blk(i, be, nv)], 0, 0))],
        out_specs=pl.BlockSpec((BLOCK_ROWS, D), lambda i, be, nv: (blk(i, be, nv), 0)),
    )
    return pl.pallas_call(
        _expert_kernel,
        grid_spec=grid_spec,
        out_shape=jax.ShapeDtypeStruct((n_rows, D), BF16),
        compiler_params=_cparams(("arbitrary",)),
        name="experts",
    )(bexp, nvalid, xs, wgu, wd)


def _combine_kernel(off_ref, m_ref, base_ref, ys_hbm, ptok_ref, wtok_ref, h_ref, x_ref, g2_ref,
                    wsgu_ref, wsd_ref, o_ref, ys_ref, pw_ref, sem):
    i = pl.program_id(0)
    tm = h_ref.shape[0]

    @pl.when(i == 0)
    def _():
        ys_ref[...] = jnp.zeros_like(ys_ref)

    def run(e, carry, wait):
        idx = i * N_EXPERTS + e
        _run_copies(m_ref[idx], ys_hbm, off_ref[idx], ys_ref, base_ref[idx], sem, RUN_SIZES, wait)
        return carry

    lax.fori_loop(0, N_EXPERTS, functools.partial(run, wait=False), 0)

    h = h_ref[...]
    gu = _dot(h, wsgu_ref[...])
    g = gu[:, :EXPERT_FF]
    shared = _dot((g * _sigmoid(g) * gu[:, EXPERT_FF:]).astype(BF16), wsd_ref[...])
    ptok = ptok_ref[...]
    wtok = wtok_ref[...]
    for c in range(K_SORT // MXU_DIM):
        cols = (lax.broadcasted_iota(jnp.int32, (tm, MXU_DIM), 1) + c * MXU_DIM).astype(F32)
        pw = jnp.zeros((tm, MXU_DIM), F32)
        for k in range(TOP_K):
            pw = pw + jnp.where(cols == ptok[:, k:k + 1], wtok[:, k:k + 1], 0.0)
        pw_ref[:, c * MXU_DIM:(c + 1) * MXU_DIM] = pw.astype(BF16)

    lax.fori_loop(0, N_EXPERTS, functools.partial(run, wait=True), 0)
    routed = _dot(pw_ref[...], ys_ref[...])
    o_ref[...] = x_ref[...] + g2_ref[0] * (routed + shared)


def _combine(off, m16, base, ys, ptok, wtok, h2, x1, g2, wsgu, wsd, tiles_per_batch):
    T, D = h2.shape
    tm = TM_MOE
    const = lambda i, *_: (0, 0)
    tok = lambda w: pl.BlockSpec((tm, w), lambda i, *_: (i, 0))
    grid_spec = pltpu.PrefetchScalarGridSpec(
        num_scalar_prefetch=3,
        grid=(T // tm,),
        in_specs=[pl.BlockSpec(memory_space=pl.ANY), tok(LANES), tok(LANES), tok(D), tok(D),
                  pl.BlockSpec((1, 1, D), lambda i, *_: (i // tiles_per_batch, 0, 0)),
                  pl.BlockSpec(wsgu.shape, const), pl.BlockSpec(wsd.shape, const)],
        out_specs=tok(D),
        scratch_shapes=[pltpu.VMEM((K_SORT, D), BF16),
                        pltpu.VMEM((tm, K_SORT), BF16),
                        pltpu.SemaphoreType.DMA],
    )
    return pl.pallas_call(
        _combine_kernel,
        grid_spec=grid_spec,
        out_shape=jax.ShapeDtypeStruct((T, D), F32),
        compiler_params=_cparams(("arbitrary",)),
        name="combine",
    )(off, m16, base, ys, ptok, wtok, h2, x1, g2, wsgu, wsd)


def _moe(h2, x1, g2, wr_t, rb, wgu, wd, wsgu, wsd, ustrict, lstrict, tiles_per_batch):
    T, D = h2.shape
    nt = T // TM_MOE
    posk, ptok, wtok, cnt = _router(h2, wr_t, rb, ustrict, lstrict)

    pad = cnt[:, :, 0].astype(jnp.int32)
    base = jnp.cumsum(pad, axis=1) - pad
    tile_off = jnp.cumsum(pad, axis=0) - pad
    total = jnp.sum(pad, axis=0)
    region = ((total + BLOCK_ROWS - 1) // BLOCK_ROWS) * BLOCK_ROWS
    rend = jnp.cumsum(region)
    rstart = rend - region
    off = rstart[None, :] + tile_off
    nb_max = -(-(TOP_K * T + nt * N_EXPERTS * (RUN_ALIGN - 1)) // BLOCK_ROWS) + N_EXPERTS
    bexp = jnp.minimum(jnp.searchsorted(rend, jnp.arange(nb_max, dtype=jnp.int32) * BLOCK_ROWS, side="right"),
                       N_EXPERTS - 1).astype(jnp.int32)
    nvalid = (rend[-1:] // BLOCK_ROWS).astype(jnp.int32)
    flat = lambda a: a.reshape(-1).astype(jnp.int32)

    xs = _dispatch(flat(off), flat(pad // RUN_ALIGN), flat(base), flat(rstart + total),
                   flat((region - total) // RUN_ALIGN), h2, posk, nb_max * BLOCK_ROWS)
    ys = _experts(bexp, nvalid, xs, wgu, wd)
    return _combine(flat(off), flat(pad // RUN_ALIGN), flat(base), ys, ptok, wtok, h2, x1, g2, wsgu, wsd,
                    tiles_per_batch)


def kernel(x, c, ada_w, ada_b, mix_norm_g, w_in, b_fgate, qn_a, kn_a, qn_b, kn_b, rel_bias, w_proj_a,
           w_proj_b, w_out, ffn_norm_g, w_router, router_bias, w_gate_e, w_up_e, w_down_e, w_gate_s,
           w_up_s, w_down_s):
    B, S, D = x.shape
    L = ada_w.shape[0]
    T = B * S
    assert D == D_MODEL and S % TM_PRE == 0 and S % TQ_FOX == 0 and S % TM_MOE == 0

    mod = _adaln(c, ada_w, ada_b).reshape(L, B, 6, 1, D)

    hid = jnp.arange(MXU_DIM) // HEAD_DIM
    bd = jnp.where(hid[:, None] == hid[None, :], 1.0 / HEAD_DIM, 0.0).astype(BF16)
    r = jnp.arange(TM_PRE)
    tri = (r[:, None] <= r[None, :]).astype(BF16)
    r = jnp.arange(TM_MOE)
    ustrict = (r[:, None] < r[None, :]).astype(BF16)
    r = jnp.arange(N_EXPERTS)
    lstrict = (r[None, :] < r[:, None]).astype(BF16)

    q_scale = 1.0 / math.sqrt(HEAD_DIM)
    for l in range(L):
        sh1, sc1, g1, sh2, sc2, g2 = (mod[l, :, j] for j in range(6))
        w = w_in[l]
        wqkv = w[:, :6 * WIDTH].astype(BF16)
        wf = jnp.zeros((2 * N_HEADS, D), BF16).at[:N_HEADS].set(w[:, 6 * WIDTH:6 * WIDTH + N_HEADS].T.astype(BF16))
        wg = w[:, 6 * WIDTH + N_HEADS:].astype(BF16)
        hn = jnp.zeros((8, WIDTH), F32)
        hn = hn.at[0].set(jnp.tile(qn_a[l], N_HEADS) * q_scale).at[1].set(jnp.tile(kn_a[l], N_HEADS))
        hn = hn.at[2].set(jnp.tile(qn_b[l], N_HEADS) * q_scale).at[3].set(jnp.tile(kn_b[l], N_HEADS))

        qkv, cum, ga, gb = _premix(x, sh1, sc1, mix_norm_g[l][None], wqkv, wf, wg,
                                   b_fgate[l][:, None], hn, bd, tri)
        ya = _band_attention(qkv, _band_bias(rel_bias[l]))
        yb = _fox_attention(qkv, cum)
        x1, h2 = _postmix(ya, yb, ga, gb, x, g1, w_proj_a[l].astype(BF16), w_proj_b[l].astype(BF16),
                          w_out[l].astype(BF16), ffn_norm_g[l][None], sh2, sc2)

        wgu = jnp.concatenate([w_gate_e[l], w_up_e[l]], axis=-1).astype(BF16)
        wsgu = jnp.concatenate([w_gate_s[l], w_up_s[l]], axis=-1).astype(BF16)
        x = _moe(h2.reshape(T, D), x1.reshape(T, D), g2, w_router[l].T.astype(BF16),
                 router_bias[l][:, None], wgu, w_down_e[l].astype(BF16), wsgu,
                 w_down_s[l].astype(BF16), ustrict, lstrict, S // TM_MOE).reshape(B, S, D)
    return x
```

```python
import functools
import math

import jax
import jax.numpy as jnp
from jax import lax
from jax.experimental import pallas as pl
from jax.experimental.pallas import tpu as pltpu

F32 = jnp.float32
BF16 = jnp.bfloat16

D_MODEL = 1024
HEAD_DIM = 64
N_HEADS = 8
WIDTH = N_HEADS * HEAD_DIM
CHUNK = 64
LOOKBACK_CHUNKS = 8
REL_CLIP = 128
N_EXPERTS = 64
TOP_K = 8
N_GROUPS = 8
TOPK_GROUPS = 4
EXPERT_FF = 256
ROUTED_SCALE = 2.5
EPS = 1e-6
NEG_INF = -1e30
LOG2E = math.log2(math.e)

LANES = 128
MXU_DIM = 256
VMEM_LIMIT = 56 * 1024 * 1024

TM_PRE = 512
TQ_BAND = 256
BAND_WIN = TQ_BAND + LOOKBACK_CHUNKS * CHUNK
TQ_FOX = 512
TM_MOE = 256
RUN_ALIGN = 16
K_SORT = ((TOP_K * TM_MOE + N_EXPERTS * (RUN_ALIGN - 1) + MXU_DIM - 1) // MXU_DIM) * MXU_DIM
BLOCK_ROWS = 512
RUN_SIZES = tuple(RUN_ALIGN << b for b in range(int(math.log2(TM_MOE // RUN_ALIGN)) + 1))
GAP_SIZES = tuple(RUN_ALIGN << b for b in range(int(math.log2(BLOCK_ROWS // RUN_ALIGN))))


def _dot(a, b):
    return jnp.dot(a, b, preferred_element_type=F32)


def _dot_nt(a, b):
    return lax.dot_general(a, b, (((1,), (1,)), ((), ())), preferred_element_type=F32)


def _sigmoid(v):
    return 1.0 / (1.0 + jnp.exp(-v))


def _cparams(sem):
    return pltpu.CompilerParams(dimension_semantics=sem, vmem_limit_bytes=VMEM_LIMIT)


def _adaln_kernel(c_ref, w_ref, b_ref, o_ref):
    c = c_ref[...]
    ca = c * _sigmoid(c)
    o_ref[0] = jnp.dot(ca, w_ref[0], preferred_element_type=F32,
                       precision=lax.Precision.HIGHEST) + b_ref[0]


def _adaln(c, ada_w, ada_b):
    L, D, N = ada_w.shape
    B = c.shape[0]
    tn = 1536
    return pl.pallas_call(
        _adaln_kernel,
        grid=(L, N // tn),
        in_specs=[pl.BlockSpec((B, D), lambda l, j: (0, 0)),
                  pl.BlockSpec((1, D, tn), lambda l, j: (l, 0, j)),
                  pl.BlockSpec((1, 1, tn), lambda l, j: (l, 0, j))],
        out_specs=pl.BlockSpec((1, B, tn), lambda l, j: (l, 0, j)),
        out_shape=jax.ShapeDtypeStruct((L, B, N), F32),
        compiler_params=_cparams(("arbitrary", "arbitrary")),
        name="adaln",
    )(c, ada_w, ada_b.reshape(L, 1, N))


def _modnorm(x, g, sc, sh):
    ms = jnp.mean(x * x, axis=-1, keepdims=True)
    y = x * lax.rsqrt(ms + EPS) * g
    return y * (1.0 + sc) + sh


def _premix_kernel(x_ref, sh_ref, sc_ref, g_ref, wqkv_ref, wvt_ref, wf_ref, wg_ref, bf_ref, hn_ref, bd_ref,
                   tri_ref, qkv_ref, vt_ref, dec_ref, cbase_ref, ga_ref, gb_ref, carry_ref):
    si = pl.program_id(1)
    tm = x_ref.shape[1]
    h = _modnorm(x_ref[0], g_ref[...], sc_ref[0], sh_ref[0]).astype(BF16)

    normed = {0: 0, 1: 1, 3: 2, 4: 3}
    for c in range(6):
        if c == 5:
            vt_ref[0] = _dot_nt(wvt_ref[...], h).astype(BF16).reshape(N_HEADS // 2, LANES, tm)
            continue
        z = _dot(h, wqkv_ref[:, c * WIDTH:(c + 1) * WIDTH])
        if c in normed:
            sq = (z * z).astype(BF16)
            ms = jnp.concatenate(
                [_dot(sq[:, j * MXU_DIM:(j + 1) * MXU_DIM], bd_ref[...]) for j in range(WIDTH // MXU_DIM)],
                axis=1)
            r = normed[c]
            z = z * lax.rsqrt(ms + EPS) * hn_ref[r:r + 1, :]
        qkv_ref[c, 0] = z.astype(BF16)

    fr = _dot_nt(wf_ref[...], h)[:N_HEADS]
    xg = fr + bf_ref[...]
    logf = jnp.minimum(xg, 0.0) - jnp.log(1.0 + jnp.exp(-jnp.abs(xg)))
    hi = logf.astype(BF16).astype(F32)
    r1 = logf - hi
    mid = r1.astype(BF16).astype(F32)
    lo = r1 - mid
    parts = jnp.concatenate([hi, mid, lo, jnp.zeros_like(hi)], axis=0)
    cs3 = _dot(parts.astype(BF16), tri_ref[...])
    cs = cs3[0:8] + cs3[8:16] + cs3[16:24]

    @pl.when(si == 0)
    def _():
        carry_ref[...] = jnp.zeros_like(carry_ref)

    drel = cs * (-LOG2E)
    dhi = drel.astype(BF16).astype(F32)
    dlo = drel - dhi
    dec = jnp.concatenate([dhi, dlo, jnp.zeros((LANES - 2 * N_HEADS, tm), F32)], axis=0)
    dec_ref[0] = dec.T.astype(BF16)
    carry = carry_ref[:, 0:1]
    cbase_ref[0, 0] = jnp.broadcast_to(carry * (-LOG2E), (N_HEADS, tm))
    carry_ref[...] = jnp.broadcast_to(carry + cs[:, tm - 1:tm], carry_ref.shape)

    for c in range(4):
        zg = _dot(h, wg_ref[:, c * WIDTH:(c + 1) * WIDTH])
        sg = _sigmoid(zg).astype(BF16)
        if c < 2:
            ga_ref[0, :, c * WIDTH:(c + 1) * WIDTH] = sg
        else:
            gb_ref[0, :, (c - 2) * WIDTH:(c - 1) * WIDTH] = sg


def _premix(x, sh, sc, g, wqkv, wvt, wf, wg, bf, hn, bd, tri):
    B, S, D = x.shape
    tm = TM_PRE
    const = lambda b, s: (0, 0)
    return pl.pallas_call(
        _premix_kernel,
        grid=(B, S // tm),
        in_specs=[pl.BlockSpec((1, tm, D), lambda b, s: (b, s, 0)),
                  pl.BlockSpec((1, 1, D), lambda b, s: (b, 0, 0)),
                  pl.BlockSpec((1, 1, D), lambda b, s: (b, 0, 0)),
                  pl.BlockSpec((1, D), const),
                  pl.BlockSpec(wqkv.shape, const),
                  pl.BlockSpec(wvt.shape, const),
                  pl.BlockSpec(wf.shape, const),
                  pl.BlockSpec(wg.shape, const),
                  pl.BlockSpec(bf.shape, const),
                  pl.BlockSpec(hn.shape, const),
                  pl.BlockSpec(bd.shape, const),
                  pl.BlockSpec(tri.shape, const)],
        out_specs=[pl.BlockSpec((5, 1, tm, WIDTH), lambda b, s: (0, b, s, 0)),
                   pl.BlockSpec((1, N_HEADS // 2, LANES, tm), lambda b, s: (b, 0, 0, s)),
                   pl.BlockSpec((1, tm, LANES), lambda b, s: (b, s, 0)),
                   pl.BlockSpec((1, 1, N_HEADS, tm), lambda b, s: (b, s, 0, 0)),
                   pl.BlockSpec((1, tm, D), lambda b, s: (b, s, 0)),
                   pl.BlockSpec((1, tm, D), lambda b, s: (b, s, 0))],
        out_shape=[jax.ShapeDtypeStruct((5, B, S, WIDTH), BF16),
                   jax.ShapeDtypeStruct((B, N_HEADS // 2, LANES, S), BF16),
                   jax.ShapeDtypeStruct((B, S, LANES), BF16),
                   jax.ShapeDtypeStruct((B, S // tm, N_HEADS, tm), F32),
                   jax.ShapeDtypeStruct((B, S, D), BF16),
                   jax.ShapeDtypeStruct((B, S, D), BF16)],
        scratch_shapes=[pltpu.VMEM((N_HEADS, LANES), F32)],
        compiler_params=_cparams(("arbitrary", "arbitrary")),
        name="premix",
    )(x, sh, sc, g, wqkv, wvt, wf, wg, bf, hn, bd, tri)


def _pair_finish(acc0, acc1, lane):
    l0 = pltpu.roll(acc0, HEAD_DIM, axis=1)
    l1 = pltpu.roll(acc1, HEAD_DIM, axis=1)
    return jnp.where(lane < HEAD_DIM, acc0 / l0, acc1 / l1)


def _band_kernel(q_ref, k0_ref, k1_ref, k2_ref, v0_ref, v1_ref, v2_ref, bias_ref, o_ref):
    qi = pl.program_id(2)
    tq = q_ref.shape[0]
    q = q_ref[...]
    lane = lax.broadcasted_iota(jnp.int32, (1, LANES), 1)
    col = lax.broadcasted_iota(jnp.int32, (1, BAND_WIN), 1)
    kvalid = (col + (qi * tq - LOOKBACK_CHUNKS * CHUNK)) >= 0
    ks = (k0_ref[...], k1_ref[...], k2_ref[...])
    vs = (v0_ref[...], v1_ref[...], v2_ref[...])
    accs = []
    for hh in range(2):
        hm = (lane // HEAD_DIM) == hh
        qm = jnp.where(hm, q, jnp.zeros_like(q))
        s = jnp.concatenate([_dot_nt(qm, k) for k in ks], axis=1) + bias_ref[hh]
        s = jnp.where(kvalid, s, NEG_INF)
        m = jnp.max(s, axis=1, keepdims=True)
        p = jnp.exp(s - m).astype(BF16)
        acc = None
        for j in range(3):
            va = jnp.where(hm, vs[j], jnp.ones_like(vs[j]))
            t = _dot(p[:, j * tq:(j + 1) * tq], va)
            acc = t if acc is None else acc + t
        accs.append(acc)
    o_ref[...] = _pair_finish(accs[0], accs[1], lane).astype(o_ref.dtype)


def _band_attention(qkv, bias):
    _, B, S, _ = qkv.shape
    tq = TQ_BAND
    nq = S // tq

    def kv_spec(c, back):
        return pl.BlockSpec((None, None, tq, LANES),
                            lambda b, hp, qi: (c, b, jnp.maximum(qi - back, 0), hp))

    return pl.pallas_call(
        _band_kernel,
        grid=(B, N_HEADS // 2, nq),
        in_specs=[pl.BlockSpec((None, None, tq, LANES), lambda b, hp, qi: (0, b, qi, hp)),
                  kv_spec(1, 2), kv_spec(1, 1), kv_spec(1, 0),
                  kv_spec(2, 2), kv_spec(2, 1), kv_spec(2, 0),
                  pl.BlockSpec((2, tq, BAND_WIN), lambda b, hp, qi: (hp, 0, 0))],
        out_specs=pl.BlockSpec((None, tq, LANES), lambda b, hp, qi: (b, qi, hp)),
        out_shape=jax.ShapeDtypeStruct((B, S, WIDTH), BF16),
        compiler_params=_cparams(("arbitrary", "arbitrary", "arbitrary")),
        name="band_attn",
    )(qkv, qkv, qkv, qkv, qkv, qkv, qkv, bias)


def _band_bias(rel_table):
    pad = LOOKBACK_CHUNKS * CHUNK
    i = jnp.arange(TQ_BAND)[:, None]
    j = jnp.arange(BAND_WIN)[None, :]
    period = 1024
    assert BAND_WIN + TQ_BAND <= period
    m = jnp.arange(period)
    rel = jnp.where(m <= BAND_WIN, pad - m, pad + period - m)
    g = rel_table[:, jnp.clip(rel, -REL_CLIP, REL_CLIP) + REL_CLIP].astype(F32)
    bias = jnp.tile(g, (1, TQ_BAND))[:, :TQ_BAND * (period - 1)].reshape(-1, TQ_BAND, period - 1)[:, :, :BAND_WIN]
    c0 = (i // CHUNK) * CHUNK
    inband = (j >= c0) & (j < c0 + pad + CHUNK)
    return jnp.where(inband[None], bias, NEG_INF)


def _fox_kernel(qi_tab, kj_tab, q_ref, k_ref, vt_ref, dec_ref, cbase_ref, o_ref, acc_ref, m_ref):
    hp = pl.program_id(1)
    t = pl.program_id(2)
    qi = qi_tab[t]
    kj = kj_tab[t]
    tq, tk = q_ref.shape[0], k_ref.shape[0]
    lane2 = lax.broadcasted_iota(jnp.int32, (1, 2 * LANES), 1)
    sub = lax.broadcasted_iota(jnp.int32, (LANES, 1), 0)

    @pl.when(kj == 0)
    def _():
        acc_ref[...] = jnp.zeros_like(acc_ref)
        m_ref[...] = jnp.full_like(m_ref, NEG_INF)

    def step(diagonal):
        ones = jnp.where(lax.broadcasted_iota(jnp.int32, (tq, LANES), 1) < 2 * N_HEADS, 1.0, 0.0).astype(BF16)
        q_aug = jnp.concatenate([q_ref[...], ones], axis=1)
        k_aug = jnp.concatenate([k_ref[...], dec_ref[...]], axis=1)
        vt = vt_ref[...]
        if diagonal:
            causal = (lax.broadcasted_iota(jnp.int32, (tk, tq), 0)
                      <= lax.broadcasted_iota(jnp.int32, (tk, tq), 1))
        scores = []
        for hh in range(2):
            head = 2 * hp + hh
            use = (((lane2 // HEAD_DIM) == hh) | (lane2 == LANES + head) | (lane2 == LANES + N_HEADS + head))
            qm = jnp.where(use, q_aug, jnp.zeros_like(q_aug))
            scores.append(_dot_nt(k_aug, qm))
        m_all = m_ref[...]
        acc_all = acc_ref[...]
        m_out, acc_out = [], []
        for hh in range(2):
            s = scores[hh]
            if diagonal:
                s = jnp.where(causal, s, NEG_INF)
            base = cbase_ref[pl.ds(2 * hp + hh, 1), :]
            m_old = m_all[hh]
            m_new = jnp.maximum(m_old, jnp.max(s, axis=0, keepdims=True) + base)
            alpha = jnp.exp2(m_old - m_new)
            p = jnp.exp2(s - (m_new - base)).astype(BF16)
            va = jnp.where((sub // HEAD_DIM) == hh, vt, jnp.ones_like(vt))
            acc_out.append(acc_all[hh] * alpha + _dot(va, p))
            m_out.append(m_new)
        m_ref[...] = jnp.stack(m_out)
        acc_ref[...] = jnp.stack(acc_out)

    @pl.when(kj < qi)
    def _():
        step(False)

    @pl.when(kj == qi)
    def _():
        step(True)
        a0 = acc_ref[0]
        a1 = acc_ref[1]
        out_t = jnp.where(sub < HEAD_DIM, a0 / a0[HEAD_DIM:HEAD_DIM + 1, :], a1 / a1[0:1, :])
        o_ref[...] = out_t.T.astype(o_ref.dtype)


def _fox_attention(qkv, vt, dec, cbase):
    _, B, S, _ = qkv.shape
    tq = TQ_FOX
    nq = S // tq
    pairs = [(i, j) for i in range(nq) for j in range(i + 1)]
    qi_tab = jnp.array([p[0] for p in pairs], jnp.int32)
    kj_tab = jnp.array([p[1] for p in pairs], jnp.int32)
    grid_spec = pltpu.PrefetchScalarGridSpec(
        num_scalar_prefetch=2,
        grid=(B, N_HEADS // 2, len(pairs)),
        in_specs=[pl.BlockSpec((None, None, tq, LANES), lambda b, hp, t, qt, kt: (3, b, qt[t], hp)),
                  pl.BlockSpec((None, None, tq, LANES), lambda b, hp, t, qt, kt: (4, b, kt[t], hp)),
                  pl.BlockSpec((None, None, LANES, tq), lambda b, hp, t, qt, kt: (b, hp, 0, kt[t])),
                  pl.BlockSpec((None, tq, LANES), lambda b, hp, t, qt, kt: (b, kt[t], 0)),
                  pl.BlockSpec((None, None, N_HEADS, tq), lambda b, hp, t, qt, kt: (b, kt[t], 0, 0))],
        out_specs=pl.BlockSpec((None, tq, LANES), lambda b, hp, t, qt, kt: (b, qt[t], hp)),
        scratch_shapes=[pltpu.VMEM((2, LANES, tq), F32), pltpu.VMEM((2, 1, tq), F32)],
    )
    return pl.pallas_call(
        _fox_kernel,
        grid_spec=grid_spec,
        out_shape=jax.ShapeDtypeStruct((B, S, WIDTH), BF16),
        compiler_params=_cparams(("arbitrary", "arbitrary", "arbitrary")),
        name="fox_attn",
    )(qi_tab, kj_tab, qkv, qkv, vt, dec, cbase)


def _postmix_kernel(ya_ref, yb_ref, ga_ref, gb_ref, x_ref, g1_ref, wpa_ref, wpb_ref, wo_ref,
                    g_ref, sh_ref, sc_ref, x1_ref, h2_ref):
    ua = _dot(ya_ref[0], wpa_ref[...])
    ub = _dot(yb_ref[0], wpb_ref[...])
    m = ga_ref[0].astype(F32) * ua + gb_ref[0].astype(F32) * ub
    mo = _dot(m.astype(BF16), wo_ref[...])
    x1 = x_ref[0] + g1_ref[0] * mo
    x1_ref[0] = x1
    h2_ref[0] = _modnorm(x1, g_ref[...], sc_ref[0], sh_ref[0]).astype(BF16)


def _postmix(ya, yb, ga, gb, x, g1, wpa, wpb, wo, g, sh, sc):
    B, S, D = x.shape
    tm = TM_PRE
    const = lambda b, s: (0, 0)
    tok = lambda w: pl.BlockSpec((1, tm, w), lambda b, s: (b, s, 0))
    row = pl.BlockSpec((1, 1, D), lambda b, s: (b, 0, 0))
    return pl.pallas_call(
        _postmix_kernel,
        grid=(B, S // tm),
        in_specs=[tok(WIDTH), tok(WIDTH), tok(D), tok(D), tok(D), row,
                  pl.BlockSpec(wpa.shape, const), pl.BlockSpec(wpb.shape, const),
                  pl.BlockSpec(wo.shape, const), pl.BlockSpec((1, D), const), row, row],
        out_specs=[tok(D), tok(D)],
        out_shape=[jax.ShapeDtypeStruct((B, S, D), F32), jax.ShapeDtypeStruct((B, S, D), BF16)],
        compiler_params=_cparams(("arbitrary", "arbitrary")),
        name="postmix",
    )(ya, yb, ga, gb, x, g1, wpa, wpb, wo, g, sh, sc)


def _router_kernel(h_ref, wr_ref, rb_ref, ustrict_ref, lstrict_ref, posk_ref, wk_ref, cnt_ref):
    tm = h_ref.shape[0]
    per_group = N_EXPERTS // N_GROUPS
    logits = _dot_nt(wr_ref[...], h_ref[...])
    scores = _sigmoid(logits)
    biased = scores + rb_ref[...]
    sub8 = lax.broadcasted_iota(jnp.int32, (per_group, tm), 0).astype(F32)
    neg = -jnp.inf

    grp_rows = []
    for g in range(N_GROUPS):
        a = biased[g * per_group:(g + 1) * per_group]
        m1 = jnp.max(a, axis=0, keepdims=True)
        i1 = jnp.min(jnp.where(a == m1, sub8, float(per_group)), axis=0, keepdims=True)
        m2 = jnp.max(jnp.where(sub8 == i1, neg, a), axis=0, keepdims=True)
        grp_rows.append(m1 + m2)
    grp = jnp.concatenate(grp_rows, axis=0)

    gsub = lax.broadcasted_iota(jnp.int32, (N_GROUPS, tm), 0).astype(F32)
    gmask = jnp.zeros((N_GROUPS, tm), F32)
    for _ in range(TOPK_GROUPS):
        mx = jnp.max(grp, axis=0, keepdims=True)
        gi = jnp.min(jnp.where(grp == mx, gsub, float(N_GROUPS)), axis=0, keepdims=True)
        sel = gsub == gi
        gmask = jnp.where(sel, 1.0, gmask)
        grp = jnp.where(sel, neg, grp)
    masked = jnp.concatenate(
        [jnp.where(gmask[g:g + 1] > 0.5, biased[g * per_group:(g + 1) * per_group], neg)
         for g in range(N_GROUPS)], axis=0)

    esub = lax.broadcasted_iota(jnp.int32, (N_EXPERTS, tm), 0).astype(F32)
    sels, ws = [], []
    for _ in range(TOP_K):
        mx = jnp.max(masked, axis=0, keepdims=True)
        ei = jnp.min(jnp.where(masked == mx, esub, float(N_EXPERTS)), axis=0, keepdims=True)
        sel = esub == ei
        sels.append(sel)
        ws.append(jnp.sum(jnp.where(sel, scores, 0.0), axis=0, keepdims=True))
        masked = jnp.where(sel, neg, masked)
    wsum = ws[0]
    for w in ws[1:]:
        wsum = wsum + w

    selmask = jnp.zeros((N_EXPERTS, tm), F32)
    for sel in sels:
        selmask = selmask + jnp.where(sel, 1.0, 0.0)
    rank = _dot(selmask.astype(BF16), ustrict_ref[...])
    counts = jnp.sum(selmask, axis=1, keepdims=True)
    padded = jnp.ceil(counts * (1.0 / RUN_ALIGN)) * RUN_ALIGN
    padded_b = jnp.broadcast_to(padded, (N_EXPERTS, LANES))
    base = _dot(lstrict_ref[...], padded_b.astype(BF16))
    pos = base[:, 0:1] + rank

    posk = jnp.concatenate([jnp.sum(jnp.where(sel, pos, 0.0), axis=0, keepdims=True) for sel in sels], axis=0)
    wk = jnp.concatenate([(w / wsum) * ROUTED_SCALE for w in ws], axis=0)
    posk_ref[0] = posk
    wk_ref[0] = wk
    cnt_ref[0] = padded_b


def _router(h2, wr_t, rb, ustrict, lstrict):
    T, D = h2.shape
    tm = TM_MOE
    nt = T // tm
    const = lambda i: (0, 0)
    return pl.pallas_call(
        _router_kernel,
        grid=(nt,),
        in_specs=[pl.BlockSpec((tm, D), lambda i: (i, 0)),
                  pl.BlockSpec(wr_t.shape, const), pl.BlockSpec(rb.shape, const),
                  pl.BlockSpec(ustrict.shape, const), pl.BlockSpec(lstrict.shape, const)],
        out_specs=[pl.BlockSpec((1, TOP_K, tm), lambda i: (i, 0, 0)),
                   pl.BlockSpec((1, TOP_K, tm), lambda i: (i, 0, 0)),
                   pl.BlockSpec((1, N_EXPERTS, LANES), lambda i: (i, 0, 0))],
        out_shape=[jax.ShapeDtypeStruct((nt, TOP_K, tm), F32),
                   jax.ShapeDtypeStruct((nt, TOP_K, tm), F32),
                   jax.ShapeDtypeStruct((nt, N_EXPERTS, LANES), F32)],
        compiler_params=_cparams(("arbitrary",)),
        name="router",
    )(h2, wr_t, rb, ustrict, lstrict)


def _run_copies(m, src_ref, src0, dst_ref, dst0, sem, sizes, wait, src_fixed=False):
    for bit, size in enumerate(sizes):
        @pl.when(((m >> bit) & 1) == 1)
        def _(bit=bit, size=size):
            low = (m & ((1 << bit) - 1)) * RUN_ALIGN
            src_at = src0 if src_fixed else pl.multiple_of(src0 + low, RUN_ALIGN)
            cp = pltpu.make_async_copy(
                src_ref.at[pl.ds(src_at, size)],
                dst_ref.at[pl.ds(pl.multiple_of(dst0 + low, RUN_ALIGN), size)],
                sem)
            if wait:
                cp.wait()
            else:
                cp.start()


def _dispatch_kernel(off_ref, m_ref, base_ref, gapoff_ref, gapm_ref, h_ref, posk_ref, xs_hbm,
                     xs_ref, zero_ref, sem):
    i = pl.program_id(0)
    tm = h_ref.shape[0]
    n_chunks = K_SORT // MXU_DIM

    @pl.when(i == 0)
    def _():
        zero_ref[...] = jnp.zeros_like(zero_ref)
        for wait in (False, True):
            def gap(e, carry, wait=wait):
                _run_copies(gapm_ref[e], zero_ref, 0, xs_hbm, gapoff_ref[e], sem, GAP_SIZES, wait,
                            src_fixed=True)
                return carry
            lax.fori_loop(0, N_EXPERTS, gap, 0)

    slot = i % 2
    xs_slot = xs_ref.at[slot]
    h = h_ref[...]
    posk = posk_ref[0]
    rows = lax.broadcasted_iota(jnp.int32, (MXU_DIM, tm), 0).astype(F32).astype(BF16)
    one = jnp.ones((MXU_DIM, tm), BF16)
    for c in range(n_chunks):
        rel = posk - float(c * MXU_DIM)
        p = jnp.zeros((MXU_DIM, tm), BF16)
        for k in range(TOP_K):
            p = jnp.where(rows == rel[k:k + 1, :].astype(BF16), one, p)
        xs_slot[c * MXU_DIM:(c + 1) * MXU_DIM, :] = _dot(p, h).astype(BF16)

    def runs(tile, buf, wait):
        def run(e, carry):
            idx = tile * N_EXPERTS + e
            _run_copies(m_ref[idx], buf, base_ref[idx], xs_hbm, off_ref[idx], sem, RUN_SIZES, wait)
            return carry
        lax.fori_loop(0, N_EXPERTS, run, 0)

    @pl.when(i > 0)
    def _():
        runs(i - 1, xs_ref.at[1 - slot], True)

    runs(i, xs_slot, False)

    @pl.when(i == pl.num_programs(0) - 1)
    def _():
        runs(i, xs_slot, True)


def _dispatch(off, m16, base, gapoff, gapm, h2, posk, n_rows):
    T, D = h2.shape
    tm = TM_MOE
    grid_spec = pltpu.PrefetchScalarGridSpec(
        num_scalar_prefetch=5,
        grid=(T // tm,),
        in_specs=[pl.BlockSpec((tm, D), lambda i, *_: (i, 0)),
                  pl.BlockSpec((1, TOP_K, tm), lambda i, *_: (i, 0, 0))],
        out_specs=pl.BlockSpec(memory_space=pl.ANY),
        scratch_shapes=[pltpu.VMEM((2, K_SORT, D), BF16),
                        pltpu.VMEM((GAP_SIZES[-1], D), BF16),
                        pltpu.SemaphoreType.DMA],
    )
    return pl.pallas_call(
        _dispatch_kernel,
        grid_spec=grid_spec,
        out_shape=jax.ShapeDtypeStruct((n_rows, D), BF16),
        compiler_params=_cparams(("arbitrary",)),
        name="dispatch",
    )(off, m16, base, gapoff, gapm, h2, posk)


def _expert_kernel(bexp_ref, nvalid_ref, x_ref, wgu_ref, wd_ref, y_ref):
    @pl.when(pl.program_id(0) < nvalid_ref[0])
    def _():
        gu = _dot(x_ref[...], wgu_ref[...])
        g = gu[:, :EXPERT_FF]
        a = (g * _sigmoid(g) * gu[:, EXPERT_FF:]).astype(BF16)
        y_ref[...] = _dot(a, wd_ref[...]).astype(y_ref.dtype)


def _experts(bexp, nvalid, xs, wgu, wd):
    n_rows, D = xs.shape
    nb = n_rows // BLOCK_ROWS

    def blk(i, be, nv):
        return jnp.minimum(i, nv[0] - 1)

    grid_spec = pltpu.PrefetchScalarGridSpec(
        num_scalar_prefetch=2,
        grid=(nb,),
        in_specs=[pl.BlockSpec((BLOCK_ROWS, D), lambda i, be, nv: (blk(i, be, nv), 0)),
                  pl.BlockSpec((None, D, 2 * EXPERT_FF), lambda i, be, nv: (be[blk(i, be, nv)], 0, 0)),
                  pl.BlockSpec((None, EXPERT_FF, D), lambda i, be, nv: (be[blk(i, be, nv)], 0, 0))],
        out_specs=pl.BlockSpec((BLOCK_ROWS, D), lambda i, be, nv: (blk(i, be, nv), 0)),
    )
    return pl.pallas_call(
        _expert_kernel,
        grid_spec=grid_spec,
        out_shape=jax.ShapeDtypeStruct((n_rows, D), BF16),
        compiler_params=_cparams(("arbitrary",)),
        name="experts",
    )(bexp, nvalid, xs, wgu, wd)


def _combine_kernel(off_ref, m_ref, base_ref, ys_hbm, posk_ref, wk_ref, h_ref, x_ref, g2_ref,
                    wsgu_ref, wsd_ref, o_ref, ys_ref, pw_ref, sem):
    i = pl.program_id(0)
    tm = h_ref.shape[0]
    slot = i % 2

    def runs(tile, buf_slot, wait):
        def run(e, carry):
            idx = tile * N_EXPERTS + e
            _run_copies(m_ref[idx], ys_hbm, off_ref[idx], ys_ref.at[buf_slot], base_ref[idx],
                        sem.at[buf_slot], RUN_SIZES, wait)
            return carry
        lax.fori_loop(0, N_EXPERTS, run, 0)

    @pl.when(i == 0)
    def _():
        ys_ref[...] = jnp.zeros_like(ys_ref)
        runs(i, slot, False)

    @pl.when(i + 1 < pl.num_programs(0))
    def _():
        runs(i + 1, 1 - slot, False)

    h = h_ref[...]
    gu = _dot(h, wsgu_ref[...])
    g = gu[:, :EXPERT_FF]
    shared = _dot((g * _sigmoid(g) * gu[:, EXPERT_FF:]).astype(BF16), wsd_ref[...])

    posk = posk_ref[0]
    wk = wk_ref[0]
    rows = lax.broadcasted_iota(jnp.int32, (MXU_DIM, tm), 0).astype(F32).astype(BF16)
    for c in range(K_SORT // MXU_DIM):
        rel = posk - float(c * MXU_DIM)
        pw = jnp.zeros((MXU_DIM, tm), BF16)
        for k in range(TOP_K):
            wrow = jnp.broadcast_to(wk[k:k + 1, :], (MXU_DIM, tm)).astype(BF16)
            pw = jnp.where(rows == rel[k:k + 1, :].astype(BF16), wrow, pw)
        pw_ref[c * MXU_DIM:(c + 1) * MXU_DIM, :] = pw

    runs(i, slot, True)
    routed = lax.dot_general(pw_ref[...], ys_ref[slot], (((0,), (0,)), ((), ())),
                             preferred_element_type=F32)
    o_ref[...] = x_ref[...] + g2_ref[0] * (routed + shared)


def _combine(off, m16, base, ys, posk, wk, h2, x1, g2, wsgu, wsd, tiles_per_batch):
    T, D = h2.shape
    tm = TM_MOE
    const = lambda i, *_: (0, 0)
    tok = lambda w: pl.BlockSpec((tm, w), lambda i, *_: (i, 0))
    sel = pl.BlockSpec((1, TOP_K, tm), lambda i, *_: (i, 0, 0))
    grid_spec = pltpu.PrefetchScalarGridSpec(
        num_scalar_prefetch=3,
        grid=(T // tm,),
        in_specs=[pl.BlockSpec(memory_space=pl.ANY), sel, sel, tok(D), tok(D),
                  pl.BlockSpec((1, 1, D), lambda i, *_: (i // tiles_per_batch, 0, 0)),
                  pl.BlockSpec(wsgu.shape, const), pl.BlockSpec(wsd.shape, const)],
        out_specs=tok(D),
        scratch_shapes=[pltpu.VMEM((2, K_SORT, D), BF16),
                        pltpu.VMEM((K_SORT, tm), BF16),
                        pltpu.SemaphoreType.DMA((2,))],
    )
    return pl.pallas_call(
        _combine_kernel,
        grid_spec=grid_spec,
        out_shape=jax.ShapeDtypeStruct((T, D), F32),
        compiler_params=_cparams(("arbitrary",)),
        name="combine",
    )(off, m16, base, ys, posk, wk, h2, x1, g2, wsgu, wsd)


def _moe(h2, x1, g2, wr_t, rb, wgu, wd, wsgu, wsd, ustrict, lstrict, tiles_per_batch):
    T, D = h2.shape
    nt = T // TM_MOE
    posk, wk, cnt = _router(h2, wr_t, rb, ustrict, lstrict)

    pad = cnt[:, :, 0].astype(jnp.int32)
    base = jnp.cumsum(pad, axis=1) - pad
    tile_off = jnp.cumsum(pad, axis=0) - pad
    total = jnp.sum(pad, axis=0)
    region = ((total + BLOCK_ROWS - 1) // BLOCK_ROWS) * BLOCK_ROWS
    rend = jnp.cumsum(region)
    rstart = rend - region
    off = rstart[None, :] + tile_off
    nb_max = -(-(TOP_K * T + nt * N_EXPERTS * (RUN_ALIGN - 1)) // BLOCK_ROWS) + N_EXPERTS
    blk_row = jnp.arange(nb_max, dtype=jnp.int32) * BLOCK_ROWS
    bexp = jnp.minimum(jnp.sum((rend[None, :] <= blk_row[:, None]).astype(jnp.int32), axis=1), N_EXPERTS - 1)
    nvalid = (rend[-1:] // BLOCK_ROWS).astype(jnp.int32)
    flat = lambda a: a.reshape(-1).astype(jnp.int32)

    xs = _dispatch(flat(off), flat(pad // RUN_ALIGN), flat(base), flat(rstart + total),
                   flat((region - total) // RUN_ALIGN), h2, posk, nb_max * BLOCK_ROWS)
    ys = _experts(bexp, nvalid, xs, wgu, wd)
    return _combine(flat(off), flat(pad // RUN_ALIGN), flat(base), ys, posk, wk, h2, x1, g2, wsgu, wsd,
                    tiles_per_batch)


def kernel(x, c, ada_w, ada_b, mix_norm_g, w_in, b_fgate, qn_a, kn_a, qn_b, kn_b, rel_bias, w_proj_a,
           w_proj_b, w_out, ffn_norm_g, w_router, router_bias, w_gate_e, w_up_e, w_down_e, w_gate_s,
           w_up_s, w_down_s):
    B, S, D = x.shape
    L = ada_w.shape[0]
    T = B * S
    assert D == D_MODEL and S % TM_PRE == 0 and TQ_FOX == TM_PRE and S % TM_MOE == 0 and S % TQ_BAND == 0

    mod = _adaln(c, ada_w, ada_b).reshape(L, B, 6, 1, D)

    hid = jnp.arange(MXU_DIM) // HEAD_DIM
    bd = jnp.where(hid[:, None] == hid[None, :], 1.0 / HEAD_DIM, 0.0).astype(BF16)
    r = jnp.arange(TM_PRE)
    tri = (r[:, None] <= r[None, :]).astype(BF16)
    r = jnp.arange(TM_MOE)
    ustrict = (r[:, None] < r[None, :]).astype(BF16)
    r = jnp.arange(N_EXPERTS)
    lstrict = (r[None, :] < r[:, None]).astype(BF16)

    q_scale = 1.0 / math.sqrt(HEAD_DIM)
    for l in range(L):
        sh1, sc1, g1, sh2, sc2, g2 = (mod[l, :, j] for j in range(6))
        w = w_in[l]
        wqkv = w[:, :5 * WIDTH].astype(BF16)
        wvt = w[:, 5 * WIDTH:6 * WIDTH].T.astype(BF16)
        wf = jnp.zeros((2 * N_HEADS, D), BF16).at[:N_HEADS].set(w[:, 6 * WIDTH:6 * WIDTH + N_HEADS].T.astype(BF16))
        wg = w[:, 6 * WIDTH + N_HEADS:].astype(BF16)
        hn = jnp.zeros((8, WIDTH), F32)
        hn = hn.at[0].set(jnp.tile(qn_a[l], N_HEADS) * q_scale).at[1].set(jnp.tile(kn_a[l], N_HEADS))
        hn = hn.at[2].set(jnp.tile(qn_b[l], N_HEADS) * (q_scale * LOG2E)).at[3].set(jnp.tile(kn_b[l], N_HEADS))

        qkv, vt, dec, cbase, ga, gb = _premix(x, sh1, sc1, mix_norm_g[l][None], wqkv, wvt, wf, wg,
                                              b_fgate[l][:, None], hn, bd, tri)
        ya = _band_attention(qkv, _band_bias(rel_bias[l]))
        yb = _fox_attention(qkv, vt, dec, cbase)
        x1, h2 = _postmix(ya, yb, ga, gb, x, g1, w_proj_a[l].astype(BF16), w_proj_b[l].astype(BF16),
                          w_out[l].astype(BF16), ffn_norm_g[l][None], sh2, sc2)

        wgu = jnp.concatenate([w_gate_e[l], w_up_e[l]], axis=-1).astype(BF16)
        wsgu = jnp.concatenate([w_gate_s[l], w_up_s[l]], axis=-1).astype(BF16)
        x = _moe(h2.reshape(T, D), x1.reshape(T, D), g2, w_router[l].T.astype(BF16),
                 router_bias[l][:, None], wgu, w_down_e[l].astype(BF16), wsgu,
                 w_down_s[l].astype(BF16), ustrict, lstrict, S // TM_MOE).reshape(B, S, D)
    return x
```

```python
import functools
import math

import jax
import jax.numpy as jnp
from jax import lax
from jax.experimental import pallas as pl
from jax.experimental.pallas import tpu as pltpu

F32 = jnp.float32
BF16 = jnp.bfloat16

D_MODEL = 1024
HEAD_DIM = 64
N_HEADS = 8
WIDTH = N_HEADS * HEAD_DIM
CHUNK = 64
LOOKBACK_CHUNKS = 8
REL_CLIP = 128
N_EXPERTS = 64
TOP_K = 8
N_GROUPS = 8
TOPK_GROUPS = 4
EXPERT_FF = 256
ROUTED_SCALE = 2.5
EPS = 1e-6
NEG_INF = -1e30
LOG2E = math.log2(math.e)

LANES = 128
MXU_DIM = 256
VMEM_LIMIT = 56 * 1024 * 1024

TM_PRE = 512
TQ_BAND = 256
BAND_WIN = TQ_BAND + LOOKBACK_CHUNKS * CHUNK
TQ_FOX = 512
PAIRS_PER_STEP = 4
TM_MOE = 256
RUN_ALIGN = 16
K_SORT = ((TOP_K * TM_MOE + N_EXPERTS * (RUN_ALIGN - 1) + MXU_DIM - 1) // MXU_DIM) * MXU_DIM
BLOCK_ROWS = 1024


def _dot(a, b):
    return jnp.dot(a, b, preferred_element_type=F32)


def _dot_nt(a, b):
    return lax.dot_general(a, b, (((1,), (1,)), ((), ())), preferred_element_type=F32)


def _sigmoid(v):
    return 1.0 / (1.0 + jnp.exp(-v))


def _cparams(sem, unchecked_dma=False):
    return pltpu.CompilerParams(dimension_semantics=sem, vmem_limit_bytes=VMEM_LIMIT,
                                disable_bounds_checks=unchecked_dma)


def _adaln_kernel(c_ref, w_ref, b_ref, o_ref):
    c = c_ref[...]
    ca = c * _sigmoid(c)
    o_ref[0] = jnp.dot(ca, w_ref[0], preferred_element_type=F32,
                       precision=lax.Precision.HIGHEST) + b_ref[0]


def _adaln(c, ada_w, ada_b):
    L, D, N = ada_w.shape
    B = c.shape[0]
    tn = 1536
    return pl.pallas_call(
        _adaln_kernel,
        grid=(L, N // tn),
        in_specs=[pl.BlockSpec((B, D), lambda l, j: (0, 0)),
                  pl.BlockSpec((1, D, tn), lambda l, j: (l, 0, j)),
                  pl.BlockSpec((1, 1, tn), lambda l, j: (l, 0, j))],
        out_specs=pl.BlockSpec((1, B, tn), lambda l, j: (l, 0, j)),
        out_shape=jax.ShapeDtypeStruct((L, B, N), F32),
        compiler_params=_cparams(("arbitrary", "arbitrary")),
        name="adaln",
    )(c, ada_w, ada_b.reshape(L, 1, N))


def _modnorm(x, g, sc, sh):
    ms = jnp.mean(x * x, axis=-1, keepdims=True)
    y = x * lax.rsqrt(ms + EPS) * g
    return y * (1.0 + sc) + sh


def _premix_kernel(x_ref, sh_ref, sc_ref, g_ref, wqk_ref, wvt_ref, wf_ref, wg_ref, bf_ref, hn_ref, bd_ref,
                   tri_ref, qk_ref, vt_ref, dec_ref, cbase_ref, ga_ref, gb_ref, carry_ref):
    si = pl.program_id(1)
    tm = x_ref.shape[1]
    h = _modnorm(x_ref[0], g_ref[...], sc_ref[0], sh_ref[0]).astype(BF16)

    for c in range(4):
        z = _dot(h, wqk_ref[:, c * WIDTH:(c + 1) * WIDTH])
        sq = (z * z).astype(BF16)
        ms = jnp.concatenate(
            [_dot(sq[:, j * MXU_DIM:(j + 1) * MXU_DIM], bd_ref[...]) for j in range(WIDTH // MXU_DIM)],
            axis=1)
        z = z * lax.rsqrt(ms + EPS) * hn_ref[c:c + 1, :]
        qk_ref[c, 0] = z.astype(BF16)

    for c in range(2):
        vt = _dot_nt(wvt_ref[c * WIDTH:(c + 1) * WIDTH, :], h).astype(BF16)
        vt_ref[0, c * (N_HEADS // 2):(c + 1) * (N_HEADS // 2)] = vt.reshape(N_HEADS // 2, LANES, tm)

    fr = _dot_nt(wf_ref[...], h)[:N_HEADS]
    xg = fr + bf_ref[...]
    logf = jnp.minimum(xg, 0.0) - jnp.log(1.0 + jnp.exp(-jnp.abs(xg)))
    hi = logf.astype(BF16).astype(F32)
    r1 = logf - hi
    mid = r1.astype(BF16).astype(F32)
    lo = r1 - mid
    parts = jnp.concatenate([hi, mid, lo, jnp.zeros_like(hi)], axis=0)
    cs3 = _dot(parts.astype(BF16), tri_ref[...])
    cs = cs3[0:8] + cs3[8:16] + cs3[16:24]

    @pl.when(si == 0)
    def _():
        carry_ref[...] = jnp.zeros_like(carry_ref)

    drel = cs * (-LOG2E)
    dhi = drel.astype(BF16).astype(F32)
    dlo = drel - dhi
    dec = jnp.concatenate([dhi, dlo, jnp.zeros((LANES - 2 * N_HEADS, tm), F32)], axis=0)
    dec_ref[0] = dec.T.astype(BF16)
    carry = carry_ref[:, 0:1]
    cbase_ref[0, 0] = jnp.broadcast_to(carry * (-LOG2E), (N_HEADS, tm))
    carry_ref[...] = jnp.broadcast_to(carry + cs[:, tm - 1:tm], carry_ref.shape)

    for c in range(4):
        zg = _dot(h, wg_ref[:, c * WIDTH:(c + 1) * WIDTH])
        sg = _sigmoid(zg).astype(BF16)
        if c < 2:
            ga_ref[0, :, c * WIDTH:(c + 1) * WIDTH] = sg
        else:
            gb_ref[0, :, (c - 2) * WIDTH:(c - 1) * WIDTH] = sg


def _premix(x, sh, sc, g, wqk, wvt, wf, wg, bf, hn, bd, tri):
    B, S, D = x.shape
    tm = TM_PRE
    const = lambda b, s: (0, 0)
    return pl.pallas_call(
        _premix_kernel,
        grid=(B, S // tm),
        in_specs=[pl.BlockSpec((1, tm, D), lambda b, s: (b, s, 0)),
                  pl.BlockSpec((1, 1, D), lambda b, s: (b, 0, 0)),
                  pl.BlockSpec((1, 1, D), lambda b, s: (b, 0, 0)),
                  pl.BlockSpec((1, D), const),
                  pl.BlockSpec(wqk.shape, const),
                  pl.BlockSpec(wvt.shape, const),
                  pl.BlockSpec(wf.shape, const),
                  pl.BlockSpec(wg.shape, const),
                  pl.BlockSpec(bf.shape, const),
                  pl.BlockSpec(hn.shape, const),
                  pl.BlockSpec(bd.shape, const),
                  pl.BlockSpec(tri.shape, const)],
        out_specs=[pl.BlockSpec((4, 1, tm, WIDTH), lambda b, s: (0, b, s, 0)),
                   pl.BlockSpec((1, N_HEADS, LANES, tm), lambda b, s: (b, 0, 0, s)),
                   pl.BlockSpec((1, tm, LANES), lambda b, s: (b, s, 0)),
                   pl.BlockSpec((1, 1, N_HEADS, tm), lambda b, s: (b, s, 0, 0)),
                   pl.BlockSpec((1, tm, D), lambda b, s: (b, s, 0)),
                   pl.BlockSpec((1, tm, D), lambda b, s: (b, s, 0))],
        out_shape=[jax.ShapeDtypeStruct((4, B, S, WIDTH), BF16),
                   jax.ShapeDtypeStruct((B, N_HEADS, LANES, S), BF16),
                   jax.ShapeDtypeStruct((B, S, LANES), BF16),
                   jax.ShapeDtypeStruct((B, S // tm, N_HEADS, tm), F32),
                   jax.ShapeDtypeStruct((B, S, D), BF16),
                   jax.ShapeDtypeStruct((B, S, D), BF16)],
        scratch_shapes=[pltpu.VMEM((N_HEADS, LANES), F32)],
        compiler_params=_cparams(("arbitrary", "arbitrary")),
        name="premix",
    )(x, sh, sc, g, wqk, wvt, wf, wg, bf, hn, bd, tri)


def _pair_finish_t(acc0, acc1, sub):
    out_t = jnp.where(sub < HEAD_DIM, acc0 / acc0[HEAD_DIM:HEAD_DIM + 1, :], acc1 / acc1[0:1, :])
    return out_t.T


def _band_kernel(q_ref, k0_ref, k1_ref, k2_ref, v0_ref, v1_ref, v2_ref, bias_ref, o_ref):
    qi = pl.program_id(2)
    tq = q_ref.shape[0]
    lane = lax.broadcasted_iota(jnp.int32, (1, LANES), 1)
    sub = lax.broadcasted_iota(jnp.int32, (LANES, 1), 0)
    krow = lax.broadcasted_iota(jnp.int32, (BAND_WIN, 1), 0)
    kvalid = (krow + (qi * tq - LOOKBACK_CHUNKS * CHUNK)) >= 0
    k_refs = (k0_ref, k1_ref, k2_ref)
    v_refs = (v0_ref, v1_ref, v2_ref)
    heads = [(pp, hh) for pp in range(PAIRS_PER_STEP) for hh in range(2)]
    scores = []
    for pp, hh in heads:
        q = q_ref[:, pp * LANES:(pp + 1) * LANES]
        qm = jnp.where((lane // HEAD_DIM) == hh, q, jnp.zeros_like(q))
        scores.append(jnp.concatenate(
            [_dot_nt(k[:, pp * LANES:(pp + 1) * LANES], qm) for k in k_refs], axis=0))
    accs = []
    for n, (pp, hh) in enumerate(heads):
        s = jnp.where(kvalid, scores[n] + bias_ref[n], NEG_INF)
        m = jnp.max(s, axis=0, keepdims=True)
        p = jnp.exp2(s - m).astype(BF16)
        acc = None
        for j in range(3):
            vt = v_refs[j][pp]
            va = jnp.where((sub // HEAD_DIM) == hh, vt, jnp.ones_like(vt))
            t = _dot(va, p[j * tq:(j + 1) * tq, :])
            acc = t if acc is None else acc + t
        accs.append(acc)
    for pp in range(PAIRS_PER_STEP):
        o_ref[:, pp * LANES:(pp + 1) * LANES] = _pair_finish_t(
            accs[2 * pp], accs[2 * pp + 1], sub).astype(o_ref.dtype)


def _band_attention(qk, vt, bias_t):
    _, B, S, _ = qk.shape
    tq = TQ_BAND
    nq = S // tq

    groups = N_HEADS // 2 // PAIRS_PER_STEP
    gw = PAIRS_PER_STEP * LANES

    def k_spec(back):
        return pl.BlockSpec((None, None, tq, gw),
                            lambda b, g, qi: (1, b, jnp.maximum(qi - back, 0), g))

    def v_spec(back):
        return pl.BlockSpec((None, PAIRS_PER_STEP, LANES, tq),
                            lambda b, g, qi: (b, g, 0, jnp.maximum(qi - back, 0)))

    return pl.pallas_call(
        _band_kernel,
        grid=(B, groups, nq),
        in_specs=[pl.BlockSpec((None, None, tq, gw), lambda b, g, qi: (0, b, qi, g)),
                  k_spec(2), k_spec(1), k_spec(0), v_spec(2), v_spec(1), v_spec(0),
                  pl.BlockSpec((2 * PAIRS_PER_STEP, BAND_WIN, tq), lambda b, g, qi: (g, 0, 0))],
        out_specs=pl.BlockSpec((None, tq, gw), lambda b, g, qi: (b, qi, g)),
        out_shape=jax.ShapeDtypeStruct((B, S, WIDTH), BF16),
        compiler_params=_cparams(("arbitrary", "arbitrary", "arbitrary")),
        name="band_attn",
    )(qk, qk, qk, qk, vt, vt, vt, bias_t)


def _band_bias(rel_table):
    pad = LOOKBACK_CHUNKS * CHUNK
    i = jnp.arange(TQ_BAND)[:, None]
    j = jnp.arange(BAND_WIN)[None, :]
    period = 1024
    assert BAND_WIN + TQ_BAND <= period
    m = jnp.arange(period)
    rel = jnp.where(m <= BAND_WIN, pad - m, pad + period - m)
    g = rel_table[:, jnp.clip(rel, -REL_CLIP, REL_CLIP) + REL_CLIP].astype(F32)
    bias = jnp.tile(g, (1, TQ_BAND))[:, :TQ_BAND * (period - 1)].reshape(-1, TQ_BAND, period - 1)[:, :, :BAND_WIN]
    c0 = (i // CHUNK) * CHUNK
    inband = (j >= c0) & (j < c0 + pad + CHUNK)
    return jnp.where(inband[None], bias * LOG2E, NEG_INF).transpose(0, 2, 1)


def _fox_kernel(qi_tab, kj_tab, q_ref, k_ref, vt_ref, dec_ref, cbase_ref, o_ref, acc_ref, m_ref):
    grp = pl.program_id(1)
    t = pl.program_id(2)
    qi = qi_tab[t]
    kj = kj_tab[t]
    tq, tk = q_ref.shape[0], k_ref.shape[0]
    lane2 = lax.broadcasted_iota(jnp.int32, (1, 2 * LANES), 1)
    sub = lax.broadcasted_iota(jnp.int32, (LANES, 1), 0)
    heads = [(pp, hh) for pp in range(PAIRS_PER_STEP) for hh in range(2)]

    @pl.when(kj == 0)
    def _():
        acc_ref[...] = jnp.zeros_like(acc_ref)
        m_ref[...] = jnp.full_like(m_ref, NEG_INF)

    def step(diagonal):
        ones = jnp.where(lax.broadcasted_iota(jnp.int32, (tq, LANES), 1) < 2 * N_HEADS, 1.0, 0.0).astype(BF16)
        dec = dec_ref[...]
        if diagonal:
            causal = (lax.broadcasted_iota(jnp.int32, (tk, tq), 0)
                      <= lax.broadcasted_iota(jnp.int32, (tk, tq), 1))
        scores = []
        for pp, hh in heads:
            head = 2 * (PAIRS_PER_STEP * grp + pp) + hh
            q_aug = jnp.concatenate([q_ref[:, pp * LANES:(pp + 1) * LANES], ones], axis=1)
            k_aug = jnp.concatenate([k_ref[:, pp * LANES:(pp + 1) * LANES], dec], axis=1)
            use = (((lane2 // HEAD_DIM) == hh) | (lane2 == LANES + head) | (lane2 == LANES + N_HEADS + head))
            qm = jnp.where(use, q_aug, jnp.zeros_like(q_aug))
            scores.append(_dot_nt(k_aug, qm))
        for n, (pp, hh) in enumerate(heads):
            head = 2 * (PAIRS_PER_STEP * grp + pp) + hh
            s = scores[n]
            if diagonal:
                s = jnp.where(causal, s, NEG_INF)
            base = cbase_ref[pl.ds(head, 1), :]
            m_old = m_ref[n]
            m_new = jnp.maximum(m_old, jnp.max(s, axis=0, keepdims=True) + base)
            alpha = jnp.exp2(m_old - m_new)
            p = jnp.exp2(s - (m_new - base)).astype(BF16)
            vt = vt_ref[pp]
            va = jnp.where((sub // HEAD_DIM) == hh, vt, jnp.ones_like(vt))
            acc_ref[n] = acc_ref[n] * alpha + _dot(va, p)
            m_ref[n] = m_new

    @pl.when(kj < qi)
    def _():
        step(False)

    @pl.when(kj == qi)
    def _():
        step(True)
        for pp in range(PAIRS_PER_STEP):
            o_ref[:, pp * LANES:(pp + 1) * LANES] = _pair_finish_t(
                acc_ref[2 * pp], acc_ref[2 * pp + 1], sub).astype(o_ref.dtype)


def _fox_attention(qkv, vt, dec, cbase):
    _, B, S, _ = qkv.shape
    tq = TQ_FOX
    nq = S // tq
    groups = N_HEADS // 2 // PAIRS_PER_STEP
    gw = PAIRS_PER_STEP * LANES
    pairs = [(i, j) for i in range(nq) for j in range(i + 1)]
    qi_tab = jnp.array([p[0] for p in pairs], jnp.int32)
    kj_tab = jnp.array([p[1] for p in pairs], jnp.int32)
    grid_spec = pltpu.PrefetchScalarGridSpec(
        num_scalar_prefetch=2,
        grid=(B, groups, len(pairs)),
        in_specs=[pl.BlockSpec((None, None, tq, gw), lambda b, g, t, qt, kt: (2, b, qt[t], g)),
                  pl.BlockSpec((None, None, tq, gw), lambda b, g, t, qt, kt: (3, b, kt[t], g)),
                  pl.BlockSpec((None, PAIRS_PER_STEP, LANES, tq),
                               lambda b, g, t, qt, kt: (b, groups + g, 0, kt[t])),
                  pl.BlockSpec((None, tq, LANES), lambda b, g, t, qt, kt: (b, kt[t], 0)),
                  pl.BlockSpec((None, None, N_HEADS, tq), lambda b, g, t, qt, kt: (b, kt[t], 0, 0))],
        out_specs=pl.BlockSpec((None, tq, gw), lambda b, g, t, qt, kt: (b, qt[t], g)),
        scratch_shapes=[pltpu.VMEM((2 * PAIRS_PER_STEP, LANES, tq), F32),
                        pltpu.VMEM((2 * PAIRS_PER_STEP, 1, tq), F32)],
    )
    return pl.pallas_call(
        _fox_kernel,
        grid_spec=grid_spec,
        out_shape=jax.ShapeDtypeStruct((B, S, WIDTH), BF16),
        compiler_params=_cparams(("arbitrary", "arbitrary", "arbitrary")),
        name="fox_attn",
    )(qi_tab, kj_tab, qkv, qkv, vt, dec, cbase)


def _postmix_kernel(ya_ref, yb_ref, ga_ref, gb_ref, x_ref, g1_ref, wpa_ref, wpb_ref, wo_ref,
                    g_ref, sh_ref, sc_ref, x1_ref, h2_ref):
    ua = _dot(ya_ref[0], wpa_ref[...])
    ub = _dot(yb_ref[0], wpb_ref[...])
    m = ga_ref[0].astype(F32) * ua + gb_ref[0].astype(F32) * ub
    mo = _dot(m.astype(BF16), wo_ref[...])
    x1 = x_ref[0] + g1_ref[0] * mo
    x1_ref[0] = x1
    h2_ref[0] = _modnorm(x1, g_ref[...], sc_ref[0], sh_ref[0]).astype(BF16)


def _postmix(ya, yb, ga, gb, x, g1, wpa, wpb, wo, g, sh, sc):
    B, S, D = x.shape
    tm = TM_PRE
    const = lambda b, s: (0, 0)
    tok = lambda w: pl.BlockSpec((1, tm, w), lambda b, s: (b, s, 0))
    row = pl.BlockSpec((1, 1, D), lambda b, s: (b, 0, 0))
    return pl.pallas_call(
        _postmix_kernel,
        grid=(B, S // tm),
        in_specs=[tok(WIDTH), tok(WIDTH), tok(D), tok(D), tok(D), row,
                  pl.BlockSpec(wpa.shape, const), pl.BlockSpec(wpb.shape, const),
                  pl.BlockSpec(wo.shape, const), pl.BlockSpec((1, D), const), row, row],
        out_specs=[tok(D), tok(D)],
        out_shape=[jax.ShapeDtypeStruct((B, S, D), F32), jax.ShapeDtypeStruct((B, S, D), BF16)],
        compiler_params=_cparams(("arbitrary", "arbitrary")),
        name="postmix",
    )(ya, yb, ga, gb, x, g1, wpa, wpb, wo, g, sh, sc)


def _router_kernel(h_ref, wr_ref, rb_ref, ustrict_ref, lstrict_ref, posk_ref, wk_ref, cnt_ref):
    tm = h_ref.shape[0]
    per_group = N_EXPERTS // N_GROUPS
    logits = _dot_nt(wr_ref[...], h_ref[...])
    scores = _sigmoid(logits)
    biased = scores + rb_ref[...]
    sub8 = lax.broadcasted_iota(jnp.int32, (per_group, tm), 0).astype(F32)
    neg = -jnp.inf

    grp_rows = []
    for g in range(N_GROUPS):
        a = biased[g * per_group:(g + 1) * per_group]
        m1 = jnp.max(a, axis=0, keepdims=True)
        i1 = jnp.min(jnp.where(a == m1, sub8, float(per_group)), axis=0, keepdims=True)
        m2 = jnp.max(jnp.where(sub8 == i1, neg, a), axis=0, keepdims=True)
        grp_rows.append(m1 + m2)
    grp = jnp.concatenate(grp_rows, axis=0)

    gsub = lax.broadcasted_iota(jnp.int32, (N_GROUPS, tm), 0).astype(F32)
    gmask = jnp.zeros((N_GROUPS, tm), F32)
    for _ in range(TOPK_GROUPS):
        mx = jnp.max(grp, axis=0, keepdims=True)
        gi = jnp.min(jnp.where(grp == mx, gsub, float(N_GROUPS)), axis=0, keepdims=True)
        sel = gsub == gi
        gmask = jnp.where(sel, 1.0, gmask)
        grp = jnp.where(sel, neg, grp)
    masked = jnp.concatenate(
        [jnp.where(gmask[g:g + 1] > 0.5, biased[g * per_group:(g + 1) * per_group], neg)
         for g in range(N_GROUPS)], axis=0)

    esub = lax.broadcasted_iota(jnp.int32, (N_EXPERTS, tm), 0).astype(F32)
    sels, ws = [], []
    for _ in range(TOP_K):
        mx = jnp.max(masked, axis=0, keepdims=True)
        ei = jnp.min(jnp.where(masked == mx, esub, float(N_EXPERTS)), axis=0, keepdims=True)
        sel = esub == ei
        sels.append(sel)
        ws.append(jnp.sum(jnp.where(sel, scores, 0.0), axis=0, keepdims=True))
        masked = jnp.where(sel, neg, masked)
    wsum = ws[0]
    for w in ws[1:]:
        wsum = wsum + w

    selmask = jnp.zeros((N_EXPERTS, tm), F32)
    for sel in sels:
        selmask = selmask + jnp.where(sel, 1.0, 0.0)
    rank = _dot(selmask.astype(BF16), ustrict_ref[...])
    counts = jnp.sum(selmask, axis=1, keepdims=True)
    padded = jnp.ceil(counts * (1.0 / RUN_ALIGN)) * RUN_ALIGN
    padded_b = jnp.broadcast_to(padded, (N_EXPERTS, LANES))
    base = _dot(lstrict_ref[...], padded_b.astype(BF16))
    pos = base[:, 0:1] + rank

    posk = jnp.concatenate([jnp.sum(jnp.where(sel, pos, 0.0), axis=0, keepdims=True) for sel in sels], axis=0)
    wk = jnp.concatenate([(w / wsum) * ROUTED_SCALE for w in ws], axis=0)
    posk_ref[0] = posk
    wk_ref[0] = wk
    cnt_ref[0] = padded_b


def _router(h2, wr_t, rb, ustrict, lstrict):
    T, D = h2.shape
    tm = TM_MOE
    nt = T // tm
    const = lambda i: (0, 0)
    return pl.pallas_call(
        _router_kernel,
        grid=(nt,),
        in_specs=[pl.BlockSpec((tm, D), lambda i: (i, 0)),
                  pl.BlockSpec(wr_t.shape, const), pl.BlockSpec(rb.shape, const),
                  pl.BlockSpec(ustrict.shape, const), pl.BlockSpec(lstrict.shape, const)],
        out_specs=[pl.BlockSpec((1, TOP_K, tm), lambda i: (i, 0, 0)),
                   pl.BlockSpec((1, TOP_K, tm), lambda i: (i, 0, 0)),
                   pl.BlockSpec((1, N_EXPERTS, LANES), lambda i: (i, 0, 0))],
        out_shape=[jax.ShapeDtypeStruct((nt, TOP_K, tm), F32),
                   jax.ShapeDtypeStruct((nt, TOP_K, tm), F32),
                   jax.ShapeDtypeStruct((nt, N_EXPERTS, LANES), F32)],
        compiler_params=_cparams(("arbitrary",)),
        name="router",
    )(h2, wr_t, rb, ustrict, lstrict)


def _rows_copy(m, src_ref, src0, dst_ref, dst0, sem):
    rows = m * RUN_ALIGN
    return pltpu.make_async_copy(src_ref.at[pl.ds(pl.multiple_of(src0, RUN_ALIGN), rows)],
                                 dst_ref.at[pl.ds(pl.multiple_of(dst0, RUN_ALIGN), rows)], sem)


def _start_rows(m, src_ref, src0, dst_ref, dst0, sem):
    @pl.when(m > 0)
    def _():
        _rows_copy(m, src_ref, src0, dst_ref, dst0, sem).start()


def _wait_rows(m, src_ref, dst_ref, sem):
    @pl.when(m > 0)
    def _():
        _rows_copy(m, src_ref, 0, dst_ref, 0, sem).wait()


def _dispatch_kernel(off_ref, m_ref, base_ref, tot_ref, gapoff_ref, gapm_ref, h_ref, posk_ref, xs_hbm,
                     xs_ref, zero_ref, sem):
    i = pl.program_id(0)
    tm = h_ref.shape[0]
    n_chunks = K_SORT // MXU_DIM

    @pl.when(i == 0)
    def _():
        zero_ref[...] = jnp.zeros_like(zero_ref)

        def gap(e, carry):
            _start_rows(gapm_ref[e], zero_ref, 0, xs_hbm, gapoff_ref[e], sem)
            return carry
        lax.fori_loop(0, N_EXPERTS, gap, 0)

        def gap_wait(e, carry):
            _wait_rows(gapm_ref[e], zero_ref, xs_hbm, sem)
            return carry
        lax.fori_loop(0, N_EXPERTS, gap_wait, 0)

    slot = i % 2
    xs_slot = xs_ref.at[slot]
    h = h_ref[...]
    posk = posk_ref[0]
    rows = lax.broadcasted_iota(jnp.int32, (MXU_DIM, tm), 0).astype(F32).astype(BF16)
    one = jnp.ones((MXU_DIM, tm), BF16)
    for c in range(n_chunks):
        rel = posk - float(c * MXU_DIM)
        p = jnp.zeros((MXU_DIM, tm), BF16)
        for k in range(TOP_K):
            p = jnp.where(rows == rel[k:k + 1, :].astype(BF16), one, p)
        xs_slot[c * MXU_DIM:(c + 1) * MXU_DIM, :] = _dot(p, h).astype(BF16)

    @pl.when(i > 0)
    def _():
        _wait_rows(tot_ref[i - 1], xs_ref.at[1 - slot], xs_hbm, sem)

    def run(e, carry):
        idx = i * N_EXPERTS + e
        _start_rows(m_ref[idx], xs_slot, base_ref[idx], xs_hbm, off_ref[idx], sem)
        return carry
    lax.fori_loop(0, N_EXPERTS, run, 0)

    @pl.when(i == pl.num_programs(0) - 1)
    def _():
        _wait_rows(tot_ref[i], xs_slot, xs_hbm, sem)


def _dispatch(off, m16, base, tot, gapoff, gapm, h2, posk, n_rows):
    T, D = h2.shape
    tm = TM_MOE
    grid_spec = pltpu.PrefetchScalarGridSpec(
        num_scalar_prefetch=6,
        grid=(T // tm,),
        in_specs=[pl.BlockSpec((tm, D), lambda i, *_: (i, 0)),
                  pl.BlockSpec((1, TOP_K, tm), lambda i, *_: (i, 0, 0))],
        out_specs=pl.BlockSpec(memory_space=pl.ANY),
        scratch_shapes=[pltpu.VMEM((2, K_SORT, D), BF16),
                        pltpu.VMEM((BLOCK_ROWS, D), BF16),
                        pltpu.SemaphoreType.DMA],
    )
    return pl.pallas_call(
        _dispatch_kernel,
        grid_spec=grid_spec,
        out_shape=jax.ShapeDtypeStruct((n_rows, D), BF16),
        compiler_params=_cparams(("arbitrary",), unchecked_dma=True),
        name="dispatch",
    )(off, m16, base, tot, gapoff, gapm, h2, posk)


def _expert_kernel(bexp_ref, nvalid_ref, x_ref, wgu_ref, wd_ref, y_ref):
    @pl.when(pl.program_id(0) < nvalid_ref[0])
    def _():
        gu = _dot(x_ref[...], wgu_ref[...])
        g = gu[:, :EXPERT_FF]
        a = (g * _sigmoid(g) * gu[:, EXPERT_FF:]).astype(BF16)
        y_ref[...] = _dot(a, wd_ref[...]).astype(y_ref.dtype)


def _experts(bexp, nvalid, xs, wgu, wd):
    n_rows, D = xs.shape
    nb = n_rows // BLOCK_ROWS

    def blk(i, be, nv):
        return jnp.minimum(i, nv[0] - 1)

    grid_spec = pltpu.PrefetchScalarGridSpec(
        num_scalar_prefetch=2,
        grid=(nb,),
        in_specs=[pl.BlockSpec((BLOCK_ROWS, D), lambda i, be, nv: (blk(i, be, nv), 0)),
                  pl.BlockSpec((None, D, 2 * EXPERT_FF), lambda i, be, nv: (be[blk(i, be, nv)], 0, 0)),
                  pl.BlockSpec((None, EXPERT_FF, D), lambda i, be, nv: (be[blk(i, be, nv)], 0, 0))],
        out_specs=pl.BlockSpec((BLOCK_ROWS, D), lambda i, be, nv: (blk(i, be, nv), 0)),
    )
    return pl.pallas_call(
        _expert_kernel,
        grid_spec=grid_spec,
        out_shape=jax.ShapeDtypeStruct((n_rows, D), BF16),
        compiler_params=_cparams(("arbitrary",)),
        name="experts",
    )(bexp, nvalid, xs, wgu, wd)


def _combine_kernel(off_ref, m_ref, base_ref, tot_ref, ys_hbm, posk_ref, wk_ref, h_ref, x_ref, g2_ref,
                    wsgu_ref, wsd_ref, o_ref, ys_ref, pw_ref, sem):
    i = pl.program_id(0)
    tm = h_ref.shape[0]
    slot = i % 2

    def fetch(tile, buf_slot):
        def run(e, carry):
            idx = tile * N_EXPERTS + e
            _start_rows(m_ref[idx], ys_hbm, off_ref[idx], ys_ref.at[buf_slot], base_ref[idx],
                        sem.at[buf_slot])
            return carry
        lax.fori_loop(0, N_EXPERTS, run, 0)

    @pl.when(i == 0)
    def _():
        ys_ref[...] = jnp.zeros_like(ys_ref)
        fetch(i, slot)

    @pl.when(i + 1 < pl.num_programs(0))
    def _():
        fetch(i + 1, 1 - slot)

    h = h_ref[...]
    gu = _dot(h, wsgu_ref[...])
    g = gu[:, :EXPERT_FF]
    shared = _dot((g * _sigmoid(g) * gu[:, EXPERT_FF:]).astype(BF16), wsd_ref[...])

    posk = posk_ref[0]
    wk = wk_ref[0]
    rows = lax.broadcasted_iota(jnp.int32, (MXU_DIM, tm), 0).astype(F32).astype(BF16)
    for c in range(K_SORT // MXU_DIM):
        rel = posk - float(c * MXU_DIM)
        pw = jnp.zeros((MXU_DIM, tm), BF16)
        for k in range(TOP_K):
            wrow = jnp.broadcast_to(wk[k:k + 1, :], (MXU_DIM, tm)).astype(BF16)
            pw = jnp.where(rows == rel[k:k + 1, :].astype(BF16), wrow, pw)
        pw_ref[c * MXU_DIM:(c + 1) * MXU_DIM, :] = pw

    _wait_rows(tot_ref[i], ys_hbm, ys_ref.at[slot], sem.at[slot])
    routed = lax.dot_general(pw_ref[...], ys_ref[slot], (((0,), (0,)), ((), ())),
                             preferred_element_type=F32)
    o_ref[...] = x_ref[...] + g2_ref[0] * (routed + shared)


def _combine(off, m16, base, tot, ys, posk, wk, h2, x1, g2, wsgu, wsd, tiles_per_batch):
    T, D = h2.shape
    tm = TM_MOE
    const = lambda i, *_: (0, 0)
    tok = lambda w: pl.BlockSpec((tm, w), lambda i, *_: (i, 0))
    sel = pl.BlockSpec((1, TOP_K, tm), lambda i, *_: (i, 0, 0))
    grid_spec = pltpu.PrefetchScalarGridSpec(
        num_scalar_prefetch=4,
        grid=(T // tm,),
        in_specs=[pl.BlockSpec(memory_space=pl.ANY), sel, sel, tok(D), tok(D),
                  pl.BlockSpec((1, 1, D), lambda i, *_: (i // tiles_per_batch, 0, 0)),
                  pl.BlockSpec(wsgu.shape, const), pl.BlockSpec(wsd.shape, const)],
        out_specs=tok(D),
        scratch_shapes=[pltpu.VMEM((2, K_SORT, D), BF16),
                        pltpu.VMEM((K_SORT, tm), BF16),
                        pltpu.SemaphoreType.DMA((2,))],
    )
    return pl.pallas_call(
        _combine_kernel,
        grid_spec=grid_spec,
        out_shape=jax.ShapeDtypeStruct((T, D), F32),
        compiler_params=_cparams(("arbitrary",), unchecked_dma=True),
        name="combine",
    )(off, m16, base, tot, ys, posk, wk, h2, x1, g2, wsgu, wsd)


def _moe(h2, x1, g2, wr_t, rb, wgu, wd, wsgu, wsd, ustrict, lstrict, tiles_per_batch):
    T, D = h2.shape
    nt = T // TM_MOE
    posk, wk, cnt = _router(h2, wr_t, rb, ustrict, lstrict)

    pad = cnt[:, :, 0].astype(jnp.int32)
    base = jnp.cumsum(pad, axis=1) - pad
    tile_off = jnp.cumsum(pad, axis=0) - pad
    total = jnp.sum(pad, axis=0)
    region = ((total + BLOCK_ROWS - 1) // BLOCK_ROWS) * BLOCK_ROWS
    rend = jnp.cumsum(region)
    rstart = rend - region
    off = rstart[None, :] + tile_off
    nb_max = -(-(TOP_K * T + nt * N_EXPERTS * (RUN_ALIGN - 1)) // BLOCK_ROWS) + N_EXPERTS
    blk_row = jnp.arange(nb_max, dtype=jnp.int32) * BLOCK_ROWS
    bexp = jnp.minimum(jnp.sum((rend[None, :] <= blk_row[:, None]).astype(jnp.int32), axis=1), N_EXPERTS - 1)
    nvalid = (rend[-1:] // BLOCK_ROWS).astype(jnp.int32)
    flat = lambda a: a.reshape(-1).astype(jnp.int32)

    m16 = flat(pad // RUN_ALIGN)
    tot = flat(jnp.sum(pad, axis=1) // RUN_ALIGN)
    xs = _dispatch(flat(off), m16, flat(base), tot, flat(rstart + total),
                   flat((region - total) // RUN_ALIGN), h2, posk, nb_max * BLOCK_ROWS)
    ys = _experts(bexp, nvalid, xs, wgu, wd)
    return _combine(flat(off), m16, flat(base), tot, ys, posk, wk, h2, x1, g2, wsgu, wsd, tiles_per_batch)


def kernel(x, c, ada_w, ada_b, mix_norm_g, w_in, b_fgate, qn_a, kn_a, qn_b, kn_b, rel_bias, w_proj_a,
           w_proj_b, w_out, ffn_norm_g, w_router, router_bias, w_gate_e, w_up_e, w_down_e, w_gate_s,
           w_up_s, w_down_s):
    B, S, D = x.shape
    L = ada_w.shape[0]
    T = B * S
    assert D == D_MODEL and S % TM_PRE == 0 and TQ_FOX == TM_PRE and S % TM_MOE == 0 and S % TQ_BAND == 0

    mod = _adaln(c, ada_w, ada_b).reshape(L, B, 6, 1, D)

    hid = jnp.arange(MXU_DIM) // HEAD_DIM
    bd = jnp.where(hid[:, None] == hid[None, :], 1.0 / HEAD_DIM, 0.0).astype(BF16)
    r = jnp.arange(TM_PRE)
    tri = (r[:, None] <= r[None, :]).astype(BF16)
    r = jnp.arange(TM_MOE)
    ustrict = (r[:, None] < r[None, :]).astype(BF16)
    r = jnp.arange(N_EXPERTS)
    lstrict = (r[None, :] < r[:, None]).astype(BF16)

    q_scale = 1.0 / math.sqrt(HEAD_DIM)
    for l in range(L):
        sh1, sc1, g1, sh2, sc2, g2 = (mod[l, :, j] for j in range(6))
        w = w_in[l]
        cols = lambda c: w[:, c * WIDTH:(c + 1) * WIDTH]
        wqk = jnp.concatenate([cols(0), cols(1), cols(3), cols(4)], axis=1).astype(BF16)
        wvt = jnp.concatenate([cols(2), cols(5)], axis=1).T.astype(BF16)
        wf = jnp.zeros((2 * N_HEADS, D), BF16).at[:N_HEADS].set(w[:, 6 * WIDTH:6 * WIDTH + N_HEADS].T.astype(BF16))
        wg = w[:, 6 * WIDTH + N_HEADS:].astype(BF16)
        hn = jnp.zeros((8, WIDTH), F32)
        hn = hn.at[0].set(jnp.tile(qn_a[l], N_HEADS) * (q_scale * LOG2E)).at[1].set(jnp.tile(kn_a[l], N_HEADS))
        hn = hn.at[2].set(jnp.tile(qn_b[l], N_HEADS) * (q_scale * LOG2E)).at[3].set(jnp.tile(kn_b[l], N_HEADS))

        qk, vt, dec, cbase, ga, gb = _premix(x, sh1, sc1, mix_norm_g[l][None], wqk, wvt, wf, wg,
                                             b_fgate[l][:, None], hn, bd, tri)
        ya = _band_attention(qk, vt, _band_bias(rel_bias[l]))
        yb = _fox_attention(qk, vt, dec, cbase)
        x1, h2 = _postmix(ya, yb, ga, gb, x, g1, w_proj_a[l].astype(BF16), w_proj_b[l].astype(BF16),
                          w_out[l].astype(BF16), ffn_norm_g[l][None], sh2, sc2)

        wgu = jnp.concatenate([w_gate_e[l], w_up_e[l]], axis=-1).astype(BF16)
        wsgu = jnp.concatenate([w_gate_s[l], w_up_s[l]], axis=-1).astype(BF16)
        x = _moe(h2.reshape(T, D), x1.reshape(T, D), g2, w_router[l].T.astype(BF16),
                 router_bias[l][:, None], wgu, w_down_e[l].astype(BF16), wsgu,
                 w_down_s[l].astype(BF16), ustrict, lstrict, S // TM_MOE).reshape(B, S, D)
    return x
```

```python
import functools
import math

import jax
import jax.numpy as jnp
from jax import lax
from jax.experimental import pallas as pl
from jax.experimental.pallas import tpu as pltpu

F32 = jnp.float32
BF16 = jnp.bfloat16

D_MODEL = 1024
HEAD_DIM = 64
N_HEADS = 8
WIDTH = N_HEADS * HEAD_DIM
CHUNK = 64
LOOKBACK_CHUNKS = 8
REL_CLIP = 128
N_EXPERTS = 64
TOP_K = 8
N_GROUPS = 8
TOPK_GROUPS = 4
EXPERT_FF = 256
ROUTED_SCALE = 2.5
EPS = 1e-6
NEG_INF = -1e30
LOG2E = math.log2(math.e)

LANES = 128
MXU_DIM = 256
VMEM_LIMIT = 56 * 1024 * 1024

TM_PRE = 512
TQ_BAND = 256
BAND_WIN = TQ_BAND + LOOKBACK_CHUNKS * CHUNK
TQ_FOX = 512
PAIRS_PER_STEP = 4
TM_MOE = 256
RUN_ALIGN = 16
K_SORT = ((TOP_K * TM_MOE + N_EXPERTS * (RUN_ALIGN - 1) + MXU_DIM - 1) // MXU_DIM) * MXU_DIM
K_SORT_COMMON = 2560
BLOCK_ROWS = 1024


def _dot(a, b):
    return jnp.dot(a, b, preferred_element_type=F32)


def _dot_nt(a, b):
    return lax.dot_general(a, b, (((1,), (1,)), ((), ())), preferred_element_type=F32)


def _sigmoid(v):
    return 1.0 / (1.0 + jnp.exp(-v))


def _cparams(sem, unchecked_dma=False):
    return pltpu.CompilerParams(dimension_semantics=sem, vmem_limit_bytes=VMEM_LIMIT,
                                disable_bounds_checks=unchecked_dma)


def _adaln_kernel(c_ref, w_ref, b_ref, o_ref):
    c = c_ref[...]
    ca = c * _sigmoid(c)
    o_ref[0] = jnp.dot(ca, w_ref[0], preferred_element_type=F32,
                       precision=lax.Precision.HIGHEST) + b_ref[0]


def _adaln(c, ada_w, ada_b):
    L, D, N = ada_w.shape
    B = c.shape[0]
    tn = 1536
    return pl.pallas_call(
        _adaln_kernel,
        grid=(L, N // tn),
        in_specs=[pl.BlockSpec((B, D), lambda l, j: (0, 0)),
                  pl.BlockSpec((1, D, tn), lambda l, j: (l, 0, j)),
                  pl.BlockSpec((1, 1, tn), lambda l, j: (l, 0, j))],
        out_specs=pl.BlockSpec((1, B, tn), lambda l, j: (l, 0, j)),
        out_shape=jax.ShapeDtypeStruct((L, B, N), F32),
        compiler_params=_cparams(("arbitrary", "arbitrary")),
        name="adaln",
    )(c, ada_w, ada_b.reshape(L, 1, N))


def _modnorm(x, g, sc, sh):
    ms = jnp.mean(x * x, axis=-1, keepdims=True)
    y = x * lax.rsqrt(ms + EPS) * g
    return y * (1.0 + sc) + sh


def _premix_kernel(x_ref, sh_ref, sc_ref, g_ref, wqk_ref, wvt_ref, wf_ref, wg_ref, bf_ref, hn_ref, bd_ref,
                   tri_ref, qk_ref, vt_ref, dec_ref, cbase_ref, ga_ref, gb_ref, carry_ref):
    si = pl.program_id(1)
    tm = x_ref.shape[1]
    h = _modnorm(x_ref[0], g_ref[...], sc_ref[0], sh_ref[0]).astype(BF16)

    for c in range(4):
        z = _dot(h, wqk_ref[:, c * WIDTH:(c + 1) * WIDTH])
        sq = (z * z).astype(BF16)
        ms = jnp.concatenate(
            [_dot(sq[:, j * MXU_DIM:(j + 1) * MXU_DIM], bd_ref[...]) for j in range(WIDTH // MXU_DIM)],
            axis=1)
        z = z * lax.rsqrt(ms + EPS) * hn_ref[c:c + 1, :]
        qk_ref[c, 0] = z.astype(BF16)

    for c in range(2):
        vt = _dot_nt(wvt_ref[c * WIDTH:(c + 1) * WIDTH, :], h).astype(BF16)
        vt_ref[0, c * (N_HEADS // 2):(c + 1) * (N_HEADS // 2)] = vt.reshape(N_HEADS // 2, LANES, tm)

    fr = _dot_nt(wf_ref[...], h)[:N_HEADS]
    xg = fr + bf_ref[...]
    logf = jnp.minimum(xg, 0.0) - jnp.log(1.0 + jnp.exp(-jnp.abs(xg)))
    hi = logf.astype(BF16).astype(F32)
    r1 = logf - hi
    mid = r1.astype(BF16).astype(F32)
    lo = r1 - mid
    parts = jnp.concatenate([hi, mid, lo, jnp.zeros_like(hi)], axis=0)
    cs3 = _dot(parts.astype(BF16), tri_ref[...])
    cs = cs3[0:8] + cs3[8:16] + cs3[16:24]

    @pl.when(si == 0)
    def _():
        carry_ref[...] = jnp.zeros_like(carry_ref)

    drel = cs * (-LOG2E)
    dhi = drel.astype(BF16).astype(F32)
    dlo = drel - dhi
    dec = jnp.concatenate([dhi, dlo, jnp.zeros((LANES - 2 * N_HEADS, tm), F32)], axis=0)
    dec_ref[0] = dec.T.astype(BF16)
    carry = carry_ref[:, 0:1]
    cbase_ref[0, 0] = jnp.broadcast_to(carry * (-LOG2E), (N_HEADS, tm))
    carry_ref[...] = jnp.broadcast_to(carry + cs[:, tm - 1:tm], carry_ref.shape)

    for c in range(4):
        zg = _dot(h, wg_ref[:, c * WIDTH:(c + 1) * WIDTH])
        sg = _sigmoid(zg).astype(BF16)
        if c < 2:
            ga_ref[0, :, c * WIDTH:(c + 1) * WIDTH] = sg
        else:
            gb_ref[0, :, (c - 2) * WIDTH:(c - 1) * WIDTH] = sg


def _premix(x, sh, sc, g, wqk, wvt, wf, wg, bf, hn, bd, tri):
    B, S, D = x.shape
    tm = TM_PRE
    const = lambda b, s: (0, 0)
    return pl.pallas_call(
        _premix_kernel,
        grid=(B, S // tm),
        in_specs=[pl.BlockSpec((1, tm, D), lambda b, s: (b, s, 0)),
                  pl.BlockSpec((1, 1, D), lambda b, s: (b, 0, 0)),
                  pl.BlockSpec((1, 1, D), lambda b, s: (b, 0, 0)),
                  pl.BlockSpec((1, D), const),
                  pl.BlockSpec(wqk.shape, const),
                  pl.BlockSpec(wvt.shape, const),
                  pl.BlockSpec(wf.shape, const),
                  pl.BlockSpec(wg.shape, const),
                  pl.BlockSpec(bf.shape, const),
                  pl.BlockSpec(hn.shape, const),
                  pl.BlockSpec(bd.shape, const),
                  pl.BlockSpec(tri.shape, const)],
        out_specs=[pl.BlockSpec((4, 1, tm, WIDTH), lambda b, s: (0, b, s, 0)),
                   pl.BlockSpec((1, N_HEADS, LANES, tm), lambda b, s: (b, 0, 0, s)),
                   pl.BlockSpec((1, tm, LANES), lambda b, s: (b, s, 0)),
                   pl.BlockSpec((1, 1, N_HEADS, tm), lambda b, s: (b, s, 0, 0)),
                   pl.BlockSpec((1, tm, D), lambda b, s: (b, s, 0)),
                   pl.BlockSpec((1, tm, D), lambda b, s: (b, s, 0))],
        out_shape=[jax.ShapeDtypeStruct((4, B, S, WIDTH), BF16),
                   jax.ShapeDtypeStruct((B, N_HEADS, LANES, S), BF16),
                   jax.ShapeDtypeStruct((B, S, LANES), BF16),
                   jax.ShapeDtypeStruct((B, S // tm, N_HEADS, tm), F32),
                   jax.ShapeDtypeStruct((B, S, D), BF16),
                   jax.ShapeDtypeStruct((B, S, D), BF16)],
        scratch_shapes=[pltpu.VMEM((N_HEADS, LANES), F32)],
        compiler_params=_cparams(("arbitrary", "arbitrary")),
        name="premix",
    )(x, sh, sc, g, wqk, wvt, wf, wg, bf, hn, bd, tri)


def _pair_finish_t(acc0, acc1, sub):
    out_t = jnp.where(sub < HEAD_DIM, acc0 / acc0[HEAD_DIM:HEAD_DIM + 1, :], acc1 / acc1[0:1, :])
    return out_t.T


def _band_kernel(q_ref, k0_ref, k1_ref, k2_ref, v0_ref, v1_ref, v2_ref, bias_ref, o_ref):
    qi = pl.program_id(2)
    tq = q_ref.shape[0]
    lane = lax.broadcasted_iota(jnp.int32, (1, LANES), 1)
    sub = lax.broadcasted_iota(jnp.int32, (LANES, 1), 0)
    krow = lax.broadcasted_iota(jnp.int32, (BAND_WIN, 1), 0)
    kvalid = (krow + (qi * tq - LOOKBACK_CHUNKS * CHUNK)) >= 0
    k_refs = (k0_ref, k1_ref, k2_ref)
    v_refs = (v0_ref, v1_ref, v2_ref)
    heads = [(pp, hh) for pp in range(PAIRS_PER_STEP) for hh in range(2)]
    scores = []
    for pp, hh in heads:
        q = q_ref[:, pp * LANES:(pp + 1) * LANES]
        qm = jnp.where((lane // HEAD_DIM) == hh, q, jnp.zeros_like(q))
        scores.append(jnp.concatenate(
            [_dot_nt(k[:, pp * LANES:(pp + 1) * LANES], qm) for k in k_refs], axis=0))
    accs = []
    for n, (pp, hh) in enumerate(heads):
        s = jnp.where(kvalid, scores[n] + bias_ref[n], NEG_INF)
        m = jnp.max(s, axis=0, keepdims=True)
        p = jnp.exp2(s - m).astype(BF16)
        acc = None
        for j in range(3):
            vt = v_refs[j][pp]
            va = jnp.where((sub // HEAD_DIM) == hh, vt, jnp.ones_like(vt))
            t = _dot(va, p[j * tq:(j + 1) * tq, :])
            acc = t if acc is None else acc + t
        accs.append(acc)
    for pp in range(PAIRS_PER_STEP):
        o_ref[:, pp * LANES:(pp + 1) * LANES] = _pair_finish_t(
            accs[2 * pp], accs[2 * pp + 1], sub).astype(o_ref.dtype)


def _band_attention(qk, vt, bias_t):
    _, B, S, _ = qk.shape
    tq = TQ_BAND
    nq = S // tq

    groups = N_HEADS // 2 // PAIRS_PER_STEP
    gw = PAIRS_PER_STEP * LANES

    def k_spec(back):
        return pl.BlockSpec((None, None, tq, gw),
                            lambda b, g, qi: (1, b, jnp.maximum(qi - back, 0), g))

    def v_spec(back):
        return pl.BlockSpec((None, PAIRS_PER_STEP, LANES, tq),
                            lambda b, g, qi: (b, g, 0, jnp.maximum(qi - back, 0)))

    return pl.pallas_call(
        _band_kernel,
        grid=(B, groups, nq),
        in_specs=[pl.BlockSpec((None, None, tq, gw), lambda b, g, qi: (0, b, qi, g)),
                  k_spec(2), k_spec(1), k_spec(0), v_spec(2), v_spec(1), v_spec(0),
                  pl.BlockSpec((2 * PAIRS_PER_STEP, BAND_WIN, tq), lambda b, g, qi: (g, 0, 0))],
        out_specs=pl.BlockSpec((None, tq, gw), lambda b, g, qi: (b, qi, g)),
        out_shape=jax.ShapeDtypeStruct((B, S, WIDTH), BF16),
        compiler_params=_cparams(("arbitrary", "arbitrary", "arbitrary")),
        name="band_attn",
    )(qk, qk, qk, qk, vt, vt, vt, bias_t)


def _band_bias(rel_table):
    pad = LOOKBACK_CHUNKS * CHUNK
    i = jnp.arange(TQ_BAND)[:, None]
    j = jnp.arange(BAND_WIN)[None, :]
    period = 1024
    assert BAND_WIN + TQ_BAND <= period
    m = jnp.arange(period)
    rel = jnp.where(m <= BAND_WIN, pad - m, pad + period - m)
    g = rel_table[:, jnp.clip(rel, -REL_CLIP, REL_CLIP) + REL_CLIP].astype(F32)
    bias = jnp.tile(g, (1, TQ_BAND))[:, :TQ_BAND * (period - 1)].reshape(-1, TQ_BAND, period - 1)[:, :, :BAND_WIN]
    c0 = (i // CHUNK) * CHUNK
    inband = (j >= c0) & (j < c0 + pad + CHUNK)
    return jnp.where(inband[None], bias * LOG2E, NEG_INF).transpose(0, 2, 1)


def _fox_kernel(qi_tab, kj_tab, q_ref, k_ref, vt_ref, dec_ref, cbase_ref, o_ref, acc_ref, m_ref):
    grp = pl.program_id(1)
    t = pl.program_id(2)
    qi = qi_tab[t]
    kj = kj_tab[t]
    tq, tk = q_ref.shape[0], k_ref.shape[0]
    lane2 = lax.broadcasted_iota(jnp.int32, (1, 2 * LANES), 1)
    sub = lax.broadcasted_iota(jnp.int32, (LANES, 1), 0)
    heads = [(pp, hh) for pp in range(PAIRS_PER_STEP) for hh in range(2)]

    @pl.when(kj == 0)
    def _():
        acc_ref[...] = jnp.zeros_like(acc_ref)
        m_ref[...] = jnp.full_like(m_ref, NEG_INF)

    def step(diagonal):
        ones = jnp.where(lax.broadcasted_iota(jnp.int32, (tq, LANES), 1) < 2 * N_HEADS, 1.0, 0.0).astype(BF16)
        dec = dec_ref[...]
        if diagonal:
            causal = (lax.broadcasted_iota(jnp.int32, (tk, tq), 0)
                      <= lax.broadcasted_iota(jnp.int32, (tk, tq), 1))
        scores = []
        for pp, hh in heads:
            head = 2 * (PAIRS_PER_STEP * grp + pp) + hh
            q_aug = jnp.concatenate([q_ref[:, pp * LANES:(pp + 1) * LANES], ones], axis=1)
            k_aug = jnp.concatenate([k_ref[:, pp * LANES:(pp + 1) * LANES], dec], axis=1)
            use = (((lane2 // HEAD_DIM) == hh) | (lane2 == LANES + head) | (lane2 == LANES + N_HEADS + head))
            qm = jnp.where(use, q_aug, jnp.zeros_like(q_aug))
            scores.append(_dot_nt(k_aug, qm))
        for n, (pp, hh) in enumerate(heads):
            head = 2 * (PAIRS_PER_STEP * grp + pp) + hh
            s = scores[n]
            if diagonal:
                s = jnp.where(causal, s, NEG_INF)
            base = cbase_ref[pl.ds(head, 1), :]
            m_old = m_ref[n]
            m_new = jnp.maximum(m_old, jnp.max(s, axis=0, keepdims=True) + base)
            alpha = jnp.exp2(m_old - m_new)
            p = jnp.exp2(s - (m_new - base)).astype(BF16)
            vt = vt_ref[pp]
            va = jnp.where((sub // HEAD_DIM) == hh, vt, jnp.ones_like(vt))
            acc_ref[n] = acc_ref[n] * alpha + _dot(va, p)
            m_ref[n] = m_new

    @pl.when(kj < qi)
    def _():
        step(False)

    @pl.when(kj == qi)
    def _():
        step(True)
        for pp in range(PAIRS_PER_STEP):
            o_ref[:, pp * LANES:(pp + 1) * LANES] = _pair_finish_t(
                acc_ref[2 * pp], acc_ref[2 * pp + 1], sub).astype(o_ref.dtype)


def _fox_attention(qkv, vt, dec, cbase):
    _, B, S, _ = qkv.shape
    tq = TQ_FOX
    nq = S // tq
    groups = N_HEADS // 2 // PAIRS_PER_STEP
    gw = PAIRS_PER_STEP * LANES
    pairs = [(i, j) for i in range(nq) for j in range(i + 1)]
    qi_tab = jnp.array([p[0] for p in pairs], jnp.int32)
    kj_tab = jnp.array([p[1] for p in pairs], jnp.int32)
    grid_spec = pltpu.PrefetchScalarGridSpec(
        num_scalar_prefetch=2,
        grid=(B, groups, len(pairs)),
        in_specs=[pl.BlockSpec((None, None, tq, gw), lambda b, g, t, qt, kt: (2, b, qt[t], g)),
                  pl.BlockSpec((None, None, tq, gw), lambda b, g, t, qt, kt: (3, b, kt[t], g)),
                  pl.BlockSpec((None, PAIRS_PER_STEP, LANES, tq),
                               lambda b, g, t, qt, kt: (b, groups + g, 0, kt[t])),
                  pl.BlockSpec((None, tq, LANES), lambda b, g, t, qt, kt: (b, kt[t], 0)),
                  pl.BlockSpec((None, None, N_HEADS, tq), lambda b, g, t, qt, kt: (b, kt[t], 0, 0))],
        out_specs=pl.BlockSpec((None, tq, gw), lambda b, g, t, qt, kt: (b, qt[t], g)),
        scratch_shapes=[pltpu.VMEM((2 * PAIRS_PER_STEP, LANES, tq), F32),
                        pltpu.VMEM((2 * PAIRS_PER_STEP, 1, tq), F32)],
    )
    return pl.pallas_call(
        _fox_kernel,
        grid_spec=grid_spec,
        out_shape=jax.ShapeDtypeStruct((B, S, WIDTH), BF16),
        compiler_params=_cparams(("arbitrary", "arbitrary", "arbitrary")),
        name="fox_attn",
    )(qi_tab, kj_tab, qkv, qkv, vt, dec, cbase)


def _postmix_kernel(ya_ref, yb_ref, ga_ref, gb_ref, x_ref, g1_ref, wpa_ref, wpb_ref, wo_ref,
                    g_ref, sh_ref, sc_ref, x1_ref, h2_ref):
    ua = _dot(ya_ref[0], wpa_ref[...])
    ub = _dot(yb_ref[0], wpb_ref[...])
    m = ga_ref[0].astype(F32) * ua + gb_ref[0].astype(F32) * ub
    mo = _dot(m.astype(BF16), wo_ref[...])
    x1 = x_ref[0] + g1_ref[0] * mo
    x1_ref[0] = x1
    h2_ref[0] = _modnorm(x1, g_ref[...], sc_ref[0], sh_ref[0]).astype(BF16)


def _postmix(ya, yb, ga, gb, x, g1, wpa, wpb, wo, g, sh, sc):
    B, S, D = x.shape
    tm = TM_PRE
    const = lambda b, s: (0, 0)
    tok = lambda w: pl.BlockSpec((1, tm, w), lambda b, s: (b, s, 0))
    row = pl.BlockSpec((1, 1, D), lambda b, s: (b, 0, 0))
    return pl.pallas_call(
        _postmix_kernel,
        grid=(B, S // tm),
        in_specs=[tok(WIDTH), tok(WIDTH), tok(D), tok(D), tok(D), row,
                  pl.BlockSpec(wpa.shape, const), pl.BlockSpec(wpb.shape, const),
                  pl.BlockSpec(wo.shape, const), pl.BlockSpec((1, D), const), row, row],
        out_specs=[tok(D), tok(D)],
        out_shape=[jax.ShapeDtypeStruct((B, S, D), F32), jax.ShapeDtypeStruct((B, S, D), BF16)],
        compiler_params=_cparams(("arbitrary", "arbitrary")),
        name="postmix",
    )(ya, yb, ga, gb, x, g1, wpa, wpb, wo, g, sh, sc)


def _router_kernel(h_ref, wr_ref, rb_ref, ustrict_ref, lstrict_ref, posk_ref, wk_ref, cnt_ref):
    tm = h_ref.shape[0]
    per_group = N_EXPERTS // N_GROUPS
    logits = _dot_nt(wr_ref[...], h_ref[...])
    scores = _sigmoid(logits)
    biased = scores + rb_ref[...]
    sub8 = lax.broadcasted_iota(jnp.int32, (per_group, tm), 0).astype(F32)
    neg = -jnp.inf

    grp_rows = []
    for g in range(N_GROUPS):
        a = biased[g * per_group:(g + 1) * per_group]
        m1 = jnp.max(a, axis=0, keepdims=True)
        i1 = jnp.min(jnp.where(a == m1, sub8, float(per_group)), axis=0, keepdims=True)
        m2 = jnp.max(jnp.where(sub8 == i1, neg, a), axis=0, keepdims=True)
        grp_rows.append(m1 + m2)
    grp = jnp.concatenate(grp_rows, axis=0)

    gsub = lax.broadcasted_iota(jnp.int32, (N_GROUPS, tm), 0).astype(F32)
    gmask = jnp.zeros((N_GROUPS, tm), F32)
    for _ in range(TOPK_GROUPS):
        mx = jnp.max(grp, axis=0, keepdims=True)
        gi = jnp.min(jnp.where(grp == mx, gsub, float(N_GROUPS)), axis=0, keepdims=True)
        sel = gsub == gi
        gmask = jnp.where(sel, 1.0, gmask)
        grp = jnp.where(sel, neg, grp)
    masked = jnp.concatenate(
        [jnp.where(gmask[g:g + 1] > 0.5, biased[g * per_group:(g + 1) * per_group], neg)
         for g in range(N_GROUPS)], axis=0)

    esub = lax.broadcasted_iota(jnp.int32, (N_EXPERTS, tm), 0).astype(F32)
    sels, ws = [], []
    for _ in range(TOP_K):
        mx = jnp.max(masked, axis=0, keepdims=True)
        ei = jnp.min(jnp.where(masked == mx, esub, float(N_EXPERTS)), axis=0, keepdims=True)
        sel = esub == ei
        sels.append(sel)
        ws.append(jnp.sum(jnp.where(sel, scores, 0.0), axis=0, keepdims=True))
        masked = jnp.where(sel, neg, masked)
    wsum = ws[0]
    for w in ws[1:]:
        wsum = wsum + w

    selmask = jnp.zeros((N_EXPERTS, tm), F32)
    for sel in sels:
        selmask = selmask + jnp.where(sel, 1.0, 0.0)
    rank = _dot(selmask.astype(BF16), ustrict_ref[...])
    counts = jnp.sum(selmask, axis=1, keepdims=True)
    padded = jnp.ceil(counts * (1.0 / RUN_ALIGN)) * RUN_ALIGN
    padded_b = jnp.broadcast_to(padded, (N_EXPERTS, LANES))
    base = _dot(lstrict_ref[...], padded_b.astype(BF16))
    pos = base[:, 0:1] + rank

    posk = jnp.concatenate([jnp.sum(jnp.where(sel, pos, 0.0), axis=0, keepdims=True) for sel in sels], axis=0)
    wk = jnp.concatenate([(w / wsum) * ROUTED_SCALE for w in ws], axis=0)
    posk_ref[0] = posk
    wk_ref[0] = wk
    cnt_ref[0] = padded_b


def _router(h2, wr_t, rb, ustrict, lstrict):
    T, D = h2.shape
    tm = TM_MOE
    nt = T // tm
    const = lambda i: (0, 0)
    return pl.pallas_call(
        _router_kernel,
        grid=(nt,),
        in_specs=[pl.BlockSpec((tm, D), lambda i: (i, 0)),
                  pl.BlockSpec(wr_t.shape, const), pl.BlockSpec(rb.shape, const),
                  pl.BlockSpec(ustrict.shape, const), pl.BlockSpec(lstrict.shape, const)],
        out_specs=[pl.BlockSpec((1, TOP_K, tm), lambda i: (i, 0, 0)),
                   pl.BlockSpec((1, TOP_K, tm), lambda i: (i, 0, 0)),
                   pl.BlockSpec((1, N_EXPERTS, LANES), lambda i: (i, 0, 0))],
        out_shape=[jax.ShapeDtypeStruct((nt, TOP_K, tm), F32),
                   jax.ShapeDtypeStruct((nt, TOP_K, tm), F32),
                   jax.ShapeDtypeStruct((nt, N_EXPERTS, LANES), F32)],
        compiler_params=_cparams(("arbitrary",)),
        name="router",
    )(h2, wr_t, rb, ustrict, lstrict)


def _rows_copy(m, src_ref, src0, dst_ref, dst0, sem):
    rows = m * RUN_ALIGN
    return pltpu.make_async_copy(src_ref.at[pl.ds(pl.multiple_of(src0, RUN_ALIGN), rows)],
                                 dst_ref.at[pl.ds(pl.multiple_of(dst0, RUN_ALIGN), rows)], sem)


def _start_rows(m, src_ref, src0, dst_ref, dst0, sem):
    @pl.when(m > 0)
    def _():
        _rows_copy(m, src_ref, src0, dst_ref, dst0, sem).start()


def _wait_rows(m, src_ref, dst_ref, sem):
    @pl.when(m > 0)
    def _():
        _rows_copy(m, src_ref, 0, dst_ref, 0, sem).wait()


def _dispatch_kernel(off_ref, m_ref, base_ref, tot_ref, gapoff_ref, gapm_ref, h_ref, posk_ref, xs_hbm,
                     xs_ref, zero_ref, sem):
    i = pl.program_id(0)
    tm = h_ref.shape[0]
    n_chunks = K_SORT // MXU_DIM

    @pl.when(i == 0)
    def _():
        zero_ref[...] = jnp.zeros_like(zero_ref)

        def gap(e, carry):
            _start_rows(gapm_ref[e], zero_ref, 0, xs_hbm, gapoff_ref[e], sem)
            return carry
        lax.fori_loop(0, N_EXPERTS, gap, 0)

        def gap_wait(e, carry):
            _wait_rows(gapm_ref[e], zero_ref, xs_hbm, sem)
            return carry
        lax.fori_loop(0, N_EXPERTS, gap_wait, 0)

    slot = i % 2
    xs_slot = xs_ref.at[slot]
    h = h_ref[...]
    posk = posk_ref[0]
    rows = lax.broadcasted_iota(jnp.int32, (MXU_DIM, tm), 0).astype(F32).astype(BF16)
    one = jnp.ones((MXU_DIM, tm), BF16)
    def sort_chunks(lo, hi):
        for c in range(lo, hi):
            rel = posk - float(c * MXU_DIM)
            p = jnp.zeros((MXU_DIM, tm), BF16)
            for k in range(TOP_K):
                p = jnp.where(rows == rel[k:k + 1, :].astype(BF16), one, p)
            xs_slot[c * MXU_DIM:(c + 1) * MXU_DIM, :] = _dot(p, h).astype(BF16)

    sort_chunks(0, K_SORT_COMMON // MXU_DIM)

    @pl.when(tot_ref[i] * RUN_ALIGN > K_SORT_COMMON)
    def _():
        sort_chunks(K_SORT_COMMON // MXU_DIM, n_chunks)

    @pl.when(i > 0)
    def _():
        _wait_rows(tot_ref[i - 1], xs_ref.at[1 - slot], xs_hbm, sem)

    def run(e, carry):
        idx = i * N_EXPERTS + e
        _start_rows(m_ref[idx], xs_slot, base_ref[idx], xs_hbm, off_ref[idx], sem)
        return carry
    lax.fori_loop(0, N_EXPERTS, run, 0)

    @pl.when(i == pl.num_programs(0) - 1)
    def _():
        _wait_rows(tot_ref[i], xs_slot, xs_hbm, sem)


def _dispatch(off, m16, base, tot, gapoff, gapm, h2, posk, n_rows):
    T, D = h2.shape
    tm = TM_MOE
    grid_spec = pltpu.PrefetchScalarGridSpec(
        num_scalar_prefetch=6,
        grid=(T // tm,),
        in_specs=[pl.BlockSpec((tm, D), lambda i, *_: (i, 0)),
                  pl.BlockSpec((1, TOP_K, tm), lambda i, *_: (i, 0, 0))],
        out_specs=pl.BlockSpec(memory_space=pl.ANY),
        scratch_shapes=[pltpu.VMEM((2, K_SORT, D), BF16),
                        pltpu.VMEM((BLOCK_ROWS, D), BF16),
                        pltpu.SemaphoreType.DMA],
    )
    return pl.pallas_call(
        _dispatch_kernel,
        grid_spec=grid_spec,
        out_shape=jax.ShapeDtypeStruct((n_rows, D), BF16),
        compiler_params=_cparams(("arbitrary",), unchecked_dma=True),
        name="dispatch",
    )(off, m16, base, tot, gapoff, gapm, h2, posk)


def _expert_kernel(bexp_ref, nvalid_ref, x_ref, wg_ref, wu_ref, wd_ref, y_ref, wgu_bf, wd_bf):
    i = pl.program_id(0)

    @pl.when(i < nvalid_ref[0])
    def _():
        @pl.when((i == 0) | (bexp_ref[i] != bexp_ref[jnp.maximum(i - 1, 0)]))
        def _():
            wgu_bf[:, :EXPERT_FF] = wg_ref[...].astype(BF16)
            wgu_bf[:, EXPERT_FF:] = wu_ref[...].astype(BF16)
            wd_bf[...] = wd_ref[...].astype(BF16)

        gu = _dot(x_ref[...], wgu_bf[...])
        g = gu[:, :EXPERT_FF]
        a = (g * _sigmoid(g) * gu[:, EXPERT_FF:]).astype(BF16)
        y_ref[...] = _dot(a, wd_bf[...]).astype(y_ref.dtype)


def _experts(bexp, nvalid, xs, wg, wu, wd, layer):
    n_rows, D = xs.shape
    nb = n_rows // BLOCK_ROWS

    def blk(i, be, nv):
        return jnp.minimum(i, nv[0] - 1)

    wspec = lambda r, c: pl.BlockSpec((None, None, r, c),
                                      lambda i, be, nv: (layer, be[blk(i, be, nv)], 0, 0))
    grid_spec = pltpu.PrefetchScalarGridSpec(
        num_scalar_prefetch=2,
        grid=(nb,),
        in_specs=[pl.BlockSpec((BLOCK_ROWS, D), lambda i, be, nv: (blk(i, be, nv), 0)),
                  wspec(D, EXPERT_FF), wspec(D, EXPERT_FF), wspec(EXPERT_FF, D)],
        out_specs=pl.BlockSpec((BLOCK_ROWS, D), lambda i, be, nv: (blk(i, be, nv), 0)),
        scratch_shapes=[pltpu.VMEM((D, 2 * EXPERT_FF), BF16), pltpu.VMEM((EXPERT_FF, D), BF16)],
    )
    return pl.pallas_call(
        _expert_kernel,
        grid_spec=grid_spec,
        out_shape=jax.ShapeDtypeStruct((n_rows, D), BF16),
        compiler_params=_cparams(("arbitrary",)),
        name="experts",
    )(bexp, nvalid, xs, wg, wu, wd)


def _combine_kernel(off_ref, m_ref, base_ref, tot_ref, ys_hbm, posk_ref, wk_ref, h_ref, x_ref, g2_ref,
                    wsgu_ref, wsd_ref, o_ref, ys_ref, pw_ref, sem):
    i = pl.program_id(0)
    tm = h_ref.shape[0]
    slot = i % 2

    def fetch(tile, buf_slot):
        def run(e, carry):
            idx = tile * N_EXPERTS + e
            _start_rows(m_ref[idx], ys_hbm, off_ref[idx], ys_ref.at[buf_slot], base_ref[idx],
                        sem.at[buf_slot])
            return carry
        lax.fori_loop(0, N_EXPERTS, run, 0)

    @pl.when(i == 0)
    def _():
        ys_ref[...] = jnp.zeros_like(ys_ref)
        fetch(i, slot)

    @pl.when(i + 1 < pl.num_programs(0))
    def _():
        fetch(i + 1, 1 - slot)

    h = h_ref[...]
    gu = _dot(h, wsgu_ref[...])
    g = gu[:, :EXPERT_FF]
    shared = _dot((g * _sigmoid(g) * gu[:, EXPERT_FF:]).astype(BF16), wsd_ref[...])

    posk = posk_ref[0]
    wk = wk_ref[0]
    rows = lax.broadcasted_iota(jnp.int32, (MXU_DIM, tm), 0).astype(F32).astype(BF16)

    def weight_chunks(lo, hi):
        for c in range(lo, hi):
            rel = posk - float(c * MXU_DIM)
            pw = jnp.zeros((MXU_DIM, tm), BF16)
            for k in range(TOP_K):
                wrow = jnp.broadcast_to(wk[k:k + 1, :], (MXU_DIM, tm)).astype(BF16)
                pw = jnp.where(rows == rel[k:k + 1, :].astype(BF16), wrow, pw)
            pw_ref[c * MXU_DIM:(c + 1) * MXU_DIM, :] = pw

    def unsort(lo, hi):
        return lax.dot_general(pw_ref[lo:hi, :], ys_ref[slot, lo:hi, :], (((0,), (0,)), ((), ())),
                               preferred_element_type=F32)

    weight_chunks(0, K_SORT_COMMON // MXU_DIM)
    _wait_rows(tot_ref[i], ys_hbm, ys_ref.at[slot], sem.at[slot])
    o_ref[...] = x_ref[...] + g2_ref[0] * (unsort(0, K_SORT_COMMON) + shared)

    @pl.when(tot_ref[i] * RUN_ALIGN > K_SORT_COMMON)
    def _():
        weight_chunks(K_SORT_COMMON // MXU_DIM, K_SORT // MXU_DIM)
        o_ref[...] = o_ref[...] + g2_ref[0] * unsort(K_SORT_COMMON, K_SORT)


def _combine(off, m16, base, tot, ys, posk, wk, h2, x1, g2, wsgu, wsd, tiles_per_batch):
    T, D = h2.shape
    tm = TM_MOE
    const = lambda i, *_: (0, 0)
    tok = lambda w: pl.BlockSpec((tm, w), lambda i, *_: (i, 0))
    sel = pl.BlockSpec((1, TOP_K, tm), lambda i, *_: (i, 0, 0))
    grid_spec = pltpu.PrefetchScalarGridSpec(
        num_scalar_prefetch=4,
        grid=(T // tm,),
        in_specs=[pl.BlockSpec(memory_space=pl.ANY), sel, sel, tok(D), tok(D),
                  pl.BlockSpec((1, 1, D), lambda i, *_: (i // tiles_per_batch, 0, 0)),
                  pl.BlockSpec(wsgu.shape, const), pl.BlockSpec(wsd.shape, const)],
        out_specs=tok(D),
        scratch_shapes=[pltpu.VMEM((2, K_SORT, D), BF16),
                        pltpu.VMEM((K_SORT, tm), BF16),
                        pltpu.SemaphoreType.DMA((2,))],
    )
    return pl.pallas_call(
        _combine_kernel,
        grid_spec=grid_spec,
        out_shape=jax.ShapeDtypeStruct((T, D), F32),
        compiler_params=_cparams(("arbitrary",), unchecked_dma=True),
        name="combine",
    )(off, m16, base, tot, ys, posk, wk, h2, x1, g2, wsgu, wsd)


def _moe(h2, x1, g2, wr_t, rb, wg, wu, wd, layer, wsgu, wsd, ustrict, lstrict, tiles_per_batch):
    T, D = h2.shape
    nt = T // TM_MOE
    posk, wk, cnt = _router(h2, wr_t, rb, ustrict, lstrict)

    pad = cnt[:, :, 0].astype(jnp.int32)
    base = jnp.cumsum(pad, axis=1) - pad
    tile_off = jnp.cumsum(pad, axis=0) - pad
    total = jnp.sum(pad, axis=0)
    region = ((total + BLOCK_ROWS - 1) // BLOCK_ROWS) * BLOCK_ROWS
    rend = jnp.cumsum(region)
    rstart = rend - region
    off = rstart[None, :] + tile_off
    nb_max = -(-(TOP_K * T + nt * N_EXPERTS * (RUN_ALIGN - 1)) // BLOCK_ROWS) + N_EXPERTS
    blk_row = jnp.arange(nb_max, dtype=jnp.int32) * BLOCK_ROWS
    bexp = jnp.minimum(jnp.sum((rend[None, :] <= blk_row[:, None]).astype(jnp.int32), axis=1), N_EXPERTS - 1)
    nvalid = (rend[-1:] // BLOCK_ROWS).astype(jnp.int32)
    flat = lambda a: a.reshape(-1).astype(jnp.int32)

    m16 = flat(pad // RUN_ALIGN)
    tot = flat(jnp.sum(pad, axis=1) // RUN_ALIGN)
    xs = _dispatch(flat(off), m16, flat(base), tot, flat(rstart + total),
                   flat((region - total) // RUN_ALIGN), h2, posk, nb_max * BLOCK_ROWS)
    ys = _experts(bexp, nvalid, xs, wg, wu, wd, layer)
    return _combine(flat(off), m16, flat(base), tot, ys, posk, wk, h2, x1, g2, wsgu, wsd, tiles_per_batch)


def kernel(x, c, ada_w, ada_b, mix_norm_g, w_in, b_fgate, qn_a, kn_a, qn_b, kn_b, rel_bias, w_proj_a,
           w_proj_b, w_out, ffn_norm_g, w_router, router_bias, w_gate_e, w_up_e, w_down_e, w_gate_s,
           w_up_s, w_down_s):
    B, S, D = x.shape
    L = ada_w.shape[0]
    T = B * S
    assert D == D_MODEL and S % TM_PRE == 0 and TQ_FOX == TM_PRE and S % TM_MOE == 0 and S % TQ_BAND == 0

    mod = _adaln(c, ada_w, ada_b).reshape(L, B, 6, 1, D)

    hid = jnp.arange(MXU_DIM) // HEAD_DIM
    bd = jnp.where(hid[:, None] == hid[None, :], 1.0 / HEAD_DIM, 0.0).astype(BF16)
    r = jnp.arange(TM_PRE)
    tri = (r[:, None] <= r[None, :]).astype(BF16)
    r = jnp.arange(TM_MOE)
    ustrict = (r[:, None] < r[None, :]).astype(BF16)
    r = jnp.arange(N_EXPERTS)
    lstrict = (r[None, :] < r[:, None]).astype(BF16)

    q_scale = 1.0 / math.sqrt(HEAD_DIM)
    for l in range(L):
        sh1, sc1, g1, sh2, sc2, g2 = (mod[l, :, j] for j in range(6))
        w = w_in[l]
        cols = lambda c: w[:, c * WIDTH:(c + 1) * WIDTH]
        wqk = jnp.concatenate([cols(0), cols(1), cols(3), cols(4)], axis=1).astype(BF16)
        wvt = jnp.concatenate([cols(2), cols(5)], axis=1).T.astype(BF16)
        wf = jnp.zeros((2 * N_HEADS, D), BF16).at[:N_HEADS].set(w[:, 6 * WIDTH:6 * WIDTH + N_HEADS].T.astype(BF16))
        wg = w[:, 6 * WIDTH + N_HEADS:].astype(BF16)
        hn = jnp.zeros((8, WIDTH), F32)
        hn = hn.at[0].set(jnp.tile(qn_a[l], N_HEADS) * (q_scale * LOG2E)).at[1].set(jnp.tile(kn_a[l], N_HEADS))
        hn = hn.at[2].set(jnp.tile(qn_b[l], N_HEADS) * (q_scale * LOG2E)).at[3].set(jnp.tile(kn_b[l], N_HEADS))

        qk, vt, dec, cbase, ga, gb = _premix(x, sh1, sc1, mix_norm_g[l][None], wqk, wvt, wf, wg,
                                             b_fgate[l][:, None], hn, bd, tri)
        ya = _band_attention(qk, vt, _band_bias(rel_bias[l]))
        yb = _fox_attention(qk, vt, dec, cbase)
        x1, h2 = _postmix(ya, yb, ga, gb, x, g1, w_proj_a[l].astype(BF16), w_proj_b[l].astype(BF16),
                          w_out[l].astype(BF16), ffn_norm_g[l][None], sh2, sc2)

        wsgu = jnp.concatenate([w_gate_s[l], w_up_s[l]], axis=-1).astype(BF16)
        x = _moe(h2.reshape(T, D), x1.reshape(T, D), g2, w_router[l].T.astype(BF16),
                 router_bias[l][:, None], w_gate_e, w_up_e, w_down_e, l, wsgu,
                 w_down_s[l].astype(BF16), ustrict, lstrict, S // TM_MOE).reshape(B, S, D)
    return x
```

```python
import functools
import math

import jax
import jax.numpy as jnp
from jax import lax
from jax.experimental import pallas as pl
from jax.experimental.pallas import tpu as pltpu

F32 = jnp.float32
BF16 = jnp.bfloat16

D_MODEL = 1024
HEAD_DIM = 64
N_HEADS = 8
WIDTH = N_HEADS * HEAD_DIM
CHUNK = 64
LOOKBACK_CHUNKS = 8
REL_CLIP = 128
N_EXPERTS = 64
TOP_K = 8
N_GROUPS = 8
TOPK_GROUPS = 4
EXPERT_FF = 256
ROUTED_SCALE = 2.5
EPS = 1e-6
NEG_INF = -1e30
LOG2E = math.log2(math.e)

LANES = 128
MXU_DIM = 256
VMEM_LIMIT = 56 * 1024 * 1024

TM_PRE = 512
TQ_BAND = 256
BAND_WIN = TQ_BAND + LOOKBACK_CHUNKS * CHUNK
TQ_FOX = 512
PAIRS_PER_STEP = 4
PV_ROWS = HEAD_DIM + 16
TM_MOE = 256
RUN_ALIGN = 16
K_SORT = ((TOP_K * TM_MOE + N_EXPERTS * (RUN_ALIGN - 1) + MXU_DIM - 1) // MXU_DIM) * MXU_DIM
K_SORT_COMMON = 2560
BLOCK_ROWS = 1024


def _dot(a, b):
    return jnp.dot(a, b, preferred_element_type=F32)


def _dot_nt(a, b):
    return lax.dot_general(a, b, (((1,), (1,)), ((), ())), preferred_element_type=F32)


def _sigmoid(v):
    return 1.0 / (1.0 + jnp.exp(-v))


def _cparams(sem, unchecked_dma=False):
    return pltpu.CompilerParams(dimension_semantics=sem, vmem_limit_bytes=VMEM_LIMIT,
                                disable_bounds_checks=unchecked_dma)


def _adaln_kernel(c_ref, w_ref, b_ref, o_ref):
    c = c_ref[...]
    ca = c * _sigmoid(c)
    o_ref[0] = jnp.dot(ca, w_ref[0], preferred_element_type=F32,
                       precision=lax.Precision.HIGHEST) + b_ref[0]


def _adaln(c, ada_w, ada_b):
    L, D, N = ada_w.shape
    B = c.shape[0]
    tn = 1536
    return pl.pallas_call(
        _adaln_kernel,
        grid=(L, N // tn),
        in_specs=[pl.BlockSpec((B, D), lambda l, j: (0, 0)),
                  pl.BlockSpec((1, D, tn), lambda l, j: (l, 0, j)),
                  pl.BlockSpec((1, 1, tn), lambda l, j: (l, 0, j))],
        out_specs=pl.BlockSpec((1, B, tn), lambda l, j: (l, 0, j)),
        out_shape=jax.ShapeDtypeStruct((L, B, N), F32),
        compiler_params=_cparams(("arbitrary", "arbitrary")),
        name="adaln",
    )(c, ada_w, ada_b.reshape(L, 1, N))


def _modnorm(x, g, sc, sh):
    ms = jnp.mean(x * x, axis=-1, keepdims=True)
    y = x * lax.rsqrt(ms + EPS) * g
    return y * (1.0 + sc) + sh


def _premix_kernel(x_ref, sh_ref, sc_ref, g_ref, wqk_ref, wvt_ref, wf_ref, wg_ref, bf_ref, hn_ref, bd_ref,
                   tri_ref, qk_ref, vt_ref, dec_ref, cbase_ref, ga_ref, gb_ref, carry_ref):
    si = pl.program_id(1)
    tm = x_ref.shape[1]
    h = _modnorm(x_ref[0], g_ref[...], sc_ref[0], sh_ref[0]).astype(BF16)

    for c in range(4):
        z = _dot(h, wqk_ref[:, c * WIDTH:(c + 1) * WIDTH])
        sq = (z * z).astype(BF16)
        ms = jnp.concatenate(
            [_dot(sq[:, j * MXU_DIM:(j + 1) * MXU_DIM], bd_ref[...]) for j in range(WIDTH // MXU_DIM)],
            axis=1)
        z = z * lax.rsqrt(ms + EPS) * hn_ref[c:c + 1, :]
        qk_ref[c, 0] = z.astype(BF16)

    for c in range(2):
        vt = _dot_nt(wvt_ref[c * WIDTH:(c + 1) * WIDTH, :], h).astype(BF16)
        vt_ref[0, c * (N_HEADS // 2):(c + 1) * (N_HEADS // 2)] = vt.reshape(N_HEADS // 2, LANES, tm)

    fr = _dot_nt(wf_ref[...], h)[:N_HEADS]
    xg = fr + bf_ref[...]
    logf = jnp.minimum(xg, 0.0) - jnp.log(1.0 + jnp.exp(-jnp.abs(xg)))
    hi = logf.astype(BF16).astype(F32)
    r1 = logf - hi
    mid = r1.astype(BF16).astype(F32)
    lo = r1 - mid
    parts = jnp.concatenate([hi, mid, lo, jnp.zeros_like(hi)], axis=0)
    cs3 = _dot(parts.astype(BF16), tri_ref[...])
    cs = cs3[0:8] + cs3[8:16] + cs3[16:24]

    @pl.when(si == 0)
    def _():
        carry_ref[...] = jnp.zeros_like(carry_ref)

    drel = cs * (-LOG2E)
    dhi = drel.astype(BF16).astype(F32)
    dlo = drel - dhi
    dec = jnp.concatenate([dhi, dlo, jnp.zeros((LANES - 2 * N_HEADS, tm), F32)], axis=0)
    dec_ref[0] = dec.T.astype(BF16)
    carry = carry_ref[:, 0:1]
    cbase_ref[0, 0] = jnp.broadcast_to(carry * (-LOG2E), (N_HEADS, tm))
    carry_ref[...] = jnp.broadcast_to(carry + cs[:, tm - 1:tm], carry_ref.shape)

    for c in range(4):
        zg = _dot(h, wg_ref[:, c * WIDTH:(c + 1) * WIDTH])
        sg = _sigmoid(zg).astype(BF16)
        if c < 2:
            ga_ref[0, :, c * WIDTH:(c + 1) * WIDTH] = sg
        else:
            gb_ref[0, :, (c - 2) * WIDTH:(c - 1) * WIDTH] = sg


def _premix(x, sh, sc, g, wqk, wvt, wf, wg, bf, hn, bd, tri):
    B, S, D = x.shape
    tm = TM_PRE
    const = lambda b, s: (0, 0)
    return pl.pallas_call(
        _premix_kernel,
        grid=(B, S // tm),
        in_specs=[pl.BlockSpec((1, tm, D), lambda b, s: (b, s, 0)),
                  pl.BlockSpec((1, 1, D), lambda b, s: (b, 0, 0)),
                  pl.BlockSpec((1, 1, D), lambda b, s: (b, 0, 0)),
                  pl.BlockSpec((1, D), const),
                  pl.BlockSpec(wqk.shape, const),
                  pl.BlockSpec(wvt.shape, const),
                  pl.BlockSpec(wf.shape, const),
                  pl.BlockSpec(wg.shape, const),
                  pl.BlockSpec(bf.shape, const),
                  pl.BlockSpec(hn.shape, const),
                  pl.BlockSpec(bd.shape, const),
                  pl.BlockSpec(tri.shape, const)],
        out_specs=[pl.BlockSpec((4, 1, tm, WIDTH), lambda b, s: (0, b, s, 0)),
                   pl.BlockSpec((1, N_HEADS, LANES, tm), lambda b, s: (b, 0, 0, s)),
                   pl.BlockSpec((1, tm, LANES), lambda b, s: (b, s, 0)),
                   pl.BlockSpec((1, 1, N_HEADS, tm), lambda b, s: (b, s, 0, 0)),
                   pl.BlockSpec((1, tm, D), lambda b, s: (b, s, 0)),
                   pl.BlockSpec((1, tm, D), lambda b, s: (b, s, 0))],
        out_shape=[jax.ShapeDtypeStruct((4, B, S, WIDTH), BF16),
                   jax.ShapeDtypeStruct((B, N_HEADS, LANES, S), BF16),
                   jax.ShapeDtypeStruct((B, S, LANES), BF16),
                   jax.ShapeDtypeStruct((B, S // tm, N_HEADS, tm), F32),
                   jax.ShapeDtypeStruct((B, S, D), BF16),
                   jax.ShapeDtypeStruct((B, S, D), BF16)],
        scratch_shapes=[pltpu.VMEM((N_HEADS, LANES), F32)],
        compiler_params=_cparams(("arbitrary", "arbitrary")),
        name="premix",
    )(x, sh, sc, g, wqk, wvt, wf, wg, bf, hn, bd, tri)


def _values_with_ones(vt, hh):
    ones = jnp.ones((PV_ROWS - HEAD_DIM, vt.shape[1]), vt.dtype)
    return jnp.concatenate([vt[hh * HEAD_DIM:(hh + 1) * HEAD_DIM, :], ones], axis=0)


def _pair_finish_t(acc0, acc1):
    out_t = jnp.concatenate([acc[:HEAD_DIM] / acc[HEAD_DIM:HEAD_DIM + 1, :] for acc in (acc0, acc1)], axis=0)
    return out_t.T


def _band_kernel(q_ref, k0_ref, k1_ref, k2_ref, v0_ref, v1_ref, v2_ref, bias_ref, o_ref):
    qi = pl.program_id(2)
    tq = q_ref.shape[0]
    lane = lax.broadcasted_iota(jnp.int32, (1, LANES), 1)
    krow = lax.broadcasted_iota(jnp.int32, (BAND_WIN, 1), 0)
    kvalid = (krow + (qi * tq - LOOKBACK_CHUNKS * CHUNK)) >= 0
    k_refs = (k0_ref, k1_ref, k2_ref)
    v_refs = (v0_ref, v1_ref, v2_ref)
    heads = [(pp, hh) for pp in range(PAIRS_PER_STEP) for hh in range(2)]
    scores = []
    for pp, hh in heads:
        q = q_ref[:, pp * LANES:(pp + 1) * LANES]
        qm = jnp.where((lane // HEAD_DIM) == hh, q, jnp.zeros_like(q))
        scores.append(jnp.concatenate(
            [_dot_nt(k[:, pp * LANES:(pp + 1) * LANES], qm) for k in k_refs], axis=0))
    accs = []
    for n, (pp, hh) in enumerate(heads):
        s = jnp.where(kvalid, scores[n] + bias_ref[n], NEG_INF)
        m = jnp.max(s, axis=0, keepdims=True)
        p = jnp.exp2(s - m).astype(BF16)
        acc = None
        for j in range(3):
            t = _dot(_values_with_ones(v_refs[j][pp], hh), p[j * tq:(j + 1) * tq, :])
            acc = t if acc is None else acc + t
        accs.append(acc)
    for pp in range(PAIRS_PER_STEP):
        o_ref[:, pp * LANES:(pp + 1) * LANES] = _pair_finish_t(
            accs[2 * pp], accs[2 * pp + 1]).astype(o_ref.dtype)


def _band_attention(qk, vt, bias_t):
    _, B, S, _ = qk.shape
    tq = TQ_BAND
    nq = S // tq

    groups = N_HEADS // 2 // PAIRS_PER_STEP
    gw = PAIRS_PER_STEP * LANES

    def k_spec(back):
        return pl.BlockSpec((None, None, tq, gw),
                            lambda b, g, qi: (1, b, jnp.maximum(qi - back, 0), g))

    def v_spec(back):
        return pl.BlockSpec((None, PAIRS_PER_STEP, LANES, tq),
                            lambda b, g, qi: (b, g, 0, jnp.maximum(qi - back, 0)))

    return pl.pallas_call(
        _band_kernel,
        grid=(B, groups, nq),
        in_specs=[pl.BlockSpec((None, None, tq, gw), lambda b, g, qi: (0, b, qi, g)),
                  k_spec(2), k_spec(1), k_spec(0), v_spec(2), v_spec(1), v_spec(0),
                  pl.BlockSpec((2 * PAIRS_PER_STEP, BAND_WIN, tq), lambda b, g, qi: (g, 0, 0))],
        out_specs=pl.BlockSpec((None, tq, gw), lambda b, g, qi: (b, qi, g)),
        out_shape=jax.ShapeDtypeStruct((B, S, WIDTH), BF16),
        compiler_params=_cparams(("arbitrary", "arbitrary", "arbitrary")),
        name="band_attn",
    )(qk, qk, qk, qk, vt, vt, vt, bias_t)


def _band_bias(rel_table):
    pad = LOOKBACK_CHUNKS * CHUNK
    i = jnp.arange(TQ_BAND)[:, None]
    j = jnp.arange(BAND_WIN)[None, :]
    period = 1024
    assert BAND_WIN + TQ_BAND <= period
    m = jnp.arange(period)
    rel = jnp.where(m <= BAND_WIN, pad - m, pad + period - m)
    g = rel_table[:, jnp.clip(rel, -REL_CLIP, REL_CLIP) + REL_CLIP].astype(F32)
    bias = jnp.tile(g, (1, TQ_BAND))[:, :TQ_BAND * (period - 1)].reshape(-1, TQ_BAND, period - 1)[:, :, :BAND_WIN]
    c0 = (i // CHUNK) * CHUNK
    inband = (j >= c0) & (j < c0 + pad + CHUNK)
    return jnp.where(inband[None], bias * LOG2E, NEG_INF).transpose(0, 2, 1)


def _fox_kernel(qi_tab, kj_tab, q_ref, k_ref, vt_ref, dec_ref, cbase_ref, o_ref, acc_ref, m_ref):
    grp = pl.program_id(1)
    t = pl.program_id(2)
    qi = qi_tab[t]
    kj = kj_tab[t]
    tq, tk = q_ref.shape[0], k_ref.shape[0]
    lane2 = lax.broadcasted_iota(jnp.int32, (1, 2 * LANES), 1)
    heads = [(pp, hh) for pp in range(PAIRS_PER_STEP) for hh in range(2)]

    @pl.when(kj == 0)
    def _():
        acc_ref[...] = jnp.zeros_like(acc_ref)
        m_ref[...] = jnp.full_like(m_ref, NEG_INF)

    def step(diagonal):
        ones = jnp.where(lax.broadcasted_iota(jnp.int32, (tq, LANES), 1) < 2 * N_HEADS, 1.0, 0.0).astype(BF16)
        dec = dec_ref[...]
        if diagonal:
            causal = (lax.broadcasted_iota(jnp.int32, (tk, tq), 0)
                      <= lax.broadcasted_iota(jnp.int32, (tk, tq), 1))
        scores = []
        for pp, hh in heads:
            head = 2 * (PAIRS_PER_STEP * grp + pp) + hh
            q_aug = jnp.concatenate([q_ref[:, pp * LANES:(pp + 1) * LANES], ones], axis=1)
            k_aug = jnp.concatenate([k_ref[:, pp * LANES:(pp + 1) * LANES], dec], axis=1)
            use = (((lane2 // HEAD_DIM) == hh) | (lane2 == LANES + head) | (lane2 == LANES + N_HEADS + head))
            qm = jnp.where(use, q_aug, jnp.zeros_like(q_aug))
            scores.append(_dot_nt(k_aug, qm))
        for n, (pp, hh) in enumerate(heads):
            head = 2 * (PAIRS_PER_STEP * grp + pp) + hh
            s = scores[n]
            if diagonal:
                s = jnp.where(causal, s, NEG_INF)
            base = cbase_ref[pl.ds(head, 1), :]
            m_old = m_ref[n]
            m_new = jnp.maximum(m_old, jnp.max(s, axis=0, keepdims=True) + base)
            alpha = jnp.exp2(m_old - m_new)
            p = jnp.exp2(s - (m_new - base)).astype(BF16)
            acc_ref[n] = acc_ref[n] * alpha + _dot(_values_with_ones(vt_ref[pp], hh), p)
            m_ref[n] = m_new

    @pl.when(kj < qi)
    def _():
        step(False)

    @pl.when(kj == qi)
    def _():
        step(True)
        for pp in range(PAIRS_PER_STEP):
            o_ref[:, pp * LANES:(pp + 1) * LANES] = _pair_finish_t(
                acc_ref[2 * pp], acc_ref[2 * pp + 1]).astype(o_ref.dtype)


def _fox_attention(qkv, vt, dec, cbase):
    _, B, S, _ = qkv.shape
    tq = TQ_FOX
    nq = S // tq
    groups = N_HEADS // 2 // PAIRS_PER_STEP
    gw = PAIRS_PER_STEP * LANES
    pairs = [(i, j) for i in range(nq) for j in range(i + 1)]
    qi_tab = jnp.array([p[0] for p in pairs], jnp.int32)
    kj_tab = jnp.array([p[1] for p in pairs], jnp.int32)
    grid_spec = pltpu.PrefetchScalarGridSpec(
        num_scalar_prefetch=2,
        grid=(B, groups, len(pairs)),
        in_specs=[pl.BlockSpec((None, None, tq, gw), lambda b, g, t, qt, kt: (2, b, qt[t], g)),
                  pl.BlockSpec((None, None, tq, gw), lambda b, g, t, qt, kt: (3, b, kt[t], g)),
                  pl.BlockSpec((None, PAIRS_PER_STEP, LANES, tq),
                               lambda b, g, t, qt, kt: (b, groups + g, 0, kt[t])),
                  pl.BlockSpec((None, tq, LANES), lambda b, g, t, qt, kt: (b, kt[t], 0)),
                  pl.BlockSpec((None, None, N_HEADS, tq), lambda b, g, t, qt, kt: (b, kt[t], 0, 0))],
        out_specs=pl.BlockSpec((None, tq, gw), lambda b, g, t, qt, kt: (b, qt[t], g)),
        scratch_shapes=[pltpu.VMEM((2 * PAIRS_PER_STEP, PV_ROWS, tq), F32),
                        pltpu.VMEM((2 * PAIRS_PER_STEP, 1, tq), F32)],
    )
    return pl.pallas_call(
        _fox_kernel,
        grid_spec=grid_spec,
        out_shape=jax.ShapeDtypeStruct((B, S, WIDTH), BF16),
        compiler_params=_cparams(("arbitrary", "arbitrary", "arbitrary")),
        name="fox_attn",
    )(qi_tab, kj_tab, qkv, qkv, vt, dec, cbase)


def _postmix_kernel(ya_ref, yb_ref, ga_ref, gb_ref, x_ref, g1_ref, wpa_ref, wpb_ref, wo_ref,
                    g_ref, sh_ref, sc_ref, wr_ref, rb_ref, ustrict_ref, lstrict_ref,
                    x1_ref, h2_ref, posk_ref, wk_ref, cnt_ref):
    ua = _dot(ya_ref[0], wpa_ref[...])
    ub = _dot(yb_ref[0], wpb_ref[...])
    m = ga_ref[0].astype(F32) * ua + gb_ref[0].astype(F32) * ub
    mo = _dot(m.astype(BF16), wo_ref[...])
    x1 = x_ref[0] + g1_ref[0] * mo
    x1_ref[0] = x1
    h2 = _modnorm(x1, g_ref[...], sc_ref[0], sh_ref[0]).astype(BF16)
    h2_ref[0] = h2
    for j in range(h2.shape[0] // TM_MOE):
        posk, wk, cnt = _route(h2[j * TM_MOE:(j + 1) * TM_MOE], wr_ref[...], rb_ref[...],
                               ustrict_ref[...], lstrict_ref[...])
        posk_ref[j] = posk
        wk_ref[j] = wk
        cnt_ref[j] = cnt


def _postmix(ya, yb, ga, gb, x, g1, wpa, wpb, wo, g, sh, sc, wr_t, rb, ustrict, lstrict):
    B, S, D = x.shape
    tm = TM_PRE
    sub_tiles = tm // TM_MOE
    nt = B * S // TM_MOE
    const = lambda b, s: (0, 0)
    tok = lambda w: pl.BlockSpec((1, tm, w), lambda b, s: (b, s, 0))
    row = pl.BlockSpec((1, 1, D), lambda b, s: (b, 0, 0))
    tiles = lambda r, c: pl.BlockSpec((sub_tiles, r, c), lambda b, s: (b * (S // tm) + s, 0, 0))
    return pl.pallas_call(
        _postmix_kernel,
        grid=(B, S // tm),
        in_specs=[tok(WIDTH), tok(WIDTH), tok(D), tok(D), tok(D), row,
                  pl.BlockSpec(wpa.shape, const), pl.BlockSpec(wpb.shape, const),
                  pl.BlockSpec(wo.shape, const), pl.BlockSpec((1, D), const), row, row,
                  pl.BlockSpec(wr_t.shape, const), pl.BlockSpec(rb.shape, const),
                  pl.BlockSpec(ustrict.shape, const), pl.BlockSpec(lstrict.shape, const)],
        out_specs=[tok(D), tok(D), tiles(TOP_K, TM_MOE), tiles(TOP_K, TM_MOE), tiles(N_EXPERTS, LANES)],
        out_shape=[jax.ShapeDtypeStruct((B, S, D), F32), jax.ShapeDtypeStruct((B, S, D), BF16),
                   jax.ShapeDtypeStruct((nt, TOP_K, TM_MOE), F32),
                   jax.ShapeDtypeStruct((nt, TOP_K, TM_MOE), F32),
                   jax.ShapeDtypeStruct((nt, N_EXPERTS, LANES), F32)],
        compiler_params=_cparams(("arbitrary", "arbitrary")),
        name="postmix",
    )(ya, yb, ga, gb, x, g1, wpa, wpb, wo, g, sh, sc, wr_t, rb, ustrict, lstrict)


def _route(h, wr, rb, ustrict, lstrict):
    tm = h.shape[0]
    per_group = N_EXPERTS // N_GROUPS
    logits = _dot_nt(wr, h)
    scores = _sigmoid(logits)
    biased = scores + rb
    sub8 = lax.broadcasted_iota(jnp.int32, (per_group, tm), 0).astype(F32)
    neg = -jnp.inf

    grp_rows = []
    for g in range(N_GROUPS):
        a = biased[g * per_group:(g + 1) * per_group]
        m1 = jnp.max(a, axis=0, keepdims=True)
        i1 = jnp.min(jnp.where(a == m1, sub8, float(per_group)), axis=0, keepdims=True)
        m2 = jnp.max(jnp.where(sub8 == i1, neg, a), axis=0, keepdims=True)
        grp_rows.append(m1 + m2)
    grp = jnp.concatenate(grp_rows, axis=0)

    gsub = lax.broadcasted_iota(jnp.int32, (N_GROUPS, tm), 0).astype(F32)
    gmask = jnp.zeros((N_GROUPS, tm), F32)
    for _ in range(TOPK_GROUPS):
        mx = jnp.max(grp, axis=0, keepdims=True)
        gi = jnp.min(jnp.where(grp == mx, gsub, float(N_GROUPS)), axis=0, keepdims=True)
        sel = gsub == gi
        gmask = jnp.where(sel, 1.0, gmask)
        grp = jnp.where(sel, neg, grp)
    masked = jnp.concatenate(
        [jnp.where(gmask[g:g + 1] > 0.5, biased[g * per_group:(g + 1) * per_group], neg)
         for g in range(N_GROUPS)], axis=0)

    esub = lax.broadcasted_iota(jnp.int32, (N_EXPERTS, tm), 0).astype(F32)
    sels, ws = [], []
    for _ in range(TOP_K):
        mx = jnp.max(masked, axis=0, keepdims=True)
        ei = jnp.min(jnp.where(masked == mx, esub, float(N_EXPERTS)), axis=0, keepdims=True)
        sel = esub == ei
        sels.append(sel)
        ws.append(jnp.sum(jnp.where(sel, scores, 0.0), axis=0, keepdims=True))
        masked = jnp.where(sel, neg, masked)
    wsum = ws[0]
    for w in ws[1:]:
        wsum = wsum + w

    selmask = jnp.zeros((N_EXPERTS, tm), F32)
    for sel in sels:
        selmask = selmask + jnp.where(sel, 1.0, 0.0)
    rank = _dot(selmask.astype(BF16), ustrict)
    counts = jnp.sum(selmask, axis=1, keepdims=True)
    padded = jnp.ceil(counts * (1.0 / RUN_ALIGN)) * RUN_ALIGN
    padded_b = jnp.broadcast_to(padded, (N_EXPERTS, LANES))
    base = _dot(lstrict, padded_b.astype(BF16))
    pos = base[:, 0:1] + rank

    posk = jnp.concatenate([jnp.sum(jnp.where(sel, pos, 0.0), axis=0, keepdims=True) for sel in sels], axis=0)
    wk = jnp.concatenate([(w / wsum) * ROUTED_SCALE for w in ws], axis=0)
    return posk, wk, padded_b


def _rows_copy(m, src_ref, src0, dst_ref, dst0, sem):
    rows = m * RUN_ALIGN
    return pltpu.make_async_copy(src_ref.at[pl.ds(pl.multiple_of(src0, RUN_ALIGN), rows)],
                                 dst_ref.at[pl.ds(pl.multiple_of(dst0, RUN_ALIGN), rows)], sem)


def _start_rows(m, src_ref, src0, dst_ref, dst0, sem):
    @pl.when(m > 0)
    def _():
        _rows_copy(m, src_ref, src0, dst_ref, dst0, sem).start()


def _wait_rows(m, src_ref, dst_ref, sem):
    @pl.when(m > 0)
    def _():
        _rows_copy(m, src_ref, 0, dst_ref, 0, sem).wait()


def _dispatch_kernel(off_ref, m_ref, base_ref, tot_ref, gapoff_ref, gapm_ref, h_ref, posk_ref, xs_hbm,
                     xs_ref, zero_ref, sem):
    i = pl.program_id(0)
    tm = h_ref.shape[0]
    n_chunks = K_SORT // MXU_DIM

    @pl.when(i == 0)
    def _():
        zero_ref[...] = jnp.zeros_like(zero_ref)

        def gap(e, carry):
            _start_rows(gapm_ref[e], zero_ref, 0, xs_hbm, gapoff_ref[e], sem)
            return carry
        lax.fori_loop(0, N_EXPERTS, gap, 0)

        def gap_wait(e, carry):
            _wait_rows(gapm_ref[e], zero_ref, xs_hbm, sem)
            return carry
        lax.fori_loop(0, N_EXPERTS, gap_wait, 0)

    slot = i % 2
    xs_slot = xs_ref.at[slot]
    h = h_ref[...]
    posk = posk_ref[0]
    rows = lax.broadcasted_iota(jnp.int32, (MXU_DIM, tm), 0).astype(F32).astype(BF16)
    one = jnp.ones((MXU_DIM, tm), BF16)
    def sort_chunks(lo, hi):
        for c in range(lo, hi):
            rel = posk - float(c * MXU_DIM)
            p = jnp.zeros((MXU_DIM, tm), BF16)
            for k in range(TOP_K):
                p = jnp.where(rows == rel[k:k + 1, :].astype(BF16), one, p)
            xs_slot[c * MXU_DIM:(c + 1) * MXU_DIM, :] = _dot(p, h).astype(BF16)

    sort_chunks(0, K_SORT_COMMON // MXU_DIM)

    @pl.when(tot_ref[i] * RUN_ALIGN > K_SORT_COMMON)
    def _():
        sort_chunks(K_SORT_COMMON // MXU_DIM, n_chunks)

    @pl.when(i > 0)
    def _():
        _wait_rows(tot_ref[i - 1], xs_ref.at[1 - slot], xs_hbm, sem)

    def run(e, carry):
        idx = i * N_EXPERTS + e
        _start_rows(m_ref[idx], xs_slot, base_ref[idx], xs_hbm, off_ref[idx], sem)
        return carry
    lax.fori_loop(0, N_EXPERTS, run, 0)

    @pl.when(i == pl.num_programs(0) - 1)
    def _():
        _wait_rows(tot_ref[i], xs_slot, xs_hbm, sem)


def _dispatch(off, m16, base, tot, gapoff, gapm, h2, posk, n_rows):
    T, D = h2.shape
    tm = TM_MOE
    grid_spec = pltpu.PrefetchScalarGridSpec(
        num_scalar_prefetch=6,
        grid=(T // tm,),
        in_specs=[pl.BlockSpec((tm, D), lambda i, *_: (i, 0)),
                  pl.BlockSpec((1, TOP_K, tm), lambda i, *_: (i, 0, 0))],
        out_specs=pl.BlockSpec(memory_space=pl.ANY),
        scratch_shapes=[pltpu.VMEM((2, K_SORT, D), BF16),
                        pltpu.VMEM((BLOCK_ROWS, D), BF16),
                        pltpu.SemaphoreType.DMA],
    )
    return pl.pallas_call(
        _dispatch_kernel,
        grid_spec=grid_spec,
        out_shape=jax.ShapeDtypeStruct((n_rows, D), BF16),
        compiler_params=_cparams(("arbitrary",), unchecked_dma=True),
        name="dispatch",
    )(off, m16, base, tot, gapoff, gapm, h2, posk)


def _expert_kernel(bexp_ref, nvalid_ref, x_ref, wg_ref, wu_ref, wd_ref, y_ref, wgu_bf, wd_bf):
    i = pl.program_id(0)

    @pl.when(i < nvalid_ref[0])
    def _():
        @pl.when((i == 0) | (bexp_ref[i] != bexp_ref[jnp.maximum(i - 1, 0)]))
        def _():
            wgu_bf[:, :EXPERT_FF] = wg_ref[...].astype(BF16)
            wgu_bf[:, EXPERT_FF:] = wu_ref[...].astype(BF16)
            wd_bf[...] = wd_ref[...].astype(BF16)

        gu = _dot(x_ref[...], wgu_bf[...])
        g = gu[:, :EXPERT_FF]
        a = (g * _sigmoid(g) * gu[:, EXPERT_FF:]).astype(BF16)
        y_ref[...] = _dot(a, wd_bf[...]).astype(y_ref.dtype)


def _experts(bexp, nvalid, xs, wg, wu, wd, layer):
    n_rows, D = xs.shape
    nb = n_rows // BLOCK_ROWS

    def blk(i, be, nv):
        return jnp.minimum(i, nv[0] - 1)

    wspec = lambda r, c: pl.BlockSpec((None, None, r, c),
                                      lambda i, be, nv: (layer, be[blk(i, be, nv)], 0, 0))
    grid_spec = pltpu.PrefetchScalarGridSpec(
        num_scalar_prefetch=2,
        grid=(nb,),
        in_specs=[pl.BlockSpec((BLOCK_ROWS, D), lambda i, be, nv: (blk(i, be, nv), 0)),
                  wspec(D, EXPERT_FF), wspec(D, EXPERT_FF), wspec(EXPERT_FF, D)],
        out_specs=pl.BlockSpec((BLOCK_ROWS, D), lambda i, be, nv: (blk(i, be, nv), 0)),
        scratch_shapes=[pltpu.VMEM((D, 2 * EXPERT_FF), BF16), pltpu.VMEM((EXPERT_FF, D), BF16)],
    )
    return pl.pallas_call(
        _expert_kernel,
        grid_spec=grid_spec,
        out_shape=jax.ShapeDtypeStruct((n_rows, D), BF16),
        compiler_params=_cparams(("arbitrary",)),
        name="experts",
    )(bexp, nvalid, xs, wg, wu, wd)


def _combine_kernel(off_ref, m_ref, base_ref, tot_ref, ys_hbm, posk_ref, wk_ref, h_ref, x_ref, g2_ref,
                    wsgu_ref, wsd_ref, o_ref, ys_ref, pw_ref, sem):
    i = pl.program_id(0)
    tm = h_ref.shape[0]
    slot = i % 2

    def fetch(tile, buf_slot):
        def run(e, carry):
            idx = tile * N_EXPERTS + e
            _start_rows(m_ref[idx], ys_hbm, off_ref[idx], ys_ref.at[buf_slot], base_ref[idx],
                        sem.at[buf_slot])
            return carry
        lax.fori_loop(0, N_EXPERTS, run, 0)

    @pl.when(i == 0)
    def _():
        ys_ref[...] = jnp.zeros_like(ys_ref)
        fetch(i, slot)

    @pl.when(i + 1 < pl.num_programs(0))
    def _():
        fetch(i + 1, 1 - slot)

    h = h_ref[...]
    gu = _dot(h, wsgu_ref[...])
    g = gu[:, :EXPERT_FF]
    shared = _dot((g * _sigmoid(g) * gu[:, EXPERT_FF:]).astype(BF16), wsd_ref[...])

    posk = posk_ref[0]
    wk = wk_ref[0]
    rows = lax.broadcasted_iota(jnp.int32, (MXU_DIM, tm), 0).astype(F32).astype(BF16)

    def weight_chunks(lo, hi):
        for c in range(lo, hi):
            rel = posk - float(c * MXU_DIM)
            pw = jnp.zeros((MXU_DIM, tm), BF16)
            for k in range(TOP_K):
                wrow = jnp.broadcast_to(wk[k:k + 1, :], (MXU_DIM, tm)).astype(BF16)
                pw = jnp.where(rows == rel[k:k + 1, :].astype(BF16), wrow, pw)
            pw_ref[c * MXU_DIM:(c + 1) * MXU_DIM, :] = pw

    def unsort(lo, hi):
        return lax.dot_general(pw_ref[lo:hi, :], ys_ref[slot, lo:hi, :], (((0,), (0,)), ((), ())),
                               preferred_element_type=F32)

    weight_chunks(0, K_SORT_COMMON // MXU_DIM)
    _wait_rows(tot_ref[i], ys_hbm, ys_ref.at[slot], sem.at[slot])
    o_ref[...] = x_ref[...] + g2_ref[0] * (unsort(0, K_SORT_COMMON) + shared)

    @pl.when(tot_ref[i] * RUN_ALIGN > K_SORT_COMMON)
    def _():
        weight_chunks(K_SORT_COMMON // MXU_DIM, K_SORT // MXU_DIM)
        o_ref[...] = o_ref[...] + g2_ref[0] * unsort(K_SORT_COMMON, K_SORT)


def _combine(off, m16, base, tot, ys, posk, wk, h2, x1, g2, wsgu, wsd, tiles_per_batch):
    T, D = h2.shape
    tm = TM_MOE
    const = lambda i, *_: (0, 0)
    tok = lambda w: pl.BlockSpec((tm, w), lambda i, *_: (i, 0))
    sel = pl.BlockSpec((1, TOP_K, tm), lambda i, *_: (i, 0, 0))
    grid_spec = pltpu.PrefetchScalarGridSpec(
        num_scalar_prefetch=4,
        grid=(T // tm,),
        in_specs=[pl.BlockSpec(memory_space=pl.ANY), sel, sel, tok(D), tok(D),
                  pl.BlockSpec((1, 1, D), lambda i, *_: (i // tiles_per_batch, 0, 0)),
                  pl.BlockSpec(wsgu.shape, const), pl.BlockSpec(wsd.shape, const)],
        out_specs=tok(D),
        scratch_shapes=[pltpu.VMEM((2, K_SORT, D), BF16),
                        pltpu.VMEM((K_SORT, tm), BF16),
                        pltpu.SemaphoreType.DMA((2,))],
    )
    return pl.pallas_call(
        _combine_kernel,
        grid_spec=grid_spec,
        out_shape=jax.ShapeDtypeStruct((T, D), F32),
        compiler_params=_cparams(("arbitrary",), unchecked_dma=True),
        name="combine",
    )(off, m16, base, tot, ys, posk, wk, h2, x1, g2, wsgu, wsd)


def _moe(h2, x1, g2, posk, wk, cnt, wg, wu, wd, layer, wsgu, wsd, tiles_per_batch):
    T, D = h2.shape
    nt = T // TM_MOE

    pad = cnt[:, :, 0].astype(jnp.int32)
    base = jnp.cumsum(pad, axis=1) - pad
    tile_off = jnp.cumsum(pad, axis=0) - pad
    total = jnp.sum(pad, axis=0)
    region = ((total + BLOCK_ROWS - 1) // BLOCK_ROWS) * BLOCK_ROWS
    rend = jnp.cumsum(region)
    rstart = rend - region
    off = rstart[None, :] + tile_off
    nb_max = -(-(TOP_K * T + nt * N_EXPERTS * (RUN_ALIGN - 1)) // BLOCK_ROWS) + N_EXPERTS
    blk_row = jnp.arange(nb_max, dtype=jnp.int32) * BLOCK_ROWS
    bexp = jnp.minimum(jnp.sum((rend[None, :] <= blk_row[:, None]).astype(jnp.int32), axis=1), N_EXPERTS - 1)
    nvalid = (rend[-1:] // BLOCK_ROWS).astype(jnp.int32)
    flat = lambda a: a.reshape(-1).astype(jnp.int32)

    m16 = flat(pad // RUN_ALIGN)
    tot = flat(jnp.sum(pad, axis=1) // RUN_ALIGN)
    xs = _dispatch(flat(off), m16, flat(base), tot, flat(rstart + total),
                   flat((region - total) // RUN_ALIGN), h2, posk, nb_max * BLOCK_ROWS)
    ys = _experts(bexp, nvalid, xs, wg, wu, wd, layer)
    return _combine(flat(off), m16, flat(base), tot, ys, posk, wk, h2, x1, g2, wsgu, wsd, tiles_per_batch)


def kernel(x, c, ada_w, ada_b, mix_norm_g, w_in, b_fgate, qn_a, kn_a, qn_b, kn_b, rel_bias, w_proj_a,
           w_proj_b, w_out, ffn_norm_g, w_router, router_bias, w_gate_e, w_up_e, w_down_e, w_gate_s,
           w_up_s, w_down_s):
    B, S, D = x.shape
    L = ada_w.shape[0]
    T = B * S
    assert D == D_MODEL and S % TM_PRE == 0 and TQ_FOX == TM_PRE and S % TM_MOE == 0 and S % TQ_BAND == 0

    mod = _adaln(c, ada_w, ada_b).reshape(L, B, 6, 1, D)

    hid = jnp.arange(MXU_DIM) // HEAD_DIM
    bd = jnp.where(hid[:, None] == hid[None, :], 1.0 / HEAD_DIM, 0.0).astype(BF16)
    r = jnp.arange(TM_PRE)
    tri = (r[:, None] <= r[None, :]).astype(BF16)
    r = jnp.arange(TM_MOE)
    ustrict = (r[:, None] < r[None, :]).astype(BF16)
    r = jnp.arange(N_EXPERTS)
    lstrict = (r[None, :] < r[:, None]).astype(BF16)

    q_scale = 1.0 / math.sqrt(HEAD_DIM)
    for l in range(L):
        sh1, sc1, g1, sh2, sc2, g2 = (mod[l, :, j] for j in range(6))
        w = w_in[l]
        cols = lambda c: w[:, c * WIDTH:(c + 1) * WIDTH]
        wqk = jnp.concatenate([cols(0), cols(1), cols(3), cols(4)], axis=1).astype(BF16)
        wvt = jnp.concatenate([cols(2), cols(5)], axis=1).T.astype(BF16)
        wf = jnp.zeros((2 * N_HEADS, D), BF16).at[:N_HEADS].set(w[:, 6 * WIDTH:6 * WIDTH + N_HEADS].T.astype(BF16))
        wg = w[:, 6 * WIDTH + N_HEADS:].astype(BF16)
        hn = jnp.zeros((8, WIDTH), F32)
        hn = hn.at[0].set(jnp.tile(qn_a[l], N_HEADS) * (q_scale * LOG2E)).at[1].set(jnp.tile(kn_a[l], N_HEADS))
        hn = hn.at[2].set(jnp.tile(qn_b[l], N_HEADS) * (q_scale * LOG2E)).at[3].set(jnp.tile(kn_b[l], N_HEADS))

        qk, vt, dec, cbase, ga, gb = _premix(x, sh1, sc1, mix_norm_g[l][None], wqk, wvt, wf, wg,
                                             b_fgate[l][:, None], hn, bd, tri)
        ya = _band_attention(qk, vt, _band_bias(rel_bias[l]))
        yb = _fox_attention(qk, vt, dec, cbase)
        x1, h2, posk, wk, cnt = _postmix(
            ya, yb, ga, gb, x, g1, w_proj_a[l].astype(BF16), w_proj_b[l].astype(BF16),
            w_out[l].astype(BF16), ffn_norm_g[l][None], sh2, sc2, w_router[l].T.astype(BF16),
            router_bias[l][:, None], ustrict, lstrict)

        wsgu = jnp.concatenate([w_gate_s[l], w_up_s[l]], axis=-1).astype(BF16)
        x = _moe(h2.reshape(T, D), x1.reshape(T, D), g2, posk, wk, cnt, w_gate_e, w_up_e, w_down_e, l,
                 wsgu, w_down_s[l].astype(BF16), S // TM_MOE).reshape(B, S, D)
    return x
```

```python
import functools
import math

import jax
import jax.numpy as jnp
from jax import lax
from jax.experimental import pallas as pl
from jax.experimental.pallas import tpu as pltpu

F32 = jnp.float32
BF16 = jnp.bfloat16

D_MODEL = 1024
HEAD_DIM = 64
N_HEADS = 8
WIDTH = N_HEADS * HEAD_DIM
CHUNK = 64
LOOKBACK_CHUNKS = 8
REL_CLIP = 128
N_EXPERTS = 64
TOP_K = 8
N_GROUPS = 8
TOPK_GROUPS = 4
EXPERT_FF = 256
ROUTED_SCALE = 2.5
EPS = 1e-6
NEG_INF = -1e30
LOG2E = math.log2(math.e)

LANES = 128
MXU_DIM = 256
VMEM_LIMIT = 56 * 1024 * 1024

TM_PRE = 512
TQ_BAND = 256
BAND_WIN = TQ_BAND + LOOKBACK_CHUNKS * CHUNK
BAND_TILES = LOOKBACK_CHUNKS * CHUNK // TQ_BAND
TQ_FOX = 512
PAIRS_PER_STEP = 4
PV_ROWS = HEAD_DIM + 16
TM_MOE = 256
RUN_ALIGN = 16
K_SORT = ((TOP_K * TM_MOE + N_EXPERTS * (RUN_ALIGN - 1) + MXU_DIM - 1) // MXU_DIM) * MXU_DIM
K_SORT_COMMON = 2560
BLOCK_ROWS = 1024


def _dot(a, b):
    return jnp.dot(a, b, preferred_element_type=F32)


def _dot_nt(a, b):
    return lax.dot_general(a, b, (((1,), (1,)), ((), ())), preferred_element_type=F32)


def _sigmoid(v):
    return 1.0 / (1.0 + jnp.exp(-v))


def _cparams(sem, unchecked_dma=False):
    return pltpu.CompilerParams(dimension_semantics=sem, vmem_limit_bytes=VMEM_LIMIT,
                                disable_bounds_checks=unchecked_dma)


def _adaln_kernel(c_ref, w_ref, b_ref, o_ref):
    c = c_ref[...]
    ca = c * _sigmoid(c)
    o_ref[0] = jnp.dot(ca, w_ref[0], preferred_element_type=F32,
                       precision=lax.Precision.HIGHEST) + b_ref[0]


def _adaln(c, ada_w, ada_b):
    L, D, N = ada_w.shape
    B = c.shape[0]
    tn = 1536
    return pl.pallas_call(
        _adaln_kernel,
        grid=(L, N // tn),
        in_specs=[pl.BlockSpec((B, D), lambda l, j: (0, 0)),
                  pl.BlockSpec((1, D, tn), lambda l, j: (l, 0, j)),
                  pl.BlockSpec((1, 1, tn), lambda l, j: (l, 0, j))],
        out_specs=pl.BlockSpec((1, B, tn), lambda l, j: (l, 0, j)),
        out_shape=jax.ShapeDtypeStruct((L, B, N), F32),
        compiler_params=_cparams(("arbitrary", "arbitrary")),
        name="adaln",
    )(c, ada_w, ada_b.reshape(L, 1, N))


def _modnorm(x, g, sc, sh):
    ms = jnp.mean(x * x, axis=-1, keepdims=True)
    y = x * lax.rsqrt(ms + EPS) * g
    return y * (1.0 + sc) + sh


def _premix_kernel(x_ref, sh_ref, sc_ref, g_ref, wqk_ref, wvt_ref, wf_ref, wg_ref, bf_ref, hn_ref, bd_ref,
                   tri_ref, qk_ref, vt_ref, dec_ref, cbase_ref, ga_ref, gb_ref, carry_ref):
    si = pl.program_id(1)
    tm = x_ref.shape[1]
    h = _modnorm(x_ref[0], g_ref[...], sc_ref[0], sh_ref[0]).astype(BF16)

    for c in range(4):
        z = _dot(h, wqk_ref[:, c * WIDTH:(c + 1) * WIDTH])
        sq = (z * z).astype(BF16)
        ms = jnp.concatenate(
            [_dot(sq[:, j * MXU_DIM:(j + 1) * MXU_DIM], bd_ref[...]) for j in range(WIDTH // MXU_DIM)],
            axis=1)
        z = z * lax.rsqrt(ms + EPS) * hn_ref[c:c + 1, :]
        qk_ref[c, 0] = z.astype(BF16)

    for c in range(2):
        vt = _dot_nt(wvt_ref[c * WIDTH:(c + 1) * WIDTH, :], h).astype(BF16)
        vt_ref[0, c * (N_HEADS // 2):(c + 1) * (N_HEADS // 2)] = vt.reshape(N_HEADS // 2, LANES, tm)

    fr = _dot_nt(wf_ref[...], h)[:N_HEADS]
    xg = fr + bf_ref[...]
    logf = jnp.minimum(xg, 0.0) - jnp.log(1.0 + jnp.exp(-jnp.abs(xg)))
    hi = logf.astype(BF16).astype(F32)
    r1 = logf - hi
    mid = r1.astype(BF16).astype(F32)
    lo = r1 - mid
    parts = jnp.concatenate([hi, mid, lo, jnp.zeros_like(hi)], axis=0)
    cs3 = _dot(parts.astype(BF16), tri_ref[...])
    cs = cs3[0:8] + cs3[8:16] + cs3[16:24]

    @pl.when(si == 0)
    def _():
        carry_ref[...] = jnp.zeros_like(carry_ref)

    drel = cs * (-LOG2E)
    dhi = drel.astype(BF16).astype(F32)
    dlo = drel - dhi
    dec = jnp.concatenate([dhi, dlo, jnp.zeros((LANES - 2 * N_HEADS, tm), F32)], axis=0)
    dec_ref[0] = dec.T.astype(BF16)
    carry = carry_ref[:, 0:1]
    cbase_ref[0, 0] = jnp.broadcast_to(carry * (-LOG2E), (N_HEADS, tm))
    carry_ref[...] = jnp.broadcast_to(carry + cs[:, tm - 1:tm], carry_ref.shape)

    for c in range(4):
        zg = _dot(h, wg_ref[:, c * WIDTH:(c + 1) * WIDTH])
        sg = _sigmoid(zg).astype(BF16)
        if c < 2:
            ga_ref[0, :, c * WIDTH:(c + 1) * WIDTH] = sg
        else:
            gb_ref[0, :, (c - 2) * WIDTH:(c - 1) * WIDTH] = sg


def _premix(x, sh, sc, g, wqk, wvt, wf, wg, bf, hn, bd, tri):
    B, S, D = x.shape
    tm = TM_PRE
    const = lambda b, s: (0, 0)
    return pl.pallas_call(
        _premix_kernel,
        grid=(B, S // tm),
        in_specs=[pl.BlockSpec((1, tm, D), lambda b, s: (b, s, 0)),
                  pl.BlockSpec((1, 1, D), lambda b, s: (b, 0, 0)),
                  pl.BlockSpec((1, 1, D), lambda b, s: (b, 0, 0)),
                  pl.BlockSpec((1, D), const),
                  pl.BlockSpec(wqk.shape, const),
                  pl.BlockSpec(wvt.shape, const),
                  pl.BlockSpec(wf.shape, const),
                  pl.BlockSpec(wg.shape, const),
                  pl.BlockSpec(bf.shape, const),
                  pl.BlockSpec(hn.shape, const),
                  pl.BlockSpec(bd.shape, const),
                  pl.BlockSpec(tri.shape, const)],
        out_specs=[pl.BlockSpec((4, 1, tm, WIDTH), lambda b, s: (0, b, s, 0)),
                   pl.BlockSpec((1, N_HEADS, LANES, tm), lambda b, s: (b, 0, 0, s)),
                   pl.BlockSpec((1, tm, LANES), lambda b, s: (b, s, 0)),
                   pl.BlockSpec((1, 1, N_HEADS, tm), lambda b, s: (b, s, 0, 0)),
                   pl.BlockSpec((1, tm, D), lambda b, s: (b, s, 0)),
                   pl.BlockSpec((1, tm, D), lambda b, s: (b, s, 0))],
        out_shape=[jax.ShapeDtypeStruct((4, B, S, WIDTH), BF16),
                   jax.ShapeDtypeStruct((B, N_HEADS, LANES, S), BF16),
                   jax.ShapeDtypeStruct((B, S, LANES), BF16),
                   jax.ShapeDtypeStruct((B, S // tm, N_HEADS, tm), F32),
                   jax.ShapeDtypeStruct((B, S, D), BF16),
                   jax.ShapeDtypeStruct((B, S, D), BF16)],
        scratch_shapes=[pltpu.VMEM((N_HEADS, LANES), F32)],
        compiler_params=_cparams(("arbitrary", "arbitrary")),
        name="premix",
    )(x, sh, sc, g, wqk, wvt, wf, wg, bf, hn, bd, tri)


def _values_with_ones(vt, hh):
    ones = jnp.ones((PV_ROWS - HEAD_DIM, vt.shape[1]), vt.dtype)
    return jnp.concatenate([vt[hh * HEAD_DIM:(hh + 1) * HEAD_DIM, :], ones], axis=0)


def _pair_finish_t(acc0, acc1):
    out_t = jnp.concatenate([acc[:HEAD_DIM] / acc[HEAD_DIM:HEAD_DIM + 1, :] for acc in (acc0, acc1)], axis=0)
    return out_t.T


def _band_kernel(q_ref, kprev_ref, kcur_ref, vprev_ref, vcur_ref, bias_ref, o_ref):
    tq = TQ_BAND
    step_start = pl.program_id(2) * (BAND_TILES * tq)
    lane = lax.broadcasted_iota(jnp.int32, (1, LANES), 1)
    krow = lax.broadcasted_iota(jnp.int32, (BAND_WIN, 1), 0)
    pieces = [(ref, j) for ref in (kprev_ref, kcur_ref) for j in range(BAND_TILES)]
    vpieces = [(ref, j) for ref in (vprev_ref, vcur_ref) for j in range(BAND_TILES)]
    units = [(u, pp, hh) for u in range(BAND_TILES) for pp in range(PAIRS_PER_STEP) for hh in range(2)]
    scores = []
    for u, pp, hh in units:
        q = q_ref[u * tq:(u + 1) * tq, pp * LANES:(pp + 1) * LANES]
        qm = jnp.where((lane // HEAD_DIM) == hh, q, jnp.zeros_like(q))
        scores.append(jnp.concatenate(
            [_dot_nt(ref[j * tq:(j + 1) * tq, pp * LANES:(pp + 1) * LANES], qm)
             for ref, j in pieces[u:u + 3]], axis=0))
    accs = []
    for n, (u, pp, hh) in enumerate(units):
        kvalid = (krow + (step_start + u * tq - LOOKBACK_CHUNKS * CHUNK)) >= 0
        s = jnp.where(kvalid, scores[n] + bias_ref[2 * pp + hh], NEG_INF)
        m = jnp.max(s, axis=0, keepdims=True)
        p = jnp.exp2(s - m).astype(BF16)
        acc = None
        for c, (ref, j) in enumerate(vpieces[u:u + 3]):
            t = _dot(_values_with_ones(ref[pp, :, j * tq:(j + 1) * tq], hh), p[c * tq:(c + 1) * tq, :])
            acc = t if acc is None else acc + t
        accs.append(acc)
    for n in range(0, len(units), 2):
        u, pp, _ = units[n]
        o_ref[u * tq:(u + 1) * tq, pp * LANES:(pp + 1) * LANES] = _pair_finish_t(
            accs[n], accs[n + 1]).astype(o_ref.dtype)


def _band_attention(qk, vt, bias_t):
    _, B, S, _ = qk.shape
    rows = BAND_TILES * TQ_BAND
    groups = N_HEADS // 2 // PAIRS_PER_STEP
    gw = PAIRS_PER_STEP * LANES

    def k_spec(back):
        return pl.BlockSpec((None, None, rows, gw),
                            lambda b, g, qi: (1, b, jnp.maximum(qi - back, 0), g))

    def v_spec(back):
        return pl.BlockSpec((None, PAIRS_PER_STEP, LANES, rows),
                            lambda b, g, qi: (b, g, 0, jnp.maximum(qi - back, 0)))

    return pl.pallas_call(
        _band_kernel,
        grid=(B, groups, S // rows),
        in_specs=[pl.BlockSpec((None, None, rows, gw), lambda b, g, qi: (0, b, qi, g)),
                  k_spec(1), k_spec(0), v_spec(1), v_spec(0),
                  pl.BlockSpec((2 * PAIRS_PER_STEP, BAND_WIN, TQ_BAND), lambda b, g, qi: (g, 0, 0))],
        out_specs=pl.BlockSpec((None, rows, gw), lambda b, g, qi: (b, qi, g)),
        out_shape=jax.ShapeDtypeStruct((B, S, WIDTH), BF16),
        compiler_params=_cparams(("arbitrary", "arbitrary", "arbitrary")),
        name="band_attn",
    )(qk, qk, qk, vt, vt, bias_t)


def _band_bias(rel_table):
    pad = LOOKBACK_CHUNKS * CHUNK
    i = jnp.arange(TQ_BAND)[:, None]
    j = jnp.arange(BAND_WIN)[None, :]
    period = 1024
    assert BAND_WIN + TQ_BAND <= period
    m = jnp.arange(period)
    rel = jnp.where(m <= BAND_WIN, pad - m, pad + period - m)
    g = rel_table[:, jnp.clip(rel, -REL_CLIP, REL_CLIP) + REL_CLIP].astype(F32)
    bias = jnp.tile(g, (1, TQ_BAND))[:, :TQ_BAND * (period - 1)].reshape(-1, TQ_BAND, period - 1)[:, :, :BAND_WIN]
    c0 = (i // CHUNK) * CHUNK
    inband = (j >= c0) & (j < c0 + pad + CHUNK)
    return jnp.where(inband[None], bias * LOG2E, NEG_INF).transpose(0, 2, 1)


FOX_DIAG, FOX_TWO_PAST, FOX_PAST_DIAG = 0, 1, 2


def _fox_kernel(qi_tab, ka_tab, kb_tab, kind_tab, first_tab, q_ref, ka_ref, kb_ref, vta_ref, vtb_ref,
                deca_ref, decb_ref, cba_ref, cbb_ref, o_ref, acc_ref, m_ref):
    grp = pl.program_id(1)
    t = pl.program_id(2)
    kind = kind_tab[t]
    tq, tk = q_ref.shape[0], ka_ref.shape[0]
    lane2 = lax.broadcasted_iota(jnp.int32, (1, 2 * LANES), 1)
    heads = [(pp, hh) for pp in range(PAIRS_PER_STEP) for hh in range(2)]
    tile_a = (ka_ref, vta_ref, deca_ref, cba_ref)
    tile_b = (kb_ref, vtb_ref, decb_ref, cbb_ref)

    @pl.when(first_tab[t] == 1)
    def _():
        acc_ref[...] = jnp.zeros_like(acc_ref)
        m_ref[...] = jnp.full_like(m_ref, NEG_INF)

    def step(tiles):
        ones = jnp.where(lax.broadcasted_iota(jnp.int32, (tq, LANES), 1) < 2 * N_HEADS, 1.0, 0.0).astype(BF16)
        causal = (lax.broadcasted_iota(jnp.int32, (tk, tq), 0)
                  <= lax.broadcasted_iota(jnp.int32, (tk, tq), 1))
        scores = []
        for pp, hh in heads:
            head = 2 * (PAIRS_PER_STEP * grp + pp) + hh
            q_aug = jnp.concatenate([q_ref[:, pp * LANES:(pp + 1) * LANES], ones], axis=1)
            use = (((lane2 // HEAD_DIM) == hh) | (lane2 == LANES + head) | (lane2 == LANES + N_HEADS + head))
            qm = jnp.where(use, q_aug, jnp.zeros_like(q_aug))
            per_tile = []
            for (k_ref, _, dec_ref, _), diagonal in tiles:
                k_aug = jnp.concatenate([k_ref[:, pp * LANES:(pp + 1) * LANES], dec_ref[...]], axis=1)
                s = _dot_nt(k_aug, qm)
                per_tile.append(jnp.where(causal, s, NEG_INF) if diagonal else s)
            scores.append(per_tile)
        for n, (pp, hh) in enumerate(heads):
            head = 2 * (PAIRS_PER_STEP * grp + pp) + hh
            bases = [cb_ref[pl.ds(head, 1), :] for (_, _, _, cb_ref), _ in tiles]
            m_old = m_ref[n]
            m_new = m_old
            for s, base in zip(scores[n], bases):
                m_new = jnp.maximum(m_new, jnp.max(s, axis=0, keepdims=True) + base)
            acc = acc_ref[n] * jnp.exp2(m_old - m_new)
            for s, base, ((_, vt_ref, _, _), _) in zip(scores[n], bases, tiles):
                p = jnp.exp2(s - (m_new - base)).astype(BF16)
                acc = acc + _dot(_values_with_ones(vt_ref[pp], hh), p)
            acc_ref[n] = acc
            m_ref[n] = m_new

    def finish():
        for pp in range(PAIRS_PER_STEP):
            o_ref[:, pp * LANES:(pp + 1) * LANES] = _pair_finish_t(
                acc_ref[2 * pp], acc_ref[2 * pp + 1]).astype(o_ref.dtype)

    @pl.when(kind == FOX_TWO_PAST)
    def _():
        step([(tile_a, False), (tile_b, False)])

    @pl.when(kind == FOX_PAST_DIAG)
    def _():
        step([(tile_a, False), (tile_b, True)])
        finish()

    @pl.when(kind == FOX_DIAG)
    def _():
        step([(tile_a, True)])
        finish()


def _fox_attention(qkv, vt, dec, cbase):
    _, B, S, _ = qkv.shape
    tq = TQ_FOX
    nq = S // tq
    groups = N_HEADS // 2 // PAIRS_PER_STEP
    gw = PAIRS_PER_STEP * LANES
    steps = []
    for i in range(nq):
        past = list(range(i))
        first = 1
        while len(past) >= 2:
            steps.append((i, past[0], past[1], FOX_TWO_PAST, first))
            past, first = past[2:], 0
        if past:
            steps.append((i, past[0], i, FOX_PAST_DIAG, first))
        else:
            steps.append((i, i, i, FOX_DIAG, first))
    tabs = [jnp.array([s[c] for s in steps], jnp.int32) for c in range(5)]

    def key_specs(col):
        tile = lambda *a: a[3 + col][a[2]]
        return [pl.BlockSpec((None, None, tq, gw), lambda b, g, t, *tb: (3, b, tile(b, g, t, *tb), g)),
                pl.BlockSpec((None, PAIRS_PER_STEP, LANES, tq),
                             lambda b, g, t, *tb: (b, groups + g, 0, tile(b, g, t, *tb))),
                pl.BlockSpec((None, tq, LANES), lambda b, g, t, *tb: (b, tile(b, g, t, *tb), 0)),
                pl.BlockSpec((None, None, N_HEADS, tq), lambda b, g, t, *tb: (b, tile(b, g, t, *tb), 0, 0))]

    ka, va, da, ca = key_specs(1)
    kb, vb, db, cb = key_specs(2)
    grid_spec = pltpu.PrefetchScalarGridSpec(
        num_scalar_prefetch=5,
        grid=(B, groups, len(steps)),
        in_specs=[pl.BlockSpec((None, None, tq, gw), lambda b, g, t, *tb: (2, b, tb[0][t], g)),
                  ka, kb, va, vb, da, db, ca, cb],
        out_specs=pl.BlockSpec((None, tq, gw), lambda b, g, t, *tb: (b, tb[0][t], g)),
        scratch_shapes=[pltpu.VMEM((2 * PAIRS_PER_STEP, PV_ROWS, tq), F32),
                        pltpu.VMEM((2 * PAIRS_PER_STEP, 1, tq), F32)],
    )
    return pl.pallas_call(
        _fox_kernel,
        grid_spec=grid_spec,
        out_shape=jax.ShapeDtypeStruct((B, S, WIDTH), BF16),
        compiler_params=_cparams(("arbitrary", "arbitrary", "arbitrary")),
        name="fox_attn",
    )(*tabs, qkv, qkv, qkv, vt, vt, dec, dec, cbase, cbase)


def _postmix_kernel(ya_ref, yb_ref, ga_ref, gb_ref, x_ref, g1_ref, wpa_ref, wpb_ref, wo_ref,
                    g_ref, sh_ref, sc_ref, wr_ref, rb_ref, ustrict_ref, lstrict_ref,
                    x1_ref, h2_ref, posk_ref, wk_ref, cnt_ref):
    ua = _dot(ya_ref[0], wpa_ref[...])
    ub = _dot(yb_ref[0], wpb_ref[...])
    m = ga_ref[0].astype(F32) * ua + gb_ref[0].astype(F32) * ub
    mo = _dot(m.astype(BF16), wo_ref[...])
    x1 = x_ref[0] + g1_ref[0] * mo
    x1_ref[0] = x1
    h2 = _modnorm(x1, g_ref[...], sc_ref[0], sh_ref[0]).astype(BF16)
    h2_ref[0] = h2
    for j in range(h2.shape[0] // TM_MOE):
        posk, wk, cnt = _route(h2[j * TM_MOE:(j + 1) * TM_MOE], wr_ref[...], rb_ref[...],
                               ustrict_ref[...], lstrict_ref[...])
        posk_ref[j] = posk
        wk_ref[j] = wk
        cnt_ref[j] = cnt


def _postmix(ya, yb, ga, gb, x, g1, wpa, wpb, wo, g, sh, sc, wr_t, rb, ustrict, lstrict):
    B, S, D = x.shape
    tm = TM_PRE
    sub_tiles = tm // TM_MOE
    nt = B * S // TM_MOE
    const = lambda b, s: (0, 0)
    tok = lambda w: pl.BlockSpec((1, tm, w), lambda b, s: (b, s, 0))
    row = pl.BlockSpec((1, 1, D), lambda b, s: (b, 0, 0))
    tiles = lambda r, c: pl.BlockSpec((sub_tiles, r, c), lambda b, s: (b * (S // tm) + s, 0, 0))
    return pl.pallas_call(
        _postmix_kernel,
        grid=(B, S // tm),
        in_specs=[tok(WIDTH), tok(WIDTH), tok(D), tok(D), tok(D), row,
                  pl.BlockSpec(wpa.shape, const), pl.BlockSpec(wpb.shape, const),
                  pl.BlockSpec(wo.shape, const), pl.BlockSpec((1, D), const), row, row,
                  pl.BlockSpec(wr_t.shape, const), pl.BlockSpec(rb.shape, const),
                  pl.BlockSpec(ustrict.shape, const), pl.BlockSpec(lstrict.shape, const)],
        out_specs=[tok(D), tok(D), tiles(TOP_K, TM_MOE), tiles(TOP_K, TM_MOE), tiles(N_EXPERTS, LANES)],
        out_shape=[jax.ShapeDtypeStruct((B, S, D), F32), jax.ShapeDtypeStruct((B, S, D), BF16),
                   jax.ShapeDtypeStruct((nt, TOP_K, TM_MOE), F32),
                   jax.ShapeDtypeStruct((nt, TOP_K, TM_MOE), F32),
                   jax.ShapeDtypeStruct((nt, N_EXPERTS, LANES), F32)],
        compiler_params=_cparams(("arbitrary", "arbitrary")),
        name="postmix",
    )(ya, yb, ga, gb, x, g1, wpa, wpb, wo, g, sh, sc, wr_t, rb, ustrict, lstrict)


def _route(h, wr, rb, ustrict, lstrict):
    tm = h.shape[0]
    per_group = N_EXPERTS // N_GROUPS
    logits = _dot_nt(wr, h)
    scores = _sigmoid(logits)
    biased = scores + rb
    sub8 = lax.broadcasted_iota(jnp.int32, (per_group, tm), 0).astype(F32)
    neg = -jnp.inf

    grp_rows = []
    for g in range(N_GROUPS):
        a = biased[g * per_group:(g + 1) * per_group]
        m1 = jnp.max(a, axis=0, keepdims=True)
        i1 = jnp.min(jnp.where(a == m1, sub8, float(per_group)), axis=0, keepdims=True)
        m2 = jnp.max(jnp.where(sub8 == i1, neg, a), axis=0, keepdims=True)
        grp_rows.append(m1 + m2)
    grp = jnp.concatenate(grp_rows, axis=0)

    gsub = lax.broadcasted_iota(jnp.int32, (N_GROUPS, tm), 0).astype(F32)
    gmask = jnp.zeros((N_GROUPS, tm), F32)
    for _ in range(TOPK_GROUPS):
        mx = jnp.max(grp, axis=0, keepdims=True)
        gi = jnp.min(jnp.where(grp == mx, gsub, float(N_GROUPS)), axis=0, keepdims=True)
        sel = gsub == gi
        gmask = jnp.where(sel, 1.0, gmask)
        grp = jnp.where(sel, neg, grp)
    masked = jnp.concatenate(
        [jnp.where(gmask[g:g + 1] > 0.5, biased[g * per_group:(g + 1) * per_group], neg)
         for g in range(N_GROUPS)], axis=0)

    esub = lax.broadcasted_iota(jnp.int32, (N_EXPERTS, tm), 0).astype(F32)
    sels, ws = [], []
    for _ in range(TOP_K):
        mx = jnp.max(masked, axis=0, keepdims=True)
        ei = jnp.min(jnp.where(masked == mx, esub, float(N_EXPERTS)), axis=0, keepdims=True)
        sel = esub == ei
        sels.append(sel)
        ws.append(jnp.sum(jnp.where(sel, scores, 0.0), axis=0, keepdims=True))
        masked = jnp.where(sel, neg, masked)
    wsum = ws[0]
    for w in ws[1:]:
        wsum = wsum + w

    selmask = jnp.zeros((N_EXPERTS, tm), F32)
    for sel in sels:
        selmask = selmask + jnp.where(sel, 1.0, 0.0)
    rank = _dot(selmask.astype(BF16), ustrict)
    counts = jnp.sum(selmask, axis=1, keepdims=True)
    padded = jnp.ceil(counts * (1.0 / RUN_ALIGN)) * RUN_ALIGN
    padded_b = jnp.broadcast_to(padded, (N_EXPERTS, LANES))
    base = _dot(lstrict, padded_b.astype(BF16))
    pos = base[:, 0:1] + rank

    posk = jnp.concatenate([jnp.sum(jnp.where(sel, pos, 0.0), axis=0, keepdims=True) for sel in sels], axis=0)
    wk = jnp.concatenate([(w / wsum) * ROUTED_SCALE for w in ws], axis=0)
    return posk, wk, padded_b


def _rows_copy(m, src_ref, src0, dst_ref, dst0, sem):
    rows = m * RUN_ALIGN
    return pltpu.make_async_copy(src_ref.at[pl.ds(pl.multiple_of(src0, RUN_ALIGN), rows)],
                                 dst_ref.at[pl.ds(pl.multiple_of(dst0, RUN_ALIGN), rows)], sem)


def _start_rows(m, src_ref, src0, dst_ref, dst0, sem):
    @pl.when(m > 0)
    def _():
        _rows_copy(m, src_ref, src0, dst_ref, dst0, sem).start()


def _wait_rows(m, src_ref, dst_ref, sem):
    @pl.when(m > 0)
    def _():
        _rows_copy(m, src_ref, 0, dst_ref, 0, sem).wait()


def _dispatch_kernel(off_ref, m_ref, base_ref, tot_ref, gapoff_ref, gapm_ref, h_ref, posk_ref, xs_hbm,
                     xs_ref, zero_ref, sem):
    i = pl.program_id(0)
    tm = h_ref.shape[0]
    n_chunks = K_SORT // MXU_DIM

    @pl.when(i == 0)
    def _():
        zero_ref[...] = jnp.zeros_like(zero_ref)

        def gap(e, carry):
            _start_rows(gapm_ref[e], zero_ref, 0, xs_hbm, gapoff_ref[e], sem)
            return carry
        lax.fori_loop(0, N_EXPERTS, gap, 0)

        def gap_wait(e, carry):
            _wait_rows(gapm_ref[e], zero_ref, xs_hbm, sem)
            return carry
        lax.fori_loop(0, N_EXPERTS, gap_wait, 0)

    slot = i % 2
    xs_slot = xs_ref.at[slot]
    h = h_ref[...]
    posk = posk_ref[0]
    rows = lax.broadcasted_iota(jnp.int32, (MXU_DIM, tm), 0).astype(F32).astype(BF16)
    one = jnp.ones((MXU_DIM, tm), BF16)
    def sort_chunks(lo, hi):
        for c in range(lo, hi):
            rel = posk - float(c * MXU_DIM)
            p = jnp.zeros((MXU_DIM, tm), BF16)
            for k in range(TOP_K):
                p = jnp.where(rows == rel[k:k + 1, :].astype(BF16), one, p)
            xs_slot[c * MXU_DIM:(c + 1) * MXU_DIM, :] = _dot(p, h).astype(BF16)

    sort_chunks(0, K_SORT_COMMON // MXU_DIM)

    @pl.when(tot_ref[i] * RUN_ALIGN > K_SORT_COMMON)
    def _():
        sort_chunks(K_SORT_COMMON // MXU_DIM, n_chunks)

    @pl.when(i > 0)
    def _():
        _wait_rows(tot_ref[i - 1], xs_ref.at[1 - slot], xs_hbm, sem)

    def run(e, carry):
        idx = i * N_EXPERTS + e
        _start_rows(m_ref[idx], xs_slot, base_ref[idx], xs_hbm, off_ref[idx], sem)
        return carry
    lax.fori_loop(0, N_EXPERTS, run, 0)

    @pl.when(i == pl.num_programs(0) - 1)
    def _():
        _wait_rows(tot_ref[i], xs_slot, xs_hbm, sem)


def _dispatch(off, m16, base, tot, gapoff, gapm, h2, posk, n_rows):
    T, D = h2.shape
    tm = TM_MOE
    grid_spec = pltpu.PrefetchScalarGridSpec(
        num_scalar_prefetch=6,
        grid=(T // tm,),
        in_specs=[pl.BlockSpec((tm, D), lambda i, *_: (i, 0)),
                  pl.BlockSpec((1, TOP_K, tm), lambda i, *_: (i, 0, 0))],
        out_specs=pl.BlockSpec(memory_space=pl.ANY),
        scratch_shapes=[pltpu.VMEM((2, K_SORT, D), BF16),
                        pltpu.VMEM((BLOCK_ROWS, D), BF16),
                        pltpu.SemaphoreType.DMA],
    )
    return pl.pallas_call(
        _dispatch_kernel,
        grid_spec=grid_spec,
        out_shape=jax.ShapeDtypeStruct((n_rows, D), BF16),
        compiler_params=_cparams(("arbitrary",), unchecked_dma=True),
        name="dispatch",
    )(off, m16, base, tot, gapoff, gapm, h2, posk)


def _expert_kernel(bexp_ref, nvalid_ref, x_ref, wg_ref, wu_ref, wd_ref, y_ref, wgu_bf, wd_bf):
    i = pl.program_id(0)

    @pl.when(i < nvalid_ref[0])
    def _():
        @pl.when((i == 0) | (bexp_ref[i] != bexp_ref[jnp.maximum(i - 1, 0)]))
        def _():
            wgu_bf[:, :EXPERT_FF] = wg_ref[...].astype(BF16)
            wgu_bf[:, EXPERT_FF:] = wu_ref[...].astype(BF16)
            wd_bf[...] = wd_ref[...].astype(BF16)

        gu = _dot(x_ref[...], wgu_bf[...])
        g = gu[:, :EXPERT_FF]
        a = (g * _sigmoid(g) * gu[:, EXPERT_FF:]).astype(BF16)
        y_ref[...] = _dot(a, wd_bf[...]).astype(y_ref.dtype)


def _experts(bexp, nvalid, xs, wg, wu, wd, layer):
    n_rows, D = xs.shape
    nb = n_rows // BLOCK_ROWS

    def blk(i, be, nv):
        return jnp.minimum(i, nv[0] - 1)

    wspec = lambda r, c: pl.BlockSpec((None, None, r, c),
                                      lambda i, be, nv: (layer, be[blk(i, be, nv)], 0, 0))
    grid_spec = pltpu.PrefetchScalarGridSpec(
        num_scalar_prefetch=2,
        grid=(nb,),
        in_specs=[pl.BlockSpec((BLOCK_ROWS, D), lambda i, be, nv: (blk(i, be, nv), 0)),
                  wspec(D, EXPERT_FF), wspec(D, EXPERT_FF), wspec(EXPERT_FF, D)],
        out_specs=pl.BlockSpec((BLOCK_ROWS, D), lambda i, be, nv: (blk(i, be, nv), 0)),
        scratch_shapes=[pltpu.VMEM((D, 2 * EXPERT_FF), BF16), pltpu.VMEM((EXPERT_FF, D), BF16)],
    )
    return pl.pallas_call(
        _expert_kernel,
        grid_spec=grid_spec,
        out_shape=jax.ShapeDtypeStruct((n_rows, D), BF16),
        compiler_params=_cparams(("arbitrary",)),
        name="experts",
    )(bexp, nvalid, xs, wg, wu, wd)


def _combine_kernel(off_ref, m_ref, base_ref, tot_ref, ys_hbm, posk_ref, wk_ref, h_ref, x_ref, g2_ref,
                    wsgu_ref, wsd_ref, o_ref, ys_ref, pw_ref, sem):
    i = pl.program_id(0)
    tm = h_ref.shape[0]
    slot = i % 2

    def fetch(tile, buf_slot):
        def run(e, carry):
            idx = tile * N_EXPERTS + e
            _start_rows(m_ref[idx], ys_hbm, off_ref[idx], ys_ref.at[buf_slot], base_ref[idx],
                        sem.at[buf_slot])
            return carry
        lax.fori_loop(0, N_EXPERTS, run, 0)

    @pl.when(i == 0)
    def _():
        ys_ref[...] = jnp.zeros_like(ys_ref)
        fetch(i, slot)

    @pl.when(i + 1 < pl.num_programs(0))
    def _():
        fetch(i + 1, 1 - slot)

    h = h_ref[...]
    gu = _dot(h, wsgu_ref[...])
    g = gu[:, :EXPERT_FF]
    shared = _dot((g * _sigmoid(g) * gu[:, EXPERT_FF:]).astype(BF16), wsd_ref[...])

    posk = posk_ref[0]
    wk = wk_ref[0]
    rows = lax.broadcasted_iota(jnp.int32, (MXU_DIM, tm), 0).astype(F32).astype(BF16)

    def weight_chunks(lo, hi):
        for c in range(lo, hi):
            rel = posk - float(c * MXU_DIM)
            pw = jnp.zeros((MXU_DIM, tm), BF16)
            for k in range(TOP_K):
                wrow = jnp.broadcast_to(wk[k:k + 1, :], (MXU_DIM, tm)).astype(BF16)
                pw = jnp.where(rows == rel[k:k + 1, :].astype(BF16), wrow, pw)
            pw_ref[c * MXU_DIM:(c + 1) * MXU_DIM, :] = pw

    def unsort(lo, hi):
        return lax.dot_general(pw_ref[lo:hi, :], ys_ref[slot, lo:hi, :], (((0,), (0,)), ((), ())),
                               preferred_element_type=F32)

    weight_chunks(0, K_SORT_COMMON // MXU_DIM)
    _wait_rows(tot_ref[i], ys_hbm, ys_ref.at[slot], sem.at[slot])
    o_ref[...] = x_ref[...] + g2_ref[0] * (unsort(0, K_SORT_COMMON) + shared)

    @pl.when(tot_ref[i] * RUN_ALIGN > K_SORT_COMMON)
    def _():
        weight_chunks(K_SORT_COMMON // MXU_DIM, K_SORT // MXU_DIM)
        o_ref[...] = o_ref[...] + g2_ref[0] * unsort(K_SORT_COMMON, K_SORT)


def _combine(off, m16, base, tot, ys, posk, wk, h2, x1, g2, wsgu, wsd, tiles_per_batch):
    T, D = h2.shape
    tm = TM_MOE
    const = lambda i, *_: (0, 0)
    tok = lambda w: pl.BlockSpec((tm, w), lambda i, *_: (i, 0))
    sel = pl.BlockSpec((1, TOP_K, tm), lambda i, *_: (i, 0, 0))
    grid_spec = pltpu.PrefetchScalarGridSpec(
        num_scalar_prefetch=4,
        grid=(T // tm,),
        in_specs=[pl.BlockSpec(memory_space=pl.ANY), sel, sel, tok(D), tok(D),
                  pl.BlockSpec((1, 1, D), lambda i, *_: (i // tiles_per_batch, 0, 0)),
                  pl.BlockSpec(wsgu.shape, const), pl.BlockSpec(wsd.shape, const)],
        out_specs=tok(D),
        scratch_shapes=[pltpu.VMEM((2, K_SORT, D), BF16),
                        pltpu.VMEM((K_SORT, tm), BF16),
                        pltpu.SemaphoreType.DMA((2,))],
    )
    return pl.pallas_call(
        _combine_kernel,
        grid_spec=grid_spec,
        out_shape=jax.ShapeDtypeStruct((T, D), F32),
        compiler_params=_cparams(("arbitrary",), unchecked_dma=True),
        name="combine",
    )(off, m16, base, tot, ys, posk, wk, h2, x1, g2, wsgu, wsd)


def _moe(h2, x1, g2, posk, wk, cnt, wg, wu, wd, layer, wsgu, wsd, tiles_per_batch):
    T, D = h2.shape
    nt = T // TM_MOE

    pad = cnt[:, :, 0].astype(jnp.int32)
    base = jnp.cumsum(pad, axis=1) - pad
    tile_off = jnp.cumsum(pad, axis=0) - pad
    total = jnp.sum(pad, axis=0)
    region = ((total + BLOCK_ROWS - 1) // BLOCK_ROWS) * BLOCK_ROWS
    rend = jnp.cumsum(region)
    rstart = rend - region
    off = rstart[None, :] + tile_off
    nb_max = -(-(TOP_K * T + nt * N_EXPERTS * (RUN_ALIGN - 1)) // BLOCK_ROWS) + N_EXPERTS
    blk_row = jnp.arange(nb_max, dtype=jnp.int32) * BLOCK_ROWS
    bexp = jnp.minimum(jnp.sum((rend[None, :] <= blk_row[:, None]).astype(jnp.int32), axis=1), N_EXPERTS - 1)
    nvalid = (rend[-1:] // BLOCK_ROWS).astype(jnp.int32)
    flat = lambda a: a.reshape(-1).astype(jnp.int32)

    m16 = flat(pad // RUN_ALIGN)
    tot = flat(jnp.sum(pad, axis=1) // RUN_ALIGN)
    xs = _dispatch(flat(off), m16, flat(base), tot, flat(rstart + total),
                   flat((region - total) // RUN_ALIGN), h2, posk, nb_max * BLOCK_ROWS)
    ys = _experts(bexp, nvalid, xs, wg, wu, wd, layer)
    return _combine(flat(off), m16, flat(base), tot, ys, posk, wk, h2, x1, g2, wsgu, wsd, tiles_per_batch)


def kernel(x, c, ada_w, ada_b, mix_norm_g, w_in, b_fgate, qn_a, kn_a, qn_b, kn_b, rel_bias, w_proj_a,
           w_proj_b, w_out, ffn_norm_g, w_router, router_bias, w_gate_e, w_up_e, w_down_e, w_gate_s,
           w_up_s, w_down_s):
    B, S, D = x.shape
    L = ada_w.shape[0]
    T = B * S
    assert D == D_MODEL and S % TM_PRE == 0 and TQ_FOX == TM_PRE and S % TM_MOE == 0 and S % (BAND_TILES * TQ_BAND) == 0

    mod = _adaln(c, ada_w, ada_b).reshape(L, B, 6, 1, D)

    hid = jnp.arange(MXU_DIM) // HEAD_DIM
    bd = jnp.where(hid[:, None] == hid[None, :], 1.0 / HEAD_DIM, 0.0).astype(BF16)
    r = jnp.arange(TM_PRE)
    tri = (r[:, None] <= r[None, :]).astype(BF16)
    r = jnp.arange(TM_MOE)
    ustrict = (r[:, None] < r[None, :]).astype(BF16)
    r = jnp.arange(N_EXPERTS)
    lstrict = (r[None, :] < r[:, None]).astype(BF16)

    q_scale = 1.0 / math.sqrt(HEAD_DIM)
    for l in range(L):
        sh1, sc1, g1, sh2, sc2, g2 = (mod[l, :, j] for j in range(6))
        w = w_in[l]
        cols = lambda c: w[:, c * WIDTH:(c + 1) * WIDTH]
        wqk = jnp.concatenate([cols(0), cols(1), cols(3), cols(4)], axis=1).astype(BF16)
        wvt = jnp.concatenate([cols(2), cols(5)], axis=1).T.astype(BF16)
        wf = jnp.zeros((2 * N_HEADS, D), BF16).at[:N_HEADS].set(w[:, 6 * WIDTH:6 * WIDTH + N_HEADS].T.astype(BF16))
        wg = w[:, 6 * WIDTH + N_HEADS:].astype(BF16)
        hn = jnp.zeros((8, WIDTH), F32)
        hn = hn.at[0].set(jnp.tile(qn_a[l], N_HEADS) * (q_scale * LOG2E)).at[1].set(jnp.tile(kn_a[l], N_HEADS))
        hn = hn.at[2].set(jnp.tile(qn_b[l], N_HEADS) * (q_scale * LOG2E)).at[3].set(jnp.tile(kn_b[l], N_HEADS))

        qk, vt, dec, cbase, ga, gb = _premix(x, sh1, sc1, mix_norm_g[l][None], wqk, wvt, wf, wg,
                                             b_fgate[l][:, None], hn, bd, tri)
        ya = _band_attention(qk, vt, _band_bias(rel_bias[l]))
        yb = _fox_attention(qk, vt, dec, cbase)
        x1, h2, posk, wk, cnt = _postmix(
            ya, yb, ga, gb, x, g1, w_proj_a[l].astype(BF16), w_proj_b[l].astype(BF16),
            w_out[l].astype(BF16), ffn_norm_g[l][None], sh2, sc2, w_router[l].T.astype(BF16),
            router_bias[l][:, None], ustrict, lstrict)

        wsgu = jnp.concatenate([w_gate_s[l], w_up_s[l]], axis=-1).astype(BF16)
        x = _moe(h2.reshape(T, D), x1.reshape(T, D), g2, posk, wk, cnt, w_gate_e, w_up_e, w_down_e, l,
                 wsgu, w_down_s[l].astype(BF16), S // TM_MOE).reshape(B, S, D)
    return x
```

```python
import functools
import math

import jax
import jax.numpy as jnp
from jax import lax
from jax.experimental import pallas as pl
from jax.experimental.pallas import tpu as pltpu

F32 = jnp.float32
BF16 = jnp.bfloat16

D_MODEL = 1024
HEAD_DIM = 64
N_HEADS = 8
WIDTH = N_HEADS * HEAD_DIM
CHUNK = 64
LOOKBACK_CHUNKS = 8
REL_CLIP = 128
N_EXPERTS = 64
TOP_K = 8
N_GROUPS = 8
TOPK_GROUPS = 4
EXPERT_FF = 256
ROUTED_SCALE = 2.5
EPS = 1e-6
NEG_INF = -1e30
LOG2E = math.log2(math.e)

LANES = 128
MXU_DIM = 256
VMEM_LIMIT = 56 * 1024 * 1024

TM_PRE = 512
TQ_BAND = 256
BAND_WIN = TQ_BAND + LOOKBACK_CHUNKS * CHUNK
BAND_TILES = LOOKBACK_CHUNKS * CHUNK // TQ_BAND
TQ_FOX = 512
PAIRS_PER_STEP = 4
PV_ROWS = HEAD_DIM + 16
TM_MOE = 256
RUN_ALIGN = 16
K_SORT = ((TOP_K * TM_MOE + N_EXPERTS * (RUN_ALIGN - 1) + MXU_DIM - 1) // MXU_DIM) * MXU_DIM
K_SORT_COMMON = 2560
BLOCK_ROWS = 1024


def _dot(a, b):
    return jnp.dot(a, b, preferred_element_type=F32)


def _dot_nt(a, b):
    return lax.dot_general(a, b, (((1,), (1,)), ((), ())), preferred_element_type=F32)


def _sigmoid(v):
    return 1.0 / (1.0 + jnp.exp(-v))


def _cparams(sem, unchecked_dma=False):
    return pltpu.CompilerParams(dimension_semantics=sem, vmem_limit_bytes=VMEM_LIMIT,
                                disable_bounds_checks=unchecked_dma)


def _adaln_kernel(c_ref, w_ref, b_ref, o_ref):
    c = c_ref[...]
    ca = c * _sigmoid(c)
    o_ref[0] = jnp.dot(ca, w_ref[0], preferred_element_type=F32,
                       precision=lax.Precision.HIGHEST) + b_ref[0]


def _adaln(c, ada_w, ada_b):
    L, D, N = ada_w.shape
    B = c.shape[0]
    tn = 1536
    return pl.pallas_call(
        _adaln_kernel,
        grid=(L, N // tn),
        in_specs=[pl.BlockSpec((B, D), lambda l, j: (0, 0)),
                  pl.BlockSpec((1, D, tn), lambda l, j: (l, 0, j)),
                  pl.BlockSpec((1, 1, tn), lambda l, j: (l, 0, j))],
        out_specs=pl.BlockSpec((1, B, tn), lambda l, j: (l, 0, j)),
        out_shape=jax.ShapeDtypeStruct((L, B, N), F32),
        compiler_params=_cparams(("arbitrary", "arbitrary")),
        name="adaln",
    )(c, ada_w, ada_b.reshape(L, 1, N))


def _modnorm(x, g, sc, sh):
    ms = jnp.mean(x * x, axis=-1, keepdims=True)
    y = x * lax.rsqrt(ms + EPS) * g
    return y * (1.0 + sc) + sh


def _premix_kernel(x_ref, sh_ref, sc_ref, g_ref, wqk_ref, wvt_ref, wf_ref, wg_ref, bf_ref, hn_ref, bd_ref,
                   tri_ref, qk_ref, vt_ref, dec_ref, cbase_ref, ga_ref, gb_ref, carry_ref):
    si = pl.program_id(1)
    tm = x_ref.shape[1]
    h = _modnorm(x_ref[0], g_ref[...], sc_ref[0], sh_ref[0]).astype(BF16)

    for c in range(4):
        z = _dot(h, wqk_ref[:, c * WIDTH:(c + 1) * WIDTH])
        sq = (z * z).astype(BF16)
        ms = jnp.concatenate(
            [_dot(sq[:, j * MXU_DIM:(j + 1) * MXU_DIM], bd_ref[...]) for j in range(WIDTH // MXU_DIM)],
            axis=1)
        z = z * lax.rsqrt(ms + EPS) * hn_ref[c:c + 1, :]
        qk_ref[c, 0] = z.astype(BF16)

    for c in range(2):
        vt = _dot_nt(wvt_ref[c * WIDTH:(c + 1) * WIDTH, :], h).astype(BF16)
        vt_ref[0, c * (N_HEADS // 2):(c + 1) * (N_HEADS // 2)] = vt.reshape(N_HEADS // 2, LANES, tm)

    fr = _dot_nt(wf_ref[...], h)[:N_HEADS]
    xg = fr + bf_ref[...]
    logf = jnp.minimum(xg, 0.0) - jnp.log(1.0 + jnp.exp(-jnp.abs(xg)))
    hi = logf.astype(BF16).astype(F32)
    r1 = logf - hi
    mid = r1.astype(BF16).astype(F32)
    lo = r1 - mid
    parts = jnp.concatenate([hi, mid, lo, jnp.zeros_like(hi)], axis=0)
    cs3 = _dot(parts.astype(BF16), tri_ref[...])
    cs = cs3[0:8] + cs3[8:16] + cs3[16:24]

    @pl.when(si == 0)
    def _():
        carry_ref[...] = jnp.zeros_like(carry_ref)

    drel = cs * (-LOG2E)
    dhi = drel.astype(BF16).astype(F32)
    dlo = drel - dhi
    dec = jnp.concatenate([dhi, dlo, jnp.zeros((LANES - 2 * N_HEADS, tm), F32)], axis=0)
    dec_ref[0] = dec.T.astype(BF16)
    carry = carry_ref[:, 0:1]
    cbase_ref[0, 0] = jnp.broadcast_to(carry * (-LOG2E), (N_HEADS, tm))
    carry_ref[...] = jnp.broadcast_to(carry + cs[:, tm - 1:tm], carry_ref.shape)

    for c in range(4):
        zg = _dot(h, wg_ref[:, c * WIDTH:(c + 1) * WIDTH])
        sg = _sigmoid(zg).astype(BF16)
        if c < 2:
            ga_ref[0, :, c * WIDTH:(c + 1) * WIDTH] = sg
        else:
            gb_ref[0, :, (c - 2) * WIDTH:(c - 1) * WIDTH] = sg


def _premix(x, sh, sc, g, wqk, wvt, wf, wg, bf, hn, bd, tri):
    B, S, D = x.shape
    tm = TM_PRE
    const = lambda b, s: (0, 0)
    return pl.pallas_call(
        _premix_kernel,
        grid=(B, S // tm),
        in_specs=[pl.BlockSpec((1, tm, D), lambda b, s: (b, s, 0)),
                  pl.BlockSpec((1, 1, D), lambda b, s: (b, 0, 0)),
                  pl.BlockSpec((1, 1, D), lambda b, s: (b, 0, 0)),
                  pl.BlockSpec((1, D), const),
                  pl.BlockSpec(wqk.shape, const),
                  pl.BlockSpec(wvt.shape, const),
                  pl.BlockSpec(wf.shape, const),
                  pl.BlockSpec(wg.shape, const),
                  pl.BlockSpec(bf.shape, const),
                  pl.BlockSpec(hn.shape, const),
                  pl.BlockSpec(bd.shape, const),
                  pl.BlockSpec(tri.shape, const)],
        out_specs=[pl.BlockSpec((4, 1, tm, WIDTH), lambda b, s: (0, b, s, 0)),
                   pl.BlockSpec((1, N_HEADS, LANES, tm), lambda b, s: (b, 0, 0, s)),
                   pl.BlockSpec((1, tm, LANES), lambda b, s: (b, s, 0)),
                   pl.BlockSpec((1, 1, N_HEADS, tm), lambda b, s: (b, s, 0, 0)),
                   pl.BlockSpec((1, tm, D), lambda b, s: (b, s, 0)),
                   pl.BlockSpec((1, tm, D), lambda b, s: (b, s, 0))],
        out_shape=[jax.ShapeDtypeStruct((4, B, S, WIDTH), BF16),
                   jax.ShapeDtypeStruct((B, N_HEADS, LANES, S), BF16),
                   jax.ShapeDtypeStruct((B, S, LANES), BF16),
                   jax.ShapeDtypeStruct((B, S // tm, N_HEADS, tm), F32),
                   jax.ShapeDtypeStruct((B, S, D), BF16),
                   jax.ShapeDtypeStruct((B, S, D), BF16)],
        scratch_shapes=[pltpu.VMEM((N_HEADS, LANES), F32)],
        compiler_params=_cparams(("arbitrary", "arbitrary")),
        name="premix",
    )(x, sh, sc, g, wqk, wvt, wf, wg, bf, hn, bd, tri)


def _values_with_ones(vt, hh):
    ones = jnp.ones((PV_ROWS - HEAD_DIM, vt.shape[1]), vt.dtype)
    return jnp.concatenate([vt[hh * HEAD_DIM:(hh + 1) * HEAD_DIM, :], ones], axis=0)


def _pair_finish_t(acc0, acc1):
    out_t = jnp.concatenate([acc[:HEAD_DIM] / acc[HEAD_DIM:HEAD_DIM + 1, :] for acc in (acc0, acc1)], axis=0)
    return out_t.T


def _band_kernel(q_ref, kprev_ref, kcur_ref, vprev_ref, vcur_ref, bias_ref, o_ref):
    tq = TQ_BAND
    step_start = pl.program_id(2) * (BAND_TILES * tq)
    lane = lax.broadcasted_iota(jnp.int32, (1, LANES), 1)
    krow = lax.broadcasted_iota(jnp.int32, (BAND_WIN, 1), 0)
    pieces = [(ref, j) for ref in (kprev_ref, kcur_ref) for j in range(BAND_TILES)]
    vpieces = [(ref, j) for ref in (vprev_ref, vcur_ref) for j in range(BAND_TILES)]
    units = [(u, pp, hh) for u in range(BAND_TILES) for pp in range(PAIRS_PER_STEP) for hh in range(2)]
    scores = []
    for u, pp, hh in units:
        q = q_ref[u * tq:(u + 1) * tq, pp * LANES:(pp + 1) * LANES]
        qm = jnp.where((lane // HEAD_DIM) == hh, q, jnp.zeros_like(q))
        scores.append(jnp.concatenate(
            [_dot_nt(ref[j * tq:(j + 1) * tq, pp * LANES:(pp + 1) * LANES], qm)
             for ref, j in pieces[u:u + 3]], axis=0))
    accs = []
    for n, (u, pp, hh) in enumerate(units):
        kvalid = (krow + (step_start + u * tq - LOOKBACK_CHUNKS * CHUNK)) >= 0
        s = jnp.where(kvalid, scores[n] + bias_ref[2 * pp + hh], NEG_INF)
        m = jnp.max(s, axis=0, keepdims=True)
        p = jnp.exp2(s - m).astype(BF16)
        acc = None
        for c, (ref, j) in enumerate(vpieces[u:u + 3]):
            t = _dot(_values_with_ones(ref[pp, :, j * tq:(j + 1) * tq], hh), p[c * tq:(c + 1) * tq, :])
            acc = t if acc is None else acc + t
        accs.append(acc)
    for n in range(0, len(units), 2):
        u, pp, _ = units[n]
        o_ref[u * tq:(u + 1) * tq, pp * LANES:(pp + 1) * LANES] = _pair_finish_t(
            accs[n], accs[n + 1]).astype(o_ref.dtype)


def _band_attention(qk, vt, bias_t):
    _, B, S, _ = qk.shape
    rows = BAND_TILES * TQ_BAND
    groups = N_HEADS // 2 // PAIRS_PER_STEP
    gw = PAIRS_PER_STEP * LANES

    def k_spec(back):
        return pl.BlockSpec((None, None, rows, gw),
                            lambda b, g, qi: (1, b, jnp.maximum(qi - back, 0), g))

    def v_spec(back):
        return pl.BlockSpec((None, PAIRS_PER_STEP, LANES, rows),
                            lambda b, g, qi: (b, g, 0, jnp.maximum(qi - back, 0)))

    return pl.pallas_call(
        _band_kernel,
        grid=(B, groups, S // rows),
        in_specs=[pl.BlockSpec((None, None, rows, gw), lambda b, g, qi: (0, b, qi, g)),
                  k_spec(1), k_spec(0), v_spec(1), v_spec(0),
                  pl.BlockSpec((2 * PAIRS_PER_STEP, BAND_WIN, TQ_BAND), lambda b, g, qi: (g, 0, 0))],
        out_specs=pl.BlockSpec((None, rows, gw), lambda b, g, qi: (b, qi, g)),
        out_shape=jax.ShapeDtypeStruct((B, S, WIDTH), BF16),
        compiler_params=_cparams(("arbitrary", "arbitrary", "arbitrary")),
        name="band_attn",
    )(qk, qk, qk, vt, vt, bias_t)


def _band_bias(rel_table):
    pad = LOOKBACK_CHUNKS * CHUNK
    i = jnp.arange(TQ_BAND)[:, None]
    j = jnp.arange(BAND_WIN)[None, :]
    period = 1024
    assert BAND_WIN + TQ_BAND <= period
    m = jnp.arange(period)
    rel = jnp.where(m <= BAND_WIN, pad - m, pad + period - m)
    g = rel_table[:, jnp.clip(rel, -REL_CLIP, REL_CLIP) + REL_CLIP].astype(F32)
    bias = jnp.tile(g, (1, TQ_BAND))[:, :TQ_BAND * (period - 1)].reshape(-1, TQ_BAND, period - 1)[:, :, :BAND_WIN]
    c0 = (i // CHUNK) * CHUNK
    inband = (j >= c0) & (j < c0 + pad + CHUNK)
    return jnp.where(inband[None], bias * LOG2E, NEG_INF).transpose(0, 2, 1)


FOX_DIAG, FOX_TWO_PAST, FOX_PAST_DIAG = 0, 1, 2


def _fox_kernel(qi_tab, ka_tab, kb_tab, kind_tab, first_tab, q_ref, ka_ref, kb_ref, vta_ref, vtb_ref,
                deca_ref, decb_ref, cba_ref, cbb_ref, o_ref, acc_ref, m_ref):
    grp = pl.program_id(1)
    t = pl.program_id(2)
    kind = kind_tab[t]
    tq, tk = q_ref.shape[0], ka_ref.shape[0]
    lane2 = lax.broadcasted_iota(jnp.int32, (1, 2 * LANES), 1)
    heads = [(pp, hh) for pp in range(PAIRS_PER_STEP) for hh in range(2)]
    tile_a = (ka_ref, vta_ref, deca_ref, cba_ref)
    tile_b = (kb_ref, vtb_ref, decb_ref, cbb_ref)

    @pl.when(first_tab[t] == 1)
    def _():
        acc_ref[...] = jnp.zeros_like(acc_ref)
        m_ref[...] = jnp.full_like(m_ref, NEG_INF)

    def step(tiles):
        ones = jnp.where(lax.broadcasted_iota(jnp.int32, (tq, LANES), 1) < 2 * N_HEADS, 1.0, 0.0).astype(BF16)
        causal = (lax.broadcasted_iota(jnp.int32, (tk, tq), 0)
                  <= lax.broadcasted_iota(jnp.int32, (tk, tq), 1))
        scores = []
        for pp, hh in heads:
            head = 2 * (PAIRS_PER_STEP * grp + pp) + hh
            q_aug = jnp.concatenate([q_ref[:, pp * LANES:(pp + 1) * LANES], ones], axis=1)
            use = (((lane2 // HEAD_DIM) == hh) | (lane2 == LANES + head) | (lane2 == LANES + N_HEADS + head))
            qm = jnp.where(use, q_aug, jnp.zeros_like(q_aug))
            per_tile = []
            for (k_ref, _, dec_ref, _), diagonal in tiles:
                k_aug = jnp.concatenate([k_ref[:, pp * LANES:(pp + 1) * LANES], dec_ref[...]], axis=1)
                s = _dot_nt(k_aug, qm)
                per_tile.append(jnp.where(causal, s, NEG_INF) if diagonal else s)
            scores.append(per_tile)
        for n, (pp, hh) in enumerate(heads):
            head = 2 * (PAIRS_PER_STEP * grp + pp) + hh
            bases = [cb_ref[pl.ds(head, 1), :] for (_, _, _, cb_ref), _ in tiles]
            m_old = m_ref[n]
            m_new = m_old
            for s, base in zip(scores[n], bases):
                m_new = jnp.maximum(m_new, jnp.max(s, axis=0, keepdims=True) + base)
            acc = acc_ref[n] * jnp.exp2(m_old - m_new)
            for s, base, ((_, vt_ref, _, _), _) in zip(scores[n], bases, tiles):
                p = jnp.exp2(s - (m_new - base)).astype(BF16)
                acc = acc + _dot(_values_with_ones(vt_ref[pp], hh), p)
            acc_ref[n] = acc
            m_ref[n] = m_new

    def finish():
        for pp in range(PAIRS_PER_STEP):
            o_ref[:, pp * LANES:(pp + 1) * LANES] = _pair_finish_t(
                acc_ref[2 * pp], acc_ref[2 * pp + 1]).astype(o_ref.dtype)

    @pl.when(kind == FOX_TWO_PAST)
    def _():
        step([(tile_a, False), (tile_b, False)])

    @pl.when(kind == FOX_PAST_DIAG)
    def _():
        step([(tile_a, False), (tile_b, True)])
        finish()

    @pl.when(kind == FOX_DIAG)
    def _():
        step([(tile_a, True)])
        finish()


def _fox_attention(qkv, vt, dec, cbase):
    _, B, S, _ = qkv.shape
    tq = TQ_FOX
    nq = S // tq
    groups = N_HEADS // 2 // PAIRS_PER_STEP
    gw = PAIRS_PER_STEP * LANES
    steps = []
    for i in range(nq):
        past = list(range(i))
        first = 1
        while len(past) >= 2:
            steps.append((i, past[0], past[1], FOX_TWO_PAST, first))
            past, first = past[2:], 0
        if past:
            steps.append((i, past[0], i, FOX_PAST_DIAG, first))
        else:
            steps.append((i, i, i, FOX_DIAG, first))
    tabs = [jnp.array([s[c] for s in steps], jnp.int32) for c in range(5)]

    def key_specs(col):
        tile = lambda *a: a[3 + col][a[2]]
        return [pl.BlockSpec((None, None, tq, gw), lambda b, g, t, *tb: (3, b, tile(b, g, t, *tb), g)),
                pl.BlockSpec((None, PAIRS_PER_STEP, LANES, tq),
                             lambda b, g, t, *tb: (b, groups + g, 0, tile(b, g, t, *tb))),
                pl.BlockSpec((None, tq, LANES), lambda b, g, t, *tb: (b, tile(b, g, t, *tb), 0)),
                pl.BlockSpec((None, None, N_HEADS, tq), lambda b, g, t, *tb: (b, tile(b, g, t, *tb), 0, 0))]

    ka, va, da, ca = key_specs(1)
    kb, vb, db, cb = key_specs(2)
    grid_spec = pltpu.PrefetchScalarGridSpec(
        num_scalar_prefetch=5,
        grid=(B, groups, len(steps)),
        in_specs=[pl.BlockSpec((None, None, tq, gw), lambda b, g, t, *tb: (2, b, tb[0][t], g)),
                  ka, kb, va, vb, da, db, ca, cb],
        out_specs=pl.BlockSpec((None, tq, gw), lambda b, g, t, *tb: (b, tb[0][t], g)),
        scratch_shapes=[pltpu.VMEM((2 * PAIRS_PER_STEP, PV_ROWS, tq), F32),
                        pltpu.VMEM((2 * PAIRS_PER_STEP, 1, tq), F32)],
    )
    return pl.pallas_call(
        _fox_kernel,
        grid_spec=grid_spec,
        out_shape=jax.ShapeDtypeStruct((B, S, WIDTH), BF16),
        compiler_params=_cparams(("arbitrary", "arbitrary", "arbitrary")),
        name="fox_attn",
    )(*tabs, qkv, qkv, qkv, vt, vt, dec, dec, cbase, cbase)


def _postmix_kernel(ya_ref, yb_ref, ga_ref, gb_ref, x_ref, g1_ref, wpa_ref, wpb_ref, wo_ref,
                    g_ref, sh_ref, sc_ref, wr_ref, rb_ref, ustrict_ref, lstrict_ref,
                    x1_ref, h2_ref, posk_ref, wk_ref, cnt_ref):
    ua = _dot(ya_ref[0], wpa_ref[...])
    ub = _dot(yb_ref[0], wpb_ref[...])
    m = ga_ref[0].astype(F32) * ua + gb_ref[0].astype(F32) * ub
    mo = _dot(m.astype(BF16), wo_ref[...])
    x1 = x_ref[0] + g1_ref[0] * mo
    x1_ref[0] = x1
    h2 = _modnorm(x1, g_ref[...], sc_ref[0], sh_ref[0]).astype(BF16)
    h2_ref[0] = h2
    for j in range(h2.shape[0] // TM_MOE):
        posk, wk, cnt = _route(h2[j * TM_MOE:(j + 1) * TM_MOE], wr_ref[...], rb_ref[...],
                               ustrict_ref[...], lstrict_ref[...])
        posk_ref[j] = posk
        wk_ref[j] = wk
        cnt_ref[j] = cnt


def _postmix(ya, yb, ga, gb, x, g1, wpa, wpb, wo, g, sh, sc, wr_t, rb, ustrict, lstrict):
    B, S, D = x.shape
    tm = TM_PRE
    sub_tiles = tm // TM_MOE
    nt = B * S // TM_MOE
    const = lambda b, s: (0, 0)
    tok = lambda w: pl.BlockSpec((1, tm, w), lambda b, s: (b, s, 0))
    row = pl.BlockSpec((1, 1, D), lambda b, s: (b, 0, 0))
    tiles = lambda r, c: pl.BlockSpec((sub_tiles, r, c), lambda b, s: (b * (S // tm) + s, 0, 0))
    return pl.pallas_call(
        _postmix_kernel,
        grid=(B, S // tm),
        in_specs=[tok(WIDTH), tok(WIDTH), tok(D), tok(D), tok(D), row,
                  pl.BlockSpec(wpa.shape, const), pl.BlockSpec(wpb.shape, const),
                  pl.BlockSpec(wo.shape, const), pl.BlockSpec((1, D), const), row, row,
                  pl.BlockSpec(wr_t.shape, const), pl.BlockSpec(rb.shape, const),
                  pl.BlockSpec(ustrict.shape, const), pl.BlockSpec(lstrict.shape, const)],
        out_specs=[tok(D), tok(D), tiles(TOP_K, TM_MOE), tiles(TOP_K, TM_MOE), tiles(N_EXPERTS, LANES)],
        out_shape=[jax.ShapeDtypeStruct((B, S, D), F32), jax.ShapeDtypeStruct((B, S, D), BF16),
                   jax.ShapeDtypeStruct((nt, TOP_K, TM_MOE), F32),
                   jax.ShapeDtypeStruct((nt, TOP_K, TM_MOE), F32),
                   jax.ShapeDtypeStruct((nt, N_EXPERTS, LANES), F32)],
        compiler_params=_cparams(("arbitrary", "arbitrary")),
        name="postmix",
    )(ya, yb, ga, gb, x, g1, wpa, wpb, wo, g, sh, sc, wr_t, rb, ustrict, lstrict)


def _route(h, wr, rb, ustrict, lstrict):
    tm = h.shape[0]
    per_group = N_EXPERTS // N_GROUPS
    logits = _dot_nt(wr, h)
    scores = _sigmoid(logits)
    biased = scores + rb
    sub8 = lax.broadcasted_iota(jnp.int32, (per_group, tm), 0).astype(F32)
    neg = -jnp.inf

    grp_rows = []
    for g in range(N_GROUPS):
        a = biased[g * per_group:(g + 1) * per_group]
        m1 = jnp.max(a, axis=0, keepdims=True)
        i1 = jnp.min(jnp.where(a == m1, sub8, float(per_group)), axis=0, keepdims=True)
        m2 = jnp.max(jnp.where(sub8 == i1, neg, a), axis=0, keepdims=True)
        grp_rows.append(m1 + m2)
    grp = jnp.concatenate(grp_rows, axis=0)

    gsub = lax.broadcasted_iota(jnp.int32, (N_GROUPS, tm), 0).astype(F32)
    gmask = jnp.zeros((N_GROUPS, tm), F32)
    for _ in range(TOPK_GROUPS):
        mx = jnp.max(grp, axis=0, keepdims=True)
        gi = jnp.min(jnp.where(grp == mx, gsub, float(N_GROUPS)), axis=0, keepdims=True)
        sel = gsub == gi
        gmask = jnp.where(sel, 1.0, gmask)
        grp = jnp.where(sel, neg, grp)
    masked = jnp.concatenate(
        [jnp.where(gmask[g:g + 1] > 0.5, biased[g * per_group:(g + 1) * per_group], neg)
         for g in range(N_GROUPS)], axis=0)

    esub = lax.broadcasted_iota(jnp.int32, (N_EXPERTS, tm), 0).astype(F32)
    sels, ws = [], []
    for _ in range(TOP_K):
        mx = jnp.max(masked, axis=0, keepdims=True)
        ei = jnp.min(jnp.where(masked == mx, esub, float(N_EXPERTS)), axis=0, keepdims=True)
        sel = esub == ei
        sels.append(sel)
        ws.append(jnp.sum(jnp.where(sel, scores, 0.0), axis=0, keepdims=True))
        masked = jnp.where(sel, neg, masked)
    wsum = ws[0]
    for w in ws[1:]:
        wsum = wsum + w

    selmask = jnp.zeros((N_EXPERTS, tm), F32)
    for sel in sels:
        selmask = selmask + jnp.where(sel, 1.0, 0.0)
    rank = _dot(selmask.astype(BF16), ustrict)
    counts = jnp.sum(selmask, axis=1, keepdims=True)
    padded = jnp.ceil(counts * (1.0 / RUN_ALIGN)) * RUN_ALIGN
    padded_b = jnp.broadcast_to(padded, (N_EXPERTS, LANES))
    base = _dot(lstrict, padded_b.astype(BF16))
    pos = base[:, 0:1] + rank

    posk = jnp.concatenate([jnp.sum(jnp.where(sel, pos, 0.0), axis=0, keepdims=True) for sel in sels], axis=0)
    wk = jnp.concatenate([(w / wsum) * ROUTED_SCALE for w in ws], axis=0)
    return posk, wk, padded_b


def _rows_copy(m, src_ref, src0, dst_ref, dst0, sem):
    rows = m * RUN_ALIGN
    return pltpu.make_async_copy(src_ref.at[pl.ds(pl.multiple_of(src0, RUN_ALIGN), rows)],
                                 dst_ref.at[pl.ds(pl.multiple_of(dst0, RUN_ALIGN), rows)], sem)


def _start_rows(m, src_ref, src0, dst_ref, dst0, sem):
    @pl.when(m > 0)
    def _():
        _rows_copy(m, src_ref, src0, dst_ref, dst0, sem).start()


def _wait_rows(m, src_ref, dst_ref, sem):
    @pl.when(m > 0)
    def _():
        _rows_copy(m, src_ref, 0, dst_ref, 0, sem).wait()


def _dispatch_kernel(off_ref, m_ref, base_ref, tot_ref, gapoff_ref, gapm_ref, h_ref, posk_ref, xs_hbm,
                     xs_ref, zero_ref, sem):
    i = pl.program_id(0)
    tm = h_ref.shape[0]
    n_chunks = K_SORT // MXU_DIM

    @pl.when(i == 0)
    def _():
        zero_ref[...] = jnp.zeros_like(zero_ref)

        def gap(e, carry):
            _start_rows(gapm_ref[e], zero_ref, 0, xs_hbm, gapoff_ref[e], sem)
            return carry
        lax.fori_loop(0, N_EXPERTS, gap, 0)

        def gap_wait(e, carry):
            _wait_rows(gapm_ref[e], zero_ref, xs_hbm, sem)
            return carry
        lax.fori_loop(0, N_EXPERTS, gap_wait, 0)

    slot = i % 2
    xs_slot = xs_ref.at[slot]
    h = h_ref[...]
    posk = posk_ref[0]
    rows = lax.broadcasted_iota(jnp.int32, (MXU_DIM, tm), 0).astype(F32).astype(BF16)
    one = jnp.ones((MXU_DIM, tm), BF16)
    def sort_chunks(lo, hi):
        for c in range(lo, hi):
            rel = posk - float(c * MXU_DIM)
            p = jnp.zeros((MXU_DIM, tm), BF16)
            for k in range(TOP_K):
                p = jnp.where(rows == rel[k:k + 1, :].astype(BF16), one, p)
            xs_slot[c * MXU_DIM:(c + 1) * MXU_DIM, :] = _dot(p, h).astype(BF16)

    sort_chunks(0, K_SORT_COMMON // MXU_DIM)

    @pl.when(tot_ref[i] * RUN_ALIGN > K_SORT_COMMON)
    def _():
        sort_chunks(K_SORT_COMMON // MXU_DIM, n_chunks)

    @pl.when(i > 0)
    def _():
        _wait_rows(tot_ref[i - 1], xs_ref.at[1 - slot], xs_hbm, sem)

    def run(e, carry):
        idx = i * N_EXPERTS + e
        _start_rows(m_ref[idx], xs_slot, base_ref[idx], xs_hbm, off_ref[idx], sem)
        return carry
    lax.fori_loop(0, N_EXPERTS, run, 0)

    @pl.when(i == pl.num_programs(0) - 1)
    def _():
        _wait_rows(tot_ref[i], xs_slot, xs_hbm, sem)


def _dispatch(off, m16, base, tot, gapoff, gapm, h2, posk, n_rows):
    T, D = h2.shape
    tm = TM_MOE
    grid_spec = pltpu.PrefetchScalarGridSpec(
        num_scalar_prefetch=6,
        grid=(T // tm,),
        in_specs=[pl.BlockSpec((tm, D), lambda i, *_: (i, 0)),
                  pl.BlockSpec((1, TOP_K, tm), lambda i, *_: (i, 0, 0))],
        out_specs=pl.BlockSpec(memory_space=pl.ANY),
        scratch_shapes=[pltpu.VMEM((2, K_SORT, D), BF16),
                        pltpu.VMEM((BLOCK_ROWS, D), BF16),
                        pltpu.SemaphoreType.DMA],
    )
    return pl.pallas_call(
        _dispatch_kernel,
        grid_spec=grid_spec,
        out_shape=jax.ShapeDtypeStruct((n_rows, D), BF16),
        compiler_params=_cparams(("arbitrary",), unchecked_dma=True),
        name="dispatch",
    )(off, m16, base, tot, gapoff, gapm, h2, posk)


X_RING = 3
Y_RING = 2


def _expert_kernel(bexp_ref, nvalid_ref, xs_hbm, wg_ref, wu_ref, wd_ref, ys_hbm, x_buf, y_buf, wgu_bf, wd_bf,
                   x_sem, y_sem):
    i = pl.program_id(0)
    nvalid = nvalid_ref[0]

    def rows(b):
        return pl.ds(pl.multiple_of(b * BLOCK_ROWS, BLOCK_ROWS), BLOCK_ROWS)

    def load(b):
        slot = b % X_RING
        return pltpu.make_async_copy(xs_hbm.at[rows(b)], x_buf.at[slot], x_sem.at[slot])

    def store(b):
        slot = b % Y_RING
        return pltpu.make_async_copy(y_buf.at[slot], ys_hbm.at[rows(b)], y_sem.at[slot])

    @pl.when(i == 0)
    def _():
        for b in range(X_RING - 1):
            @pl.when(b < nvalid)
            def _(b=b):
                load(b).start()

    @pl.when(i < nvalid)
    def _():
        @pl.when(i + X_RING - 1 < nvalid)
        def _():
            load(i + X_RING - 1).start()

        @pl.when((i == 0) | (bexp_ref[i] != bexp_ref[jnp.maximum(i - 1, 0)]))
        def _():
            wgu_bf[:, :EXPERT_FF] = wg_ref[...].astype(BF16)
            wgu_bf[:, EXPERT_FF:] = wu_ref[...].astype(BF16)
            wd_bf[...] = wd_ref[...].astype(BF16)

        load(i).wait()
        gu = _dot(x_buf[i % X_RING], wgu_bf[...])
        g = gu[:, :EXPERT_FF]
        a = (g * _sigmoid(g) * gu[:, EXPERT_FF:]).astype(BF16)
        y = _dot(a, wd_bf[...]).astype(BF16)

        @pl.when(i >= Y_RING)
        def _():
            store(i - Y_RING).wait()
        y_buf[i % Y_RING] = y
        store(i).start()

        @pl.when(i == nvalid - 1)
        def _():
            for back in range(Y_RING):
                @pl.when(i - back >= 0)
                def _(back=back):
                    store(i - back).wait()


def _experts(bexp, nvalid, xs, wg, wu, wd, layer):
    n_rows, D = xs.shape
    nb = n_rows // BLOCK_ROWS

    def blk(i, be, nv):
        return jnp.minimum(i, nv[0] - 1)

    wspec = lambda r, c: pl.BlockSpec((None, None, r, c),
                                      lambda i, be, nv: (layer, be[blk(i, be, nv)], 0, 0))
    grid_spec = pltpu.PrefetchScalarGridSpec(
        num_scalar_prefetch=2,
        grid=(nb,),
        in_specs=[pl.BlockSpec(memory_space=pl.ANY),
                  wspec(D, EXPERT_FF), wspec(D, EXPERT_FF), wspec(EXPERT_FF, D)],
        out_specs=pl.BlockSpec(memory_space=pl.ANY),
        scratch_shapes=[pltpu.VMEM((X_RING, BLOCK_ROWS, D), BF16), pltpu.VMEM((Y_RING, BLOCK_ROWS, D), BF16),
                        pltpu.VMEM((D, 2 * EXPERT_FF), BF16), pltpu.VMEM((EXPERT_FF, D), BF16),
                        pltpu.SemaphoreType.DMA((X_RING,)), pltpu.SemaphoreType.DMA((Y_RING,))],
    )
    return pl.pallas_call(
        _expert_kernel,
        grid_spec=grid_spec,
        out_shape=jax.ShapeDtypeStruct((n_rows, D), BF16),
        compiler_params=_cparams(("arbitrary",)),
        name="experts",
    )(bexp, nvalid, xs, wg, wu, wd)


def _combine_kernel(off_ref, m_ref, base_ref, tot_ref, ys_hbm, posk_ref, wk_ref, h_ref, x_ref, g2_ref,
                    wsgu_ref, wsd_ref, o_ref, ys_ref, pw_ref, sem):
    i = pl.program_id(0)
    tm = h_ref.shape[0]
    slot = i % 2

    def fetch(tile, buf_slot):
        def run(e, carry):
            idx = tile * N_EXPERTS + e
            _start_rows(m_ref[idx], ys_hbm, off_ref[idx], ys_ref.at[buf_slot], base_ref[idx],
                        sem.at[buf_slot])
            return carry
        lax.fori_loop(0, N_EXPERTS, run, 0)

    @pl.when(i == 0)
    def _():
        ys_ref[...] = jnp.zeros_like(ys_ref)
        fetch(i, slot)

    @pl.when(i + 1 < pl.num_programs(0))
    def _():
        fetch(i + 1, 1 - slot)

    h = h_ref[...]
    gu = _dot(h, wsgu_ref[...])
    g = gu[:, :EXPERT_FF]
    shared = _dot((g * _sigmoid(g) * gu[:, EXPERT_FF:]).astype(BF16), wsd_ref[...])

    posk = posk_ref[0]
    wk = wk_ref[0]
    rows = lax.broadcasted_iota(jnp.int32, (MXU_DIM, tm), 0).astype(F32).astype(BF16)

    def weight_chunks(lo, hi):
        for c in range(lo, hi):
            rel = posk - float(c * MXU_DIM)
            pw = jnp.zeros((MXU_DIM, tm), BF16)
            for k in range(TOP_K):
                wrow = jnp.broadcast_to(wk[k:k + 1, :], (MXU_DIM, tm)).astype(BF16)
                pw = jnp.where(rows == rel[k:k + 1, :].astype(BF16), wrow, pw)
            pw_ref[c * MXU_DIM:(c + 1) * MXU_DIM, :] = pw

    def unsort(lo, hi):
        return lax.dot_general(pw_ref[lo:hi, :], ys_ref[slot, lo:hi, :], (((0,), (0,)), ((), ())),
                               preferred_element_type=F32)

    weight_chunks(0, K_SORT_COMMON // MXU_DIM)
    _wait_rows(tot_ref[i], ys_hbm, ys_ref.at[slot], sem.at[slot])
    o_ref[...] = x_ref[...] + g2_ref[0] * (unsort(0, K_SORT_COMMON) + shared)

    @pl.when(tot_ref[i] * RUN_ALIGN > K_SORT_COMMON)
    def _():
        weight_chunks(K_SORT_COMMON // MXU_DIM, K_SORT // MXU_DIM)
        o_ref[...] = o_ref[...] + g2_ref[0] * unsort(K_SORT_COMMON, K_SORT)


def _combine(off, m16, base, tot, ys, posk, wk, h2, x1, g2, wsgu, wsd, tiles_per_batch):
    T, D = h2.shape
    tm = TM_MOE
    const = lambda i, *_: (0, 0)
    tok = lambda w: pl.BlockSpec((tm, w), lambda i, *_: (i, 0))
    sel = pl.BlockSpec((1, TOP_K, tm), lambda i, *_: (i, 0, 0))
    grid_spec = pltpu.PrefetchScalarGridSpec(
        num_scalar_prefetch=4,
        grid=(T // tm,),
        in_specs=[pl.BlockSpec(memory_space=pl.ANY), sel, sel, tok(D), tok(D),
                  pl.BlockSpec((1, 1, D), lambda i, *_: (i // tiles_per_batch, 0, 0)),
                  pl.BlockSpec(wsgu.shape, const), pl.BlockSpec(wsd.shape, const)],
        out_specs=tok(D),
        scratch_shapes=[pltpu.VMEM((2, K_SORT, D), BF16),
                        pltpu.VMEM((K_SORT, tm), BF16),
                        pltpu.SemaphoreType.DMA((2,))],
    )
    return pl.pallas_call(
        _combine_kernel,
        grid_spec=grid_spec,
        out_shape=jax.ShapeDtypeStruct((T, D), F32),
        compiler_params=_cparams(("arbitrary",), unchecked_dma=True),
        name="combine",
    )(off, m16, base, tot, ys, posk, wk, h2, x1, g2, wsgu, wsd)


def _moe(h2, x1, g2, posk, wk, cnt, wg, wu, wd, layer, wsgu, wsd, tiles_per_batch):
    T, D = h2.shape
    nt = T // TM_MOE

    pad = cnt[:, :, 0].astype(jnp.int32)
    base = jnp.cumsum(pad, axis=1) - pad
    tile_off = jnp.cumsum(pad, axis=0) - pad
    total = jnp.sum(pad, axis=0)
    region = ((total + BLOCK_ROWS - 1) // BLOCK_ROWS) * BLOCK_ROWS
    rend = jnp.cumsum(region)
    rstart = rend - region
    off = rstart[None, :] + tile_off
    nb_max = -(-(TOP_K * T + nt * N_EXPERTS * (RUN_ALIGN - 1)) // BLOCK_ROWS) + N_EXPERTS
    blk_row = jnp.arange(nb_max, dtype=jnp.int32) * BLOCK_ROWS
    bexp = jnp.minimum(jnp.sum((rend[None, :] <= blk_row[:, None]).astype(jnp.int32), axis=1), N_EXPERTS - 1)
    nvalid = (rend[-1:] // BLOCK_ROWS).astype(jnp.int32)
    flat = lambda a: a.reshape(-1).astype(jnp.int32)

    m16 = flat(pad // RUN_ALIGN)
    tot = flat(jnp.sum(pad, axis=1) // RUN_ALIGN)
    xs = _dispatch(flat(off), m16, flat(base), tot, flat(rstart + total),
                   flat((region - total) // RUN_ALIGN), h2, posk, nb_max * BLOCK_ROWS)
    ys = _experts(bexp, nvalid, xs, wg, wu, wd, layer)
    return _combine(flat(off), m16, flat(base), tot, ys, posk, wk, h2, x1, g2, wsgu, wsd, tiles_per_batch)


def kernel(x, c, ada_w, ada_b, mix_norm_g, w_in, b_fgate, qn_a, kn_a, qn_b, kn_b, rel_bias, w_proj_a,
           w_proj_b, w_out, ffn_norm_g, w_router, router_bias, w_gate_e, w_up_e, w_down_e, w_gate_s,
           w_up_s, w_down_s):
    B, S, D = x.shape
    L = ada_w.shape[0]
    T = B * S
    assert D == D_MODEL and S % TM_PRE == 0 and TQ_FOX == TM_PRE and S % TM_MOE == 0 and S % (BAND_TILES * TQ_BAND) == 0

    mod = _adaln(c, ada_w, ada_b).reshape(L, B, 6, 1, D)

    hid = jnp.arange(MXU_DIM) // HEAD_DIM
    bd = jnp.where(hid[:, None] == hid[None, :], 1.0 / HEAD_DIM, 0.0).astype(BF16)
    r = jnp.arange(TM_PRE)
    tri = (r[:, None] <= r[None, :]).astype(BF16)
    r = jnp.arange(TM_MOE)
    ustrict = (r[:, None] < r[None, :]).astype(BF16)
    r = jnp.arange(N_EXPERTS)
    lstrict = (r[None, :] < r[:, None]).astype(BF16)

    q_scale = 1.0 / math.sqrt(HEAD_DIM)
    for l in range(L):
        sh1, sc1, g1, sh2, sc2, g2 = (mod[l, :, j] for j in range(6))
        w = w_in[l]
        cols = lambda c: w[:, c * WIDTH:(c + 1) * WIDTH]
        wqk = jnp.concatenate([cols(0), cols(1), cols(3), cols(4)], axis=1).astype(BF16)
        wvt = jnp.concatenate([cols(2), cols(5)], axis=1).T.astype(BF16)
        wf = jnp.zeros((2 * N_HEADS, D), BF16).at[:N_HEADS].set(w[:, 6 * WIDTH:6 * WIDTH + N_HEADS].T.astype(BF16))
        wg = w[:, 6 * WIDTH + N_HEADS:].astype(BF16)
        hn = jnp.zeros((8, WIDTH), F32)
        hn = hn.at[0].set(jnp.tile(qn_a[l], N_HEADS) * (q_scale * LOG2E)).at[1].set(jnp.tile(kn_a[l], N_HEADS))
        hn = hn.at[2].set(jnp.tile(qn_b[l], N_HEADS) * (q_scale * LOG2E)).at[3].set(jnp.tile(kn_b[l], N_HEADS))

        qk, vt, dec, cbase, ga, gb = _premix(x, sh1, sc1, mix_norm_g[l][None], wqk, wvt, wf, wg,
                                             b_fgate[l][:, None], hn, bd, tri)
        ya = _band_attention(qk, vt, _band_bias(rel_bias[l]))
        yb = _fox_attention(qk, vt, dec, cbase)
        x1, h2, posk, wk, cnt = _postmix(
            ya, yb, ga, gb, x, g1, w_proj_a[l].astype(BF16), w_proj_b[l].astype(BF16),
            w_out[l].astype(BF16), ffn_norm_g[l][None], sh2, sc2, w_router[l].T.astype(BF16),
            router_bias[l][:, None], ustrict, lstrict)

        wsgu = jnp.concatenate([w_gate_s[l], w_up_s[l]], axis=-1).astype(BF16)
        x = _moe(h2.reshape(T, D), x1.reshape(T, D), g2, posk, wk, cnt, w_gate_e, w_up_e, w_down_e, l,
                 wsgu, w_down_s[l].astype(BF16), S // TM_MOE).reshape(B, S, D)
    return x
```

```python
import functools
import math

import jax
import jax.numpy as jnp
from jax import lax
from jax.experimental import pallas as pl
from jax.experimental.pallas import tpu as pltpu

F32 = jnp.float32
BF16 = jnp.bfloat16

D_MODEL = 1024
HEAD_DIM = 64
N_HEADS = 8
WIDTH = N_HEADS * HEAD_DIM
CHUNK = 64
LOOKBACK_CHUNKS = 8
REL_CLIP = 128
N_EXPERTS = 64
TOP_K = 8
N_GROUPS = 8
TOPK_GROUPS = 4
EXPERT_FF = 256
ROUTED_SCALE = 2.5
EPS = 1e-6
NEG_INF = -1e30
LOG2E = math.log2(math.e)

LANES = 128
MXU_DIM = 256
VMEM_LIMIT = 56 * 1024 * 1024

TM_PRE = 512
TQ_BAND = 256
BAND_WIN = TQ_BAND + LOOKBACK_CHUNKS * CHUNK
BAND_TILES = LOOKBACK_CHUNKS * CHUNK // TQ_BAND
TQ_FOX = 512
PAIRS_PER_STEP = 4
PV_ROWS = HEAD_DIM + 16
TM_MOE = 256
RUN_ALIGN = 8
PACK_W = D_MODEL // 2
U32 = jnp.uint32
K_SORT = ((TOP_K * TM_MOE + N_EXPERTS * (RUN_ALIGN - 1) + MXU_DIM - 1) // MXU_DIM) * MXU_DIM
K_SORT_COMMON = 2304
BLOCK_ROWS = 1024


def _dot(a, b):
    return jnp.dot(a, b, preferred_element_type=F32)


def _dot_nt(a, b):
    return lax.dot_general(a, b, (((1,), (1,)), ((), ())), preferred_element_type=F32)


def _sigmoid(v):
    return 1.0 / (1.0 + jnp.exp(-v))


def _cparams(sem, unchecked_dma=False):
    return pltpu.CompilerParams(dimension_semantics=sem, vmem_limit_bytes=VMEM_LIMIT,
                                disable_bounds_checks=unchecked_dma)


def _adaln_kernel(c_ref, w_ref, b_ref, o_ref):
    c = c_ref[...]
    ca = c * _sigmoid(c)
    o_ref[0] = jnp.dot(ca, w_ref[0], preferred_element_type=F32,
                       precision=lax.Precision.HIGHEST) + b_ref[0]


def _adaln(c, ada_w, ada_b):
    L, D, N = ada_w.shape
    B = c.shape[0]
    tn = 1536
    return pl.pallas_call(
        _adaln_kernel,
        grid=(L, N // tn),
        in_specs=[pl.BlockSpec((B, D), lambda l, j: (0, 0)),
                  pl.BlockSpec((1, D, tn), lambda l, j: (l, 0, j)),
                  pl.BlockSpec((1, 1, tn), lambda l, j: (l, 0, j))],
        out_specs=pl.BlockSpec((1, B, tn), lambda l, j: (l, 0, j)),
        out_shape=jax.ShapeDtypeStruct((L, B, N), F32),
        compiler_params=_cparams(("arbitrary", "arbitrary")),
        name="adaln",
    )(c, ada_w, ada_b.reshape(L, 1, N))


def _modnorm(x, g, sc, sh):
    ms = jnp.mean(x * x, axis=-1, keepdims=True)
    y = x * lax.rsqrt(ms + EPS) * g
    return y * (1.0 + sc) + sh


def _premix_kernel(x_ref, sh_ref, sc_ref, g_ref, wqk_ref, wvt_ref, wf_ref, wg_ref, bf_ref, hn_ref, bd_ref,
                   tri_ref, qk_ref, vt_ref, dec_ref, cbase_ref, ga_ref, gb_ref, carry_ref):
    si = pl.program_id(1)
    tm = x_ref.shape[1]
    h = _modnorm(x_ref[0], g_ref[...], sc_ref[0], sh_ref[0]).astype(BF16)

    for c in range(4):
        z = _dot(h, wqk_ref[:, c * WIDTH:(c + 1) * WIDTH])
        sq = (z * z).astype(BF16)
        ms = jnp.concatenate(
            [_dot(sq[:, j * MXU_DIM:(j + 1) * MXU_DIM], bd_ref[...]) for j in range(WIDTH // MXU_DIM)],
            axis=1)
        z = z * lax.rsqrt(ms + EPS) * hn_ref[c:c + 1, :]
        qk_ref[c, 0] = z.astype(BF16)

    for c in range(2):
        vt = _dot_nt(wvt_ref[c * WIDTH:(c + 1) * WIDTH, :], h).astype(BF16)
        vt_ref[0, c * (N_HEADS // 2):(c + 1) * (N_HEADS // 2)] = vt.reshape(N_HEADS // 2, LANES, tm)

    fr = _dot_nt(wf_ref[...], h)[:N_HEADS]
    xg = fr + bf_ref[...]
    logf = jnp.minimum(xg, 0.0) - jnp.log(1.0 + jnp.exp(-jnp.abs(xg)))
    hi = logf.astype(BF16).astype(F32)
    r1 = logf - hi
    mid = r1.astype(BF16).astype(F32)
    lo = r1 - mid
    parts = jnp.concatenate([hi, mid, lo, jnp.zeros_like(hi)], axis=0)
    cs3 = _dot(parts.astype(BF16), tri_ref[...])
    cs = cs3[0:8] + cs3[8:16] + cs3[16:24]

    @pl.when(si == 0)
    def _():
        carry_ref[...] = jnp.zeros_like(carry_ref)

    drel = cs * (-LOG2E)
    dhi = drel.astype(BF16).astype(F32)
    dlo = drel - dhi
    dec = jnp.concatenate([dhi, dlo, jnp.zeros((LANES - 2 * N_HEADS, tm), F32)], axis=0)
    dec_ref[0] = dec.T.astype(BF16)
    carry = carry_ref[:, 0:1]
    cbase_ref[0, 0] = jnp.broadcast_to(carry * (-LOG2E), (N_HEADS, tm))
    carry_ref[...] = jnp.broadcast_to(carry + cs[:, tm - 1:tm], carry_ref.shape)

    for c in range(4):
        zg = _dot(h, wg_ref[:, c * WIDTH:(c + 1) * WIDTH])
        sg = _sigmoid(zg).astype(BF16)
        if c < 2:
            ga_ref[0, :, c * WIDTH:(c + 1) * WIDTH] = sg
        else:
            gb_ref[0, :, (c - 2) * WIDTH:(c - 1) * WIDTH] = sg


def _premix(x, sh, sc, g, wqk, wvt, wf, wg, bf, hn, bd, tri):
    B, S, D = x.shape
    tm = TM_PRE
    const = lambda b, s: (0, 0)
    return pl.pallas_call(
        _premix_kernel,
        grid=(B, S // tm),
        in_specs=[pl.BlockSpec((1, tm, D), lambda b, s: (b, s, 0)),
                  pl.BlockSpec((1, 1, D), lambda b, s: (b, 0, 0)),
                  pl.BlockSpec((1, 1, D), lambda b, s: (b, 0, 0)),
                  pl.BlockSpec((1, D), const),
                  pl.BlockSpec(wqk.shape, const),
                  pl.BlockSpec(wvt.shape, const),
                  pl.BlockSpec(wf.shape, const),
                  pl.BlockSpec(wg.shape, const),
                  pl.BlockSpec(bf.shape, const),
                  pl.BlockSpec(hn.shape, const),
                  pl.BlockSpec(bd.shape, const),
                  pl.BlockSpec(tri.shape, const)],
        out_specs=[pl.BlockSpec((4, 1, tm, WIDTH), lambda b, s: (0, b, s, 0)),
                   pl.BlockSpec((1, N_HEADS, LANES, tm), lambda b, s: (b, 0, 0, s)),
                   pl.BlockSpec((1, tm, LANES), lambda b, s: (b, s, 0)),
                   pl.BlockSpec((1, 1, N_HEADS, tm), lambda b, s: (b, s, 0, 0)),
                   pl.BlockSpec((1, tm, D), lambda b, s: (b, s, 0)),
                   pl.BlockSpec((1, tm, D), lambda b, s: (b, s, 0))],
        out_shape=[jax.ShapeDtypeStruct((4, B, S, WIDTH), BF16),
                   jax.ShapeDtypeStruct((B, N_HEADS, LANES, S), BF16),
                   jax.ShapeDtypeStruct((B, S, LANES), BF16),
                   jax.ShapeDtypeStruct((B, S // tm, N_HEADS, tm), F32),
                   jax.ShapeDtypeStruct((B, S, D), BF16),
                   jax.ShapeDtypeStruct((B, S, D), BF16)],
        scratch_shapes=[pltpu.VMEM((N_HEADS, LANES), F32)],
        compiler_params=_cparams(("arbitrary", "arbitrary")),
        name="premix",
    )(x, sh, sc, g, wqk, wvt, wf, wg, bf, hn, bd, tri)


def _values_with_ones(vt, hh):
    ones = jnp.ones((PV_ROWS - HEAD_DIM, vt.shape[1]), vt.dtype)
    return jnp.concatenate([vt[hh * HEAD_DIM:(hh + 1) * HEAD_DIM, :], ones], axis=0)


def _pair_finish_t(acc0, acc1):
    out_t = jnp.concatenate([acc[:HEAD_DIM] / acc[HEAD_DIM:HEAD_DIM + 1, :] for acc in (acc0, acc1)], axis=0)
    return out_t.T


def _band_kernel(q_ref, kprev_ref, kcur_ref, vprev_ref, vcur_ref, bias_ref, o_ref):
    tq = TQ_BAND
    step_start = pl.program_id(2) * (BAND_TILES * tq)
    lane = lax.broadcasted_iota(jnp.int32, (1, LANES), 1)
    krow = lax.broadcasted_iota(jnp.int32, (BAND_WIN, 1), 0)
    pieces = [(ref, j) for ref in (kprev_ref, kcur_ref) for j in range(BAND_TILES)]
    vpieces = [(ref, j) for ref in (vprev_ref, vcur_ref) for j in range(BAND_TILES)]
    units = [(u, pp, hh) for u in range(BAND_TILES) for pp in range(PAIRS_PER_STEP) for hh in range(2)]
    scores = []
    for u, pp, hh in units:
        q = q_ref[u * tq:(u + 1) * tq, pp * LANES:(pp + 1) * LANES]
        qm = jnp.where((lane // HEAD_DIM) == hh, q, jnp.zeros_like(q))
        scores.append(jnp.concatenate(
            [_dot_nt(ref[j * tq:(j + 1) * tq, pp * LANES:(pp + 1) * LANES], qm)
             for ref, j in pieces[u:u + 3]], axis=0))
    accs = []
    for n, (u, pp, hh) in enumerate(units):
        kvalid = (krow + (step_start + u * tq - LOOKBACK_CHUNKS * CHUNK)) >= 0
        s = jnp.where(kvalid, scores[n] + bias_ref[2 * pp + hh], NEG_INF)
        m = jnp.max(s, axis=0, keepdims=True)
        p = jnp.exp2(s - m).astype(BF16)
        acc = None
        for c, (ref, j) in enumerate(vpieces[u:u + 3]):
            t = _dot(_values_with_ones(ref[pp, :, j * tq:(j + 1) * tq], hh), p[c * tq:(c + 1) * tq, :])
            acc = t if acc is None else acc + t
        accs.append(acc)
    for n in range(0, len(units), 2):
        u, pp, _ = units[n]
        o_ref[u * tq:(u + 1) * tq, pp * LANES:(pp + 1) * LANES] = _pair_finish_t(
            accs[n], accs[n + 1]).astype(o_ref.dtype)


def _band_attention(qk, vt, bias_t):
    _, B, S, _ = qk.shape
    rows = BAND_TILES * TQ_BAND
    groups = N_HEADS // 2 // PAIRS_PER_STEP
    gw = PAIRS_PER_STEP * LANES

    def k_spec(back):
        return pl.BlockSpec((None, None, rows, gw),
                            lambda b, g, qi: (1, b, jnp.maximum(qi - back, 0), g))

    def v_spec(back):
        return pl.BlockSpec((None, PAIRS_PER_STEP, LANES, rows),
                            lambda b, g, qi: (b, g, 0, jnp.maximum(qi - back, 0)))

    return pl.pallas_call(
        _band_kernel,
        grid=(B, groups, S // rows),
        in_specs=[pl.BlockSpec((None, None, rows, gw), lambda b, g, qi: (0, b, qi, g)),
                  k_spec(1), k_spec(0), v_spec(1), v_spec(0),
                  pl.BlockSpec((2 * PAIRS_PER_STEP, BAND_WIN, TQ_BAND), lambda b, g, qi: (g, 0, 0))],
        out_specs=pl.BlockSpec((None, rows, gw), lambda b, g, qi: (b, qi, g)),
        out_shape=jax.ShapeDtypeStruct((B, S, WIDTH), BF16),
        compiler_params=_cparams(("arbitrary", "arbitrary", "arbitrary")),
        name="band_attn",
    )(qk, qk, qk, vt, vt, bias_t)


def _band_bias(rel_table):
    pad = LOOKBACK_CHUNKS * CHUNK
    i = jnp.arange(TQ_BAND)[:, None]
    j = jnp.arange(BAND_WIN)[None, :]
    period = 1024
    assert BAND_WIN + TQ_BAND <= period
    m = jnp.arange(period)
    rel = jnp.where(m <= BAND_WIN, pad - m, pad + period - m)
    g = rel_table[:, jnp.clip(rel, -REL_CLIP, REL_CLIP) + REL_CLIP].astype(F32)
    bias = jnp.tile(g, (1, TQ_BAND))[:, :TQ_BAND * (period - 1)].reshape(-1, TQ_BAND, period - 1)[:, :, :BAND_WIN]
    c0 = (i // CHUNK) * CHUNK
    inband = (j >= c0) & (j < c0 + pad + CHUNK)
    return jnp.where(inband[None], bias * LOG2E, NEG_INF).transpose(0, 2, 1)


FOX_DIAG, FOX_TWO_PAST, FOX_PAST_DIAG = 0, 1, 2


def _fox_kernel(qi_tab, ka_tab, kb_tab, kind_tab, first_tab, q_ref, ka_ref, kb_ref, vta_ref, vtb_ref,
                deca_ref, decb_ref, cba_ref, cbb_ref, o_ref, acc_ref, m_ref):
    grp = pl.program_id(1)
    t = pl.program_id(2)
    kind = kind_tab[t]
    tq, tk = q_ref.shape[0], ka_ref.shape[0]
    lane2 = lax.broadcasted_iota(jnp.int32, (1, 2 * LANES), 1)
    heads = [(pp, hh) for pp in range(PAIRS_PER_STEP) for hh in range(2)]
    tile_a = (ka_ref, vta_ref, deca_ref, cba_ref)
    tile_b = (kb_ref, vtb_ref, decb_ref, cbb_ref)

    @pl.when(first_tab[t] == 1)
    def _():
        acc_ref[...] = jnp.zeros_like(acc_ref)
        m_ref[...] = jnp.full_like(m_ref, NEG_INF)

    def step(tiles):
        ones = jnp.where(lax.broadcasted_iota(jnp.int32, (tq, LANES), 1) < 2 * N_HEADS, 1.0, 0.0).astype(BF16)
        causal = (lax.broadcasted_iota(jnp.int32, (tk, tq), 0)
                  <= lax.broadcasted_iota(jnp.int32, (tk, tq), 1))
        scores = []
        for pp, hh in heads:
            head = 2 * (PAIRS_PER_STEP * grp + pp) + hh
            q_aug = jnp.concatenate([q_ref[:, pp * LANES:(pp + 1) * LANES], ones], axis=1)
            use = (((lane2 // HEAD_DIM) == hh) | (lane2 == LANES + head) | (lane2 == LANES + N_HEADS + head))
            qm = jnp.where(use, q_aug, jnp.zeros_like(q_aug))
            per_tile = []
            for (k_ref, _, dec_ref, _), diagonal in tiles:
                k_aug = jnp.concatenate([k_ref[:, pp * LANES:(pp + 1) * LANES], dec_ref[...]], axis=1)
                s = _dot_nt(k_aug, qm)
                per_tile.append(jnp.where(causal, s, NEG_INF) if diagonal else s)
            scores.append(per_tile)
        for n, (pp, hh) in enumerate(heads):
            head = 2 * (PAIRS_PER_STEP * grp + pp) + hh
            bases = [cb_ref[pl.ds(head, 1), :] for (_, _, _, cb_ref), _ in tiles]
            m_old = m_ref[n]
            m_new = m_old
            for s, base in zip(scores[n], bases):
                m_new = jnp.maximum(m_new, jnp.max(s, axis=0, keepdims=True) + base)
            acc = acc_ref[n] * jnp.exp2(m_old - m_new)
            for s, base, ((_, vt_ref, _, _), _) in zip(scores[n], bases, tiles):
                p = jnp.exp2(s - (m_new - base)).astype(BF16)
                acc = acc + _dot(_values_with_ones(vt_ref[pp], hh), p)
            acc_ref[n] = acc
            m_ref[n] = m_new

    def finish():
        for pp in range(PAIRS_PER_STEP):
            o_ref[:, pp * LANES:(pp + 1) * LANES] = _pair_finish_t(
                acc_ref[2 * pp], acc_ref[2 * pp + 1]).astype(o_ref.dtype)

    @pl.when(kind == FOX_TWO_PAST)
    def _():
        step([(tile_a, False), (tile_b, False)])

    @pl.when(kind == FOX_PAST_DIAG)
    def _():
        step([(tile_a, False), (tile_b, True)])
        finish()

    @pl.when(kind == FOX_DIAG)
    def _():
        step([(tile_a, True)])
        finish()


def _fox_attention(qkv, vt, dec, cbase):
    _, B, S, _ = qkv.shape
    tq = TQ_FOX
    nq = S // tq
    groups = N_HEADS // 2 // PAIRS_PER_STEP
    gw = PAIRS_PER_STEP * LANES
    steps = []
    for i in range(nq):
        past = list(range(i))
        first = 1
        while len(past) >= 2:
            steps.append((i, past[0], past[1], FOX_TWO_PAST, first))
            past, first = past[2:], 0
        if past:
            steps.append((i, past[0], i, FOX_PAST_DIAG, first))
        else:
            steps.append((i, i, i, FOX_DIAG, first))
    tabs = [jnp.array([s[c] for s in steps], jnp.int32) for c in range(5)]

    def key_specs(col):
        tile = lambda *a: a[3 + col][a[2]]
        return [pl.BlockSpec((None, None, tq, gw), lambda b, g, t, *tb: (3, b, tile(b, g, t, *tb), g)),
                pl.BlockSpec((None, PAIRS_PER_STEP, LANES, tq),
                             lambda b, g, t, *tb: (b, groups + g, 0, tile(b, g, t, *tb))),
                pl.BlockSpec((None, tq, LANES), lambda b, g, t, *tb: (b, tile(b, g, t, *tb), 0)),
                pl.BlockSpec((None, None, N_HEADS, tq), lambda b, g, t, *tb: (b, tile(b, g, t, *tb), 0, 0))]

    ka, va, da, ca = key_specs(1)
    kb, vb, db, cb = key_specs(2)
    grid_spec = pltpu.PrefetchScalarGridSpec(
        num_scalar_prefetch=5,
        grid=(B, groups, len(steps)),
        in_specs=[pl.BlockSpec((None, None, tq, gw), lambda b, g, t, *tb: (2, b, tb[0][t], g)),
                  ka, kb, va, vb, da, db, ca, cb],
        out_specs=pl.BlockSpec((None, tq, gw), lambda b, g, t, *tb: (b, tb[0][t], g)),
        scratch_shapes=[pltpu.VMEM((2 * PAIRS_PER_STEP, PV_ROWS, tq), F32),
                        pltpu.VMEM((2 * PAIRS_PER_STEP, 1, tq), F32)],
    )
    return pl.pallas_call(
        _fox_kernel,
        grid_spec=grid_spec,
        out_shape=jax.ShapeDtypeStruct((B, S, WIDTH), BF16),
        compiler_params=_cparams(("arbitrary", "arbitrary", "arbitrary")),
        name="fox_attn",
    )(*tabs, qkv, qkv, qkv, vt, vt, dec, dec, cbase, cbase)


def _postmix_kernel(ya_ref, yb_ref, ga_ref, gb_ref, x_ref, g1_ref, wpa_ref, wpb_ref, wo_ref,
                    g_ref, sh_ref, sc_ref, wr_ref, rb_ref, ustrict_ref, lstrict_ref,
                    x1_ref, h2_ref, posk_ref, wk_ref, cnt_ref):
    ua = _dot(ya_ref[0], wpa_ref[...])
    ub = _dot(yb_ref[0], wpb_ref[...])
    m = ga_ref[0].astype(F32) * ua + gb_ref[0].astype(F32) * ub
    mo = _dot(m.astype(BF16), wo_ref[...])
    x1 = x_ref[0] + g1_ref[0] * mo
    x1_ref[0] = x1
    h2 = _modnorm(x1, g_ref[...], sc_ref[0], sh_ref[0]).astype(BF16)
    h2_ref[0] = h2
    for j in range(h2.shape[0] // TM_MOE):
        posk, wk, cnt = _route(h2[j * TM_MOE:(j + 1) * TM_MOE], wr_ref[...], rb_ref[...],
                               ustrict_ref[...], lstrict_ref[...])
        posk_ref[j] = posk
        wk_ref[j] = wk
        cnt_ref[j] = cnt


def _postmix(ya, yb, ga, gb, x, g1, wpa, wpb, wo, g, sh, sc, wr_t, rb, ustrict, lstrict):
    B, S, D = x.shape
    tm = TM_PRE
    sub_tiles = tm // TM_MOE
    nt = B * S // TM_MOE
    const = lambda b, s: (0, 0)
    tok = lambda w: pl.BlockSpec((1, tm, w), lambda b, s: (b, s, 0))
    row = pl.BlockSpec((1, 1, D), lambda b, s: (b, 0, 0))
    tiles = lambda r, c: pl.BlockSpec((sub_tiles, r, c), lambda b, s: (b * (S // tm) + s, 0, 0))
    return pl.pallas_call(
        _postmix_kernel,
        grid=(B, S // tm),
        in_specs=[tok(WIDTH), tok(WIDTH), tok(D), tok(D), tok(D), row,
                  pl.BlockSpec(wpa.shape, const), pl.BlockSpec(wpb.shape, const),
                  pl.BlockSpec(wo.shape, const), pl.BlockSpec((1, D), const), row, row,
                  pl.BlockSpec(wr_t.shape, const), pl.BlockSpec(rb.shape, const),
                  pl.BlockSpec(ustrict.shape, const), pl.BlockSpec(lstrict.shape, const)],
        out_specs=[tok(D), tok(D), tiles(TOP_K, TM_MOE), tiles(TOP_K, TM_MOE), tiles(N_EXPERTS, LANES)],
        out_shape=[jax.ShapeDtypeStruct((B, S, D), F32), jax.ShapeDtypeStruct((B, S, D), BF16),
                   jax.ShapeDtypeStruct((nt, TOP_K, TM_MOE), F32),
                   jax.ShapeDtypeStruct((nt, TOP_K, TM_MOE), F32),
                   jax.ShapeDtypeStruct((nt, N_EXPERTS, LANES), F32)],
        compiler_params=_cparams(("arbitrary", "arbitrary")),
        name="postmix",
    )(ya, yb, ga, gb, x, g1, wpa, wpb, wo, g, sh, sc, wr_t, rb, ustrict, lstrict)


def _route(h, wr, rb, ustrict, lstrict):
    tm = h.shape[0]
    per_group = N_EXPERTS // N_GROUPS
    logits = _dot_nt(wr, h)
    scores = _sigmoid(logits)
    biased = scores + rb
    sub8 = lax.broadcasted_iota(jnp.int32, (per_group, tm), 0).astype(F32)
    neg = -jnp.inf

    grp_rows = []
    for g in range(N_GROUPS):
        a = biased[g * per_group:(g + 1) * per_group]
        m1 = jnp.max(a, axis=0, keepdims=True)
        i1 = jnp.min(jnp.where(a == m1, sub8, float(per_group)), axis=0, keepdims=True)
        m2 = jnp.max(jnp.where(sub8 == i1, neg, a), axis=0, keepdims=True)
        grp_rows.append(m1 + m2)
    grp = jnp.concatenate(grp_rows, axis=0)

    gsub = lax.broadcasted_iota(jnp.int32, (N_GROUPS, tm), 0).astype(F32)
    gmask = jnp.zeros((N_GROUPS, tm), F32)
    for _ in range(TOPK_GROUPS):
        mx = jnp.max(grp, axis=0, keepdims=True)
        gi = jnp.min(jnp.where(grp == mx, gsub, float(N_GROUPS)), axis=0, keepdims=True)
        sel = gsub == gi
        gmask = jnp.where(sel, 1.0, gmask)
        grp = jnp.where(sel, neg, grp)
    masked = jnp.concatenate(
        [jnp.where(gmask[g:g + 1] > 0.5, biased[g * per_group:(g + 1) * per_group], neg)
         for g in range(N_GROUPS)], axis=0)

    esub = lax.broadcasted_iota(jnp.int32, (N_EXPERTS, tm), 0).astype(F32)
    sels, ws = [], []
    for _ in range(TOP_K):
        mx = jnp.max(masked, axis=0, keepdims=True)
        ei = jnp.min(jnp.where(masked == mx, esub, float(N_EXPERTS)), axis=0, keepdims=True)
        sel = esub == ei
        sels.append(sel)
        ws.append(jnp.sum(jnp.where(sel, scores, 0.0), axis=0, keepdims=True))
        masked = jnp.where(sel, neg, masked)
    wsum = ws[0]
    for w in ws[1:]:
        wsum = wsum + w

    selmask = jnp.zeros((N_EXPERTS, tm), F32)
    for sel in sels:
        selmask = selmask + jnp.where(sel, 1.0, 0.0)
    rank = _dot(selmask.astype(BF16), ustrict)
    counts = jnp.sum(selmask, axis=1, keepdims=True)
    padded = jnp.ceil(counts * (1.0 / RUN_ALIGN)) * RUN_ALIGN
    padded_b = jnp.broadcast_to(padded, (N_EXPERTS, LANES))
    base = _dot(lstrict, padded_b.astype(BF16))
    pos = base[:, 0:1] + rank

    posk = jnp.concatenate([jnp.sum(jnp.where(sel, pos, 0.0), axis=0, keepdims=True) for sel in sels], axis=0)
    wk = jnp.concatenate([(w / wsum) * ROUTED_SCALE for w in ws], axis=0)
    return posk, wk, padded_b


def _pack_halves(v):
    half = v.shape[1] // 2
    lo = lax.bitcast_convert_type(v[:, :half], U32)
    hi = lax.bitcast_convert_type(v[:, half:], U32)
    return (lo >> 16) | (hi & U32(0xFFFF0000))


def _unpack_halves(u):
    lo = lax.bitcast_convert_type(u << 16, F32).astype(BF16)
    hi = lax.bitcast_convert_type(u & U32(0xFFFF0000), F32).astype(BF16)
    return lo, hi


def _rows_copy(m, src_ref, src0, dst_ref, dst0, sem):
    rows = m * RUN_ALIGN
    return pltpu.make_async_copy(src_ref.at[pl.ds(pl.multiple_of(src0, RUN_ALIGN), rows)],
                                 dst_ref.at[pl.ds(pl.multiple_of(dst0, RUN_ALIGN), rows)], sem)


def _start_rows(m, src_ref, src0, dst_ref, dst0, sem):
    @pl.when(m > 0)
    def _():
        _rows_copy(m, src_ref, src0, dst_ref, dst0, sem).start()


def _wait_rows(m, src_ref, dst_ref, sem):
    @pl.when(m > 0)
    def _():
        _rows_copy(m, src_ref, 0, dst_ref, 0, sem).wait()


def _dispatch_kernel(off_ref, m_ref, base_ref, tot_ref, gapoff_ref, gapm_ref, h_ref, posk_ref, xs_hbm,
                     xs_ref, zero_ref, sem):
    i = pl.program_id(0)
    tm = h_ref.shape[0]
    n_chunks = K_SORT // MXU_DIM

    @pl.when(i == 0)
    def _():
        zero_ref[...] = jnp.zeros_like(zero_ref)

        def gap(e, carry):
            _start_rows(gapm_ref[e], zero_ref, 0, xs_hbm, gapoff_ref[e], sem)
            return carry
        lax.fori_loop(0, N_EXPERTS, gap, 0)

        def gap_wait(e, carry):
            _wait_rows(gapm_ref[e], zero_ref, xs_hbm, sem)
            return carry
        lax.fori_loop(0, N_EXPERTS, gap_wait, 0)

    slot = i % 2
    xs_slot = xs_ref.at[slot]
    h = h_ref[...]
    posk = posk_ref[0]
    rows = lax.broadcasted_iota(jnp.int32, (MXU_DIM, tm), 0).astype(F32).astype(BF16)
    one = jnp.ones((MXU_DIM, tm), BF16)
    def sort_chunks(lo, hi):
        for c in range(lo, hi):
            rel = posk - float(c * MXU_DIM)
            p = jnp.zeros((MXU_DIM, tm), BF16)
            for k in range(TOP_K):
                p = jnp.where(rows == rel[k:k + 1, :].astype(BF16), one, p)
            xs_slot[c * MXU_DIM:(c + 1) * MXU_DIM, :] = _pack_halves(_dot(p, h))

    sort_chunks(0, K_SORT_COMMON // MXU_DIM)

    @pl.when(tot_ref[i] * RUN_ALIGN > K_SORT_COMMON)
    def _():
        sort_chunks(K_SORT_COMMON // MXU_DIM, n_chunks)

    @pl.when(i > 0)
    def _():
        _wait_rows(tot_ref[i - 1], xs_ref.at[1 - slot], xs_hbm, sem)

    def run(e, carry):
        idx = i * N_EXPERTS + e
        _start_rows(m_ref[idx], xs_slot, base_ref[idx], xs_hbm, off_ref[idx], sem)
        return carry
    lax.fori_loop(0, N_EXPERTS, run, 0)

    @pl.when(i == pl.num_programs(0) - 1)
    def _():
        _wait_rows(tot_ref[i], xs_slot, xs_hbm, sem)


def _dispatch(off, m16, base, tot, gapoff, gapm, h2, posk, n_rows):
    T, D = h2.shape
    tm = TM_MOE
    grid_spec = pltpu.PrefetchScalarGridSpec(
        num_scalar_prefetch=6,
        grid=(T // tm,),
        in_specs=[pl.BlockSpec((tm, D), lambda i, *_: (i, 0)),
                  pl.BlockSpec((1, TOP_K, tm), lambda i, *_: (i, 0, 0))],
        out_specs=pl.BlockSpec(memory_space=pl.ANY),
        scratch_shapes=[pltpu.VMEM((2, K_SORT, PACK_W), U32),
                        pltpu.VMEM((BLOCK_ROWS, PACK_W), U32),
                        pltpu.SemaphoreType.DMA],
    )
    return pl.pallas_call(
        _dispatch_kernel,
        grid_spec=grid_spec,
        out_shape=jax.ShapeDtypeStruct((n_rows, PACK_W), U32),
        compiler_params=_cparams(("arbitrary",), unchecked_dma=True),
        name="dispatch",
    )(off, m16, base, tot, gapoff, gapm, h2, posk)


X_RING = 3
Y_RING = 2


def _expert_kernel(bexp_ref, nvalid_ref, xs_hbm, wg_ref, wu_ref, wd_ref, ys_hbm, x_buf, y_buf, wgu_bf, wd_bf,
                   x_sem, y_sem):
    i = pl.program_id(0)
    nvalid = nvalid_ref[0]

    def rows(b):
        return pl.ds(pl.multiple_of(b * BLOCK_ROWS, BLOCK_ROWS), BLOCK_ROWS)

    def load(b):
        slot = b % X_RING
        return pltpu.make_async_copy(xs_hbm.at[rows(b)], x_buf.at[slot], x_sem.at[slot])

    def store(b):
        slot = b % Y_RING
        return pltpu.make_async_copy(y_buf.at[slot], ys_hbm.at[rows(b)], y_sem.at[slot])

    @pl.when(i == 0)
    def _():
        for b in range(X_RING - 1):
            @pl.when(b < nvalid)
            def _(b=b):
                load(b).start()

    @pl.when(i < nvalid)
    def _():
        @pl.when(i + X_RING - 1 < nvalid)
        def _():
            load(i + X_RING - 1).start()

        @pl.when((i == 0) | (bexp_ref[i] != bexp_ref[jnp.maximum(i - 1, 0)]))
        def _():
            wgu_bf[:, :EXPERT_FF] = wg_ref[...].astype(BF16)
            wgu_bf[:, EXPERT_FF:] = wu_ref[...].astype(BF16)
            wd_bf[...] = wd_ref[...].astype(BF16)

        load(i).wait()
        x_lo, x_hi = _unpack_halves(x_buf[i % X_RING])
        gu = _dot(x_lo, wgu_bf[:PACK_W, :]) + _dot(x_hi, wgu_bf[PACK_W:, :])
        g = gu[:, :EXPERT_FF]
        a = (g * _sigmoid(g) * gu[:, EXPERT_FF:]).astype(BF16)
        y = _dot(a, wd_bf[...]).astype(BF16).astype(F32)

        @pl.when(i >= Y_RING)
        def _():
            store(i - Y_RING).wait()
        y_buf[i % Y_RING] = _pack_halves(y)
        store(i).start()

        @pl.when(i == nvalid - 1)
        def _():
            for back in range(Y_RING):
                @pl.when(i - back >= 0)
                def _(back=back):
                    store(i - back).wait()


def _experts(bexp, nvalid, xs, wg, wu, wd, layer):
    n_rows, D = xs.shape[0], D_MODEL
    nb = n_rows // BLOCK_ROWS

    def blk(i, be, nv):
        return jnp.minimum(i, nv[0] - 1)

    wspec = lambda r, c: pl.BlockSpec((None, None, r, c),
                                      lambda i, be, nv: (layer, be[blk(i, be, nv)], 0, 0))
    grid_spec = pltpu.PrefetchScalarGridSpec(
        num_scalar_prefetch=2,
        grid=(nb,),
        in_specs=[pl.BlockSpec(memory_space=pl.ANY),
                  wspec(D, EXPERT_FF), wspec(D, EXPERT_FF), wspec(EXPERT_FF, D)],
        out_specs=pl.BlockSpec(memory_space=pl.ANY),
        scratch_shapes=[pltpu.VMEM((X_RING, BLOCK_ROWS, PACK_W), U32), pltpu.VMEM((Y_RING, BLOCK_ROWS, PACK_W), U32),
                        pltpu.VMEM((D, 2 * EXPERT_FF), BF16), pltpu.VMEM((EXPERT_FF, D), BF16),
                        pltpu.SemaphoreType.DMA((X_RING,)), pltpu.SemaphoreType.DMA((Y_RING,))],
    )
    return pl.pallas_call(
        _expert_kernel,
        grid_spec=grid_spec,
        out_shape=jax.ShapeDtypeStruct((n_rows, PACK_W), U32),
        compiler_params=_cparams(("arbitrary",)),
        name="experts",
    )(bexp, nvalid, xs, wg, wu, wd)


def _combine_kernel(off_ref, m_ref, base_ref, tot_ref, ys_hbm, posk_ref, wk_ref, h_ref, x_ref, g2_ref,
                    wsgu_ref, wsd_ref, o_ref, ys_ref, pw_ref, sem):
    i = pl.program_id(0)
    tm = h_ref.shape[0]
    slot = i % 2

    def fetch(tile, buf_slot):
        def run(e, carry):
            idx = tile * N_EXPERTS + e
            _start_rows(m_ref[idx], ys_hbm, off_ref[idx], ys_ref.at[buf_slot], base_ref[idx],
                        sem.at[buf_slot])
            return carry
        lax.fori_loop(0, N_EXPERTS, run, 0)

    @pl.when(i == 0)
    def _():
        ys_ref[...] = jnp.zeros_like(ys_ref)
        fetch(i, slot)

    @pl.when(i + 1 < pl.num_programs(0))
    def _():
        fetch(i + 1, 1 - slot)

    h = h_ref[...]
    gu = _dot(h, wsgu_ref[...])
    g = gu[:, :EXPERT_FF]
    shared = _dot((g * _sigmoid(g) * gu[:, EXPERT_FF:]).astype(BF16), wsd_ref[...])

    posk = posk_ref[0]
    wk = wk_ref[0]
    rows = lax.broadcasted_iota(jnp.int32, (MXU_DIM, tm), 0).astype(F32).astype(BF16)

    def weight_chunks(lo, hi):
        for c in range(lo, hi):
            rel = posk - float(c * MXU_DIM)
            pw = jnp.zeros((MXU_DIM, tm), BF16)
            for k in range(TOP_K):
                wrow = jnp.broadcast_to(wk[k:k + 1, :], (MXU_DIM, tm)).astype(BF16)
                pw = jnp.where(rows == rel[k:k + 1, :].astype(BF16), wrow, pw)
            pw_ref[c * MXU_DIM:(c + 1) * MXU_DIM, :] = pw

    def unsort(lo, hi):
        pw = pw_ref[lo:hi, :]
        halves = [lax.dot_general(pw, y, (((0,), (0,)), ((), ())), preferred_element_type=F32)
                  for y in _unpack_halves(ys_ref[slot, lo:hi, :])]
        return jnp.concatenate(halves, axis=1)

    weight_chunks(0, K_SORT_COMMON // MXU_DIM)
    _wait_rows(tot_ref[i], ys_hbm, ys_ref.at[slot], sem.at[slot])
    o_ref[...] = x_ref[...] + g2_ref[0] * (unsort(0, K_SORT_COMMON) + shared)

    @pl.when(tot_ref[i] * RUN_ALIGN > K_SORT_COMMON)
    def _():
        weight_chunks(K_SORT_COMMON // MXU_DIM, K_SORT // MXU_DIM)
        o_ref[...] = o_ref[...] + g2_ref[0] * unsort(K_SORT_COMMON, K_SORT)


def _combine(off, m16, base, tot, ys, posk, wk, h2, x1, g2, wsgu, wsd, tiles_per_batch):
    T, D = h2.shape
    tm = TM_MOE
    const = lambda i, *_: (0, 0)
    tok = lambda w: pl.BlockSpec((tm, w), lambda i, *_: (i, 0))
    sel = pl.BlockSpec((1, TOP_K, tm), lambda i, *_: (i, 0, 0))
    grid_spec = pltpu.PrefetchScalarGridSpec(
        num_scalar_prefetch=4,
        grid=(T // tm,),
        in_specs=[pl.BlockSpec(memory_space=pl.ANY), sel, sel, tok(D), tok(D),
                  pl.BlockSpec((1, 1, D), lambda i, *_: (i // tiles_per_batch, 0, 0)),
                  pl.BlockSpec(wsgu.shape, const), pl.BlockSpec(wsd.shape, const)],
        out_specs=tok(D),
        scratch_shapes=[pltpu.VMEM((2, K_SORT, PACK_W), U32),
                        pltpu.VMEM((K_SORT, tm), BF16),
                        pltpu.SemaphoreType.DMA((2,))],
    )
    return pl.pallas_call(
        _combine_kernel,
        grid_spec=grid_spec,
        out_shape=jax.ShapeDtypeStruct((T, D), F32),
        compiler_params=_cparams(("arbitrary",), unchecked_dma=True),
        name="combine",
    )(off, m16, base, tot, ys, posk, wk, h2, x1, g2, wsgu, wsd)


def _moe(h2, x1, g2, posk, wk, cnt, wg, wu, wd, layer, wsgu, wsd, tiles_per_batch):
    T, D = h2.shape
    nt = T // TM_MOE

    pad = cnt[:, :, 0].astype(jnp.int32)
    base = jnp.cumsum(pad, axis=1) - pad
    tile_off = jnp.cumsum(pad, axis=0) - pad
    total = jnp.sum(pad, axis=0)
    region = ((total + BLOCK_ROWS - 1) // BLOCK_ROWS) * BLOCK_ROWS
    rend = jnp.cumsum(region)
    rstart = rend - region
    off = rstart[None, :] + tile_off
    nb_max = -(-(TOP_K * T + nt * N_EXPERTS * (RUN_ALIGN - 1)) // BLOCK_ROWS) + N_EXPERTS
    blk_row = jnp.arange(nb_max, dtype=jnp.int32) * BLOCK_ROWS
    bexp = jnp.minimum(jnp.sum((rend[None, :] <= blk_row[:, None]).astype(jnp.int32), axis=1), N_EXPERTS - 1)
    nvalid = (rend[-1:] // BLOCK_ROWS).astype(jnp.int32)
    flat = lambda a: a.reshape(-1).astype(jnp.int32)

    m16 = flat(pad // RUN_ALIGN)
    tot = flat(jnp.sum(pad, axis=1) // RUN_ALIGN)
    xs = _dispatch(flat(off), m16, flat(base), tot, flat(rstart + total),
                   flat((region - total) // RUN_ALIGN), h2, posk, nb_max * BLOCK_ROWS)
    ys = _experts(bexp, nvalid, xs, wg, wu, wd, layer)
    return _combine(flat(off), m16, flat(base), tot, ys, posk, wk, h2, x1, g2, wsgu, wsd, tiles_per_batch)


def kernel(x, c, ada_w, ada_b, mix_norm_g, w_in, b_fgate, qn_a, kn_a, qn_b, kn_b, rel_bias, w_proj_a,
           w_proj_b, w_out, ffn_norm_g, w_router, router_bias, w_gate_e, w_up_e, w_down_e, w_gate_s,
           w_up_s, w_down_s):
    B, S, D = x.shape
    L = ada_w.shape[0]
    T = B * S
    assert D == D_MODEL and S % TM_PRE == 0 and TQ_FOX == TM_PRE and S % TM_MOE == 0 and S % (BAND_TILES * TQ_BAND) == 0

    mod = _adaln(c, ada_w, ada_b).reshape(L, B, 6, 1, D)

    hid = jnp.arange(MXU_DIM) // HEAD_DIM
    bd = jnp.where(hid[:, None] == hid[None, :], 1.0 / HEAD_DIM, 0.0).astype(BF16)
    r = jnp.arange(TM_PRE)
    tri = (r[:, None] <= r[None, :]).astype(BF16)
    r = jnp.arange(TM_MOE)
    ustrict = (r[:, None] < r[None, :]).astype(BF16)
    r = jnp.arange(N_EXPERTS)
    lstrict = (r[None, :] < r[:, None]).astype(BF16)

    q_scale = 1.0 / math.sqrt(HEAD_DIM)
    for l in range(L):
        sh1, sc1, g1, sh2, sc2, g2 = (mod[l, :, j] for j in range(6))
        w = w_in[l]
        cols = lambda c: w[:, c * WIDTH:(c + 1) * WIDTH]
        wqk = jnp.concatenate([cols(0), cols(1), cols(3), cols(4)], axis=1).astype(BF16)
        wvt = jnp.concatenate([cols(2), cols(5)], axis=1).T.astype(BF16)
        wf = jnp.zeros((2 * N_HEADS, D), BF16).at[:N_HEADS].set(w[:, 6 * WIDTH:6 * WIDTH + N_HEADS].T.astype(BF16))
        wg = w[:, 6 * WIDTH + N_HEADS:].astype(BF16)
        hn = jnp.zeros((8, WIDTH), F32)
        hn = hn.at[0].set(jnp.tile(qn_a[l], N_HEADS) * (q_scale * LOG2E)).at[1].set(jnp.tile(kn_a[l], N_HEADS))
        hn = hn.at[2].set(jnp.tile(qn_b[l], N_HEADS) * (q_scale * LOG2E)).at[3].set(jnp.tile(kn_b[l], N_HEADS))

        qk, vt, dec, cbase, ga, gb = _premix(x, sh1, sc1, mix_norm_g[l][None], wqk, wvt, wf, wg,
                                             b_fgate[l][:, None], hn, bd, tri)
        ya = _band_attention(qk, vt, _band_bias(rel_bias[l]))
        yb = _fox_attention(qk, vt, dec, cbase)
        x1, h2, posk, wk, cnt = _postmix(
            ya, yb, ga, gb, x, g1, w_proj_a[l].astype(BF16), w_proj_b[l].astype(BF16),
            w_out[l].astype(BF16), ffn_norm_g[l][None], sh2, sc2, w_router[l].T.astype(BF16),
            router_bias[l][:, None], ustrict, lstrict)

        wsgu = jnp.concatenate([w_gate_s[l], w_up_s[l]], axis=-1).astype(BF16)
        x = _moe(h2.reshape(T, D), x1.reshape(T, D), g2, posk, wk, cnt, w_gate_e, w_up_e, w_down_e, l,
                 wsgu, w_down_s[l].astype(BF16), S // TM_MOE).reshape(B, S, D)
    return x
```

```python
import functools
import math

import jax
import jax.numpy as jnp
from jax import lax
from jax.experimental import pallas as pl
from jax.experimental.pallas import tpu as pltpu

F32 = jnp.float32
BF16 = jnp.bfloat16

D_MODEL = 1024
HEAD_DIM = 64
N_HEADS = 8
WIDTH = N_HEADS * HEAD_DIM
CHUNK = 64
LOOKBACK_CHUNKS = 8
REL_CLIP = 128
N_EXPERTS = 64
TOP_K = 8
N_GROUPS = 8
TOPK_GROUPS = 4
EXPERT_FF = 256
ROUTED_SCALE = 2.5
EPS = 1e-6
NEG_INF = -1e30
LOG2E = math.log2(math.e)

LANES = 128
MXU_DIM = 256
VMEM_LIMIT = 56 * 1024 * 1024

TM_PRE = 512
TQ_BAND = 256
BAND_WIN = TQ_BAND + LOOKBACK_CHUNKS * CHUNK
BAND_TILES = LOOKBACK_CHUNKS * CHUNK // TQ_BAND
TQ_FOX = 512
PAIRS_PER_STEP = 4
PV_ROWS = HEAD_DIM + 16
TM_MOE = 256
RUN_ALIGN = 8
PACK_W = D_MODEL // 2
U32 = jnp.uint32
K_SORT = ((TOP_K * TM_MOE + N_EXPERTS * (RUN_ALIGN - 1) + MXU_DIM - 1) // MXU_DIM) * MXU_DIM
K_SORT_COMMON = 2304
BLOCK_ROWS = 1024


def _dot(a, b):
    return jnp.dot(a, b, preferred_element_type=F32)


def _dot_nt(a, b):
    return lax.dot_general(a, b, (((1,), (1,)), ((), ())), preferred_element_type=F32)


def _sigmoid(v):
    return 1.0 / (1.0 + jnp.exp(-v))


def _cparams(sem):
    return pltpu.CompilerParams(dimension_semantics=sem, vmem_limit_bytes=VMEM_LIMIT)


def _adaln_kernel(c_ref, w_ref, b_ref, o_ref):
    c = c_ref[...]
    ca = c * _sigmoid(c)
    o_ref[0] = jnp.dot(ca, w_ref[0], preferred_element_type=F32,
                       precision=lax.Precision.HIGHEST) + b_ref[0]


def _adaln(c, ada_w, ada_b):
    L, D, N = ada_w.shape
    B = c.shape[0]
    tn = N // 2
    return pl.pallas_call(
        _adaln_kernel,
        grid=(L, N // tn),
        in_specs=[pl.BlockSpec((B, D), lambda l, j: (0, 0)),
                  pl.BlockSpec((1, D, tn), lambda l, j: (l, 0, j)),
                  pl.BlockSpec((1, 1, tn), lambda l, j: (l, 0, j))],
        out_specs=pl.BlockSpec((1, B, tn), lambda l, j: (l, 0, j)),
        out_shape=jax.ShapeDtypeStruct((L, B, N), F32),
        compiler_params=_cparams(("arbitrary", "arbitrary")),
        name="adaln",
    )(c, ada_w, ada_b.reshape(L, 1, N))


def _modnorm(x, g, sc, sh):
    ms = jnp.mean(x * x, axis=-1, keepdims=True)
    y = x * lax.rsqrt(ms + EPS) * g
    return y * (1.0 + sc) + sh


def _premix_kernel(x_ref, sh_ref, sc_ref, g_ref, wqk_ref, wvt_ref, wf_ref, wg_ref, bf_ref, hn_ref, bd_ref,
                   tri_ref, qk_ref, vt_ref, dec_ref, cbase_ref, ga_ref, gb_ref, carry_ref):
    si = pl.program_id(1)
    tm = x_ref.shape[1]
    h = _modnorm(x_ref[0], g_ref[...], sc_ref[0], sh_ref[0]).astype(BF16)

    for c in range(4):
        z = _dot(h, wqk_ref[:, c * WIDTH:(c + 1) * WIDTH])
        sq = (z * z).astype(BF16)
        ms = jnp.concatenate(
            [_dot(sq[:, j * MXU_DIM:(j + 1) * MXU_DIM], bd_ref[...]) for j in range(WIDTH // MXU_DIM)],
            axis=1)
        z = z * lax.rsqrt(ms + EPS) * hn_ref[c:c + 1, :]
        qk_ref[c, 0] = z.astype(BF16)

    for c in range(2):
        vt = _dot_nt(wvt_ref[c * WIDTH:(c + 1) * WIDTH, :], h).astype(BF16)
        vt_ref[0, c * (N_HEADS // 2):(c + 1) * (N_HEADS // 2)] = vt.reshape(N_HEADS // 2, LANES, tm)

    fr = _dot_nt(wf_ref[...], h)[:N_HEADS]
    xg = fr + bf_ref[...]
    logf = jnp.minimum(xg, 0.0) - jnp.log(1.0 + jnp.exp(-jnp.abs(xg)))
    hi = logf.astype(BF16).astype(F32)
    r1 = logf - hi
    mid = r1.astype(BF16).astype(F32)
    lo = r1 - mid
    parts = jnp.concatenate([hi, mid, lo, jnp.zeros_like(hi)], axis=0)
    cs3 = _dot(parts.astype(BF16), tri_ref[...])
    cs = cs3[0:8] + cs3[8:16] + cs3[16:24]

    @pl.when(si == 0)
    def _():
        carry_ref[...] = jnp.zeros_like(carry_ref)

    drel = cs * (-LOG2E)
    dhi = drel.astype(BF16).astype(F32)
    dlo = drel - dhi
    dec = jnp.concatenate([dhi, dlo, jnp.zeros((LANES - 2 * N_HEADS, tm), F32)], axis=0)
    dec_ref[0] = dec.T.astype(BF16)
    carry = carry_ref[:, 0:1]
    cbase_ref[0, 0] = jnp.broadcast_to(carry * (-LOG2E), (N_HEADS, tm))
    carry_ref[...] = jnp.broadcast_to(carry + cs[:, tm - 1:tm], carry_ref.shape)

    for c in range(4):
        zg = _dot(h, wg_ref[:, c * WIDTH:(c + 1) * WIDTH])
        sg = _sigmoid(zg).astype(BF16)
        if c < 2:
            ga_ref[0, :, c * WIDTH:(c + 1) * WIDTH] = sg
        else:
            gb_ref[0, :, (c - 2) * WIDTH:(c - 1) * WIDTH] = sg


def _premix(x, sh, sc, g, wqk, wvt, wf, wg, bf, hn, bd, tri):
    B, S, D = x.shape
    tm = TM_PRE
    const = lambda b, s: (0, 0)
    return pl.pallas_call(
        _premix_kernel,
        grid=(B, S // tm),
        in_specs=[pl.BlockSpec((1, tm, D), lambda b, s: (b, s, 0)),
                  pl.BlockSpec((1, 1, D), lambda b, s: (b, 0, 0)),
                  pl.BlockSpec((1, 1, D), lambda b, s: (b, 0, 0)),
                  pl.BlockSpec((1, D), const),
                  pl.BlockSpec(wqk.shape, const),
                  pl.BlockSpec(wvt.shape, const),
                  pl.BlockSpec(wf.shape, const),
                  pl.BlockSpec(wg.shape, const),
                  pl.BlockSpec(bf.shape, const),
                  pl.BlockSpec(hn.shape, const),
                  pl.BlockSpec(bd.shape, const),
                  pl.BlockSpec(tri.shape, const)],
        out_specs=[pl.BlockSpec((4, 1, tm, WIDTH), lambda b, s: (0, b, s, 0)),
                   pl.BlockSpec((1, N_HEADS, LANES, tm), lambda b, s: (b, 0, 0, s)),
                   pl.BlockSpec((1, tm, LANES), lambda b, s: (b, s, 0)),
                   pl.BlockSpec((1, 1, N_HEADS, tm), lambda b, s: (b, s, 0, 0)),
                   pl.BlockSpec((1, tm, D), lambda b, s: (b, s, 0)),
                   pl.BlockSpec((1, tm, D), lambda b, s: (b, s, 0))],
        out_shape=[jax.ShapeDtypeStruct((4, B, S, WIDTH), BF16),
                   jax.ShapeDtypeStruct((B, N_HEADS, LANES, S), BF16),
                   jax.ShapeDtypeStruct((B, S, LANES), BF16),
                   jax.ShapeDtypeStruct((B, S // tm, N_HEADS, tm), F32),
                   jax.ShapeDtypeStruct((B, S, D), BF16),
                   jax.ShapeDtypeStruct((B, S, D), BF16)],
        scratch_shapes=[pltpu.VMEM((N_HEADS, LANES), F32)],
        compiler_params=_cparams(("arbitrary", "arbitrary")),
        name="premix",
    )(x, sh, sc, g, wqk, wvt, wf, wg, bf, hn, bd, tri)


def _values_with_ones(vt, hh):
    ones = jnp.ones((PV_ROWS - HEAD_DIM, vt.shape[1]), vt.dtype)
    return jnp.concatenate([vt[hh * HEAD_DIM:(hh + 1) * HEAD_DIM, :], ones], axis=0)


def _pair_finish_t(acc0, acc1):
    out_t = jnp.concatenate([acc[:HEAD_DIM] / acc[HEAD_DIM:HEAD_DIM + 1, :] for acc in (acc0, acc1)], axis=0)
    return out_t.T


def _band_kernel(q_ref, kprev_ref, kcur_ref, vprev_ref, vcur_ref, bias_ref, o_ref):
    tq = TQ_BAND
    step_start = pl.program_id(2) * (BAND_TILES * tq)
    lane = lax.broadcasted_iota(jnp.int32, (1, LANES), 1)
    krow = lax.broadcasted_iota(jnp.int32, (BAND_WIN, 1), 0)
    pieces = [(ref, j) for ref in (kprev_ref, kcur_ref) for j in range(BAND_TILES)]
    vpieces = [(ref, j) for ref in (vprev_ref, vcur_ref) for j in range(BAND_TILES)]
    units = [(u, pp, hh) for u in range(BAND_TILES) for pp in range(PAIRS_PER_STEP) for hh in range(2)]
    scores = []
    for u, pp, hh in units:
        q = q_ref[u * tq:(u + 1) * tq, pp * LANES:(pp + 1) * LANES]
        qm = jnp.where((lane // HEAD_DIM) == hh, q, jnp.zeros_like(q))
        scores.append(jnp.concatenate(
            [_dot_nt(ref[j * tq:(j + 1) * tq, pp * LANES:(pp + 1) * LANES], qm)
             for ref, j in pieces[u:u + 3]], axis=0))
    accs = []
    for n, (u, pp, hh) in enumerate(units):
        kvalid = (krow + (step_start + u * tq - LOOKBACK_CHUNKS * CHUNK)) >= 0
        s = jnp.where(kvalid, scores[n] + bias_ref[2 * pp + hh], NEG_INF)
        m = jnp.max(s, axis=0, keepdims=True)
        p = jnp.exp2(s - m).astype(BF16)
        acc = None
        for c, (ref, j) in enumerate(vpieces[u:u + 3]):
            t = _dot(_values_with_ones(ref[pp, :, j * tq:(j + 1) * tq], hh), p[c * tq:(c + 1) * tq, :])
            acc = t if acc is None else acc + t
        accs.append(acc)
    for n in range(0, len(units), 2):
        u, pp, _ = units[n]
        o_ref[u * tq:(u + 1) * tq, pp * LANES:(pp + 1) * LANES] = _pair_finish_t(
            accs[n], accs[n + 1]).astype(o_ref.dtype)


def _band_attention(qk, vt, bias_t):
    _, B, S, _ = qk.shape
    rows = BAND_TILES * TQ_BAND
    groups = N_HEADS // 2 // PAIRS_PER_STEP
    gw = PAIRS_PER_STEP * LANES

    def k_spec(back):
        return pl.BlockSpec((None, None, rows, gw),
                            lambda b, g, qi: (1, b, jnp.maximum(qi - back, 0), g))

    def v_spec(back):
        return pl.BlockSpec((None, PAIRS_PER_STEP, LANES, rows),
                            lambda b, g, qi: (b, g, 0, jnp.maximum(qi - back, 0)))

    return pl.pallas_call(
        _band_kernel,
        grid=(B, groups, S // rows),
        in_specs=[pl.BlockSpec((None, None, rows, gw), lambda b, g, qi: (0, b, qi, g)),
                  k_spec(1), k_spec(0), v_spec(1), v_spec(0),
                  pl.BlockSpec((2 * PAIRS_PER_STEP, BAND_WIN, TQ_BAND), lambda b, g, qi: (g, 0, 0))],
        out_specs=pl.BlockSpec((None, rows, gw), lambda b, g, qi: (b, qi, g)),
        out_shape=jax.ShapeDtypeStruct((B, S, WIDTH), BF16),
        compiler_params=_cparams(("arbitrary", "arbitrary", "arbitrary")),
        name="band_attn",
    )(qk, qk, qk, vt, vt, bias_t)


def _band_bias(rel_table):
    pad = LOOKBACK_CHUNKS * CHUNK
    i = jnp.arange(TQ_BAND)[:, None]
    j = jnp.arange(BAND_WIN)[None, :]
    period = 1024
    assert BAND_WIN + TQ_BAND <= period
    m = jnp.arange(period)
    rel = jnp.where(m <= BAND_WIN, pad - m, pad + period - m)
    g = rel_table[:, jnp.clip(rel, -REL_CLIP, REL_CLIP) + REL_CLIP].astype(F32)
    bias = jnp.tile(g, (1, TQ_BAND))[:, :TQ_BAND * (period - 1)].reshape(-1, TQ_BAND, period - 1)[:, :, :BAND_WIN]
    c0 = (i // CHUNK) * CHUNK
    inband = (j >= c0) & (j < c0 + pad + CHUNK)
    return jnp.where(inband[None], bias * LOG2E, NEG_INF).transpose(0, 2, 1)


FOX_DIAG, FOX_TWO_PAST, FOX_PAST_DIAG = 0, 1, 2


def _fox_kernel(qi_tab, ka_tab, kb_tab, kind_tab, first_tab, q_ref, ka_ref, kb_ref, vta_ref, vtb_ref,
                deca_ref, decb_ref, cba_ref, cbb_ref, o_ref, acc_ref, m_ref):
    grp = pl.program_id(1)
    t = pl.program_id(2)
    kind = kind_tab[t]
    tq, tk = q_ref.shape[0], ka_ref.shape[0]
    lane2 = lax.broadcasted_iota(jnp.int32, (1, 2 * LANES), 1)
    heads = [(pp, hh) for pp in range(PAIRS_PER_STEP) for hh in range(2)]
    tile_a = (ka_ref, vta_ref, deca_ref, cba_ref)
    tile_b = (kb_ref, vtb_ref, decb_ref, cbb_ref)

    @pl.when(first_tab[t] == 1)
    def _():
        acc_ref[...] = jnp.zeros_like(acc_ref)
        m_ref[...] = jnp.full_like(m_ref, NEG_INF)

    def step(tiles):
        ones = jnp.where(lax.broadcasted_iota(jnp.int32, (tq, LANES), 1) < 2 * N_HEADS, 1.0, 0.0).astype(BF16)
        causal = (lax.broadcasted_iota(jnp.int32, (tk, tq), 0)
                  <= lax.broadcasted_iota(jnp.int32, (tk, tq), 1))
        scores = []
        for pp, hh in heads:
            head = 2 * (PAIRS_PER_STEP * grp + pp) + hh
            q_aug = jnp.concatenate([q_ref[:, pp * LANES:(pp + 1) * LANES], ones], axis=1)
            use = (((lane2 // HEAD_DIM) == hh) | (lane2 == LANES + head) | (lane2 == LANES + N_HEADS + head))
            qm = jnp.where(use, q_aug, jnp.zeros_like(q_aug))
            per_tile = []
            for (k_ref, _, dec_ref, _), diagonal in tiles:
                k_aug = jnp.concatenate([k_ref[:, pp * LANES:(pp + 1) * LANES], dec_ref[...]], axis=1)
                s = _dot_nt(k_aug, qm)
                per_tile.append(jnp.where(causal, s, NEG_INF) if diagonal else s)
            scores.append(per_tile)
        for n, (pp, hh) in enumerate(heads):
            head = 2 * (PAIRS_PER_STEP * grp + pp) + hh
            bases = [cb_ref[pl.ds(head, 1), :] for (_, _, _, cb_ref), _ in tiles]
            m_old = m_ref[n]
            m_new = m_old
            for s, base in zip(scores[n], bases):
                m_new = jnp.maximum(m_new, jnp.max(s, axis=0, keepdims=True) + base)
            acc = acc_ref[n] * jnp.exp2(m_old - m_new)
            for s, base, ((_, vt_ref, _, _), _) in zip(scores[n], bases, tiles):
                p = jnp.exp2(s - (m_new - base)).astype(BF16)
                acc = acc + _dot(_values_with_ones(vt_ref[pp], hh), p)
            acc_ref[n] = acc
            m_ref[n] = m_new

    def finish():
        for pp in range(PAIRS_PER_STEP):
            o_ref[:, pp * LANES:(pp + 1) * LANES] = _pair_finish_t(
                acc_ref[2 * pp], acc_ref[2 * pp + 1]).astype(o_ref.dtype)

    @pl.when(kind == FOX_TWO_PAST)
    def _():
        step([(tile_a, False), (tile_b, False)])

    @pl.when(kind == FOX_PAST_DIAG)
    def _():
        step([(tile_a, False), (tile_b, True)])
        finish()

    @pl.when(kind == FOX_DIAG)
    def _():
        step([(tile_a, True)])
        finish()


def _fox_attention(qkv, vt, dec, cbase):
    _, B, S, _ = qkv.shape
    tq = TQ_FOX
    nq = S // tq
    groups = N_HEADS // 2 // PAIRS_PER_STEP
    gw = PAIRS_PER_STEP * LANES
    steps = []
    for i in range(nq):
        past = list(range(i))
        first = 1
        while len(past) >= 2:
            steps.append((i, past[0], past[1], FOX_TWO_PAST, first))
            past, first = past[2:], 0
        if past:
            steps.append((i, past[0], i, FOX_PAST_DIAG, first))
        else:
            steps.append((i, i, i, FOX_DIAG, first))
    tabs = [jnp.array([s[c] for s in steps], jnp.int32) for c in range(5)]

    def key_specs(col):
        tile = lambda *a: a[3 + col][a[2]]
        return [pl.BlockSpec((None, None, tq, gw), lambda b, g, t, *tb: (3, b, tile(b, g, t, *tb), g)),
                pl.BlockSpec((None, PAIRS_PER_STEP, LANES, tq),
                             lambda b, g, t, *tb: (b, groups + g, 0, tile(b, g, t, *tb))),
                pl.BlockSpec((None, tq, LANES), lambda b, g, t, *tb: (b, tile(b, g, t, *tb), 0)),
                pl.BlockSpec((None, None, N_HEADS, tq), lambda b, g, t, *tb: (b, tile(b, g, t, *tb), 0, 0))]

    ka, va, da, ca = key_specs(1)
    kb, vb, db, cb = key_specs(2)
    grid_spec = pltpu.PrefetchScalarGridSpec(
        num_scalar_prefetch=5,
        grid=(B, groups, len(steps)),
        in_specs=[pl.BlockSpec((None, None, tq, gw), lambda b, g, t, *tb: (2, b, tb[0][t], g)),
                  ka, kb, va, vb, da, db, ca, cb],
        out_specs=pl.BlockSpec((None, tq, gw), lambda b, g, t, *tb: (b, tb[0][t], g)),
        scratch_shapes=[pltpu.VMEM((2 * PAIRS_PER_STEP, PV_ROWS, tq), F32),
                        pltpu.VMEM((2 * PAIRS_PER_STEP, 1, tq), F32)],
    )
    return pl.pallas_call(
        _fox_kernel,
        grid_spec=grid_spec,
        out_shape=jax.ShapeDtypeStruct((B, S, WIDTH), BF16),
        compiler_params=_cparams(("arbitrary", "arbitrary", "arbitrary")),
        name="fox_attn",
    )(*tabs, qkv, qkv, qkv, vt, vt, dec, dec, cbase, cbase)


def _postmix_kernel(ya_ref, yb_ref, ga_ref, gb_ref, x_ref, g1_ref, wpa_ref, wpb_ref, wo_ref,
                    g_ref, sh_ref, sc_ref, wr_ref, rb_ref, ustrict_ref, lstrict_ref,
                    x1_ref, h2_ref, posk_ref, wk_ref, cnt_ref):
    ua = _dot(ya_ref[0], wpa_ref[...])
    ub = _dot(yb_ref[0], wpb_ref[...])
    m = ga_ref[0].astype(F32) * ua + gb_ref[0].astype(F32) * ub
    mo = _dot(m.astype(BF16), wo_ref[...])
    x1 = x_ref[0] + g1_ref[0] * mo
    x1_ref[0] = x1
    h2 = _modnorm(x1, g_ref[...], sc_ref[0], sh_ref[0]).astype(BF16)
    h2_ref[0] = h2
    for j in range(h2.shape[0] // TM_MOE):
        posk, wk, cnt = _route(h2[j * TM_MOE:(j + 1) * TM_MOE], wr_ref[...], rb_ref[...],
                               ustrict_ref[...], lstrict_ref[...])
        posk_ref[j] = posk
        wk_ref[j] = wk
        cnt_ref[j] = cnt


def _postmix(ya, yb, ga, gb, x, g1, wpa, wpb, wo, g, sh, sc, wr_t, rb, ustrict, lstrict):
    B, S, D = x.shape
    tm = TM_PRE
    sub_tiles = tm // TM_MOE
    nt = B * S // TM_MOE
    const = lambda b, s: (0, 0)
    tok = lambda w: pl.BlockSpec((1, tm, w), lambda b, s: (b, s, 0))
    row = pl.BlockSpec((1, 1, D), lambda b, s: (b, 0, 0))
    tiles = lambda r, c: pl.BlockSpec((sub_tiles, r, c), lambda b, s: (b * (S // tm) + s, 0, 0))
    return pl.pallas_call(
        _postmix_kernel,
        grid=(B, S // tm),
        in_specs=[tok(WIDTH), tok(WIDTH), tok(D), tok(D), tok(D), row,
                  pl.BlockSpec(wpa.shape, const), pl.BlockSpec(wpb.shape, const),
                  pl.BlockSpec(wo.shape, const), pl.BlockSpec((1, D), const), row, row,
                  pl.BlockSpec(wr_t.shape, const), pl.BlockSpec(rb.shape, const),
                  pl.BlockSpec(ustrict.shape, const), pl.BlockSpec(lstrict.shape, const)],
        out_specs=[tok(D), tok(D), tiles(TOP_K, TM_MOE), tiles(TOP_K, TM_MOE), tiles(N_EXPERTS, LANES)],
        out_shape=[jax.ShapeDtypeStruct((B, S, D), F32), jax.ShapeDtypeStruct((B, S, D), BF16),
                   jax.ShapeDtypeStruct((nt, TOP_K, TM_MOE), F32),
                   jax.ShapeDtypeStruct((nt, TOP_K, TM_MOE), F32),
                   jax.ShapeDtypeStruct((nt, N_EXPERTS, LANES), F32)],
        compiler_params=_cparams(("arbitrary", "arbitrary")),
        name="postmix",
    )(ya, yb, ga, gb, x, g1, wpa, wpb, wo, g, sh, sc, wr_t, rb, ustrict, lstrict)


def _route(h, wr, rb, ustrict, lstrict):
    tm = h.shape[0]
    per_group = N_EXPERTS // N_GROUPS
    logits = _dot_nt(wr, h)
    scores = _sigmoid(logits)
    biased = scores + rb
    sub8 = lax.broadcasted_iota(jnp.int32, (per_group, tm), 0).astype(F32)
    neg = -jnp.inf

    grp_rows = []
    for g in range(N_GROUPS):
        a = biased[g * per_group:(g + 1) * per_group]
        m1 = jnp.max(a, axis=0, keepdims=True)
        i1 = jnp.min(jnp.where(a == m1, sub8, float(per_group)), axis=0, keepdims=True)
        m2 = jnp.max(jnp.where(sub8 == i1, neg, a), axis=0, keepdims=True)
        grp_rows.append(m1 + m2)
    grp = jnp.concatenate(grp_rows, axis=0)

    gsub = lax.broadcasted_iota(jnp.int32, (N_GROUPS, tm), 0).astype(F32)
    gmask = jnp.zeros((N_GROUPS, tm), F32)
    for _ in range(TOPK_GROUPS):
        mx = jnp.max(grp, axis=0, keepdims=True)
        gi = jnp.min(jnp.where(grp == mx, gsub, float(N_GROUPS)), axis=0, keepdims=True)
        sel = gsub == gi
        gmask = jnp.where(sel, 1.0, gmask)
        grp = jnp.where(sel, neg, grp)
    masked = jnp.concatenate(
        [jnp.where(gmask[g:g + 1] > 0.5, biased[g * per_group:(g + 1) * per_group], neg)
         for g in range(N_GROUPS)], axis=0)

    esub = lax.broadcasted_iota(jnp.int32, (N_EXPERTS, tm), 0).astype(F32)
    sels, ws = [], []
    for _ in range(TOP_K):
        mx = jnp.max(masked, axis=0, keepdims=True)
        ei = jnp.min(jnp.where(masked == mx, esub, float(N_EXPERTS)), axis=0, keepdims=True)
        sel = esub == ei
        sels.append(sel)
        ws.append(jnp.sum(jnp.where(sel, scores, 0.0), axis=0, keepdims=True))
        masked = jnp.where(sel, neg, masked)
    wsum = ws[0]
    for w in ws[1:]:
        wsum = wsum + w

    selmask = jnp.zeros((N_EXPERTS, tm), F32)
    for sel in sels:
        selmask = selmask + jnp.where(sel, 1.0, 0.0)
    rank = _dot(selmask.astype(BF16), ustrict)
    counts = jnp.sum(selmask, axis=1, keepdims=True)
    padded = jnp.ceil(counts * (1.0 / RUN_ALIGN)) * RUN_ALIGN
    padded_b = jnp.broadcast_to(padded, (N_EXPERTS, LANES))
    base = _dot(lstrict, padded_b.astype(BF16))
    pos = base[:, 0:1] + rank

    posk = jnp.concatenate([jnp.sum(jnp.where(sel, pos, 0.0), axis=0, keepdims=True) for sel in sels], axis=0)
    wk = jnp.concatenate([(w / wsum) * ROUTED_SCALE for w in ws], axis=0)
    return posk, wk, padded_b


def _pack_halves(v):
    half = v.shape[1] // 2
    lo = lax.bitcast_convert_type(v[:, :half], U32)
    hi = lax.bitcast_convert_type(v[:, half:], U32)
    return (lo >> 16) | (hi & U32(0xFFFF0000))


def _unpack_halves(u):
    lo = lax.bitcast_convert_type(u << 16, F32).astype(BF16)
    hi = lax.bitcast_convert_type(u & U32(0xFFFF0000), F32).astype(BF16)
    return lo, hi


def _rows_copy(m, src_ref, src0, dst_ref, dst0, sem):
    rows = m * RUN_ALIGN
    return pltpu.make_async_copy(src_ref.at[pl.ds(pl.multiple_of(src0, RUN_ALIGN), rows)],
                                 dst_ref.at[pl.ds(pl.multiple_of(dst0, RUN_ALIGN), rows)], sem)


def _start_rows(m, src_ref, src0, dst_ref, dst0, sem):
    @pl.when(m > 0)
    def _():
        _rows_copy(m, src_ref, src0, dst_ref, dst0, sem).start()


def _wait_rows(m, src_ref, dst_ref, sem):
    @pl.when(m > 0)
    def _():
        _rows_copy(m, src_ref, 0, dst_ref, 0, sem).wait()


def _dispatch_kernel(off_ref, m_ref, base_ref, tot_ref, gapoff_ref, gapm_ref, h_ref, posk_ref, xs_hbm,
                     xs_ref, zero_ref, sem):
    i = pl.program_id(0)
    tm = h_ref.shape[0]
    n_chunks = K_SORT // MXU_DIM

    @pl.when(i == 0)
    def _():
        zero_ref[...] = jnp.zeros_like(zero_ref)

        def gap(e, carry):
            _start_rows(gapm_ref[e], zero_ref, 0, xs_hbm, gapoff_ref[e], sem)
            return carry
        lax.fori_loop(0, N_EXPERTS, gap, 0)

        def gap_wait(e, carry):
            _wait_rows(gapm_ref[e], zero_ref, xs_hbm, sem)
            return carry
        lax.fori_loop(0, N_EXPERTS, gap_wait, 0)

    slot = i % 2
    xs_slot = xs_ref.at[slot]
    h = h_ref[...]
    posk = posk_ref[0]
    rows = lax.broadcasted_iota(jnp.int32, (MXU_DIM, tm), 0).astype(F32).astype(BF16)
    one = jnp.ones((MXU_DIM, tm), BF16)
    def sort_chunks(lo, hi):
        for c in range(lo, hi):
            rel = posk - float(c * MXU_DIM)
            p = jnp.zeros((MXU_DIM, tm), BF16)
            for k in range(TOP_K):
                p = jnp.where(rows == rel[k:k + 1, :].astype(BF16), one, p)
            xs_slot[c * MXU_DIM:(c + 1) * MXU_DIM, :] = _pack_halves(_dot(p, h))

    sort_chunks(0, K_SORT_COMMON // MXU_DIM)

    @pl.when(tot_ref[i] * RUN_ALIGN > K_SORT_COMMON)
    def _():
        sort_chunks(K_SORT_COMMON // MXU_DIM, n_chunks)

    @pl.when(i > 0)
    def _():
        _wait_rows(tot_ref[i - 1], xs_ref.at[1 - slot], xs_hbm, sem)

    def run(e, carry):
        idx = i * N_EXPERTS + e
        _start_rows(m_ref[idx], xs_slot, base_ref[idx], xs_hbm, off_ref[idx], sem)
        return carry
    lax.fori_loop(0, N_EXPERTS, run, 0)

    @pl.when(i == pl.num_programs(0) - 1)
    def _():
        _wait_rows(tot_ref[i], xs_slot, xs_hbm, sem)


def _dispatch(off, m16, base, tot, gapoff, gapm, h2, posk, n_rows):
    T, D = h2.shape
    tm = TM_MOE
    grid_spec = pltpu.PrefetchScalarGridSpec(
        num_scalar_prefetch=6,
        grid=(T // tm,),
        in_specs=[pl.BlockSpec((tm, D), lambda i, *_: (i, 0)),
                  pl.BlockSpec((1, TOP_K, tm), lambda i, *_: (i, 0, 0))],
        out_specs=pl.BlockSpec(memory_space=pl.ANY),
        scratch_shapes=[pltpu.VMEM((2, K_SORT, PACK_W), U32),
                        pltpu.VMEM((BLOCK_ROWS, PACK_W), U32),
                        pltpu.SemaphoreType.DMA],
    )
    return pl.pallas_call(
        _dispatch_kernel,
        grid_spec=grid_spec,
        out_shape=jax.ShapeDtypeStruct((n_rows, PACK_W), U32),
        compiler_params=_cparams(("arbitrary",)),
        name="dispatch",
    )(off, m16, base, tot, gapoff, gapm, h2, posk)


X_RING = 4
Y_RING = 3


def _expert_kernel(bexp_ref, nvalid_ref, xs_hbm, wg_ref, wu_ref, wd_ref, ys_hbm, x_buf, y_buf, wgu_bf, wd_bf,
                   x_sem, y_sem):
    i = pl.program_id(0)
    nvalid = nvalid_ref[0]

    def rows(b):
        return pl.ds(pl.multiple_of(b * BLOCK_ROWS, BLOCK_ROWS), BLOCK_ROWS)

    def load(b):
        slot = b % X_RING
        return pltpu.make_async_copy(xs_hbm.at[rows(b)], x_buf.at[slot], x_sem.at[slot])

    def store(b):
        slot = b % Y_RING
        return pltpu.make_async_copy(y_buf.at[slot], ys_hbm.at[rows(b)], y_sem.at[slot])

    @pl.when(i == 0)
    def _():
        for b in range(X_RING - 1):
            @pl.when(b < nvalid)
            def _(b=b):
                load(b).start()

    @pl.when(i < nvalid)
    def _():
        @pl.when(i + X_RING - 1 < nvalid)
        def _():
            load(i + X_RING - 1).start()

        @pl.when((i == 0) | (bexp_ref[i] != bexp_ref[jnp.maximum(i - 1, 0)]))
        def _():
            wgu_bf[:, :EXPERT_FF] = wg_ref[...].astype(BF16)
            wgu_bf[:, EXPERT_FF:] = wu_ref[...].astype(BF16)
            wd_bf[...] = wd_ref[...].astype(BF16)

        load(i).wait()
        x_lo, x_hi = _unpack_halves(x_buf[i % X_RING])
        gu = _dot(x_lo, wgu_bf[:PACK_W, :]) + _dot(x_hi, wgu_bf[PACK_W:, :])
        g = gu[:, :EXPERT_FF]
        a = (g * _sigmoid(g) * gu[:, EXPERT_FF:]).astype(BF16)
        y = _dot(a, wd_bf[...]).astype(BF16).astype(F32)

        @pl.when(i >= Y_RING)
        def _():
            store(i - Y_RING).wait()
        y_buf[i % Y_RING] = _pack_halves(y)
        store(i).start()

        @pl.when(i == nvalid - 1)
        def _():
            for back in range(Y_RING):
                @pl.when(i - back >= 0)
                def _(back=back):
                    store(i - back).wait()


def _experts(bexp, nvalid, xs, wg, wu, wd, layer):
    n_rows, D = xs.shape[0], D_MODEL
    nb = n_rows // BLOCK_ROWS

    def blk(i, be, nv):
        return jnp.minimum(i, nv[0] - 1)

    wspec = lambda r, c: pl.BlockSpec((None, None, r, c),
                                      lambda i, be, nv: (layer, be[blk(i, be, nv)], 0, 0))
    grid_spec = pltpu.PrefetchScalarGridSpec(
        num_scalar_prefetch=2,
        grid=(nb,),
        in_specs=[pl.BlockSpec(memory_space=pl.ANY),
                  wspec(D, EXPERT_FF), wspec(D, EXPERT_FF), wspec(EXPERT_FF, D)],
        out_specs=pl.BlockSpec(memory_space=pl.ANY),
        scratch_shapes=[pltpu.VMEM((X_RING, BLOCK_ROWS, PACK_W), U32), pltpu.VMEM((Y_RING, BLOCK_ROWS, PACK_W), U32),
                        pltpu.VMEM((D, 2 * EXPERT_FF), BF16), pltpu.VMEM((EXPERT_FF, D), BF16),
                        pltpu.SemaphoreType.DMA((X_RING,)), pltpu.SemaphoreType.DMA((Y_RING,))],
    )
    return pl.pallas_call(
        _expert_kernel,
        grid_spec=grid_spec,
        out_shape=jax.ShapeDtypeStruct((n_rows, PACK_W), U32),
        compiler_params=_cparams(("arbitrary",)),
        name="experts",
    )(bexp, nvalid, xs, wg, wu, wd)


def _combine_kernel(off_ref, m_ref, base_ref, tot_ref, ys_hbm, posk_ref, wk_ref, h_ref, x_ref, g2_ref,
                    wsgu_ref, wsd_ref, o_ref, ys_ref, pw_ref, sem):
    i = pl.program_id(0)
    tm = h_ref.shape[0]
    slot = i % 2

    def fetch(tile, buf_slot):
        def run(e, carry):
            idx = tile * N_EXPERTS + e
            _start_rows(m_ref[idx], ys_hbm, off_ref[idx], ys_ref.at[buf_slot], base_ref[idx],
                        sem.at[buf_slot])
            return carry
        lax.fori_loop(0, N_EXPERTS, run, 0)

    @pl.when(i == 0)
    def _():
        ys_ref[...] = jnp.zeros_like(ys_ref)
        fetch(i, slot)

    @pl.when(i + 1 < pl.num_programs(0))
    def _():
        fetch(i + 1, 1 - slot)

    h = h_ref[...]
    gu = _dot(h, wsgu_ref[...])
    g = gu[:, :EXPERT_FF]
    shared = _dot((g * _sigmoid(g) * gu[:, EXPERT_FF:]).astype(BF16), wsd_ref[...])

    posk = posk_ref[0]
    wk = wk_ref[0]
    rows = lax.broadcasted_iota(jnp.int32, (MXU_DIM, tm), 0).astype(F32).astype(BF16)

    def weight_chunks(lo, hi):
        for c in range(lo, hi):
            rel = posk - float(c * MXU_DIM)
            pw = jnp.zeros((MXU_DIM, tm), BF16)
            for k in range(TOP_K):
                wrow = jnp.broadcast_to(wk[k:k + 1, :], (MXU_DIM, tm)).astype(BF16)
                pw = jnp.where(rows == rel[k:k + 1, :].astype(BF16), wrow, pw)
            pw_ref[c * MXU_DIM:(c + 1) * MXU_DIM, :] = pw

    def unsort(lo, hi):
        pw = pw_ref[lo:hi, :]
        halves = [lax.dot_general(pw, y, (((0,), (0,)), ((), ())), preferred_element_type=F32)
                  for y in _unpack_halves(ys_ref[slot, lo:hi, :])]
        return jnp.concatenate(halves, axis=1)

    weight_chunks(0, K_SORT_COMMON // MXU_DIM)
    _wait_rows(tot_ref[i], ys_hbm, ys_ref.at[slot], sem.at[slot])
    o_ref[...] = x_ref[...] + g2_ref[0] * (unsort(0, K_SORT_COMMON) + shared)

    @pl.when(tot_ref[i] * RUN_ALIGN > K_SORT_COMMON)
    def _():
        weight_chunks(K_SORT_COMMON // MXU_DIM, K_SORT // MXU_DIM)
        o_ref[...] = o_ref[...] + g2_ref[0] * unsort(K_SORT_COMMON, K_SORT)


def _combine(off, m16, base, tot, ys, posk, wk, h2, x1, g2, wsgu, wsd, tiles_per_batch):
    T, D = h2.shape
    tm = TM_MOE
    const = lambda i, *_: (0, 0)
    tok = lambda w: pl.BlockSpec((tm, w), lambda i, *_: (i, 0))
    sel = pl.BlockSpec((1, TOP_K, tm), lambda i, *_: (i, 0, 0))
    grid_spec = pltpu.PrefetchScalarGridSpec(
        num_scalar_prefetch=4,
        grid=(T // tm,),
        in_specs=[pl.BlockSpec(memory_space=pl.ANY), sel, sel, tok(D), tok(D),
                  pl.BlockSpec((1, 1, D), lambda i, *_: (i // tiles_per_batch, 0, 0)),
                  pl.BlockSpec(wsgu.shape, const), pl.BlockSpec(wsd.shape, const)],
        out_specs=tok(D),
        scratch_shapes=[pltpu.VMEM((2, K_SORT, PACK_W), U32),
                        pltpu.VMEM((K_SORT, tm), BF16),
                        pltpu.SemaphoreType.DMA((2,))],
    )
    return pl.pallas_call(
        _combine_kernel,
        grid_spec=grid_spec,
        out_shape=jax.ShapeDtypeStruct((T, D), F32),
        compiler_params=_cparams(("arbitrary",)),
        name="combine",
    )(off, m16, base, tot, ys, posk, wk, h2, x1, g2, wsgu, wsd)


def _moe(h2, x1, g2, posk, wk, cnt, wg, wu, wd, layer, wsgu, wsd, tiles_per_batch):
    T, D = h2.shape
    nt = T // TM_MOE

    pad = cnt[:, :, 0].astype(jnp.int32)
    base = jnp.cumsum(pad, axis=1) - pad
    tile_off = jnp.cumsum(pad, axis=0) - pad
    total = jnp.sum(pad, axis=0)
    region = ((total + BLOCK_ROWS - 1) // BLOCK_ROWS) * BLOCK_ROWS
    rend = jnp.cumsum(region)
    rstart = rend - region
    off = rstart[None, :] + tile_off
    nb_max = -(-(TOP_K * T + nt * N_EXPERTS * (RUN_ALIGN - 1)) // BLOCK_ROWS) + N_EXPERTS
    blk_row = jnp.arange(nb_max, dtype=jnp.int32) * BLOCK_ROWS
    bexp = jnp.minimum(jnp.sum((rend[None, :] <= blk_row[:, None]).astype(jnp.int32), axis=1), N_EXPERTS - 1)
    nvalid = (rend[-1:] // BLOCK_ROWS).astype(jnp.int32)
    flat = lambda a: a.reshape(-1).astype(jnp.int32)

    m16 = flat(pad // RUN_ALIGN)
    tot = flat(jnp.sum(pad, axis=1) // RUN_ALIGN)
    xs = _dispatch(flat(off), m16, flat(base), tot, flat(rstart + total),
                   flat((region - total) // RUN_ALIGN), h2, posk, nb_max * BLOCK_ROWS)
    ys = _experts(bexp, nvalid, xs, wg, wu, wd, layer)
    return _combine(flat(off), m16, flat(base), tot, ys, posk, wk, h2, x1, g2, wsgu, wsd, tiles_per_batch)


def kernel(x, c, ada_w, ada_b, mix_norm_g, w_in, b_fgate, qn_a, kn_a, qn_b, kn_b, rel_bias, w_proj_a,
           w_proj_b, w_out, ffn_norm_g, w_router, router_bias, w_gate_e, w_up_e, w_down_e, w_gate_s,
           w_up_s, w_down_s):
    B, S, D = x.shape
    L = ada_w.shape[0]
    T = B * S
    assert D == D_MODEL and S % TM_PRE == 0 and TQ_FOX == TM_PRE and S % TM_MOE == 0 and S % (BAND_TILES * TQ_BAND) == 0

    mod = _adaln(c, ada_w, ada_b).reshape(L, B, 6, 1, D)

    hid = jnp.arange(MXU_DIM) // HEAD_DIM
    bd = jnp.where(hid[:, None] == hid[None, :], 1.0 / HEAD_DIM, 0.0).astype(BF16)
    r = jnp.arange(TM_PRE)
    tri = (r[:, None] <= r[None, :]).astype(BF16)
    r = jnp.arange(TM_MOE)
    ustrict = (r[:, None] < r[None, :]).astype(BF16)
    r = jnp.arange(N_EXPERTS)
    lstrict = (r[None, :] < r[:, None]).astype(BF16)

    q_scale = 1.0 / math.sqrt(HEAD_DIM)
    for l in range(L):
        sh1, sc1, g1, sh2, sc2, g2 = (mod[l, :, j] for j in range(6))
        w = w_in[l]
        cols = lambda c: w[:, c * WIDTH:(c + 1) * WIDTH]
        wqk = jnp.concatenate([cols(0), cols(1), cols(3), cols(4)], axis=1).astype(BF16)
        wvt = jnp.concatenate([cols(2), cols(5)], axis=1).T.astype(BF16)
        wf = jnp.zeros((2 * N_HEADS, D), BF16).at[:N_HEADS].set(w[:, 6 * WIDTH:6 * WIDTH + N_HEADS].T.astype(BF16))
        wg = w[:, 6 * WIDTH + N_HEADS:].astype(BF16)
        hn = jnp.zeros((8, WIDTH), F32)
        hn = hn.at[0].set(jnp.tile(qn_a[l], N_HEADS) * (q_scale * LOG2E)).at[1].set(jnp.tile(kn_a[l], N_HEADS))
        hn = hn.at[2].set(jnp.tile(qn_b[l], N_HEADS) * (q_scale * LOG2E)).at[3].set(jnp.tile(kn_b[l], N_HEADS))

        qk, vt, dec, cbase, ga, gb = _premix(x, sh1, sc1, mix_norm_g[l][None], wqk, wvt, wf, wg,
                                             b_fgate[l][:, None], hn, bd, tri)
        ya = _band_attention(qk, vt, _band_bias(rel_bias[l]))
        yb = _fox_attention(qk, vt, dec, cbase)
        x1, h2, posk, wk, cnt = _postmix(
            ya, yb, ga, gb, x, g1, w_proj_a[l].astype(BF16), w_proj_b[l].astype(BF16),
            w_out[l].astype(BF16), ffn_norm_g[l][None], sh2, sc2, w_router[l].T.astype(BF16),
            router_bias[l][:, None], ustrict, lstrict)

        wsgu = jnp.concatenate([w_gate_s[l], w_up_s[l]], axis=-1).astype(BF16)
        x = _moe(h2.reshape(T, D), x1.reshape(T, D), g2, posk, wk, cnt, w_gate_e, w_up_e, w_down_e, l,
                 wsgu, w_down_s[l].astype(BF16), S // TM_MOE).reshape(B, S, D)
    return x
```

```python
import math

import jax
import jax.numpy as jnp
import numpy as np
from jax import lax
from jax.experimental import pallas as pl
from jax.experimental.pallas import tpu as pltpu

F32 = jnp.float32
BF16 = jnp.bfloat16

D_MODEL = 1024
HEAD_DIM = 64
N_HEADS = 8
WIDTH = N_HEADS * HEAD_DIM
CHUNK = 64
LOOKBACK_CHUNKS = 8
REL_CLIP = 128
N_EXPERTS = 64
TOP_K = 8
N_GROUPS = 8
TOPK_GROUPS = 4
EXPERT_FF = 256
ROUTED_SCALE = 2.5
EPS = 1e-6
NEG_INF = -1e30
LOG2E = math.log2(math.e)

LANES = 128
MXU_DIM = 256
VMEM_LIMIT = 56 * 1024 * 1024

TM_PRE = 512
TQ_BAND = 256
BAND_WIN = TQ_BAND + LOOKBACK_CHUNKS * CHUNK
BAND_TILES = LOOKBACK_CHUNKS * CHUNK // TQ_BAND
TQ_FOX = 512
PAIRS_PER_STEP = 4
PV_ROWS = HEAD_DIM + 16
TM_MOE = 256
RUN_ALIGN = 8
PACK_W = D_MODEL // 2
U32 = jnp.uint32
K_SORT = ((TOP_K * TM_MOE + N_EXPERTS * (RUN_ALIGN - 1) + MXU_DIM - 1) // MXU_DIM) * MXU_DIM
K_SORT_COMMON = 2304
BLOCK_ROWS = 1024


def _dot(a, b):
    return jnp.dot(a, b, preferred_element_type=F32)


def _dot_nt(a, b):
    return lax.dot_general(a, b, (((1,), (1,)), ((), ())), preferred_element_type=F32)


def _sigmoid(v):
    return 1.0 / (1.0 + jnp.exp(-v))


def _cparams(sem):
    return pltpu.CompilerParams(dimension_semantics=sem, vmem_limit_bytes=VMEM_LIMIT)


def _adaln_kernel(c_ref, w_ref, b_ref, o_ref):
    c = c_ref[...]
    ca = c * _sigmoid(c)
    o_ref[0] = jnp.dot(ca, w_ref[0], preferred_element_type=F32,
                       precision=lax.Precision.HIGHEST) + b_ref[0]


def _adaln(c, ada_w, ada_b):
    L, D, N = ada_w.shape
    B = c.shape[0]
    tn = N // 4
    return pl.pallas_call(
        _adaln_kernel,
        grid=(L, N // tn),
        in_specs=[pl.BlockSpec((B, D), lambda l, j: (0, 0)),
                  pl.BlockSpec((1, D, tn), lambda l, j: (l, 0, j)),
                  pl.BlockSpec((1, 1, tn), lambda l, j: (l, 0, j))],
        out_specs=pl.BlockSpec((1, B, tn), lambda l, j: (l, 0, j)),
        out_shape=jax.ShapeDtypeStruct((L, B, N), F32),
        compiler_params=_cparams(("arbitrary", "arbitrary")),
        name="adaln",
    )(c, ada_w, ada_b.reshape(L, 1, N))


def _modnorm(x, g, sc, sh):
    ms = jnp.mean(x * x, axis=-1, keepdims=True)
    y = x * lax.rsqrt(ms + EPS) * g
    return y * (1.0 + sc) + sh


def _premix_kernel(x_ref, sh_ref, sc_ref, g_ref, wqk_ref, wvt_ref, wf_ref, wg_ref, bf_ref, hn_ref, bd_ref,
                   tri_ref, qk_ref, vt_ref, dec_ref, cbase_ref, ga_ref, gb_ref, carry_ref):
    si = pl.program_id(1)
    tm = x_ref.shape[1]
    h = _modnorm(x_ref[0], g_ref[...], sc_ref[0], sh_ref[0]).astype(BF16)

    for c in range(4):
        z = _dot(h, wqk_ref[:, c * WIDTH:(c + 1) * WIDTH])
        sq = (z * z).astype(BF16)
        ms = jnp.concatenate(
            [_dot(sq[:, j * MXU_DIM:(j + 1) * MXU_DIM], bd_ref[...]) for j in range(WIDTH // MXU_DIM)],
            axis=1)
        z = z * lax.rsqrt(ms + EPS) * hn_ref[c:c + 1, :]
        qk_ref[c, 0] = z.astype(BF16)

    for c in range(2):
        vt = _dot_nt(wvt_ref[c * WIDTH:(c + 1) * WIDTH, :], h).astype(BF16)
        vt_ref[0, c * (N_HEADS // 2):(c + 1) * (N_HEADS // 2)] = vt.reshape(N_HEADS // 2, LANES, tm)

    fr = _dot_nt(wf_ref[...], h)[:N_HEADS]
    xg = fr + bf_ref[...]
    logf = jnp.minimum(xg, 0.0) - jnp.log(1.0 + jnp.exp(-jnp.abs(xg)))
    hi = logf.astype(BF16).astype(F32)
    r1 = logf - hi
    mid = r1.astype(BF16).astype(F32)
    lo = r1 - mid
    parts = jnp.concatenate([hi, mid, lo, jnp.zeros_like(hi)], axis=0)
    cs3 = _dot(parts.astype(BF16), tri_ref[...])
    cs = cs3[0:8] + cs3[8:16] + cs3[16:24]

    @pl.when(si == 0)
    def _():
        carry_ref[...] = jnp.zeros_like(carry_ref)

    drel = cs * (-LOG2E)
    dhi = drel.astype(BF16).astype(F32)
    dlo = drel - dhi
    dec = jnp.concatenate([dhi, dlo, jnp.zeros((LANES - 2 * N_HEADS, tm), F32)], axis=0)
    dec_ref[0] = dec.T.astype(BF16)
    carry = carry_ref[:, 0:1]
    cbase_ref[0, 0] = jnp.broadcast_to(carry * (-LOG2E), (N_HEADS, tm))
    carry_ref[...] = jnp.broadcast_to(carry + cs[:, tm - 1:tm], carry_ref.shape)

    for c in range(4):
        zg = _dot(h, wg_ref[:, c * WIDTH:(c + 1) * WIDTH])
        sg = _sigmoid(zg).astype(BF16)
        if c < 2:
            ga_ref[0, :, c * WIDTH:(c + 1) * WIDTH] = sg
        else:
            gb_ref[0, :, (c - 2) * WIDTH:(c - 1) * WIDTH] = sg


def _premix(x, sh, sc, g, wqk, wvt, wf, wg, bf, hn, bd, tri):
    B, S, D = x.shape
    tm = TM_PRE
    const = lambda b, s: (0, 0)
    return pl.pallas_call(
        _premix_kernel,
        grid=(B, S // tm),
        in_specs=[pl.BlockSpec((1, tm, D), lambda b, s: (b, s, 0)),
                  pl.BlockSpec((1, 1, D), lambda b, s: (b, 0, 0)),
                  pl.BlockSpec((1, 1, D), lambda b, s: (b, 0, 0)),
                  pl.BlockSpec((1, D), const),
                  pl.BlockSpec(wqk.shape, const),
                  pl.BlockSpec(wvt.shape, const),
                  pl.BlockSpec(wf.shape, const),
                  pl.BlockSpec(wg.shape, const),
                  pl.BlockSpec(bf.shape, const),
                  pl.BlockSpec(hn.shape, const),
                  pl.BlockSpec(bd.shape, const),
                  pl.BlockSpec(tri.shape, const)],
        out_specs=[pl.BlockSpec((4, 1, tm, WIDTH), lambda b, s: (0, b, s, 0)),
                   pl.BlockSpec((1, N_HEADS, LANES, tm), lambda b, s: (b, 0, 0, s)),
                   pl.BlockSpec((1, tm, LANES), lambda b, s: (b, s, 0)),
                   pl.BlockSpec((1, 1, N_HEADS, tm), lambda b, s: (b, s, 0, 0)),
                   pl.BlockSpec((1, tm, D), lambda b, s: (b, s, 0)),
                   pl.BlockSpec((1, tm, D), lambda b, s: (b, s, 0))],
        out_shape=[jax.ShapeDtypeStruct((4, B, S, WIDTH), BF16),
                   jax.ShapeDtypeStruct((B, N_HEADS, LANES, S), BF16),
                   jax.ShapeDtypeStruct((B, S, LANES), BF16),
                   jax.ShapeDtypeStruct((B, S // tm, N_HEADS, tm), F32),
                   jax.ShapeDtypeStruct((B, S, D), BF16),
                   jax.ShapeDtypeStruct((B, S, D), BF16)],
        scratch_shapes=[pltpu.VMEM((N_HEADS, LANES), F32)],
        compiler_params=_cparams(("arbitrary", "arbitrary")),
        name="premix",
    )(x, sh, sc, g, wqk, wvt, wf, wg, bf, hn, bd, tri)


def _values_with_ones(vt, hh):
    ones = jnp.ones((PV_ROWS - HEAD_DIM, vt.shape[1]), vt.dtype)
    return jnp.concatenate([vt[hh * HEAD_DIM:(hh + 1) * HEAD_DIM, :], ones], axis=0)


def _pair_finish_t(acc0, acc1):
    out_t = jnp.concatenate([acc[:HEAD_DIM] / acc[HEAD_DIM:HEAD_DIM + 1, :] for acc in (acc0, acc1)], axis=0)
    return out_t.T


def _band_kernel(q_ref, kprev_ref, kcur_ref, vprev_ref, vcur_ref, bias_ref, o_ref):
    tq = TQ_BAND
    step_start = pl.program_id(2) * (BAND_TILES * tq)
    lane = lax.broadcasted_iota(jnp.int32, (1, LANES), 1)
    krow = lax.broadcasted_iota(jnp.int32, (BAND_WIN, 1), 0)
    pieces = [(ref, j) for ref in (kprev_ref, kcur_ref) for j in range(BAND_TILES)]
    vpieces = [(ref, j) for ref in (vprev_ref, vcur_ref) for j in range(BAND_TILES)]
    units = [(u, pp, hh) for u in range(BAND_TILES) for pp in range(PAIRS_PER_STEP) for hh in range(2)]
    scores = []
    for u, pp, hh in units:
        q = q_ref[u * tq:(u + 1) * tq, pp * LANES:(pp + 1) * LANES]
        qm = jnp.where((lane // HEAD_DIM) == hh, q, jnp.zeros_like(q))
        scores.append(jnp.concatenate(
            [_dot_nt(ref[j * tq:(j + 1) * tq, pp * LANES:(pp + 1) * LANES], qm)
             for ref, j in pieces[u:u + 3]], axis=0))
    accs = []
    for n, (u, pp, hh) in enumerate(units):
        kvalid = (krow + (step_start + u * tq - LOOKBACK_CHUNKS * CHUNK)) >= 0
        s = jnp.where(kvalid, scores[n] + bias_ref[2 * pp + hh], NEG_INF)
        m = jnp.max(s, axis=0, keepdims=True)
        p = jnp.exp2(s - m).astype(BF16)
        acc = None
        for c, (ref, j) in enumerate(vpieces[u:u + 3]):
            t = _dot(_values_with_ones(ref[pp, :, j * tq:(j + 1) * tq], hh), p[c * tq:(c + 1) * tq, :])
            acc = t if acc is None else acc + t
        accs.append(acc)
    for n in range(0, len(units), 2):
        u, pp, _ = units[n]
        o_ref[u * tq:(u + 1) * tq, pp * LANES:(pp + 1) * LANES] = _pair_finish_t(
            accs[n], accs[n + 1]).astype(o_ref.dtype)


def _band_attention(qk, vt, bias_t):
    _, B, S, _ = qk.shape
    rows = BAND_TILES * TQ_BAND
    groups = N_HEADS // 2 // PAIRS_PER_STEP
    gw = PAIRS_PER_STEP * LANES

    def k_spec(back):
        return pl.BlockSpec((None, None, rows, gw),
                            lambda b, g, qi: (1, b, jnp.maximum(qi - back, 0), g))

    def v_spec(back):
        return pl.BlockSpec((None, PAIRS_PER_STEP, LANES, rows),
                            lambda b, g, qi: (b, g, 0, jnp.maximum(qi - back, 0)))

    return pl.pallas_call(
        _band_kernel,
        grid=(B, groups, S // rows),
        in_specs=[pl.BlockSpec((None, None, rows, gw), lambda b, g, qi: (0, b, qi, g)),
                  k_spec(1), k_spec(0), v_spec(1), v_spec(0),
                  pl.BlockSpec((2 * PAIRS_PER_STEP, BAND_WIN, TQ_BAND), lambda b, g, qi: (g, 0, 0))],
        out_specs=pl.BlockSpec((None, rows, gw), lambda b, g, qi: (b, qi, g)),
        out_shape=jax.ShapeDtypeStruct((B, S, WIDTH), BF16),
        compiler_params=_cparams(("arbitrary", "arbitrary", "arbitrary")),
        name="band_attn",
    )(qk, qk, qk, vt, vt, bias_t)


def _band_bias(rel_table):
    pad = LOOKBACK_CHUNKS * CHUNK
    i = np.arange(TQ_BAND)[:, None]
    j = np.arange(BAND_WIN)[None, :]
    period = 1024
    assert BAND_WIN + TQ_BAND <= period
    m = np.arange(period)
    rel = np.where(m <= BAND_WIN, pad - m, pad + period - m)
    g = rel_table[:, np.clip(rel, -REL_CLIP, REL_CLIP) + REL_CLIP].astype(F32)
    bias = jnp.tile(g, (1, TQ_BAND))[:, :TQ_BAND * (period - 1)].reshape(-1, TQ_BAND, period - 1)[:, :, :BAND_WIN]
    c0 = (i // CHUNK) * CHUNK
    inband = (j >= c0) & (j < c0 + pad + CHUNK)
    return jnp.where(inband[None], bias * LOG2E, NEG_INF).transpose(0, 2, 1)


FOX_DIAG, FOX_TWO_PAST, FOX_PAST_DIAG = 0, 1, 2


def _fox_kernel(qi_tab, ka_tab, kb_tab, kind_tab, first_tab, q_ref, ka_ref, kb_ref, vta_ref, vtb_ref,
                deca_ref, decb_ref, cba_ref, cbb_ref, o_ref, acc_ref, m_ref):
    grp = pl.program_id(1)
    t = pl.program_id(2)
    kind = kind_tab[t]
    tq, tk = q_ref.shape[0], ka_ref.shape[0]
    lane2 = lax.broadcasted_iota(jnp.int32, (1, 2 * LANES), 1)
    heads = [(pp, hh) for pp in range(PAIRS_PER_STEP) for hh in range(2)]
    tile_a = (ka_ref, vta_ref, deca_ref, cba_ref)
    tile_b = (kb_ref, vtb_ref, decb_ref, cbb_ref)

    @pl.when(first_tab[t] == 1)
    def _():
        acc_ref[...] = jnp.zeros_like(acc_ref)
        m_ref[...] = jnp.full_like(m_ref, NEG_INF)

    def step(tiles):
        ones = jnp.where(lax.broadcasted_iota(jnp.int32, (tq, LANES), 1) < 2 * N_HEADS, 1.0, 0.0).astype(BF16)
        causal = (lax.broadcasted_iota(jnp.int32, (tk, tq), 0)
                  <= lax.broadcasted_iota(jnp.int32, (tk, tq), 1))
        scores = []
        for pp, hh in heads:
            head = 2 * (PAIRS_PER_STEP * grp + pp) + hh
            q_aug = jnp.concatenate([q_ref[:, pp * LANES:(pp + 1) * LANES], ones], axis=1)
            use = (((lane2 // HEAD_DIM) == hh) | (lane2 == LANES + head) | (lane2 == LANES + N_HEADS + head))
            qm = jnp.where(use, q_aug, jnp.zeros_like(q_aug))
            per_tile = []
            for (k_ref, _, dec_ref, _), diagonal in tiles:
                k_aug = jnp.concatenate([k_ref[:, pp * LANES:(pp + 1) * LANES], dec_ref[...]], axis=1)
                s = _dot_nt(k_aug, qm)
                per_tile.append(jnp.where(causal, s, NEG_INF) if diagonal else s)
            scores.append(per_tile)
        for n, (pp, hh) in enumerate(heads):
            head = 2 * (PAIRS_PER_STEP * grp + pp) + hh
            bases = [cb_ref[pl.ds(head, 1), :] for (_, _, _, cb_ref), _ in tiles]
            m_old = m_ref[n]
            m_new = m_old
            for s, base in zip(scores[n], bases):
                m_new = jnp.maximum(m_new, jnp.max(s, axis=0, keepdims=True) + base)
            acc = acc_ref[n] * jnp.exp2(m_old - m_new)
            for s, base, ((_, vt_ref, _, _), _) in zip(scores[n], bases, tiles):
                p = jnp.exp2(s - (m_new - base)).astype(BF16)
                acc = acc + _dot(_values_with_ones(vt_ref[pp], hh), p)
            acc_ref[n] = acc
            m_ref[n] = m_new

    def finish():
        for pp in range(PAIRS_PER_STEP):
            o_ref[:, pp * LANES:(pp + 1) * LANES] = _pair_finish_t(
                acc_ref[2 * pp], acc_ref[2 * pp + 1]).astype(o_ref.dtype)

    @pl.when(kind == FOX_TWO_PAST)
    def _():
        step([(tile_a, False), (tile_b, False)])

    @pl.when(kind == FOX_PAST_DIAG)
    def _():
        step([(tile_a, False), (tile_b, True)])
        finish()

    @pl.when(kind == FOX_DIAG)
    def _():
        step([(tile_a, True)])
        finish()


def _fox_attention(qkv, vt, dec, cbase):
    _, B, S, _ = qkv.shape
    tq = TQ_FOX
    nq = S // tq
    groups = N_HEADS // 2 // PAIRS_PER_STEP
    gw = PAIRS_PER_STEP * LANES
    steps = []
    for i in range(nq):
        past = list(range(i))
        first = 1
        while len(past) >= 2:
            steps.append((i, past[0], past[1], FOX_TWO_PAST, first))
            past, first = past[2:], 0
        if past:
            steps.append((i, past[0], i, FOX_PAST_DIAG, first))
        else:
            steps.append((i, i, i, FOX_DIAG, first))
    tabs = [jnp.array([s[c] for s in steps], jnp.int32) for c in range(5)]

    def key_specs(col):
        tile = lambda *a: a[3 + col][a[2]]
        return [pl.BlockSpec((None, None, tq, gw), lambda b, g, t, *tb: (3, b, tile(b, g, t, *tb), g)),
                pl.BlockSpec((None, PAIRS_PER_STEP, LANES, tq),
                             lambda b, g, t, *tb: (b, groups + g, 0, tile(b, g, t, *tb))),
                pl.BlockSpec((None, tq, LANES), lambda b, g, t, *tb: (b, tile(b, g, t, *tb), 0)),
                pl.BlockSpec((None, None, N_HEADS, tq), lambda b, g, t, *tb: (b, tile(b, g, t, *tb), 0, 0))]

    ka, va, da, ca = key_specs(1)
    kb, vb, db, cb = key_specs(2)
    grid_spec = pltpu.PrefetchScalarGridSpec(
        num_scalar_prefetch=5,
        grid=(B, groups, len(steps)),
        in_specs=[pl.BlockSpec((None, None, tq, gw), lambda b, g, t, *tb: (2, b, tb[0][t], g)),
                  ka, kb, va, vb, da, db, ca, cb],
        out_specs=pl.BlockSpec((None, tq, gw), lambda b, g, t, *tb: (b, tb[0][t], g)),
        scratch_shapes=[pltpu.VMEM((2 * PAIRS_PER_STEP, PV_ROWS, tq), F32),
                        pltpu.VMEM((2 * PAIRS_PER_STEP, 1, tq), F32)],
    )
    return pl.pallas_call(
        _fox_kernel,
        grid_spec=grid_spec,
        out_shape=jax.ShapeDtypeStruct((B, S, WIDTH), BF16),
        compiler_params=_cparams(("arbitrary", "arbitrary", "arbitrary")),
        name="fox_attn",
    )(*tabs, qkv, qkv, qkv, vt, vt, dec, dec, cbase, cbase)


def _postmix_kernel(ya_ref, yb_ref, ga_ref, gb_ref, x_ref, g1_ref, wpa_ref, wpb_ref, wo_ref,
                    g_ref, sh_ref, sc_ref, wr_ref, rb_ref, ustrict_ref, lstrict_ref,
                    x1_ref, h2_ref, posk_ref, wk_ref, cnt_ref):
    ua = _dot(ya_ref[0], wpa_ref[...])
    ub = _dot(yb_ref[0], wpb_ref[...])
    m = ga_ref[0].astype(F32) * ua + gb_ref[0].astype(F32) * ub
    mo = _dot(m.astype(BF16), wo_ref[...])
    x1 = x_ref[0] + g1_ref[0] * mo
    x1_ref[0] = x1
    h2 = _modnorm(x1, g_ref[...], sc_ref[0], sh_ref[0]).astype(BF16)
    h2_ref[0] = h2
    for j in range(h2.shape[0] // TM_MOE):
        posk, wk, cnt = _route(h2[j * TM_MOE:(j + 1) * TM_MOE], wr_ref[...], rb_ref[...],
                               ustrict_ref[...], lstrict_ref[...])
        posk_ref[j] = posk
        wk_ref[j] = wk
        cnt_ref[j] = cnt


def _postmix(ya, yb, ga, gb, x, g1, wpa, wpb, wo, g, sh, sc, wr_t, rb, ustrict, lstrict):
    B, S, D = x.shape
    tm = TM_PRE
    sub_tiles = tm // TM_MOE
    nt = B * S // TM_MOE
    const = lambda b, s: (0, 0)
    tok = lambda w: pl.BlockSpec((1, tm, w), lambda b, s: (b, s, 0))
    row = pl.BlockSpec((1, 1, D), lambda b, s: (b, 0, 0))
    tiles = lambda r, c: pl.BlockSpec((sub_tiles, r, c), lambda b, s: (b * (S // tm) + s, 0, 0))
    return pl.pallas_call(
        _postmix_kernel,
        grid=(B, S // tm),
        in_specs=[tok(WIDTH), tok(WIDTH), tok(D), tok(D), tok(D), row,
                  pl.BlockSpec(wpa.shape, const), pl.BlockSpec(wpb.shape, const),
                  pl.BlockSpec(wo.shape, const), pl.BlockSpec((1, D), const), row, row,
                  pl.BlockSpec(wr_t.shape, const), pl.BlockSpec(rb.shape, const),
                  pl.BlockSpec(ustrict.shape, const), pl.BlockSpec(lstrict.shape, const)],
        out_specs=[tok(D), tok(D), tiles(TOP_K, TM_MOE), tiles(TOP_K, TM_MOE), tiles(N_EXPERTS, LANES)],
        out_shape=[jax.ShapeDtypeStruct((B, S, D), F32), jax.ShapeDtypeStruct((B, S, D), BF16),
                   jax.ShapeDtypeStruct((nt, TOP_K, TM_MOE), F32),
                   jax.ShapeDtypeStruct((nt, TOP_K, TM_MOE), F32),
                   jax.ShapeDtypeStruct((nt, N_EXPERTS, LANES), F32)],
        compiler_params=_cparams(("arbitrary", "arbitrary")),
        name="postmix",
    )(ya, yb, ga, gb, x, g1, wpa, wpb, wo, g, sh, sc, wr_t, rb, ustrict, lstrict)


def _route(h, wr, rb, ustrict, lstrict):
    tm = h.shape[0]
    per_group = N_EXPERTS // N_GROUPS
    logits = _dot_nt(wr, h)
    scores = _sigmoid(logits)
    biased = scores + rb
    sub8 = lax.broadcasted_iota(jnp.int32, (per_group, tm), 0).astype(F32)
    neg = -jnp.inf

    grp_rows = []
    for g in range(N_GROUPS):
        a = biased[g * per_group:(g + 1) * per_group]
        m1 = jnp.max(a, axis=0, keepdims=True)
        i1 = jnp.min(jnp.where(a == m1, sub8, float(per_group)), axis=0, keepdims=True)
        m2 = jnp.max(jnp.where(sub8 == i1, neg, a), axis=0, keepdims=True)
        grp_rows.append(m1 + m2)
    grp = jnp.concatenate(grp_rows, axis=0)

    gsub = lax.broadcasted_iota(jnp.int32, (N_GROUPS, tm), 0).astype(F32)
    gmask = jnp.zeros((N_GROUPS, tm), F32)
    for _ in range(TOPK_GROUPS):
        mx = jnp.max(grp, axis=0, keepdims=True)
        gi = jnp.min(jnp.where(grp == mx, gsub, float(N_GROUPS)), axis=0, keepdims=True)
        sel = gsub == gi
        gmask = jnp.where(sel, 1.0, gmask)
        grp = jnp.where(sel, neg, grp)
    masked = jnp.concatenate(
        [jnp.where(gmask[g:g + 1] > 0.5, biased[g * per_group:(g + 1) * per_group], neg)
         for g in range(N_GROUPS)], axis=0)

    esub = lax.broadcasted_iota(jnp.int32, (N_EXPERTS, tm), 0).astype(F32)
    sels, ws = [], []
    for _ in range(TOP_K):
        mx = jnp.max(masked, axis=0, keepdims=True)
        ei = jnp.min(jnp.where(masked == mx, esub, float(N_EXPERTS)), axis=0, keepdims=True)
        sel = esub == ei
        sels.append(sel)
        ws.append(jnp.sum(jnp.where(sel, scores, 0.0), axis=0, keepdims=True))
        masked = jnp.where(sel, neg, masked)
    wsum = ws[0]
    for w in ws[1:]:
        wsum = wsum + w

    selmask = jnp.zeros((N_EXPERTS, tm), F32)
    for sel in sels:
        selmask = selmask + jnp.where(sel, 1.0, 0.0)
    rank = _dot(selmask.astype(BF16), ustrict)
    counts = jnp.sum(selmask, axis=1, keepdims=True)
    padded = jnp.ceil(counts * (1.0 / RUN_ALIGN)) * RUN_ALIGN
    padded_b = jnp.broadcast_to(padded, (N_EXPERTS, LANES))
    base = _dot(lstrict, padded_b.astype(BF16))
    pos = base[:, 0:1] + rank

    posk = jnp.concatenate([jnp.sum(jnp.where(sel, pos, 0.0), axis=0, keepdims=True) for sel in sels], axis=0)
    wk = jnp.concatenate([(w / wsum) * ROUTED_SCALE for w in ws], axis=0)
    return posk, wk, padded_b


def _pack_halves(v):
    half = v.shape[1] // 2
    lo = lax.bitcast_convert_type(v[:, :half], U32)
    hi = lax.bitcast_convert_type(v[:, half:], U32)
    return (lo >> 16) | (hi & U32(0xFFFF0000))


def _unpack_halves(u):
    lo = lax.bitcast_convert_type(u << 16, F32).astype(BF16)
    hi = lax.bitcast_convert_type(u & U32(0xFFFF0000), F32).astype(BF16)
    return lo, hi


def _rows_copy(m, src_ref, src0, dst_ref, dst0, sem):
    rows = m * RUN_ALIGN
    return pltpu.make_async_copy(src_ref.at[pl.ds(pl.multiple_of(src0, RUN_ALIGN), rows)],
                                 dst_ref.at[pl.ds(pl.multiple_of(dst0, RUN_ALIGN), rows)], sem)


def _start_rows(m, src_ref, src0, dst_ref, dst0, sem):
    @pl.when(m > 0)
    def _():
        _rows_copy(m, src_ref, src0, dst_ref, dst0, sem).start()


def _wait_rows(m, src_ref, dst_ref, sem):
    @pl.when(m > 0)
    def _():
        _rows_copy(m, src_ref, 0, dst_ref, 0, sem).wait()


def _dispatch_kernel(off_ref, m_ref, base_ref, tot_ref, gapoff_ref, gapm_ref, h_ref, posk_ref, xs_hbm,
                     xs_ref, zero_ref, sem):
    i = pl.program_id(0)
    tm = h_ref.shape[0]
    n_chunks = K_SORT // MXU_DIM

    @pl.when(i == 0)
    def _():
        zero_ref[...] = jnp.zeros_like(zero_ref)

        def gap(e, carry):
            _start_rows(gapm_ref[e], zero_ref, 0, xs_hbm, gapoff_ref[e], sem)
            return carry
        lax.fori_loop(0, N_EXPERTS, gap, 0)

        def gap_wait(e, carry):
            _wait_rows(gapm_ref[e], zero_ref, xs_hbm, sem)
            return carry
        lax.fori_loop(0, N_EXPERTS, gap_wait, 0)

    slot = i % 2
    xs_slot = xs_ref.at[slot]
    h = h_ref[...]
    posk = posk_ref[0]
    rows = lax.broadcasted_iota(jnp.int32, (MXU_DIM, tm), 0).astype(F32).astype(BF16)
    one = jnp.ones((MXU_DIM, tm), BF16)
    def sort_chunks(lo, hi):
        for c in range(lo, hi):
            rel = posk - float(c * MXU_DIM)
            p = jnp.zeros((MXU_DIM, tm), BF16)
            for k in range(TOP_K):
                p = jnp.where(rows == rel[k:k + 1, :].astype(BF16), one, p)
            xs_slot[c * MXU_DIM:(c + 1) * MXU_DIM, :] = _pack_halves(_dot(p, h))

    sort_chunks(0, K_SORT_COMMON // MXU_DIM)

    @pl.when(tot_ref[i] * RUN_ALIGN > K_SORT_COMMON)
    def _():
        sort_chunks(K_SORT_COMMON // MXU_DIM, n_chunks)

    @pl.when(i > 0)
    def _():
        _wait_rows(tot_ref[i - 1], xs_ref.at[1 - slot], xs_hbm, sem)

    def run(e, carry):
        idx = i * N_EXPERTS + e
        _start_rows(m_ref[idx], xs_slot, base_ref[idx], xs_hbm, off_ref[idx], sem)
        return carry
    lax.fori_loop(0, N_EXPERTS, run, 0)

    @pl.when(i == pl.num_programs(0) - 1)
    def _():
        _wait_rows(tot_ref[i], xs_slot, xs_hbm, sem)


def _dispatch(off, run_len, base, tot, gapoff, gapm, h2, posk, n_rows):
    T, D = h2.shape
    tm = TM_MOE
    grid_spec = pltpu.PrefetchScalarGridSpec(
        num_scalar_prefetch=6,
        grid=(T // tm,),
        in_specs=[pl.BlockSpec((tm, D), lambda i, *_: (i, 0)),
                  pl.BlockSpec((1, TOP_K, tm), lambda i, *_: (i, 0, 0))],
        out_specs=pl.BlockSpec(memory_space=pl.ANY),
        scratch_shapes=[pltpu.VMEM((2, K_SORT, PACK_W), U32),
                        pltpu.VMEM((BLOCK_ROWS, PACK_W), U32),
                        pltpu.SemaphoreType.DMA],
    )
    return pl.pallas_call(
        _dispatch_kernel,
        grid_spec=grid_spec,
        out_shape=jax.ShapeDtypeStruct((n_rows, PACK_W), U32),
        compiler_params=_cparams(("arbitrary",)),
        name="dispatch",
    )(off, run_len, base, tot, gapoff, gapm, h2, posk)


X_RING = 3
Y_RING = 2


def _expert_kernel(bexp_ref, nvalid_ref, xs_hbm, wg_ref, wu_ref, wd_ref, ys_hbm, x_buf, y_buf, wgu_bf, wd_bf,
                   x_sem, y_sem):
    i = pl.program_id(0)
    nvalid = nvalid_ref[0]

    def rows(b):
        return pl.ds(pl.multiple_of(b * BLOCK_ROWS, BLOCK_ROWS), BLOCK_ROWS)

    def load(b):
        slot = b % X_RING
        return pltpu.make_async_copy(xs_hbm.at[rows(b)], x_buf.at[slot], x_sem.at[slot])

    def store(b):
        slot = b % Y_RING
        return pltpu.make_async_copy(y_buf.at[slot], ys_hbm.at[rows(b)], y_sem.at[slot])

    @pl.when(i == 0)
    def _():
        for b in range(X_RING - 1):
            @pl.when(b < nvalid)
            def _(b=b):
                load(b).start()

    @pl.when(i < nvalid)
    def _():
        @pl.when(i + X_RING - 1 < nvalid)
        def _():
            load(i + X_RING - 1).start()

        @pl.when((i == 0) | (bexp_ref[i] != bexp_ref[jnp.maximum(i - 1, 0)]))
        def _():
            wgu_bf[:, :EXPERT_FF] = wg_ref[...].astype(BF16)
            wgu_bf[:, EXPERT_FF:] = wu_ref[...].astype(BF16)
            wd_bf[...] = wd_ref[...].astype(BF16)

        load(i).wait()
        x_lo, x_hi = _unpack_halves(x_buf[i % X_RING])
        gu = _dot(x_lo, wgu_bf[:PACK_W, :]) + _dot(x_hi, wgu_bf[PACK_W:, :])
        g = gu[:, :EXPERT_FF]
        a = (g * _sigmoid(g) * gu[:, EXPERT_FF:]).astype(BF16)
        y = _dot(a, wd_bf[...]).astype(BF16).astype(F32)

        @pl.when(i >= Y_RING)
        def _():
            store(i - Y_RING).wait()
        y_buf[i % Y_RING] = _pack_halves(y)
        store(i).start()

        @pl.when(i == nvalid - 1)
        def _():
            for back in range(Y_RING):
                @pl.when(i - back >= 0)
                def _(back=back):
                    store(i - back).wait()


def _experts(bexp, nvalid, xs, wg, wu, wd, layer):
    n_rows, D = xs.shape[0], D_MODEL
    nb = n_rows // BLOCK_ROWS

    def blk(i, be, nv):
        return jnp.minimum(i, nv[0] - 1)

    wspec = lambda r, c: pl.BlockSpec((None, None, r, c),
                                      lambda i, be, nv: (layer, be[blk(i, be, nv)], 0, 0))
    grid_spec = pltpu.PrefetchScalarGridSpec(
        num_scalar_prefetch=2,
        grid=(nb,),
        in_specs=[pl.BlockSpec(memory_space=pl.ANY),
                  wspec(D, EXPERT_FF), wspec(D, EXPERT_FF), wspec(EXPERT_FF, D)],
        out_specs=pl.BlockSpec(memory_space=pl.ANY),
        scratch_shapes=[pltpu.VMEM((X_RING, BLOCK_ROWS, PACK_W), U32), pltpu.VMEM((Y_RING, BLOCK_ROWS, PACK_W), U32),
                        pltpu.VMEM((D, 2 * EXPERT_FF), BF16), pltpu.VMEM((EXPERT_FF, D), BF16),
                        pltpu.SemaphoreType.DMA((X_RING,)), pltpu.SemaphoreType.DMA((Y_RING,))],
    )
    return pl.pallas_call(
        _expert_kernel,
        grid_spec=grid_spec,
        out_shape=jax.ShapeDtypeStruct((n_rows, PACK_W), U32),
        compiler_params=_cparams(("arbitrary",)),
        name="experts",
    )(bexp, nvalid, xs, wg, wu, wd)


def _combine_kernel(off_ref, m_ref, base_ref, tot_ref, ys_hbm, posk_ref, wk_ref, h_ref, x_ref, g2_ref,
                    wsgu_ref, wsd_ref, o_ref, ys_ref, pw_ref, sem):
    i = pl.program_id(0)
    tm = h_ref.shape[0]
    slot = i % 2

    def fetch(tile, buf_slot):
        def run(e, carry):
            idx = tile * N_EXPERTS + e
            _start_rows(m_ref[idx], ys_hbm, off_ref[idx], ys_ref.at[buf_slot], base_ref[idx],
                        sem.at[buf_slot])
            return carry
        lax.fori_loop(0, N_EXPERTS, run, 0)

    @pl.when(i == 0)
    def _():
        ys_ref[...] = jnp.zeros_like(ys_ref)
        fetch(i, slot)

    @pl.when(i + 1 < pl.num_programs(0))
    def _():
        fetch(i + 1, 1 - slot)

    h = h_ref[...]
    gu = _dot(h, wsgu_ref[...])
    g = gu[:, :EXPERT_FF]
    shared = _dot((g * _sigmoid(g) * gu[:, EXPERT_FF:]).astype(BF16), wsd_ref[...])

    posk = posk_ref[0]
    wk = wk_ref[0]
    rows = lax.broadcasted_iota(jnp.int32, (MXU_DIM, tm), 0).astype(F32).astype(BF16)

    def weight_chunks(lo, hi):
        for c in range(lo, hi):
            rel = posk - float(c * MXU_DIM)
            pw = jnp.zeros((MXU_DIM, tm), BF16)
            for k in range(TOP_K):
                wrow = jnp.broadcast_to(wk[k:k + 1, :], (MXU_DIM, tm)).astype(BF16)
                pw = jnp.where(rows == rel[k:k + 1, :].astype(BF16), wrow, pw)
            pw_ref[c * MXU_DIM:(c + 1) * MXU_DIM, :] = pw

    def unsort(lo, hi):
        pw = pw_ref[lo:hi, :]
        halves = [lax.dot_general(pw, y, (((0,), (0,)), ((), ())), preferred_element_type=F32)
                  for y in _unpack_halves(ys_ref[slot, lo:hi, :])]
        return jnp.concatenate(halves, axis=1)

    weight_chunks(0, K_SORT_COMMON // MXU_DIM)
    _wait_rows(tot_ref[i], ys_hbm, ys_ref.at[slot], sem.at[slot])
    o_ref[...] = x_ref[...] + g2_ref[0] * (unsort(0, K_SORT_COMMON) + shared)

    @pl.when(tot_ref[i] * RUN_ALIGN > K_SORT_COMMON)
    def _():
        weight_chunks(K_SORT_COMMON // MXU_DIM, K_SORT // MXU_DIM)
        o_ref[...] = o_ref[...] + g2_ref[0] * unsort(K_SORT_COMMON, K_SORT)


def _combine(off, run_len, base, tot, ys, posk, wk, h2, x1, g2, wsgu, wsd, tiles_per_batch):
    T, D = h2.shape
    tm = TM_MOE
    const = lambda i, *_: (0, 0)
    tok = lambda w: pl.BlockSpec((tm, w), lambda i, *_: (i, 0))
    sel = pl.BlockSpec((1, TOP_K, tm), lambda i, *_: (i, 0, 0))
    grid_spec = pltpu.PrefetchScalarGridSpec(
        num_scalar_prefetch=4,
        grid=(T // tm,),
        in_specs=[pl.BlockSpec(memory_space=pl.ANY), sel, sel, tok(D), tok(D),
                  pl.BlockSpec((1, 1, D), lambda i, *_: (i // tiles_per_batch, 0, 0)),
                  pl.BlockSpec(wsgu.shape, const), pl.BlockSpec(wsd.shape, const)],
        out_specs=tok(D),
        scratch_shapes=[pltpu.VMEM((2, K_SORT, PACK_W), U32),
                        pltpu.VMEM((K_SORT, tm), BF16),
                        pltpu.SemaphoreType.DMA((2,))],
    )
    return pl.pallas_call(
        _combine_kernel,
        grid_spec=grid_spec,
        out_shape=jax.ShapeDtypeStruct((T, D), F32),
        compiler_params=_cparams(("arbitrary",)),
        name="combine",
    )(off, run_len, base, tot, ys, posk, wk, h2, x1, g2, wsgu, wsd)


def _moe(h2, x1, g2, posk, wk, cnt, wg, wu, wd, layer, wsgu, wsd, tiles_per_batch):
    T, D = h2.shape
    nt = T // TM_MOE

    pad = cnt[:, :, 0].astype(jnp.int32)
    base = jnp.cumsum(pad, axis=1) - pad
    tile_off = jnp.cumsum(pad, axis=0) - pad
    total = jnp.sum(pad, axis=0)
    region = ((total + BLOCK_ROWS - 1) // BLOCK_ROWS) * BLOCK_ROWS
    rend = jnp.cumsum(region)
    rstart = rend - region
    off = rstart[None, :] + tile_off
    nb_max = -(-(TOP_K * T + nt * N_EXPERTS * (RUN_ALIGN - 1)) // BLOCK_ROWS) + N_EXPERTS
    blk_row = jnp.arange(nb_max, dtype=jnp.int32) * BLOCK_ROWS
    bexp = jnp.minimum(jnp.sum((rend[None, :] <= blk_row[:, None]).astype(jnp.int32), axis=1), N_EXPERTS - 1)
    nvalid = (rend[-1:] // BLOCK_ROWS).astype(jnp.int32)
    flat = lambda a: a.reshape(-1).astype(jnp.int32)

    run_len = flat(pad // RUN_ALIGN)
    tot = flat(jnp.sum(pad, axis=1) // RUN_ALIGN)
    xs = _dispatch(flat(off), run_len, flat(base), tot, flat(rstart + total),
                   flat((region - total) // RUN_ALIGN), h2, posk, nb_max * BLOCK_ROWS)
    ys = _experts(bexp, nvalid, xs, wg, wu, wd, layer)
    return _combine(flat(off), run_len, flat(base), tot, ys, posk, wk, h2, x1, g2, wsgu, wsd, tiles_per_batch)


def kernel(x, c, ada_w, ada_b, mix_norm_g, w_in, b_fgate, qn_a, kn_a, qn_b, kn_b, rel_bias, w_proj_a,
           w_proj_b, w_out, ffn_norm_g, w_router, router_bias, w_gate_e, w_up_e, w_down_e, w_gate_s,
           w_up_s, w_down_s):
    B, S, D = x.shape
    L = ada_w.shape[0]
    T = B * S
    assert D == D_MODEL and S % TM_PRE == 0 and TQ_FOX == TM_PRE and S % TM_MOE == 0 and S % (BAND_TILES * TQ_BAND) == 0

    mod = _adaln(c, ada_w, ada_b).reshape(L, B, 6, 1, D)

    hid = np.arange(MXU_DIM) // HEAD_DIM
    bd = jnp.asarray(np.where(hid[:, None] == hid[None, :], 1.0 / HEAD_DIM, 0.0), BF16)
    r = np.arange(TM_PRE)
    tri = jnp.asarray(r[:, None] <= r[None, :], BF16)
    r = np.arange(TM_MOE)
    ustrict = jnp.asarray(r[:, None] < r[None, :], BF16)
    r = np.arange(N_EXPERTS)
    lstrict = jnp.asarray(r[None, :] < r[:, None], BF16)

    q_scale = 1.0 / math.sqrt(HEAD_DIM)
    for l in range(L):
        sh1, sc1, g1, sh2, sc2, g2 = (mod[l, :, j] for j in range(6))
        w = w_in[l]
        cols = lambda c: w[:, c * WIDTH:(c + 1) * WIDTH]
        wqk = jnp.concatenate([cols(0), cols(1), cols(3), cols(4)], axis=1).astype(BF16)
        wvt = jnp.concatenate([cols(2), cols(5)], axis=1).T.astype(BF16)
        wf = jnp.pad(w[:, 6 * WIDTH:6 * WIDTH + N_HEADS].T.astype(BF16), ((0, N_HEADS), (0, 0)))
        wg = w[:, 6 * WIDTH + N_HEADS:].astype(BF16)
        gains = jnp.stack([qn_a[l] * (q_scale * LOG2E), kn_a[l], qn_b[l] * (q_scale * LOG2E), kn_b[l]])
        hn = jnp.pad(jnp.tile(gains, (1, N_HEADS)), ((0, 4), (0, 0)))

        qk, vt, dec, cbase, ga, gb = _premix(x, sh1, sc1, mix_norm_g[l][None], wqk, wvt, wf, wg,
                                             b_fgate[l][:, None], hn, bd, tri)
        ya = _band_attention(qk, vt, _band_bias(rel_bias[l]))
        yb = _fox_attention(qk, vt, dec, cbase)
        x1, h2, posk, wk, cnt = _postmix(
            ya, yb, ga, gb, x, g1, w_proj_a[l].astype(BF16), w_proj_b[l].astype(BF16),
            w_out[l].astype(BF16), ffn_norm_g[l][None], sh2, sc2, w_router[l].T.astype(BF16),
            router_bias[l][:, None], ustrict, lstrict)

        wsgu = jnp.concatenate([w_gate_s[l], w_up_s[l]], axis=-1).astype(BF16)
        x = _moe(h2.reshape(T, D), x1.reshape(T, D), g2, posk, wk, cnt, w_gate_e, w_up_e, w_down_e, l,
                 wsgu, w_down_s[l].astype(BF16), S // TM_MOE).reshape(B, S, D)
    return x
```

```python
import math

import jax
import jax.numpy as jnp
import numpy as np
from jax import lax
from jax.experimental import pallas as pl
from jax.experimental.pallas import tpu as pltpu

F32 = jnp.float32
BF16 = jnp.bfloat16

D_MODEL = 1024
HEAD_DIM = 64
N_HEADS = 8
WIDTH = N_HEADS * HEAD_DIM
CHUNK = 64
LOOKBACK_CHUNKS = 8
REL_CLIP = 128
N_EXPERTS = 64
TOP_K = 8
N_GROUPS = 8
TOPK_GROUPS = 4
EXPERT_FF = 256
ROUTED_SCALE = 2.5
EPS = 1e-6
NEG_INF = -1e30
LOG2E = math.log2(math.e)

LANES = 128
MXU_DIM = 256
VMEM_LIMIT = 56 * 1024 * 1024

TM_PRE = 512
TQ_BAND = 256
BAND_WIN = TQ_BAND + LOOKBACK_CHUNKS * CHUNK
BAND_TILES = LOOKBACK_CHUNKS * CHUNK // TQ_BAND
TQ_FOX = 512
PAIRS_PER_STEP = 2
PV_ROWS = HEAD_DIM + 16
TM_MOE = 256
RUN_ALIGN = 8
PACK_W = D_MODEL // 2
U32 = jnp.uint32
K_SORT = ((TOP_K * TM_MOE + N_EXPERTS * (RUN_ALIGN - 1) + MXU_DIM - 1) // MXU_DIM) * MXU_DIM
K_SORT_COMMON = 2304
BLOCK_ROWS = 1024


def _dot(a, b):
    return jnp.dot(a, b, preferred_element_type=F32)


def _dot_nt(a, b):
    return lax.dot_general(a, b, (((1,), (1,)), ((), ())), preferred_element_type=F32)


def _sigmoid(v):
    return 1.0 / (1.0 + jnp.exp(-v))


def _cparams(sem):
    return pltpu.CompilerParams(dimension_semantics=sem, vmem_limit_bytes=VMEM_LIMIT)


def _adaln_kernel(c_ref, w_ref, b_ref, o_ref):
    c = c_ref[...]
    ca = c * _sigmoid(c)
    o_ref[0] = jnp.dot(ca, w_ref[0], preferred_element_type=F32,
                       precision=lax.Precision.HIGHEST) + b_ref[0]


def _adaln(c, ada_w, ada_b):
    L, D, N = ada_w.shape
    B = c.shape[0]
    tn = N // 4
    return pl.pallas_call(
        _adaln_kernel,
        grid=(L, N // tn),
        in_specs=[pl.BlockSpec((B, D), lambda l, j: (0, 0)),
                  pl.BlockSpec((1, D, tn), lambda l, j: (l, 0, j)),
                  pl.BlockSpec((1, 1, tn), lambda l, j: (l, 0, j))],
        out_specs=pl.BlockSpec((1, B, tn), lambda l, j: (l, 0, j)),
        out_shape=jax.ShapeDtypeStruct((L, B, N), F32),
        compiler_params=_cparams(("arbitrary", "arbitrary")),
        name="adaln",
    )(c, ada_w, ada_b.reshape(L, 1, N))


def _modnorm(x, g, sc, sh):
    ms = jnp.mean(x * x, axis=-1, keepdims=True)
    y = x * lax.rsqrt(ms + EPS) * g
    return y * (1.0 + sc) + sh


def _premix_kernel(x_ref, sh_ref, sc_ref, g_ref, wqk_ref, wvt_ref, wf_ref, wg_ref, bf_ref, hn_ref, bd_ref,
                   tri_ref, qk_ref, vt_ref, dec_ref, cbase_ref, ga_ref, gb_ref, carry_ref):
    si = pl.program_id(1)
    tm = x_ref.shape[1]
    h = _modnorm(x_ref[0], g_ref[...], sc_ref[0], sh_ref[0]).astype(BF16)

    for c in range(4):
        z = _dot(h, wqk_ref[:, c * WIDTH:(c + 1) * WIDTH])
        sq = (z * z).astype(BF16)
        ms = jnp.concatenate(
            [_dot(sq[:, j * MXU_DIM:(j + 1) * MXU_DIM], bd_ref[...]) for j in range(WIDTH // MXU_DIM)],
            axis=1)
        z = z * lax.rsqrt(ms + EPS) * hn_ref[c:c + 1, :]
        qk_ref[c, 0] = z.astype(BF16)

    for c in range(2):
        vt = _dot_nt(wvt_ref[c * WIDTH:(c + 1) * WIDTH, :], h).astype(BF16)
        vt_ref[0, c * (N_HEADS // 2):(c + 1) * (N_HEADS // 2)] = vt.reshape(N_HEADS // 2, LANES, tm)

    fr = _dot_nt(wf_ref[...], h)[:N_HEADS]
    xg = fr + bf_ref[...]
    logf = jnp.minimum(xg, 0.0) - jnp.log(1.0 + jnp.exp(-jnp.abs(xg)))
    hi = logf.astype(BF16).astype(F32)
    r1 = logf - hi
    mid = r1.astype(BF16).astype(F32)
    lo = r1 - mid
    parts = jnp.concatenate([hi, mid, lo, jnp.zeros_like(hi)], axis=0)
    cs3 = _dot(parts.astype(BF16), tri_ref[...])
    cs = cs3[0:8] + cs3[8:16] + cs3[16:24]

    @pl.when(si == 0)
    def _():
        carry_ref[...] = jnp.zeros_like(carry_ref)

    drel = cs * (-LOG2E)
    dhi = drel.astype(BF16).astype(F32)
    dlo = drel - dhi
    dec = jnp.concatenate([dhi, dlo, jnp.zeros((LANES - 2 * N_HEADS, tm), F32)], axis=0)
    dec_ref[0] = dec.T.astype(BF16)
    carry = carry_ref[:, 0:1]
    cbase_ref[0, 0] = jnp.broadcast_to(carry * (-LOG2E), (N_HEADS, tm))
    carry_ref[...] = jnp.broadcast_to(carry + cs[:, tm - 1:tm], carry_ref.shape)

    for c in range(4):
        zg = _dot(h, wg_ref[:, c * WIDTH:(c + 1) * WIDTH])
        sg = _sigmoid(zg).astype(BF16)
        if c < 2:
            ga_ref[0, :, c * WIDTH:(c + 1) * WIDTH] = sg
        else:
            gb_ref[0, :, (c - 2) * WIDTH:(c - 1) * WIDTH] = sg


def _premix(x, sh, sc, g, wqk, wvt, wf, wg, bf, hn, bd, tri):
    B, S, D = x.shape
    tm = TM_PRE
    const = lambda b, s: (0, 0)
    return pl.pallas_call(
        _premix_kernel,
        grid=(B, S // tm),
        in_specs=[pl.BlockSpec((1, tm, D), lambda b, s: (b, s, 0)),
                  pl.BlockSpec((1, 1, D), lambda b, s: (b, 0, 0)),
                  pl.BlockSpec((1, 1, D), lambda b, s: (b, 0, 0)),
                  pl.BlockSpec((1, D), const),
                  pl.BlockSpec(wqk.shape, const),
                  pl.BlockSpec(wvt.shape, const),
                  pl.BlockSpec(wf.shape, const),
                  pl.BlockSpec(wg.shape, const),
                  pl.BlockSpec(bf.shape, const),
                  pl.BlockSpec(hn.shape, const),
                  pl.BlockSpec(bd.shape, const),
                  pl.BlockSpec(tri.shape, const)],
        out_specs=[pl.BlockSpec((4, 1, tm, WIDTH), lambda b, s: (0, b, s, 0)),
                   pl.BlockSpec((1, N_HEADS, LANES, tm), lambda b, s: (b, 0, 0, s)),
                   pl.BlockSpec((1, tm, LANES), lambda b, s: (b, s, 0)),
                   pl.BlockSpec((1, 1, N_HEADS, tm), lambda b, s: (b, s, 0, 0)),
                   pl.BlockSpec((1, tm, D), lambda b, s: (b, s, 0)),
                   pl.BlockSpec((1, tm, D), lambda b, s: (b, s, 0))],
        out_shape=[jax.ShapeDtypeStruct((4, B, S, WIDTH), BF16),
                   jax.ShapeDtypeStruct((B, N_HEADS, LANES, S), BF16),
                   jax.ShapeDtypeStruct((B, S, LANES), BF16),
                   jax.ShapeDtypeStruct((B, S // tm, N_HEADS, tm), F32),
                   jax.ShapeDtypeStruct((B, S, D), BF16),
                   jax.ShapeDtypeStruct((B, S, D), BF16)],
        scratch_shapes=[pltpu.VMEM((N_HEADS, LANES), F32)],
        compiler_params=_cparams(("arbitrary", "arbitrary")),
        name="premix",
    )(x, sh, sc, g, wqk, wvt, wf, wg, bf, hn, bd, tri)


def _values_with_ones(vt, hh):
    ones = jnp.ones((PV_ROWS - HEAD_DIM, vt.shape[1]), vt.dtype)
    return jnp.concatenate([vt[hh * HEAD_DIM:(hh + 1) * HEAD_DIM, :], ones], axis=0)


def _pair_finish_t(acc0, acc1):
    out_t = jnp.concatenate([acc[:HEAD_DIM] / acc[HEAD_DIM:HEAD_DIM + 1, :] for acc in (acc0, acc1)], axis=0)
    return out_t.T


def _band_kernel(q_ref, kprev_ref, kcur_ref, vprev_ref, vcur_ref, bias_ref, o_ref):
    tq = TQ_BAND
    step_start = pl.program_id(2) * (BAND_TILES * tq)
    lane = lax.broadcasted_iota(jnp.int32, (1, LANES), 1)
    krow = lax.broadcasted_iota(jnp.int32, (BAND_WIN, 1), 0)
    pieces = [(ref, j) for ref in (kprev_ref, kcur_ref) for j in range(BAND_TILES)]
    vpieces = [(ref, j) for ref in (vprev_ref, vcur_ref) for j in range(BAND_TILES)]
    units = [(u, pp, hh) for u in range(BAND_TILES) for pp in range(PAIRS_PER_STEP) for hh in range(2)]
    scores = []
    for u, pp, hh in units:
        q = q_ref[u * tq:(u + 1) * tq, pp * LANES:(pp + 1) * LANES]
        qm = jnp.where((lane // HEAD_DIM) == hh, q, jnp.zeros_like(q))
        scores.append(jnp.concatenate(
            [_dot_nt(ref[j * tq:(j + 1) * tq, pp * LANES:(pp + 1) * LANES], qm)
             for ref, j in pieces[u:u + 3]], axis=0))
    accs = []
    for n, (u, pp, hh) in enumerate(units):
        kvalid = (krow + (step_start + u * tq - LOOKBACK_CHUNKS * CHUNK)) >= 0
        s = jnp.where(kvalid, scores[n] + bias_ref[2 * pp + hh], NEG_INF)
        m = jnp.max(s, axis=0, keepdims=True)
        p = jnp.exp2(s - m).astype(BF16)
        acc = None
        for c, (ref, j) in enumerate(vpieces[u:u + 3]):
            t = _dot(_values_with_ones(ref[pp, :, j * tq:(j + 1) * tq], hh), p[c * tq:(c + 1) * tq, :])
            acc = t if acc is None else acc + t
        accs.append(acc)
    for n in range(0, len(units), 2):
        u, pp, _ = units[n]
        o_ref[u * tq:(u + 1) * tq, pp * LANES:(pp + 1) * LANES] = _pair_finish_t(
            accs[n], accs[n + 1]).astype(o_ref.dtype)


def _band_attention(qk, vt, bias_t):
    _, B, S, _ = qk.shape
    rows = BAND_TILES * TQ_BAND
    groups = N_HEADS // 2 // PAIRS_PER_STEP
    gw = PAIRS_PER_STEP * LANES

    def k_spec(back):
        return pl.BlockSpec((None, None, rows, gw),
                            lambda b, g, qi: (1, b, jnp.maximum(qi - back, 0), g))

    def v_spec(back):
        return pl.BlockSpec((None, PAIRS_PER_STEP, LANES, rows),
                            lambda b, g, qi: (b, g, 0, jnp.maximum(qi - back, 0)))

    return pl.pallas_call(
        _band_kernel,
        grid=(B, groups, S // rows),
        in_specs=[pl.BlockSpec((None, None, rows, gw), lambda b, g, qi: (0, b, qi, g)),
                  k_spec(1), k_spec(0), v_spec(1), v_spec(0),
                  pl.BlockSpec((2 * PAIRS_PER_STEP, BAND_WIN, TQ_BAND), lambda b, g, qi: (g, 0, 0))],
        out_specs=pl.BlockSpec((None, rows, gw), lambda b, g, qi: (b, qi, g)),
        out_shape=jax.ShapeDtypeStruct((B, S, WIDTH), BF16),
        compiler_params=_cparams(("arbitrary", "arbitrary", "arbitrary")),
        name="band_attn",
    )(qk, qk, qk, vt, vt, bias_t)


def _band_bias(rel_table):
    pad = LOOKBACK_CHUNKS * CHUNK
    i = np.arange(TQ_BAND)[:, None]
    j = np.arange(BAND_WIN)[None, :]
    period = 1024
    assert BAND_WIN + TQ_BAND <= period
    m = np.arange(period)
    rel = np.where(m <= BAND_WIN, pad - m, pad + period - m)
    g = rel_table[:, np.clip(rel, -REL_CLIP, REL_CLIP) + REL_CLIP].astype(F32)
    bias = jnp.tile(g, (1, TQ_BAND))[:, :TQ_BAND * (period - 1)].reshape(-1, TQ_BAND, period - 1)[:, :, :BAND_WIN]
    c0 = (i // CHUNK) * CHUNK
    inband = (j >= c0) & (j < c0 + pad + CHUNK)
    return jnp.where(inband[None], bias * LOG2E, NEG_INF).transpose(0, 2, 1)


FOX_DIAG, FOX_TWO_PAST, FOX_PAST_DIAG = 0, 1, 2


def _fox_kernel(qi_tab, ka_tab, kb_tab, kind_tab, first_tab, q_ref, ka_ref, kb_ref, vta_ref, vtb_ref,
                deca_ref, decb_ref, cba_ref, cbb_ref, o_ref, acc_ref, m_ref):
    grp = pl.program_id(1)
    t = pl.program_id(2)
    kind = kind_tab[t]
    tq, tk = q_ref.shape[0], ka_ref.shape[0]
    lane2 = lax.broadcasted_iota(jnp.int32, (1, 2 * LANES), 1)
    heads = [(pp, hh) for pp in range(PAIRS_PER_STEP) for hh in range(2)]
    tile_a = (ka_ref, vta_ref, deca_ref, cba_ref)
    tile_b = (kb_ref, vtb_ref, decb_ref, cbb_ref)

    @pl.when(first_tab[t] == 1)
    def _():
        acc_ref[...] = jnp.zeros_like(acc_ref)
        m_ref[...] = jnp.full_like(m_ref, NEG_INF)

    def step(tiles):
        ones = jnp.where(lax.broadcasted_iota(jnp.int32, (tq, LANES), 1) < 2 * N_HEADS, 1.0, 0.0).astype(BF16)
        causal = (lax.broadcasted_iota(jnp.int32, (tk, tq), 0)
                  <= lax.broadcasted_iota(jnp.int32, (tk, tq), 1))
        scores = []
        for pp, hh in heads:
            head = 2 * (PAIRS_PER_STEP * grp + pp) + hh
            q_aug = jnp.concatenate([q_ref[:, pp * LANES:(pp + 1) * LANES], ones], axis=1)
            use = (((lane2 // HEAD_DIM) == hh) | (lane2 == LANES + head) | (lane2 == LANES + N_HEADS + head))
            qm = jnp.where(use, q_aug, jnp.zeros_like(q_aug))
            per_tile = []
            for (k_ref, _, dec_ref, _), diagonal in tiles:
                k_aug = jnp.concatenate([k_ref[:, pp * LANES:(pp + 1) * LANES], dec_ref[...]], axis=1)
                s = _dot_nt(k_aug, qm)
                per_tile.append(jnp.where(causal, s, NEG_INF) if diagonal else s)
            scores.append(per_tile)
        for n, (pp, hh) in enumerate(heads):
            head = 2 * (PAIRS_PER_STEP * grp + pp) + hh
            bases = [cb_ref[pl.ds(head, 1), :] for (_, _, _, cb_ref), _ in tiles]
            m_old = m_ref[n]
            m_new = m_old
            for s, base in zip(scores[n], bases):
                m_new = jnp.maximum(m_new, jnp.max(s, axis=0, keepdims=True) + base)
            acc = acc_ref[n] * jnp.exp2(m_old - m_new)
            for s, base, ((_, vt_ref, _, _), _) in zip(scores[n], bases, tiles):
                p = jnp.exp2(s - (m_new - base)).astype(BF16)
                acc = acc + _dot(_values_with_ones(vt_ref[pp], hh), p)
            acc_ref[n] = acc
            m_ref[n] = m_new

    def finish():
        for pp in range(PAIRS_PER_STEP):
            o_ref[:, pp * LANES:(pp + 1) * LANES] = _pair_finish_t(
                acc_ref[2 * pp], acc_ref[2 * pp + 1]).astype(o_ref.dtype)

    @pl.when(kind == FOX_TWO_PAST)
    def _():
        step([(tile_a, False), (tile_b, False)])

    @pl.when(kind == FOX_PAST_DIAG)
    def _():
        step([(tile_a, False), (tile_b, True)])
        finish()

    @pl.when(kind == FOX_DIAG)
    def _():
        step([(tile_a, True)])
        finish()


def _fox_attention(qkv, vt, dec, cbase):
    _, B, S, _ = qkv.shape
    tq = TQ_FOX
    nq = S // tq
    groups = N_HEADS // 2 // PAIRS_PER_STEP
    gw = PAIRS_PER_STEP * LANES
    steps = []
    for i in range(nq):
        past = list(range(i))
        first = 1
        while len(past) >= 2:
            steps.append((i, past[0], past[1], FOX_TWO_PAST, first))
            past, first = past[2:], 0
        if past:
            steps.append((i, past[0], i, FOX_PAST_DIAG, first))
        else:
            steps.append((i, i, i, FOX_DIAG, first))
    tabs = [jnp.array([s[c] for s in steps], jnp.int32) for c in range(5)]

    def key_specs(col):
        tile = lambda *a: a[3 + col][a[2]]
        return [pl.BlockSpec((None, None, tq, gw), lambda b, g, t, *tb: (3, b, tile(b, g, t, *tb), g)),
                pl.BlockSpec((None, PAIRS_PER_STEP, LANES, tq),
                             lambda b, g, t, *tb: (b, groups + g, 0, tile(b, g, t, *tb))),
                pl.BlockSpec((None, tq, LANES), lambda b, g, t, *tb: (b, tile(b, g, t, *tb), 0)),
                pl.BlockSpec((None, None, N_HEADS, tq), lambda b, g, t, *tb: (b, tile(b, g, t, *tb), 0, 0))]

    ka, va, da, ca = key_specs(1)
    kb, vb, db, cb = key_specs(2)
    grid_spec = pltpu.PrefetchScalarGridSpec(
        num_scalar_prefetch=5,
        grid=(B, groups, len(steps)),
        in_specs=[pl.BlockSpec((None, None, tq, gw), lambda b, g, t, *tb: (2, b, tb[0][t], g)),
                  ka, kb, va, vb, da, db, ca, cb],
        out_specs=pl.BlockSpec((None, tq, gw), lambda b, g, t, *tb: (b, tb[0][t], g)),
        scratch_shapes=[pltpu.VMEM((2 * PAIRS_PER_STEP, PV_ROWS, tq), F32),
                        pltpu.VMEM((2 * PAIRS_PER_STEP, 1, tq), F32)],
    )
    return pl.pallas_call(
        _fox_kernel,
        grid_spec=grid_spec,
        out_shape=jax.ShapeDtypeStruct((B, S, WIDTH), BF16),
        compiler_params=_cparams(("arbitrary", "arbitrary", "arbitrary")),
        name="fox_attn",
    )(*tabs, qkv, qkv, qkv, vt, vt, dec, dec, cbase, cbase)


def _postmix_kernel(ya_ref, yb_ref, ga_ref, gb_ref, x_ref, g1_ref, wpa_ref, wpb_ref, wo_ref,
                    g_ref, sh_ref, sc_ref, wr_ref, rb_ref, ustrict_ref, lstrict_ref,
                    x1_ref, h2_ref, posk_ref, wk_ref, cnt_ref):
    ua = _dot(ya_ref[0], wpa_ref[...])
    ub = _dot(yb_ref[0], wpb_ref[...])
    m = ga_ref[0].astype(F32) * ua + gb_ref[0].astype(F32) * ub
    mo = _dot(m.astype(BF16), wo_ref[...])
    x1 = x_ref[0] + g1_ref[0] * mo
    x1_ref[0] = x1
    h2 = _modnorm(x1, g_ref[...], sc_ref[0], sh_ref[0]).astype(BF16)
    h2_ref[0] = h2
    for j in range(h2.shape[0] // TM_MOE):
        posk, wk, cnt = _route(h2[j * TM_MOE:(j + 1) * TM_MOE], wr_ref[...], rb_ref[...],
                               ustrict_ref[...], lstrict_ref[...])
        posk_ref[j] = posk
        wk_ref[j] = wk
        cnt_ref[j] = cnt


def _postmix(ya, yb, ga, gb, x, g1, wpa, wpb, wo, g, sh, sc, wr_t, rb, ustrict, lstrict):
    B, S, D = x.shape
    tm = TM_PRE
    sub_tiles = tm // TM_MOE
    nt = B * S // TM_MOE
    const = lambda b, s: (0, 0)
    tok = lambda w: pl.BlockSpec((1, tm, w), lambda b, s: (b, s, 0))
    row = pl.BlockSpec((1, 1, D), lambda b, s: (b, 0, 0))
    tiles = lambda r, c: pl.BlockSpec((sub_tiles, r, c), lambda b, s: (b * (S // tm) + s, 0, 0))
    return pl.pallas_call(
        _postmix_kernel,
        grid=(B, S // tm),
        in_specs=[tok(WIDTH), tok(WIDTH), tok(D), tok(D), tok(D), row,
                  pl.BlockSpec(wpa.shape, const), pl.BlockSpec(wpb.shape, const),
                  pl.BlockSpec(wo.shape, const), pl.BlockSpec((1, D), const), row, row,
                  pl.BlockSpec(wr_t.shape, const), pl.BlockSpec(rb.shape, const),
                  pl.BlockSpec(ustrict.shape, const), pl.BlockSpec(lstrict.shape, const)],
        out_specs=[tok(D), tok(D), tiles(TOP_K, TM_MOE), tiles(TOP_K, TM_MOE), tiles(N_EXPERTS, LANES)],
        out_shape=[jax.ShapeDtypeStruct((B, S, D), F32), jax.ShapeDtypeStruct((B, S, D), BF16),
                   jax.ShapeDtypeStruct((nt, TOP_K, TM_MOE), F32),
                   jax.ShapeDtypeStruct((nt, TOP_K, TM_MOE), F32),
                   jax.ShapeDtypeStruct((nt, N_EXPERTS, LANES), F32)],
        compiler_params=_cparams(("arbitrary", "arbitrary")),
        name="postmix",
    )(ya, yb, ga, gb, x, g1, wpa, wpb, wo, g, sh, sc, wr_t, rb, ustrict, lstrict)


def _route(h, wr, rb, ustrict, lstrict):
    tm = h.shape[0]
    per_group = N_EXPERTS // N_GROUPS
    logits = _dot_nt(wr, h)
    scores = _sigmoid(logits)
    biased = scores + rb
    sub8 = lax.broadcasted_iota(jnp.int32, (per_group, tm), 0).astype(F32)
    neg = -jnp.inf

    grp_rows = []
    for g in range(N_GROUPS):
        a = biased[g * per_group:(g + 1) * per_group]
        m1 = jnp.max(a, axis=0, keepdims=True)
        i1 = jnp.min(jnp.where(a == m1, sub8, float(per_group)), axis=0, keepdims=True)
        m2 = jnp.max(jnp.where(sub8 == i1, neg, a), axis=0, keepdims=True)
        grp_rows.append(m1 + m2)
    grp = jnp.concatenate(grp_rows, axis=0)

    gsub = lax.broadcasted_iota(jnp.int32, (N_GROUPS, tm), 0).astype(F32)
    gmask = jnp.zeros((N_GROUPS, tm), F32)
    for _ in range(TOPK_GROUPS):
        mx = jnp.max(grp, axis=0, keepdims=True)
        gi = jnp.min(jnp.where(grp == mx, gsub, float(N_GROUPS)), axis=0, keepdims=True)
        sel = gsub == gi
        gmask = jnp.where(sel, 1.0, gmask)
        grp = jnp.where(sel, neg, grp)
    masked = jnp.concatenate(
        [jnp.where(gmask[g:g + 1] > 0.5, biased[g * per_group:(g + 1) * per_group], neg)
         for g in range(N_GROUPS)], axis=0)

    esub = lax.broadcasted_iota(jnp.int32, (N_EXPERTS, tm), 0).astype(F32)
    sels, ws = [], []
    for _ in range(TOP_K):
        mx = jnp.max(masked, axis=0, keepdims=True)
        ei = jnp.min(jnp.where(masked == mx, esub, float(N_EXPERTS)), axis=0, keepdims=True)
        sel = esub == ei
        sels.append(sel)
        ws.append(jnp.sum(jnp.where(sel, scores, 0.0), axis=0, keepdims=True))
        masked = jnp.where(sel, neg, masked)
    wsum = ws[0]
    for w in ws[1:]:
        wsum = wsum + w

    selmask = jnp.zeros((N_EXPERTS, tm), F32)
    for sel in sels:
        selmask = selmask + jnp.where(sel, 1.0, 0.0)
    rank = _dot(selmask.astype(BF16), ustrict)
    counts = jnp.sum(selmask, axis=1, keepdims=True)
    padded = jnp.ceil(counts * (1.0 / RUN_ALIGN)) * RUN_ALIGN
    padded_b = jnp.broadcast_to(padded, (N_EXPERTS, LANES))
    base = _dot(lstrict, padded_b.astype(BF16))
    pos = base[:, 0:1] + rank

    posk = jnp.concatenate([jnp.sum(jnp.where(sel, pos, 0.0), axis=0, keepdims=True) for sel in sels], axis=0)
    wk = jnp.concatenate([(w / wsum) * ROUTED_SCALE for w in ws], axis=0)
    return posk, wk, padded_b


def _pack_halves(v):
    half = v.shape[1] // 2
    lo = lax.bitcast_convert_type(v[:, :half], U32)
    hi = lax.bitcast_convert_type(v[:, half:], U32)
    return (lo >> 16) | (hi & U32(0xFFFF0000))


def _unpack_halves(u):
    lo = lax.bitcast_convert_type(u << 16, F32).astype(BF16)
    hi = lax.bitcast_convert_type(u & U32(0xFFFF0000), F32).astype(BF16)
    return lo, hi


def _rows_copy(m, src_ref, src0, dst_ref, dst0, sem):
    rows = m * RUN_ALIGN
    return pltpu.make_async_copy(src_ref.at[pl.ds(pl.multiple_of(src0, RUN_ALIGN), rows)],
                                 dst_ref.at[pl.ds(pl.multiple_of(dst0, RUN_ALIGN), rows)], sem)


def _start_rows(m, src_ref, src0, dst_ref, dst0, sem):
    @pl.when(m > 0)
    def _():
        _rows_copy(m, src_ref, src0, dst_ref, dst0, sem).start()


def _wait_rows(m, src_ref, dst_ref, sem):
    @pl.when(m > 0)
    def _():
        _rows_copy(m, src_ref, 0, dst_ref, 0, sem).wait()


def _dispatch_kernel(off_ref, m_ref, base_ref, tot_ref, gapoff_ref, gapm_ref, h_ref, posk_ref, xs_hbm,
                     xs_ref, zero_ref, sem):
    i = pl.program_id(0)
    tm = h_ref.shape[0]
    n_chunks = K_SORT // MXU_DIM

    @pl.when(i == 0)
    def _():
        zero_ref[...] = jnp.zeros_like(zero_ref)

        def gap(e, carry):
            _start_rows(gapm_ref[e], zero_ref, 0, xs_hbm, gapoff_ref[e], sem)
            return carry
        lax.fori_loop(0, N_EXPERTS, gap, 0)

        def gap_wait(e, carry):
            _wait_rows(gapm_ref[e], zero_ref, xs_hbm, sem)
            return carry
        lax.fori_loop(0, N_EXPERTS, gap_wait, 0)

    slot = i % 2
    xs_slot = xs_ref.at[slot]
    h = h_ref[...]
    posk = posk_ref[0]
    rows = lax.broadcasted_iota(jnp.int32, (MXU_DIM, tm), 0).astype(F32).astype(BF16)
    one = jnp.ones((MXU_DIM, tm), BF16)
    def sort_chunks(lo, hi):
        for c in range(lo, hi):
            rel = posk - float(c * MXU_DIM)
            p = jnp.zeros((MXU_DIM, tm), BF16)
            for k in range(TOP_K):
                p = jnp.where(rows == rel[k:k + 1, :].astype(BF16), one, p)
            xs_slot[c * MXU_DIM:(c + 1) * MXU_DIM, :] = _pack_halves(_dot(p, h))

    sort_chunks(0, K_SORT_COMMON // MXU_DIM)

    @pl.when(tot_ref[i] * RUN_ALIGN > K_SORT_COMMON)
    def _():
        sort_chunks(K_SORT_COMMON // MXU_DIM, n_chunks)

    @pl.when(i > 0)
    def _():
        _wait_rows(tot_ref[i - 1], xs_ref.at[1 - slot], xs_hbm, sem)

    def run(e, carry):
        idx = i * N_EXPERTS + e
        _start_rows(m_ref[idx], xs_slot, base_ref[idx], xs_hbm, off_ref[idx], sem)
        return carry
    lax.fori_loop(0, N_EXPERTS, run, 0)

    @pl.when(i == pl.num_programs(0) - 1)
    def _():
        _wait_rows(tot_ref[i], xs_slot, xs_hbm, sem)


def _dispatch(off, run_len, base, tot, gapoff, gapm, h2, posk, n_rows):
    T, D = h2.shape
    tm = TM_MOE
    grid_spec = pltpu.PrefetchScalarGridSpec(
        num_scalar_prefetch=6,
        grid=(T // tm,),
        in_specs=[pl.BlockSpec((tm, D), lambda i, *_: (i, 0)),
                  pl.BlockSpec((1, TOP_K, tm), lambda i, *_: (i, 0, 0))],
        out_specs=pl.BlockSpec(memory_space=pl.ANY),
        scratch_shapes=[pltpu.VMEM((2, K_SORT, PACK_W), U32),
                        pltpu.VMEM((BLOCK_ROWS, PACK_W), U32),
                        pltpu.SemaphoreType.DMA],
    )
    return pl.pallas_call(
        _dispatch_kernel,
        grid_spec=grid_spec,
        out_shape=jax.ShapeDtypeStruct((n_rows, PACK_W), U32),
        compiler_params=_cparams(("arbitrary",)),
        name="dispatch",
    )(off, run_len, base, tot, gapoff, gapm, h2, posk)


X_RING = 3
Y_RING = 2


def _expert_kernel(bexp_ref, nvalid_ref, xs_hbm, wg_ref, wu_ref, wd_ref, ys_hbm, x_buf, y_buf, wgu_bf, wd_bf,
                   x_sem, y_sem):
    i = pl.program_id(0)
    nvalid = nvalid_ref[0]

    def rows(b):
        return pl.ds(pl.multiple_of(b * BLOCK_ROWS, BLOCK_ROWS), BLOCK_ROWS)

    def load(b):
        slot = b % X_RING
        return pltpu.make_async_copy(xs_hbm.at[rows(b)], x_buf.at[slot], x_sem.at[slot])

    def store(b):
        slot = b % Y_RING
        return pltpu.make_async_copy(y_buf.at[slot], ys_hbm.at[rows(b)], y_sem.at[slot])

    @pl.when(i == 0)
    def _():
        for b in range(X_RING - 1):
            @pl.when(b < nvalid)
            def _(b=b):
                load(b).start()

    @pl.when(i < nvalid)
    def _():
        @pl.when(i + X_RING - 1 < nvalid)
        def _():
            load(i + X_RING - 1).start()

        @pl.when((i == 0) | (bexp_ref[i] != bexp_ref[jnp.maximum(i - 1, 0)]))
        def _():
            wgu_bf[:, :EXPERT_FF] = wg_ref[...].astype(BF16)
            wgu_bf[:, EXPERT_FF:] = wu_ref[...].astype(BF16)
            wd_bf[...] = wd_ref[...].astype(BF16)

        load(i).wait()
        x_lo, x_hi = _unpack_halves(x_buf[i % X_RING])
        gu = _dot(x_lo, wgu_bf[:PACK_W, :]) + _dot(x_hi, wgu_bf[PACK_W:, :])
        g = gu[:, :EXPERT_FF]
        a = (g * _sigmoid(g) * gu[:, EXPERT_FF:]).astype(BF16)
        y = _dot(a, wd_bf[...]).astype(BF16).astype(F32)

        @pl.when(i >= Y_RING)
        def _():
            store(i - Y_RING).wait()
        y_buf[i % Y_RING] = _pack_halves(y)
        store(i).start()

        @pl.when(i == nvalid - 1)
        def _():
            for back in range(Y_RING):
                @pl.when(i - back >= 0)
                def _(back=back):
                    store(i - back).wait()


def _experts(bexp, nvalid, xs, wg, wu, wd, layer):
    n_rows, D = xs.shape[0], D_MODEL
    nb = n_rows // BLOCK_ROWS

    def blk(i, be, nv):
        return jnp.minimum(i, nv[0] - 1)

    wspec = lambda r, c: pl.BlockSpec((None, None, r, c),
                                      lambda i, be, nv: (layer, be[blk(i, be, nv)], 0, 0))
    grid_spec = pltpu.PrefetchScalarGridSpec(
        num_scalar_prefetch=2,
        grid=(nb,),
        in_specs=[pl.BlockSpec(memory_space=pl.ANY),
                  wspec(D, EXPERT_FF), wspec(D, EXPERT_FF), wspec(EXPERT_FF, D)],
        out_specs=pl.BlockSpec(memory_space=pl.ANY),
        scratch_shapes=[pltpu.VMEM((X_RING, BLOCK_ROWS, PACK_W), U32), pltpu.VMEM((Y_RING, BLOCK_ROWS, PACK_W), U32),
                        pltpu.VMEM((D, 2 * EXPERT_FF), BF16), pltpu.VMEM((EXPERT_FF, D), BF16),
                        pltpu.SemaphoreType.DMA((X_RING,)), pltpu.SemaphoreType.DMA((Y_RING,))],
    )
    return pl.pallas_call(
        _expert_kernel,
        grid_spec=grid_spec,
        out_shape=jax.ShapeDtypeStruct((n_rows, PACK_W), U32),
        compiler_params=_cparams(("arbitrary",)),
        name="experts",
    )(bexp, nvalid, xs, wg, wu, wd)


def _combine_kernel(off_ref, m_ref, base_ref, tot_ref, ys_hbm, posk_ref, wk_ref, h_ref, x_ref, g2_ref,
                    wsgu_ref, wsd_ref, o_ref, ys_ref, pw_ref, sem):
    i = pl.program_id(0)
    tm = h_ref.shape[0]
    slot = i % 2

    def fetch(tile, buf_slot):
        def run(e, carry):
            idx = tile * N_EXPERTS + e
            _start_rows(m_ref[idx], ys_hbm, off_ref[idx], ys_ref.at[buf_slot], base_ref[idx],
                        sem.at[buf_slot])
            return carry
        lax.fori_loop(0, N_EXPERTS, run, 0)

    @pl.when(i == 0)
    def _():
        ys_ref[...] = jnp.zeros_like(ys_ref)
        fetch(i, slot)

    @pl.when(i + 1 < pl.num_programs(0))
    def _():
        fetch(i + 1, 1 - slot)

    h = h_ref[...]
    gu = _dot(h, wsgu_ref[...])
    g = gu[:, :EXPERT_FF]
    shared = _dot((g * _sigmoid(g) * gu[:, EXPERT_FF:]).astype(BF16), wsd_ref[...])

    posk = posk_ref[0]
    wk = wk_ref[0]
    rows = lax.broadcasted_iota(jnp.int32, (MXU_DIM, tm), 0).astype(F32).astype(BF16)

    def weight_chunks(lo, hi):
        for c in range(lo, hi):
            rel = posk - float(c * MXU_DIM)
            pw = jnp.zeros((MXU_DIM, tm), BF16)
            for k in range(TOP_K):
                wrow = jnp.broadcast_to(wk[k:k + 1, :], (MXU_DIM, tm)).astype(BF16)
                pw = jnp.where(rows == rel[k:k + 1, :].astype(BF16), wrow, pw)
            pw_ref[c * MXU_DIM:(c + 1) * MXU_DIM, :] = pw

    def unsort(lo, hi):
        pw = pw_ref[lo:hi, :]
        halves = [lax.dot_general(pw, y, (((0,), (0,)), ((), ())), preferred_element_type=F32)
                  for y in _unpack_halves(ys_ref[slot, lo:hi, :])]
        return jnp.concatenate(halves, axis=1)

    weight_chunks(0, K_SORT_COMMON // MXU_DIM)
    _wait_rows(tot_ref[i], ys_hbm, ys_ref.at[slot], sem.at[slot])
    o_ref[...] = x_ref[...] + g2_ref[0] * (unsort(0, K_SORT_COMMON) + shared)

    @pl.when(tot_ref[i] * RUN_ALIGN > K_SORT_COMMON)
    def _():
        weight_chunks(K_SORT_COMMON // MXU_DIM, K_SORT // MXU_DIM)
        o_ref[...] = o_ref[...] + g2_ref[0] * unsort(K_SORT_COMMON, K_SORT)


def _combine(off, run_len, base, tot, ys, posk, wk, h2, x1, g2, wsgu, wsd, tiles_per_batch):
    T, D = h2.shape
    tm = TM_MOE
    const = lambda i, *_: (0, 0)
    tok = lambda w: pl.BlockSpec((tm, w), lambda i, *_: (i, 0))
    sel = pl.BlockSpec((1, TOP_K, tm), lambda i, *_: (i, 0, 0))
    grid_spec = pltpu.PrefetchScalarGridSpec(
        num_scalar_prefetch=4,
        grid=(T // tm,),
        in_specs=[pl.BlockSpec(memory_space=pl.ANY), sel, sel, tok(D), tok(D),
                  pl.BlockSpec((1, 1, D), lambda i, *_: (i // tiles_per_batch, 0, 0)),
                  pl.BlockSpec(wsgu.shape, const), pl.BlockSpec(wsd.shape, const)],
        out_specs=tok(D),
        scratch_shapes=[pltpu.VMEM((2, K_SORT, PACK_W), U32),
                        pltpu.VMEM((K_SORT, tm), BF16),
                        pltpu.SemaphoreType.DMA((2,))],
    )
    return pl.pallas_call(
        _combine_kernel,
        grid_spec=grid_spec,
        out_shape=jax.ShapeDtypeStruct((T, D), F32),
        compiler_params=_cparams(("arbitrary",)),
        name="combine",
    )(off, run_len, base, tot, ys, posk, wk, h2, x1, g2, wsgu, wsd)


def _moe(h2, x1, g2, posk, wk, cnt, wg, wu, wd, layer, wsgu, wsd, tiles_per_batch):
    T, D = h2.shape
    nt = T // TM_MOE

    pad = cnt[:, :, 0].astype(jnp.int32)
    base = jnp.cumsum(pad, axis=1) - pad
    tile_off = jnp.cumsum(pad, axis=0) - pad
    total = jnp.sum(pad, axis=0)
    region = ((total + BLOCK_ROWS - 1) // BLOCK_ROWS) * BLOCK_ROWS
    rend = jnp.cumsum(region)
    rstart = rend - region
    off = rstart[None, :] + tile_off
    nb_max = -(-(TOP_K * T + nt * N_EXPERTS * (RUN_ALIGN - 1)) // BLOCK_ROWS) + N_EXPERTS
    blk_row = jnp.arange(nb_max, dtype=jnp.int32) * BLOCK_ROWS
    bexp = jnp.minimum(jnp.sum((rend[None, :] <= blk_row[:, None]).astype(jnp.int32), axis=1), N_EXPERTS - 1)
    nvalid = (rend[-1:] // BLOCK_ROWS).astype(jnp.int32)
    flat = lambda a: a.reshape(-1).astype(jnp.int32)

    run_len = flat(pad // RUN_ALIGN)
    tot = flat(jnp.sum(pad, axis=1) // RUN_ALIGN)
    xs = _dispatch(flat(off), run_len, flat(base), tot, flat(rstart + total),
                   flat((region - total) // RUN_ALIGN), h2, posk, nb_max * BLOCK_ROWS)
    ys = _experts(bexp, nvalid, xs, wg, wu, wd, layer)
    return _combine(flat(off), run_len, flat(base), tot, ys, posk, wk, h2, x1, g2, wsgu, wsd, tiles_per_batch)


def kernel(x, c, ada_w, ada_b, mix_norm_g, w_in, b_fgate, qn_a, kn_a, qn_b, kn_b, rel_bias, w_proj_a,
           w_proj_b, w_out, ffn_norm_g, w_router, router_bias, w_gate_e, w_up_e, w_down_e, w_gate_s,
           w_up_s, w_down_s):
    B, S, D = x.shape
    L = ada_w.shape[0]
    T = B * S
    assert D == D_MODEL and S % TM_PRE == 0 and TQ_FOX == TM_PRE and S % TM_MOE == 0 and S % (BAND_TILES * TQ_BAND) == 0

    mod = _adaln(c, ada_w, ada_b).reshape(L, B, 6, 1, D)

    hid = np.arange(MXU_DIM) // HEAD_DIM
    bd = jnp.asarray(np.where(hid[:, None] == hid[None, :], 1.0 / HEAD_DIM, 0.0), BF16)
    r = np.arange(TM_PRE)
    tri = jnp.asarray(r[:, None] <= r[None, :], BF16)
    r = np.arange(TM_MOE)
    ustrict = jnp.asarray(r[:, None] < r[None, :], BF16)
    r = np.arange(N_EXPERTS)
    lstrict = jnp.asarray(r[None, :] < r[:, None], BF16)

    q_scale = 1.0 / math.sqrt(HEAD_DIM)
    for l in range(L):
        sh1, sc1, g1, sh2, sc2, g2 = (mod[l, :, j] for j in range(6))
        w = w_in[l]
        cols = lambda c: w[:, c * WIDTH:(c + 1) * WIDTH]
        wqk = jnp.concatenate([cols(0), cols(1), cols(3), cols(4)], axis=1).astype(BF16)
        wvt = jnp.concatenate([cols(2), cols(5)], axis=1).T.astype(BF16)
        wf = jnp.pad(w[:, 6 * WIDTH:6 * WIDTH + N_HEADS].T.astype(BF16), ((0, N_HEADS), (0, 0)))
        wg = w[:, 6 * WIDTH + N_HEADS:].astype(BF16)
        gains = jnp.stack([qn_a[l] * (q_scale * LOG2E), kn_a[l], qn_b[l] * (q_scale * LOG2E), kn_b[l]])
        hn = jnp.pad(jnp.tile(gains, (1, N_HEADS)), ((0, 4), (0, 0)))

        qk, vt, dec, cbase, ga, gb = _premix(x, sh1, sc1, mix_norm_g[l][None], wqk, wvt, wf, wg,
                                             b_fgate[l][:, None], hn, bd, tri)
        ya = _band_attention(qk, vt, _band_bias(rel_bias[l]))
        yb = _fox_attention(qk, vt, dec, cbase)
        x1, h2, posk, wk, cnt = _postmix(
            ya, yb, ga, gb, x, g1, w_proj_a[l].astype(BF16), w_proj_b[l].astype(BF16),
            w_out[l].astype(BF16), ffn_norm_g[l][None], sh2, sc2, w_router[l].T.astype(BF16),
            router_bias[l][:, None], ustrict, lstrict)

        wsgu = jnp.concatenate([w_gate_s[l], w_up_s[l]], axis=-1).astype(BF16)
        x = _moe(h2.reshape(T, D), x1.reshape(T, D), g2, posk, wk, cnt, w_gate_e, w_up_e, w_down_e, l,
                 wsgu, w_down_s[l].astype(BF16), S // TM_MOE).reshape(B, S, D)
    return x
```

```python
import math

import jax
import jax.numpy as jnp
import numpy as np
from jax import lax
from jax.experimental import pallas as pl
from jax.experimental.pallas import tpu as pltpu

F32 = jnp.float32
BF16 = jnp.bfloat16

D_MODEL = 1024
HEAD_DIM = 64
N_HEADS = 8
WIDTH = N_HEADS * HEAD_DIM
CHUNK = 64
LOOKBACK_CHUNKS = 8
REL_CLIP = 128
N_EXPERTS = 64
TOP_K = 8
N_GROUPS = 8
TOPK_GROUPS = 4
EXPERT_FF = 256
ROUTED_SCALE = 2.5
EPS = 1e-6
NEG_INF = -1e30
LOG2E = math.log2(math.e)

LANES = 128
MXU_DIM = 256
VMEM_LIMIT = 56 * 1024 * 1024

TM_PRE = 512
TQ_BAND = 256
BAND_WIN = TQ_BAND + LOOKBACK_CHUNKS * CHUNK
BAND_TILES = LOOKBACK_CHUNKS * CHUNK // TQ_BAND
TQ_FOX = 512
PAIRS_PER_STEP = 4
PV_ROWS = HEAD_DIM + 16
TM_MOE = 256
RUN_ALIGN = 8
PACK_W = D_MODEL // 2
U32 = jnp.uint32
K_SORT = ((TOP_K * TM_MOE + N_EXPERTS * (RUN_ALIGN - 1) + MXU_DIM - 1) // MXU_DIM) * MXU_DIM
K_SORT_COMMON = 2304
BLOCK_ROWS = 1024
HALF_ROWS = BLOCK_ROWS // 2


def _dot(a, b):
    return jnp.dot(a, b, preferred_element_type=F32)


def _dot_nt(a, b):
    return lax.dot_general(a, b, (((1,), (1,)), ((), ())), preferred_element_type=F32)


def _sigmoid(v):
    return 1.0 / (1.0 + jnp.exp(-v))


def _cparams(sem):
    return pltpu.CompilerParams(dimension_semantics=sem, vmem_limit_bytes=VMEM_LIMIT)


def _adaln_kernel(c_ref, w_ref, b_ref, o_ref):
    c = c_ref[...]
    ca = c * _sigmoid(c)
    o_ref[0] = jnp.dot(ca, w_ref[0], preferred_element_type=F32,
                       precision=lax.Precision.HIGHEST) + b_ref[0]


def _adaln(c, ada_w, ada_b):
    L, D, N = ada_w.shape
    B = c.shape[0]
    tn = N // 4
    return pl.pallas_call(
        _adaln_kernel,
        grid=(L, N // tn),
        in_specs=[pl.BlockSpec((B, D), lambda l, j: (0, 0)),
                  pl.BlockSpec((1, D, tn), lambda l, j: (l, 0, j)),
                  pl.BlockSpec((1, 1, tn), lambda l, j: (l, 0, j))],
        out_specs=pl.BlockSpec((1, B, tn), lambda l, j: (l, 0, j)),
        out_shape=jax.ShapeDtypeStruct((L, B, N), F32),
        compiler_params=_cparams(("arbitrary", "arbitrary")),
        name="adaln",
    )(c, ada_w, ada_b.reshape(L, 1, N))


def _modnorm(x, g, sc, sh):
    ms = jnp.mean(x * x, axis=-1, keepdims=True)
    y = x * lax.rsqrt(ms + EPS) * g
    return y * (1.0 + sc) + sh


def _premix_kernel(x_ref, sh_ref, sc_ref, g_ref, wqk_ref, wvt_ref, wf_ref, wg_ref, bf_ref, hn_ref, bd_ref,
                   tri_ref, qk_ref, vt_ref, dec_ref, cbase_ref, ga_ref, gb_ref, carry_ref):
    si = pl.program_id(1)
    tm = x_ref.shape[1]
    h = _modnorm(x_ref[0], g_ref[...], sc_ref[0], sh_ref[0]).astype(BF16)

    for c in range(4):
        z = _dot(h, wqk_ref[:, c * WIDTH:(c + 1) * WIDTH])
        sq = (z * z).astype(BF16)
        ms = jnp.concatenate(
            [_dot(sq[:, j * MXU_DIM:(j + 1) * MXU_DIM], bd_ref[...]) for j in range(WIDTH // MXU_DIM)],
            axis=1)
        z = z * lax.rsqrt(ms + EPS) * hn_ref[c:c + 1, :]
        qk_ref[c, 0] = z.astype(BF16)

    for c in range(2):
        vt = _dot_nt(wvt_ref[c * WIDTH:(c + 1) * WIDTH, :], h).astype(BF16)
        vt_ref[0, c * (N_HEADS // 2):(c + 1) * (N_HEADS // 2)] = vt.reshape(N_HEADS // 2, LANES, tm)

    fr = _dot_nt(wf_ref[...], h)[:N_HEADS]
    xg = fr + bf_ref[...]
    logf = jnp.minimum(xg, 0.0) - jnp.log(1.0 + jnp.exp(-jnp.abs(xg)))
    hi = logf.astype(BF16).astype(F32)
    r1 = logf - hi
    mid = r1.astype(BF16).astype(F32)
    lo = r1 - mid
    parts = jnp.concatenate([hi, mid, lo, jnp.zeros_like(hi)], axis=0)
    cs3 = _dot(parts.astype(BF16), tri_ref[...])
    cs = cs3[0:8] + cs3[8:16] + cs3[16:24]

    @pl.when(si == 0)
    def _():
        carry_ref[...] = jnp.zeros_like(carry_ref)

    drel = cs * (-LOG2E)
    dhi = drel.astype(BF16).astype(F32)
    dlo = drel - dhi
    dec = jnp.concatenate([dhi, dlo, jnp.zeros((LANES - 2 * N_HEADS, tm), F32)], axis=0)
    dec_ref[0] = dec.T.astype(BF16)
    carry = carry_ref[:, 0:1]
    cbase_ref[0, 0] = jnp.broadcast_to(carry * (-LOG2E), (N_HEADS, tm))
    carry_ref[...] = jnp.broadcast_to(carry + cs[:, tm - 1:tm], carry_ref.shape)

    for c in range(4):
        zg = _dot(h, wg_ref[:, c * WIDTH:(c + 1) * WIDTH])
        sg = _sigmoid(zg).astype(BF16)
        if c < 2:
            ga_ref[0, :, c * WIDTH:(c + 1) * WIDTH] = sg
        else:
            gb_ref[0, :, (c - 2) * WIDTH:(c - 1) * WIDTH] = sg


def _premix(x, sh, sc, g, wqk, wvt, wf, wg, bf, hn, bd, tri):
    B, S, D = x.shape
    tm = TM_PRE
    const = lambda b, s: (0, 0)
    return pl.pallas_call(
        _premix_kernel,
        grid=(B, S // tm),
        in_specs=[pl.BlockSpec((1, tm, D), lambda b, s: (b, s, 0)),
                  pl.BlockSpec((1, 1, D), lambda b, s: (b, 0, 0)),
                  pl.BlockSpec((1, 1, D), lambda b, s: (b, 0, 0)),
                  pl.BlockSpec((1, D), const),
                  pl.BlockSpec(wqk.shape, const),
                  pl.BlockSpec(wvt.shape, const),
                  pl.BlockSpec(wf.shape, const),
                  pl.BlockSpec(wg.shape, const),
                  pl.BlockSpec(bf.shape, const),
                  pl.BlockSpec(hn.shape, const),
                  pl.BlockSpec(bd.shape, const),
                  pl.BlockSpec(tri.shape, const)],
        out_specs=[pl.BlockSpec((4, 1, tm, WIDTH), lambda b, s: (0, b, s, 0)),
                   pl.BlockSpec((1, N_HEADS, LANES, tm), lambda b, s: (b, 0, 0, s)),
                   pl.BlockSpec((1, tm, LANES), lambda b, s: (b, s, 0)),
                   pl.BlockSpec((1, 1, N_HEADS, tm), lambda b, s: (b, s, 0, 0)),
                   pl.BlockSpec((1, tm, D), lambda b, s: (b, s, 0)),
                   pl.BlockSpec((1, tm, D), lambda b, s: (b, s, 0))],
        out_shape=[jax.ShapeDtypeStruct((4, B, S, WIDTH), BF16),
                   jax.ShapeDtypeStruct((B, N_HEADS, LANES, S), BF16),
                   jax.ShapeDtypeStruct((B, S, LANES), BF16),
                   jax.ShapeDtypeStruct((B, S // tm, N_HEADS, tm), F32),
                   jax.ShapeDtypeStruct((B, S, D), BF16),
                   jax.ShapeDtypeStruct((B, S, D), BF16)],
        scratch_shapes=[pltpu.VMEM((N_HEADS, LANES), F32)],
        compiler_params=_cparams(("arbitrary", "arbitrary")),
        name="premix",
    )(x, sh, sc, g, wqk, wvt, wf, wg, bf, hn, bd, tri)


def _values_with_ones(vt, hh):
    ones = jnp.ones((PV_ROWS - HEAD_DIM, vt.shape[1]), vt.dtype)
    return jnp.concatenate([vt[hh * HEAD_DIM:(hh + 1) * HEAD_DIM, :], ones], axis=0)


def _pair_finish_t(acc0, acc1):
    out_t = jnp.concatenate([acc[:HEAD_DIM] / acc[HEAD_DIM:HEAD_DIM + 1, :] for acc in (acc0, acc1)], axis=0)
    return out_t.T


def _band_kernel(q_ref, kprev_ref, kcur_ref, vprev_ref, vcur_ref, bias_ref, o_ref):
    tq = TQ_BAND
    step_start = pl.program_id(2) * (BAND_TILES * tq)
    lane = lax.broadcasted_iota(jnp.int32, (1, LANES), 1)
    krow = lax.broadcasted_iota(jnp.int32, (BAND_WIN, 1), 0)
    pieces = [(ref, j) for ref in (kprev_ref, kcur_ref) for j in range(BAND_TILES)]
    vpieces = [(ref, j) for ref in (vprev_ref, vcur_ref) for j in range(BAND_TILES)]
    units = [(u, pp, hh) for u in range(BAND_TILES) for pp in range(PAIRS_PER_STEP) for hh in range(2)]
    scores = []
    for u, pp, hh in units:
        q = q_ref[u * tq:(u + 1) * tq, pp * LANES:(pp + 1) * LANES]
        qm = jnp.where((lane // HEAD_DIM) == hh, q, jnp.zeros_like(q))
        scores.append(jnp.concatenate(
            [_dot_nt(ref[j * tq:(j + 1) * tq, pp * LANES:(pp + 1) * LANES], qm)
             for ref, j in pieces[u:u + 3]], axis=0))
    accs = []
    for n, (u, pp, hh) in enumerate(units):
        kvalid = (krow + (step_start + u * tq - LOOKBACK_CHUNKS * CHUNK)) >= 0
        s = jnp.where(kvalid, scores[n] + bias_ref[2 * pp + hh], NEG_INF)
        m = jnp.max(s, axis=0, keepdims=True)
        p = jnp.exp2(s - m).astype(BF16)
        acc = None
        for c, (ref, j) in enumerate(vpieces[u:u + 3]):
            t = _dot(_values_with_ones(ref[pp, :, j * tq:(j + 1) * tq], hh), p[c * tq:(c + 1) * tq, :])
            acc = t if acc is None else acc + t
        accs.append(acc)
    for n in range(0, len(units), 2):
        u, pp, _ = units[n]
        o_ref[u * tq:(u + 1) * tq, pp * LANES:(pp + 1) * LANES] = _pair_finish_t(
            accs[n], accs[n + 1]).astype(o_ref.dtype)


def _band_attention(qk, vt, bias_t):
    _, B, S, _ = qk.shape
    rows = BAND_TILES * TQ_BAND
    groups = N_HEADS // 2 // PAIRS_PER_STEP
    gw = PAIRS_PER_STEP * LANES

    def k_spec(back):
        return pl.BlockSpec((None, None, rows, gw),
                            lambda b, g, qi: (1, b, jnp.maximum(qi - back, 0), g))

    def v_spec(back):
        return pl.BlockSpec((None, PAIRS_PER_STEP, LANES, rows),
                            lambda b, g, qi: (b, g, 0, jnp.maximum(qi - back, 0)))

    return pl.pallas_call(
        _band_kernel,
        grid=(B, groups, S // rows),
        in_specs=[pl.BlockSpec((None, None, rows, gw), lambda b, g, qi: (0, b, qi, g)),
                  k_spec(1), k_spec(0), v_spec(1), v_spec(0),
                  pl.BlockSpec((2 * PAIRS_PER_STEP, BAND_WIN, TQ_BAND), lambda b, g, qi: (g, 0, 0))],
        out_specs=pl.BlockSpec((None, rows, gw), lambda b, g, qi: (b, qi, g)),
        out_shape=jax.ShapeDtypeStruct((B, S, WIDTH), BF16),
        compiler_params=_cparams(("arbitrary", "arbitrary", "arbitrary")),
        name="band_attn",
    )(qk, qk, qk, vt, vt, bias_t)


def _band_bias(rel_table):
    pad = LOOKBACK_CHUNKS * CHUNK
    i = np.arange(TQ_BAND)[:, None]
    j = np.arange(BAND_WIN)[None, :]
    period = 1024
    assert BAND_WIN + TQ_BAND <= period
    m = np.arange(period)
    rel = np.where(m <= BAND_WIN, pad - m, pad + period - m)
    g = rel_table[:, np.clip(rel, -REL_CLIP, REL_CLIP) + REL_CLIP].astype(F32)
    bias = jnp.tile(g, (1, TQ_BAND))[:, :TQ_BAND * (period - 1)].reshape(-1, TQ_BAND, period - 1)[:, :, :BAND_WIN]
    c0 = (i // CHUNK) * CHUNK
    inband = (j >= c0) & (j < c0 + pad + CHUNK)
    return jnp.where(inband[None], bias * LOG2E, NEG_INF).transpose(0, 2, 1)


FOX_DIAG, FOX_TWO_PAST, FOX_PAST_DIAG = 0, 1, 2


def _fox_kernel(qi_tab, ka_tab, kb_tab, kind_tab, first_tab, q_ref, ka_ref, kb_ref, vta_ref, vtb_ref,
                deca_ref, decb_ref, cba_ref, cbb_ref, o_ref, acc_ref, m_ref):
    grp = pl.program_id(1)
    t = pl.program_id(2)
    kind = kind_tab[t]
    tq, tk = q_ref.shape[0], ka_ref.shape[0]
    lane2 = lax.broadcasted_iota(jnp.int32, (1, 2 * LANES), 1)
    heads = [(pp, hh) for pp in range(PAIRS_PER_STEP) for hh in range(2)]
    tile_a = (ka_ref, vta_ref, deca_ref, cba_ref)
    tile_b = (kb_ref, vtb_ref, decb_ref, cbb_ref)

    @pl.when(first_tab[t] == 1)
    def _():
        acc_ref[...] = jnp.zeros_like(acc_ref)
        m_ref[...] = jnp.full_like(m_ref, NEG_INF)

    def step(tiles):
        ones = jnp.where(lax.broadcasted_iota(jnp.int32, (tq, LANES), 1) < 2 * N_HEADS, 1.0, 0.0).astype(BF16)
        causal = (lax.broadcasted_iota(jnp.int32, (tk, tq), 0)
                  <= lax.broadcasted_iota(jnp.int32, (tk, tq), 1))
        scores = []
        for pp, hh in heads:
            head = 2 * (PAIRS_PER_STEP * grp + pp) + hh
            q_aug = jnp.concatenate([q_ref[:, pp * LANES:(pp + 1) * LANES], ones], axis=1)
            use = (((lane2 // HEAD_DIM) == hh) | (lane2 == LANES + head) | (lane2 == LANES + N_HEADS + head))
            qm = jnp.where(use, q_aug, jnp.zeros_like(q_aug))
            per_tile = []
            for (k_ref, _, dec_ref, _), diagonal in tiles:
                k_aug = jnp.concatenate([k_ref[:, pp * LANES:(pp + 1) * LANES], dec_ref[...]], axis=1)
                s = _dot_nt(k_aug, qm)
                per_tile.append(jnp.where(causal, s, NEG_INF) if diagonal else s)
            scores.append(per_tile)
        for n, (pp, hh) in enumerate(heads):
            head = 2 * (PAIRS_PER_STEP * grp + pp) + hh
            bases = [cb_ref[pl.ds(head, 1), :] for (_, _, _, cb_ref), _ in tiles]
            m_old = m_ref[n]
            m_new = m_old
            for s, base in zip(scores[n], bases):
                m_new = jnp.maximum(m_new, jnp.max(s, axis=0, keepdims=True) + base)
            acc = acc_ref[n] * jnp.exp2(m_old - m_new)
            for s, base, ((_, vt_ref, _, _), _) in zip(scores[n], bases, tiles):
                p = jnp.exp2(s - (m_new - base)).astype(BF16)
                acc = acc + _dot(_values_with_ones(vt_ref[pp], hh), p)
            acc_ref[n] = acc
            m_ref[n] = m_new

    def finish():
        for pp in range(PAIRS_PER_STEP):
            o_ref[:, pp * LANES:(pp + 1) * LANES] = _pair_finish_t(
                acc_ref[2 * pp], acc_ref[2 * pp + 1]).astype(o_ref.dtype)

    @pl.when(kind == FOX_TWO_PAST)
    def _():
        step([(tile_a, False), (tile_b, False)])

    @pl.when(kind == FOX_PAST_DIAG)
    def _():
        step([(tile_a, False), (tile_b, True)])
        finish()

    @pl.when(kind == FOX_DIAG)
    def _():
        step([(tile_a, True)])
        finish()


def _fox_attention(qkv, vt, dec, cbase):
    _, B, S, _ = qkv.shape
    tq = TQ_FOX
    nq = S // tq
    groups = N_HEADS // 2 // PAIRS_PER_STEP
    gw = PAIRS_PER_STEP * LANES
    steps = []
    for i in range(nq):
        past = list(range(i))
        first = 1
        while len(past) >= 2:
            steps.append((i, past[0], past[1], FOX_TWO_PAST, first))
            past, first = past[2:], 0
        if past:
            steps.append((i, past[0], i, FOX_PAST_DIAG, first))
        else:
            steps.append((i, i, i, FOX_DIAG, first))
    tabs = [jnp.array([s[c] for s in steps], jnp.int32) for c in range(5)]

    def key_specs(col):
        tile = lambda *a: a[3 + col][a[2]]
        return [pl.BlockSpec((None, None, tq, gw), lambda b, g, t, *tb: (3, b, tile(b, g, t, *tb), g)),
                pl.BlockSpec((None, PAIRS_PER_STEP, LANES, tq),
                             lambda b, g, t, *tb: (b, groups + g, 0, tile(b, g, t, *tb))),
                pl.BlockSpec((None, tq, LANES), lambda b, g, t, *tb: (b, tile(b, g, t, *tb), 0)),
                pl.BlockSpec((None, None, N_HEADS, tq), lambda b, g, t, *tb: (b, tile(b, g, t, *tb), 0, 0))]

    ka, va, da, ca = key_specs(1)
    kb, vb, db, cb = key_specs(2)
    grid_spec = pltpu.PrefetchScalarGridSpec(
        num_scalar_prefetch=5,
        grid=(B, groups, len(steps)),
        in_specs=[pl.BlockSpec((None, None, tq, gw), lambda b, g, t, *tb: (2, b, tb[0][t], g)),
                  ka, kb, va, vb, da, db, ca, cb],
        out_specs=pl.BlockSpec((None, tq, gw), lambda b, g, t, *tb: (b, tb[0][t], g)),
        scratch_shapes=[pltpu.VMEM((2 * PAIRS_PER_STEP, PV_ROWS, tq), F32),
                        pltpu.VMEM((2 * PAIRS_PER_STEP, 1, tq), F32)],
    )
    return pl.pallas_call(
        _fox_kernel,
        grid_spec=grid_spec,
        out_shape=jax.ShapeDtypeStruct((B, S, WIDTH), BF16),
        compiler_params=_cparams(("arbitrary", "arbitrary", "arbitrary")),
        name="fox_attn",
    )(*tabs, qkv, qkv, qkv, vt, vt, dec, dec, cbase, cbase)


def _postmix_kernel(ya_ref, yb_ref, ga_ref, gb_ref, x_ref, g1_ref, wpa_ref, wpb_ref, wo_ref,
                    g_ref, sh_ref, sc_ref, wr_ref, rb_ref, ustrict_ref, lstrict_ref,
                    x1_ref, h2_ref, posk_ref, wk_ref, cnt_ref):
    ua = _dot(ya_ref[0], wpa_ref[...])
    ub = _dot(yb_ref[0], wpb_ref[...])
    m = ga_ref[0].astype(F32) * ua + gb_ref[0].astype(F32) * ub
    mo = _dot(m.astype(BF16), wo_ref[...])
    x1 = x_ref[0] + g1_ref[0] * mo
    x1_ref[0] = x1
    h2 = _modnorm(x1, g_ref[...], sc_ref[0], sh_ref[0]).astype(BF16)
    h2_ref[0] = h2
    for j in range(h2.shape[0] // TM_MOE):
        posk, wk, cnt = _route(h2[j * TM_MOE:(j + 1) * TM_MOE], wr_ref[...], rb_ref[...],
                               ustrict_ref[...], lstrict_ref[...])
        posk_ref[j] = posk
        wk_ref[j] = wk
        cnt_ref[j] = cnt


def _postmix(ya, yb, ga, gb, x, g1, wpa, wpb, wo, g, sh, sc, wr_t, rb, ustrict, lstrict):
    B, S, D = x.shape
    tm = TM_PRE
    sub_tiles = tm // TM_MOE
    nt = B * S // TM_MOE
    const = lambda b, s: (0, 0)
    tok = lambda w: pl.BlockSpec((1, tm, w), lambda b, s: (b, s, 0))
    row = pl.BlockSpec((1, 1, D), lambda b, s: (b, 0, 0))
    tiles = lambda r, c: pl.BlockSpec((sub_tiles, r, c), lambda b, s: (b * (S // tm) + s, 0, 0))
    return pl.pallas_call(
        _postmix_kernel,
        grid=(B, S // tm),
        in_specs=[tok(WIDTH), tok(WIDTH), tok(D), tok(D), tok(D), row,
                  pl.BlockSpec(wpa.shape, const), pl.BlockSpec(wpb.shape, const),
                  pl.BlockSpec(wo.shape, const), pl.BlockSpec((1, D), const), row, row,
                  pl.BlockSpec(wr_t.shape, const), pl.BlockSpec(rb.shape, const),
                  pl.BlockSpec(ustrict.shape, const), pl.BlockSpec(lstrict.shape, const)],
        out_specs=[tok(D), tok(D), tiles(TOP_K, TM_MOE), tiles(TOP_K, TM_MOE), tiles(N_EXPERTS, LANES)],
        out_shape=[jax.ShapeDtypeStruct((B, S, D), F32), jax.ShapeDtypeStruct((B, S, D), BF16),
                   jax.ShapeDtypeStruct((nt, TOP_K, TM_MOE), F32),
                   jax.ShapeDtypeStruct((nt, TOP_K, TM_MOE), F32),
                   jax.ShapeDtypeStruct((nt, N_EXPERTS, LANES), F32)],
        compiler_params=_cparams(("arbitrary", "arbitrary")),
        name="postmix",
    )(ya, yb, ga, gb, x, g1, wpa, wpb, wo, g, sh, sc, wr_t, rb, ustrict, lstrict)


def _route(h, wr, rb, ustrict, lstrict):
    tm = h.shape[0]
    per_group = N_EXPERTS // N_GROUPS
    logits = _dot_nt(wr, h)
    scores = _sigmoid(logits)
    biased = scores + rb
    sub8 = lax.broadcasted_iota(jnp.int32, (per_group, tm), 0).astype(F32)
    neg = -jnp.inf

    grp_rows = []
    for g in range(N_GROUPS):
        a = biased[g * per_group:(g + 1) * per_group]
        m1 = jnp.max(a, axis=0, keepdims=True)
        i1 = jnp.min(jnp.where(a == m1, sub8, float(per_group)), axis=0, keepdims=True)
        m2 = jnp.max(jnp.where(sub8 == i1, neg, a), axis=0, keepdims=True)
        grp_rows.append(m1 + m2)
    grp = jnp.concatenate(grp_rows, axis=0)

    gsub = lax.broadcasted_iota(jnp.int32, (N_GROUPS, tm), 0).astype(F32)
    gmask = jnp.zeros((N_GROUPS, tm), F32)
    for _ in range(TOPK_GROUPS):
        mx = jnp.max(grp, axis=0, keepdims=True)
        gi = jnp.min(jnp.where(grp == mx, gsub, float(N_GROUPS)), axis=0, keepdims=True)
        sel = gsub == gi
        gmask = jnp.where(sel, 1.0, gmask)
        grp = jnp.where(sel, neg, grp)
    masked = jnp.concatenate(
        [jnp.where(gmask[g:g + 1] > 0.5, biased[g * per_group:(g + 1) * per_group], neg)
         for g in range(N_GROUPS)], axis=0)

    esub = lax.broadcasted_iota(jnp.int32, (N_EXPERTS, tm), 0).astype(F32)
    sels, ws = [], []
    for _ in range(TOP_K):
        mx = jnp.max(masked, axis=0, keepdims=True)
        ei = jnp.min(jnp.where(masked == mx, esub, float(N_EXPERTS)), axis=0, keepdims=True)
        sel = esub == ei
        sels.append(sel)
        ws.append(jnp.sum(jnp.where(sel, scores, 0.0), axis=0, keepdims=True))
        masked = jnp.where(sel, neg, masked)
    wsum = ws[0]
    for w in ws[1:]:
        wsum = wsum + w

    selmask = jnp.zeros((N_EXPERTS, tm), F32)
    for sel in sels:
        selmask = selmask + jnp.where(sel, 1.0, 0.0)
    rank = _dot(selmask.astype(BF16), ustrict)
    counts = jnp.sum(selmask, axis=1, keepdims=True)
    padded = jnp.ceil(counts * (1.0 / RUN_ALIGN)) * RUN_ALIGN
    padded_b = jnp.broadcast_to(padded, (N_EXPERTS, LANES))
    base = _dot(lstrict, padded_b.astype(BF16))
    pos = base[:, 0:1] + rank

    posk = jnp.concatenate([jnp.sum(jnp.where(sel, pos, 0.0), axis=0, keepdims=True) for sel in sels], axis=0)
    wk = jnp.concatenate([(w / wsum) * ROUTED_SCALE for w in ws], axis=0)
    return posk, wk, padded_b


def _pack_halves(v):
    half = v.shape[1] // 2
    lo = lax.bitcast_convert_type(v[:, :half], U32)
    hi = lax.bitcast_convert_type(v[:, half:], U32)
    return (lo >> 16) | (hi & U32(0xFFFF0000))


def _unpack_halves(u):
    lo = lax.bitcast_convert_type(u << 16, F32).astype(BF16)
    hi = lax.bitcast_convert_type(u & U32(0xFFFF0000), F32).astype(BF16)
    return lo, hi


def _rows_copy(m, src_ref, src0, dst_ref, dst0, sem):
    rows = m * RUN_ALIGN
    return pltpu.make_async_copy(src_ref.at[pl.ds(pl.multiple_of(src0, RUN_ALIGN), rows)],
                                 dst_ref.at[pl.ds(pl.multiple_of(dst0, RUN_ALIGN), rows)], sem)


def _start_rows(m, src_ref, src0, dst_ref, dst0, sem):
    @pl.when(m > 0)
    def _():
        _rows_copy(m, src_ref, src0, dst_ref, dst0, sem).start()


def _wait_rows(m, src_ref, dst_ref, sem):
    @pl.when(m > 0)
    def _():
        _rows_copy(m, src_ref, 0, dst_ref, 0, sem).wait()


def _dispatch_kernel(off_ref, m_ref, base_ref, tot_ref, gapoff_ref, gapm_ref, h_ref, posk_ref, xs_hbm,
                     xs_ref, zero_ref, sem):
    i = pl.program_id(0)
    tm = h_ref.shape[0]
    n_chunks = K_SORT // MXU_DIM

    @pl.when(i == 0)
    def _():
        zero_ref[...] = jnp.zeros_like(zero_ref)

        def gap(e, carry):
            _start_rows(gapm_ref[e], zero_ref, 0, xs_hbm, gapoff_ref[e], sem)
            return carry
        lax.fori_loop(0, N_EXPERTS, gap, 0)

        def gap_wait(e, carry):
            _wait_rows(gapm_ref[e], zero_ref, xs_hbm, sem)
            return carry
        lax.fori_loop(0, N_EXPERTS, gap_wait, 0)

    slot = i % 2
    xs_slot = xs_ref.at[slot]
    h = h_ref[...]
    posk = posk_ref[0]
    rows = lax.broadcasted_iota(jnp.int32, (MXU_DIM, tm), 0).astype(F32).astype(BF16)
    one = jnp.ones((MXU_DIM, tm), BF16)
    def sort_chunks(lo, hi):
        for c in range(lo, hi):
            rel = posk - float(c * MXU_DIM)
            p = jnp.zeros((MXU_DIM, tm), BF16)
            for k in range(TOP_K):
                p = jnp.where(rows == rel[k:k + 1, :].astype(BF16), one, p)
            xs_slot[c * MXU_DIM:(c + 1) * MXU_DIM, :] = _pack_halves(_dot(p, h))

    sort_chunks(0, K_SORT_COMMON // MXU_DIM)

    @pl.when(tot_ref[i] * RUN_ALIGN > K_SORT_COMMON)
    def _():
        sort_chunks(K_SORT_COMMON // MXU_DIM, n_chunks)

    @pl.when(i > 0)
    def _():
        _wait_rows(tot_ref[i - 1], xs_ref.at[1 - slot], xs_hbm, sem)

    def run(e, carry):
        idx = i * N_EXPERTS + e
        _start_rows(m_ref[idx], xs_slot, base_ref[idx], xs_hbm, off_ref[idx], sem)
        return carry
    lax.fori_loop(0, N_EXPERTS, run, 0)

    @pl.when(i == pl.num_programs(0) - 1)
    def _():
        _wait_rows(tot_ref[i], xs_slot, xs_hbm, sem)


def _dispatch(off, run_len, base, tot, gapoff, gapm, h2, posk, n_rows):
    T, D = h2.shape
    tm = TM_MOE
    grid_spec = pltpu.PrefetchScalarGridSpec(
        num_scalar_prefetch=6,
        grid=(T // tm,),
        in_specs=[pl.BlockSpec((tm, D), lambda i, *_: (i, 0)),
                  pl.BlockSpec((1, TOP_K, tm), lambda i, *_: (i, 0, 0))],
        out_specs=pl.BlockSpec(memory_space=pl.ANY),
        scratch_shapes=[pltpu.VMEM((2, K_SORT, PACK_W), U32),
                        pltpu.VMEM((BLOCK_ROWS, PACK_W), U32),
                        pltpu.SemaphoreType.DMA],
    )
    return pl.pallas_call(
        _dispatch_kernel,
        grid_spec=grid_spec,
        out_shape=jax.ShapeDtypeStruct((n_rows, PACK_W), U32),
        compiler_params=_cparams(("arbitrary",)),
        name="dispatch",
    )(off, run_len, base, tot, gapoff, gapm, h2, posk)


X_RING = 3
Y_RING = 2


def _expert_kernel(bexp_ref, bstart_ref, bhalf_ref, nvalid_ref, xs_hbm, wg_ref, wu_ref, wd_ref, ys_hbm,
                   x_buf, y_buf, wgu_bf, wd_bf, x_sem, y_sem):
    i = pl.program_id(0)
    nvalid = nvalid_ref[0]

    def rows(b, n):
        return pl.ds(pl.multiple_of(bstart_ref[b], HALF_ROWS), n)

    def load(b):
        slot = b % X_RING
        return pltpu.make_async_copy(xs_hbm.at[rows(b, BLOCK_ROWS)], x_buf.at[slot], x_sem.at[slot])

    def store(b, start):
        slot = b % Y_RING
        for flag, n in ((1, HALF_ROWS), (0, BLOCK_ROWS)):
            @pl.when(bhalf_ref[b] == flag)
            def _(n=n):
                cp = pltpu.make_async_copy(y_buf.at[slot].at[pl.ds(0, n)], ys_hbm.at[rows(b, n)], y_sem.at[slot])
                if start:
                    cp.start()
                else:
                    cp.wait()

    @pl.when(i == 0)
    def _():
        for b in range(X_RING - 1):
            @pl.when(b < nvalid)
            def _(b=b):
                load(b).start()

    @pl.when(i < nvalid)
    def _():
        @pl.when(i + X_RING - 1 < nvalid)
        def _():
            load(i + X_RING - 1).start()

        @pl.when((i == 0) | (bexp_ref[i] != bexp_ref[jnp.maximum(i - 1, 0)]))
        def _():
            wgu_bf[:, :EXPERT_FF] = wg_ref[...].astype(BF16)
            wgu_bf[:, EXPERT_FF:] = wu_ref[...].astype(BF16)
            wd_bf[...] = wd_ref[...].astype(BF16)

        load(i).wait()

        @pl.when(i >= Y_RING)
        def _():
            store(i - Y_RING, start=False)

        for flag, n in ((1, HALF_ROWS), (0, BLOCK_ROWS)):
            @pl.when(bhalf_ref[i] == flag)
            def _(n=n):
                x_lo, x_hi = _unpack_halves(x_buf[i % X_RING, :n, :])
                gu = _dot(x_lo, wgu_bf[:PACK_W, :]) + _dot(x_hi, wgu_bf[PACK_W:, :])
                g = gu[:, :EXPERT_FF]
                a = (g * _sigmoid(g) * gu[:, EXPERT_FF:]).astype(BF16)
                y = _dot(a, wd_bf[...]).astype(BF16).astype(F32)
                y_buf[i % Y_RING, :n, :] = _pack_halves(y)

        store(i, start=True)

        @pl.when(i == nvalid - 1)
        def _():
            for back in range(Y_RING):
                @pl.when(i - back >= 0)
                def _(back=back):
                    store(i - back, start=False)


def _experts(bexp, bstart, bhalf, nvalid, xs, wg, wu, wd, layer):
    n_rows, D = xs.shape[0], D_MODEL
    nb = bexp.shape[0]

    wspec = lambda r, c: pl.BlockSpec((None, None, r, c),
                                      lambda i, be, bs, bh, nv: (layer, be[jnp.minimum(i, nv[0] - 1)], 0, 0))
    grid_spec = pltpu.PrefetchScalarGridSpec(
        num_scalar_prefetch=4,
        grid=(nb,),
        in_specs=[pl.BlockSpec(memory_space=pl.ANY),
                  wspec(D, EXPERT_FF), wspec(D, EXPERT_FF), wspec(EXPERT_FF, D)],
        out_specs=pl.BlockSpec(memory_space=pl.ANY),
        scratch_shapes=[pltpu.VMEM((X_RING, BLOCK_ROWS, PACK_W), U32), pltpu.VMEM((Y_RING, BLOCK_ROWS, PACK_W), U32),
                        pltpu.VMEM((D, 2 * EXPERT_FF), BF16), pltpu.VMEM((EXPERT_FF, D), BF16),
                        pltpu.SemaphoreType.DMA((X_RING,)), pltpu.SemaphoreType.DMA((Y_RING,))],
    )
    return pl.pallas_call(
        _expert_kernel,
        grid_spec=grid_spec,
        out_shape=jax.ShapeDtypeStruct((n_rows, PACK_W), U32),
        compiler_params=_cparams(("arbitrary",)),
        name="experts",
    )(bexp, bstart, bhalf, nvalid, xs, wg, wu, wd)


def _combine_kernel(off_ref, m_ref, base_ref, tot_ref, ys_hbm, posk_ref, wk_ref, h_ref, x_ref, g2_ref,
                    wsgu_ref, wsd_ref, o_ref, ys_ref, pw_ref, sem):
    i = pl.program_id(0)
    tm = h_ref.shape[0]
    slot = i % 2

    def fetch(tile, buf_slot):
        def run(e, carry):
            idx = tile * N_EXPERTS + e
            _start_rows(m_ref[idx], ys_hbm, off_ref[idx], ys_ref.at[buf_slot], base_ref[idx],
                        sem.at[buf_slot])
            return carry
        lax.fori_loop(0, N_EXPERTS, run, 0)

    @pl.when(i == 0)
    def _():
        ys_ref[...] = jnp.zeros_like(ys_ref)
        fetch(i, slot)

    @pl.when(i + 1 < pl.num_programs(0))
    def _():
        fetch(i + 1, 1 - slot)

    h = h_ref[...]
    gu = _dot(h, wsgu_ref[...])
    g = gu[:, :EXPERT_FF]
    shared = _dot((g * _sigmoid(g) * gu[:, EXPERT_FF:]).astype(BF16), wsd_ref[...])

    posk = posk_ref[0]
    wk = wk_ref[0]
    rows = lax.broadcasted_iota(jnp.int32, (MXU_DIM, tm), 0).astype(F32).astype(BF16)

    def weight_chunks(lo, hi):
        for c in range(lo, hi):
            rel = posk - float(c * MXU_DIM)
            pw = jnp.zeros((MXU_DIM, tm), BF16)
            for k in range(TOP_K):
                wrow = jnp.broadcast_to(wk[k:k + 1, :], (MXU_DIM, tm)).astype(BF16)
                pw = jnp.where(rows == rel[k:k + 1, :].astype(BF16), wrow, pw)
            pw_ref[c * MXU_DIM:(c + 1) * MXU_DIM, :] = pw

    def unsort(lo, hi):
        pw = pw_ref[lo:hi, :]
        halves = [lax.dot_general(pw, y, (((0,), (0,)), ((), ())), preferred_element_type=F32)
                  for y in _unpack_halves(ys_ref[slot, lo:hi, :])]
        return jnp.concatenate(halves, axis=1)

    weight_chunks(0, K_SORT_COMMON // MXU_DIM)
    _wait_rows(tot_ref[i], ys_hbm, ys_ref.at[slot], sem.at[slot])
    o_ref[...] = x_ref[...] + g2_ref[0] * (unsort(0, K_SORT_COMMON) + shared)

    @pl.when(tot_ref[i] * RUN_ALIGN > K_SORT_COMMON)
    def _():
        weight_chunks(K_SORT_COMMON // MXU_DIM, K_SORT // MXU_DIM)
        o_ref[...] = o_ref[...] + g2_ref[0] * unsort(K_SORT_COMMON, K_SORT)


def _combine(off, run_len, base, tot, ys, posk, wk, h2, x1, g2, wsgu, wsd, tiles_per_batch):
    T, D = h2.shape
    tm = TM_MOE
    const = lambda i, *_: (0, 0)
    tok = lambda w: pl.BlockSpec((tm, w), lambda i, *_: (i, 0))
    sel = pl.BlockSpec((1, TOP_K, tm), lambda i, *_: (i, 0, 0))
    grid_spec = pltpu.PrefetchScalarGridSpec(
        num_scalar_prefetch=4,
        grid=(T // tm,),
        in_specs=[pl.BlockSpec(memory_space=pl.ANY), sel, sel, tok(D), tok(D),
                  pl.BlockSpec((1, 1, D), lambda i, *_: (i // tiles_per_batch, 0, 0)),
                  pl.BlockSpec(wsgu.shape, const), pl.BlockSpec(wsd.shape, const)],
        out_specs=tok(D),
        scratch_shapes=[pltpu.VMEM((2, K_SORT, PACK_W), U32),
                        pltpu.VMEM((K_SORT, tm), BF16),
                        pltpu.SemaphoreType.DMA((2,))],
    )
    return pl.pallas_call(
        _combine_kernel,
        grid_spec=grid_spec,
        out_shape=jax.ShapeDtypeStruct((T, D), F32),
        compiler_params=_cparams(("arbitrary",)),
        name="combine",
    )(off, run_len, base, tot, ys, posk, wk, h2, x1, g2, wsgu, wsd)


def _moe(h2, x1, g2, posk, wk, cnt, wg, wu, wd, layer, wsgu, wsd, tiles_per_batch):
    T, D = h2.shape
    nt = T // TM_MOE

    pad = cnt[:, :, 0].astype(jnp.int32)
    base = jnp.cumsum(pad, axis=1) - pad
    tile_off = jnp.cumsum(pad, axis=0) - pad
    total = jnp.sum(pad, axis=0)
    region = ((total + HALF_ROWS - 1) // HALF_ROWS) * HALF_ROWS
    rend = jnp.cumsum(region)
    rstart = rend - region
    off = rstart[None, :] + tile_off
    nblk = (region + BLOCK_ROWS - 1) // BLOCK_ROWS
    bend = jnp.cumsum(nblk)
    nb_max = -(-(TOP_K * T + nt * N_EXPERTS * (RUN_ALIGN - 1)) // BLOCK_ROWS) + N_EXPERTS
    b = jnp.arange(nb_max, dtype=jnp.int32)
    bexp = jnp.minimum(jnp.sum((bend[None, :] <= b[:, None]).astype(jnp.int32), axis=1), N_EXPERTS - 1)
    bstart = rstart[bexp] + (b - (bend - nblk)[bexp]) * BLOCK_ROWS
    bhalf = (b == bend[bexp] - 1) & (region[bexp] % BLOCK_ROWS == HALF_ROWS)
    nvalid = bend[-1:].astype(jnp.int32)
    flat = lambda a: a.reshape(-1).astype(jnp.int32)

    run_len = flat(pad // RUN_ALIGN)
    tot = flat(jnp.sum(pad, axis=1) // RUN_ALIGN)
    n_rows = nb_max * BLOCK_ROWS + HALF_ROWS
    xs = _dispatch(flat(off), run_len, flat(base), tot, flat(rstart + total),
                   flat((region - total) // RUN_ALIGN), h2, posk, n_rows)
    ys = _experts(flat(bexp), flat(bstart), flat(bhalf), nvalid, xs, wg, wu, wd, layer)
    return _combine(flat(off), run_len, flat(base), tot, ys, posk, wk, h2, x1, g2, wsgu, wsd, tiles_per_batch)


def kernel(x, c, ada_w, ada_b, mix_norm_g, w_in, b_fgate, qn_a, kn_a, qn_b, kn_b, rel_bias, w_proj_a,
           w_proj_b, w_out, ffn_norm_g, w_router, router_bias, w_gate_e, w_up_e, w_down_e, w_gate_s,
           w_up_s, w_down_s):
    B, S, D = x.shape
    L = ada_w.shape[0]
    T = B * S
    assert D == D_MODEL and S % TM_PRE == 0 and TQ_FOX == TM_PRE and S % TM_MOE == 0 and S % (BAND_TILES * TQ_BAND) == 0

    mod = _adaln(c, ada_w, ada_b).reshape(L, B, 6, 1, D)

    hid = np.arange(MXU_DIM) // HEAD_DIM
    bd = jnp.asarray(np.where(hid[:, None] == hid[None, :], 1.0 / HEAD_DIM, 0.0), BF16)
    r = np.arange(TM_PRE)
    tri = jnp.asarray(r[:, None] <= r[None, :], BF16)
    r = np.arange(TM_MOE)
    ustrict = jnp.asarray(r[:, None] < r[None, :], BF16)
    r = np.arange(N_EXPERTS)
    lstrict = jnp.asarray(r[None, :] < r[:, None], BF16)

    q_scale = 1.0 / math.sqrt(HEAD_DIM)
    for l in range(L):
        sh1, sc1, g1, sh2, sc2, g2 = (mod[l, :, j] for j in range(6))
        w = w_in[l]
        cols = lambda c: w[:, c * WIDTH:(c + 1) * WIDTH]
        wqk = jnp.concatenate([cols(0), cols(1), cols(3), cols(4)], axis=1).astype(BF16)
        wvt = jnp.concatenate([cols(2), cols(5)], axis=1).T.astype(BF16)
        wf = jnp.pad(w[:, 6 * WIDTH:6 * WIDTH + N_HEADS].T.astype(BF16), ((0, N_HEADS), (0, 0)))
        wg = w[:, 6 * WIDTH + N_HEADS:].astype(BF16)
        gains = jnp.stack([qn_a[l] * (q_scale * LOG2E), kn_a[l], qn_b[l] * (q_scale * LOG2E), kn_b[l]])
        hn = jnp.pad(jnp.tile(gains, (1, N_HEADS)), ((0, 4), (0, 0)))

        qk, vt, dec, cbase, ga, gb = _premix(x, sh1, sc1, mix_norm_g[l][None], wqk, wvt, wf, wg,
                                             b_fgate[l][:, None], hn, bd, tri)
        ya = _band_attention(qk, vt, _band_bias(rel_bias[l]))
        yb = _fox_attention(qk, vt, dec, cbase)
        x1, h2, posk, wk, cnt = _postmix(
            ya, yb, ga, gb, x, g1, w_proj_a[l].astype(BF16), w_proj_b[l].astype(BF16),
            w_out[l].astype(BF16), ffn_norm_g[l][None], sh2, sc2, w_router[l].T.astype(BF16),
            router_bias[l][:, None], ustrict, lstrict)

        wsgu = jnp.concatenate([w_gate_s[l], w_up_s[l]], axis=-1).astype(BF16)
        x = _moe(h2.reshape(T, D), x1.reshape(T, D), g2, posk, wk, cnt, w_gate_e, w_up_e, w_down_e, l,
                 wsgu, w_down_s[l].astype(BF16), S // TM_MOE).reshape(B, S, D)
    return x
```

```python
import math

import jax
import jax.numpy as jnp
import numpy as np
from jax import lax
from jax.experimental import pallas as pl
from jax.experimental.pallas import tpu as pltpu

F32 = jnp.float32
BF16 = jnp.bfloat16

D_MODEL = 1024
HEAD_DIM = 64
N_HEADS = 8
WIDTH = N_HEADS * HEAD_DIM
CHUNK = 64
LOOKBACK_CHUNKS = 8
REL_CLIP = 128
N_EXPERTS = 64
TOP_K = 8
N_GROUPS = 8
TOPK_GROUPS = 4
EXPERT_FF = 256
ROUTED_SCALE = 2.5
EPS = 1e-6
NEG_INF = -1e30
LOG2E = math.log2(math.e)

LANES = 128
MXU_DIM = 256
VMEM_LIMIT = 56 * 1024 * 1024

TM_PRE = 512
TQ_BAND = 256
BAND_WIN = TQ_BAND + LOOKBACK_CHUNKS * CHUNK
BAND_TILES = LOOKBACK_CHUNKS * CHUNK // TQ_BAND
TQ_FOX = 512
PAIRS_PER_STEP = 4
PV_ROWS = HEAD_DIM + 16
TM_MOE = 256
RUN_ALIGN = 8
PACK_W = D_MODEL // 2
U32 = jnp.uint32
K_SORT = ((TOP_K * TM_MOE + N_EXPERTS * (RUN_ALIGN - 1) + MXU_DIM - 1) // MXU_DIM) * MXU_DIM
K_SORT_COMMON = 2304
BLOCK_ROWS = 1024
HALF_ROWS = BLOCK_ROWS // 2


def _dot(a, b):
    return jnp.dot(a, b, preferred_element_type=F32)


def _dot_nt(a, b):
    return lax.dot_general(a, b, (((1,), (1,)), ((), ())), preferred_element_type=F32)


def _sigmoid(v):
    return 1.0 / (1.0 + jnp.exp(-v))


def _cparams(sem):
    return pltpu.CompilerParams(dimension_semantics=sem, vmem_limit_bytes=VMEM_LIMIT)


def _adaln_kernel(c_ref, w_ref, b_ref, o_ref):
    c = c_ref[...]
    ca = c * _sigmoid(c)
    o_ref[0] = jnp.dot(ca, w_ref[0], preferred_element_type=F32,
                       precision=lax.Precision.HIGHEST) + b_ref[0]


def _adaln(c, ada_w, ada_b):
    L, D, N = ada_w.shape
    B = c.shape[0]
    tn = N // 4
    return pl.pallas_call(
        _adaln_kernel,
        grid=(L, N // tn),
        in_specs=[pl.BlockSpec((B, D), lambda l, j: (0, 0)),
                  pl.BlockSpec((1, D, tn), lambda l, j: (l, 0, j)),
                  pl.BlockSpec((1, 1, tn), lambda l, j: (l, 0, j))],
        out_specs=pl.BlockSpec((1, B, tn), lambda l, j: (l, 0, j)),
        out_shape=jax.ShapeDtypeStruct((L, B, N), F32),
        compiler_params=_cparams(("arbitrary", "arbitrary")),
        name="adaln",
    )(c, ada_w, ada_b.reshape(L, 1, N))


def _modnorm(x, g, sc, sh):
    ms = jnp.mean(x * x, axis=-1, keepdims=True)
    y = x * lax.rsqrt(ms + EPS) * g
    return y * (1.0 + sc) + sh


def _premix_kernel(x_ref, sh_ref, sc_ref, g_ref, wqk_ref, wvt_ref, wf_ref, wg_ref, bf_ref, hn_ref, bd_ref,
                   tri_ref, qk_ref, vt_ref, dec_ref, cbase_ref, ga_ref, gb_ref, carry_ref):
    si = pl.program_id(1)
    tm = x_ref.shape[1]
    h = _modnorm(x_ref[0], g_ref[...], sc_ref[0], sh_ref[0]).astype(BF16)

    for c in range(4):
        z = _dot(h, wqk_ref[:, c * WIDTH:(c + 1) * WIDTH])
        sq = (z * z).astype(BF16)
        ms = jnp.concatenate(
            [_dot(sq[:, j * MXU_DIM:(j + 1) * MXU_DIM], bd_ref[...]) for j in range(WIDTH // MXU_DIM)],
            axis=1)
        z = z * lax.rsqrt(ms + EPS) * hn_ref[c:c + 1, :]
        qk_ref[c, 0] = z.astype(BF16)

    for c in range(2):
        vt = _dot_nt(wvt_ref[c * WIDTH:(c + 1) * WIDTH, :], h).astype(BF16)
        vt_ref[0, c * (N_HEADS // 2):(c + 1) * (N_HEADS // 2)] = vt.reshape(N_HEADS // 2, LANES, tm)

    fr = _dot_nt(wf_ref[...], h)[:N_HEADS]
    xg = fr + bf_ref[...]
    logf = jnp.minimum(xg, 0.0) - jnp.log(1.0 + jnp.exp(-jnp.abs(xg)))
    hi = logf.astype(BF16).astype(F32)
    r1 = logf - hi
    mid = r1.astype(BF16).astype(F32)
    lo = r1 - mid
    parts = jnp.concatenate([hi, mid, lo, jnp.zeros_like(hi)], axis=0)
    cs3 = _dot(parts.astype(BF16), tri_ref[...])
    cs = cs3[0:8] + cs3[8:16] + cs3[16:24]

    @pl.when(si == 0)
    def _():
        carry_ref[...] = jnp.zeros_like(carry_ref)

    drel = cs * (-LOG2E)
    dhi = drel.astype(BF16).astype(F32)
    dlo = drel - dhi
    dec = jnp.concatenate([dhi, dlo, jnp.zeros((LANES - 2 * N_HEADS, tm), F32)], axis=0)
    dec_ref[0] = dec.T.astype(BF16)
    carry = carry_ref[:, 0:1]
    cbase_ref[0, 0] = jnp.broadcast_to(carry * (-LOG2E), (N_HEADS, tm))
    carry_ref[...] = jnp.broadcast_to(carry + cs[:, tm - 1:tm], carry_ref.shape)

    for c in range(4):
        zg = _dot(h, wg_ref[:, c * WIDTH:(c + 1) * WIDTH])
        sg = _sigmoid(zg).astype(BF16)
        if c < 2:
            ga_ref[0, :, c * WIDTH:(c + 1) * WIDTH] = sg
        else:
            gb_ref[0, :, (c - 2) * WIDTH:(c - 1) * WIDTH] = sg


def _premix(x, sh, sc, g, wqk, wvt, wf, wg, bf, hn, bd, tri):
    B, S, D = x.shape
    tm = TM_PRE
    const = lambda b, s: (0, 0)
    return pl.pallas_call(
        _premix_kernel,
        grid=(B, S // tm),
        in_specs=[pl.BlockSpec((1, tm, D), lambda b, s: (b, s, 0)),
                  pl.BlockSpec((1, 1, D), lambda b, s: (b, 0, 0)),
                  pl.BlockSpec((1, 1, D), lambda b, s: (b, 0, 0)),
                  pl.BlockSpec((1, D), const),
                  pl.BlockSpec(wqk.shape, const),
                  pl.BlockSpec(wvt.shape, const),
                  pl.BlockSpec(wf.shape, const),
                  pl.BlockSpec(wg.shape, const),
                  pl.BlockSpec(bf.shape, const),
                  pl.BlockSpec(hn.shape, const),
                  pl.BlockSpec(bd.shape, const),
                  pl.BlockSpec(tri.shape, const)],
        out_specs=[pl.BlockSpec((4, 1, tm, WIDTH), lambda b, s: (0, b, s, 0)),
                   pl.BlockSpec((1, N_HEADS, LANES, tm), lambda b, s: (b, 0, 0, s)),
                   pl.BlockSpec((1, tm, LANES), lambda b, s: (b, s, 0)),
                   pl.BlockSpec((1, 1, N_HEADS, tm), lambda b, s: (b, s, 0, 0)),
                   pl.BlockSpec((1, tm, D), lambda b, s: (b, s, 0)),
                   pl.BlockSpec((1, tm, D), lambda b, s: (b, s, 0))],
        out_shape=[jax.ShapeDtypeStruct((4, B, S, WIDTH), BF16),
                   jax.ShapeDtypeStruct((B, N_HEADS, LANES, S), BF16),
                   jax.ShapeDtypeStruct((B, S, LANES), BF16),
                   jax.ShapeDtypeStruct((B, S // tm, N_HEADS, tm), F32),
                   jax.ShapeDtypeStruct((B, S, D), BF16),
                   jax.ShapeDtypeStruct((B, S, D), BF16)],
        scratch_shapes=[pltpu.VMEM((N_HEADS, LANES), F32)],
        compiler_params=_cparams(("arbitrary", "arbitrary")),
        name="premix",
    )(x, sh, sc, g, wqk, wvt, wf, wg, bf, hn, bd, tri)


def _values_with_ones(vt, hh):
    ones = jnp.ones((PV_ROWS - HEAD_DIM, vt.shape[1]), vt.dtype)
    return jnp.concatenate([vt[hh * HEAD_DIM:(hh + 1) * HEAD_DIM, :], ones], axis=0)


def _pair_finish_t(acc0, acc1):
    out_t = jnp.concatenate([acc[:HEAD_DIM] / acc[HEAD_DIM:HEAD_DIM + 1, :] for acc in (acc0, acc1)], axis=0)
    return out_t.T


def _band_kernel(q_ref, kprev_ref, kcur_ref, vprev_ref, vcur_ref, bias_ref, o_ref):
    tq = TQ_BAND
    step_start = pl.program_id(2) * (BAND_TILES * tq)
    lane = lax.broadcasted_iota(jnp.int32, (1, LANES), 1)
    krow = lax.broadcasted_iota(jnp.int32, (BAND_WIN, 1), 0)
    pieces = [(ref, j) for ref in (kprev_ref, kcur_ref) for j in range(BAND_TILES)]
    vpieces = [(ref, j) for ref in (vprev_ref, vcur_ref) for j in range(BAND_TILES)]
    units = [(u, pp, hh) for u in range(BAND_TILES) for pp in range(PAIRS_PER_STEP) for hh in range(2)]
    scores = []
    for u, pp, hh in units:
        q = q_ref[u * tq:(u + 1) * tq, pp * LANES:(pp + 1) * LANES]
        qm = jnp.where((lane // HEAD_DIM) == hh, q, jnp.zeros_like(q))
        scores.append(jnp.concatenate(
            [_dot_nt(ref[j * tq:(j + 1) * tq, pp * LANES:(pp + 1) * LANES], qm)
             for ref, j in pieces[u:u + 3]], axis=0))
    accs = []
    for n, (u, pp, hh) in enumerate(units):
        kvalid = (krow + (step_start + u * tq - LOOKBACK_CHUNKS * CHUNK)) >= 0
        s = jnp.where(kvalid, scores[n] + bias_ref[2 * pp + hh], NEG_INF)
        m = jnp.max(s, axis=0, keepdims=True)
        p = jnp.exp2(s - m).astype(BF16)
        acc = None
        for c, (ref, j) in enumerate(vpieces[u:u + 3]):
            t = _dot(_values_with_ones(ref[pp, :, j * tq:(j + 1) * tq], hh), p[c * tq:(c + 1) * tq, :])
            acc = t if acc is None else acc + t
        accs.append(acc)
    for n in range(0, len(units), 2):
        u, pp, _ = units[n]
        o_ref[u * tq:(u + 1) * tq, pp * LANES:(pp + 1) * LANES] = _pair_finish_t(
            accs[n], accs[n + 1]).astype(o_ref.dtype)


def _band_attention(qk, vt, bias_t):
    _, B, S, _ = qk.shape
    rows = BAND_TILES * TQ_BAND
    groups = N_HEADS // 2 // PAIRS_PER_STEP
    gw = PAIRS_PER_STEP * LANES

    def k_spec(back):
        return pl.BlockSpec((None, None, rows, gw),
                            lambda b, g, qi: (1, b, jnp.maximum(qi - back, 0), g))

    def v_spec(back):
        return pl.BlockSpec((None, PAIRS_PER_STEP, LANES, rows),
                            lambda b, g, qi: (b, g, 0, jnp.maximum(qi - back, 0)))

    return pl.pallas_call(
        _band_kernel,
        grid=(B, groups, S // rows),
        in_specs=[pl.BlockSpec((None, None, rows, gw), lambda b, g, qi: (0, b, qi, g)),
                  k_spec(1), k_spec(0), v_spec(1), v_spec(0),
                  pl.BlockSpec((2 * PAIRS_PER_STEP, BAND_WIN, TQ_BAND), lambda b, g, qi: (g, 0, 0))],
        out_specs=pl.BlockSpec((None, rows, gw), lambda b, g, qi: (b, qi, g)),
        out_shape=jax.ShapeDtypeStruct((B, S, WIDTH), BF16),
        compiler_params=_cparams(("arbitrary", "arbitrary", "arbitrary")),
        name="band_attn",
    )(qk, qk, qk, vt, vt, bias_t)


def _band_bias(rel_table):
    pad = LOOKBACK_CHUNKS * CHUNK
    i = np.arange(TQ_BAND)[:, None]
    j = np.arange(BAND_WIN)[None, :]
    period = 1024
    assert BAND_WIN + TQ_BAND <= period
    m = np.arange(period)
    rel = np.where(m <= BAND_WIN, pad - m, pad + period - m)
    g = rel_table[:, np.clip(rel, -REL_CLIP, REL_CLIP) + REL_CLIP].astype(F32)
    bias = jnp.tile(g, (1, TQ_BAND))[:, :TQ_BAND * (period - 1)].reshape(-1, TQ_BAND, period - 1)[:, :, :BAND_WIN]
    c0 = (i // CHUNK) * CHUNK
    inband = (j >= c0) & (j < c0 + pad + CHUNK)
    return jnp.where(inband[None], bias * LOG2E, NEG_INF).transpose(0, 2, 1)


FOX_DIAG, FOX_TWO_PAST, FOX_PAST_DIAG = 0, 1, 2


def _fox_kernel(qi_tab, ka_tab, kb_tab, kind_tab, first_tab, q_ref, ka_ref, kb_ref, vta_ref, vtb_ref,
                deca_ref, decb_ref, cba_ref, cbb_ref, o_ref, acc_ref, m_ref):
    grp = pl.program_id(1)
    t = pl.program_id(2)
    kind = kind_tab[t]
    tq, tk = q_ref.shape[0], ka_ref.shape[0]
    lane2 = lax.broadcasted_iota(jnp.int32, (1, 2 * LANES), 1)
    heads = [(pp, hh) for pp in range(PAIRS_PER_STEP) for hh in range(2)]
    tile_a = (ka_ref, vta_ref, deca_ref, cba_ref)
    tile_b = (kb_ref, vtb_ref, decb_ref, cbb_ref)

    @pl.when(first_tab[t] == 1)
    def _():
        acc_ref[...] = jnp.zeros_like(acc_ref)
        m_ref[...] = jnp.full_like(m_ref, NEG_INF)

    def step(tiles):
        ones = jnp.where(lax.broadcasted_iota(jnp.int32, (tq, LANES), 1) < 2 * N_HEADS, 1.0, 0.0).astype(BF16)
        causal = (lax.broadcasted_iota(jnp.int32, (tk, tq), 0)
                  <= lax.broadcasted_iota(jnp.int32, (tk, tq), 1))
        scores = []
        for pp, hh in heads:
            head = 2 * (PAIRS_PER_STEP * grp + pp) + hh
            q_aug = jnp.concatenate([q_ref[:, pp * LANES:(pp + 1) * LANES], ones], axis=1)
            use = (((lane2 // HEAD_DIM) == hh) | (lane2 == LANES + head) | (lane2 == LANES + N_HEADS + head))
            qm = jnp.where(use, q_aug, jnp.zeros_like(q_aug))
            per_tile = []
            for (k_ref, _, dec_ref, _), diagonal in tiles:
                k_aug = jnp.concatenate([k_ref[:, pp * LANES:(pp + 1) * LANES], dec_ref[...]], axis=1)
                s = _dot_nt(k_aug, qm)
                per_tile.append(jnp.where(causal, s, NEG_INF) if diagonal else s)
            scores.append(per_tile)
        for n, (pp, hh) in enumerate(heads):
            head = 2 * (PAIRS_PER_STEP * grp + pp) + hh
            bases = [cb_ref[pl.ds(head, 1), :] for (_, _, _, cb_ref), _ in tiles]
            m_old = m_ref[n]
            m_new = m_old
            for s, base in zip(scores[n], bases):
                m_new = jnp.maximum(m_new, jnp.max(s, axis=0, keepdims=True) + base)
            acc = acc_ref[n] * jnp.exp2(m_old - m_new)
            for s, base, ((_, vt_ref, _, _), _) in zip(scores[n], bases, tiles):
                p = jnp.exp2(s - (m_new - base)).astype(BF16)
                acc = acc + _dot(_values_with_ones(vt_ref[pp], hh), p)
            acc_ref[n] = acc
            m_ref[n] = m_new

    def finish():
        for pp in range(PAIRS_PER_STEP):
            o_ref[:, pp * LANES:(pp + 1) * LANES] = _pair_finish_t(
                acc_ref[2 * pp], acc_ref[2 * pp + 1]).astype(o_ref.dtype)

    @pl.when(kind == FOX_TWO_PAST)
    def _():
        step([(tile_a, False), (tile_b, False)])

    @pl.when(kind == FOX_PAST_DIAG)
    def _():
        step([(tile_a, False), (tile_b, True)])
        finish()

    @pl.when(kind == FOX_DIAG)
    def _():
        step([(tile_a, True)])
        finish()


def _fox_attention(qkv, vt, dec, cbase):
    _, B, S, _ = qkv.shape
    tq = TQ_FOX
    nq = S // tq
    groups = N_HEADS // 2 // PAIRS_PER_STEP
    gw = PAIRS_PER_STEP * LANES
    steps = []
    for i in range(nq):
        past = list(range(i))
        first = 1
        while len(past) >= 2:
            steps.append((i, past[0], past[1], FOX_TWO_PAST, first))
            past, first = past[2:], 0
        if past:
            steps.append((i, past[0], i, FOX_PAST_DIAG, first))
        else:
            steps.append((i, i, i, FOX_DIAG, first))
    tabs = [jnp.array([s[c] for s in steps], jnp.int32) for c in range(5)]

    def key_specs(col):
        tile = lambda *a: a[3 + col][a[2]]
        return [pl.BlockSpec((None, None, tq, gw), lambda b, g, t, *tb: (3, b, tile(b, g, t, *tb), g)),
                pl.BlockSpec((None, PAIRS_PER_STEP, LANES, tq),
                             lambda b, g, t, *tb: (b, groups + g, 0, tile(b, g, t, *tb))),
                pl.BlockSpec((None, tq, LANES), lambda b, g, t, *tb: (b, tile(b, g, t, *tb), 0)),
                pl.BlockSpec((None, None, N_HEADS, tq), lambda b, g, t, *tb: (b, tile(b, g, t, *tb), 0, 0))]

    ka, va, da, ca = key_specs(1)
    kb, vb, db, cb = key_specs(2)
    grid_spec = pltpu.PrefetchScalarGridSpec(
        num_scalar_prefetch=5,
        grid=(B, groups, len(steps)),
        in_specs=[pl.BlockSpec((None, None, tq, gw), lambda b, g, t, *tb: (2, b, tb[0][t], g)),
                  ka, kb, va, vb, da, db, ca, cb],
        out_specs=pl.BlockSpec((None, tq, gw), lambda b, g, t, *tb: (b, tb[0][t], g)),
        scratch_shapes=[pltpu.VMEM((2 * PAIRS_PER_STEP, PV_ROWS, tq), F32),
                        pltpu.VMEM((2 * PAIRS_PER_STEP, 1, tq), F32)],
    )
    return pl.pallas_call(
        _fox_kernel,
        grid_spec=grid_spec,
        out_shape=jax.ShapeDtypeStruct((B, S, WIDTH), BF16),
        compiler_params=_cparams(("arbitrary", "arbitrary", "arbitrary")),
        name="fox_attn",
    )(*tabs, qkv, qkv, qkv, vt, vt, dec, dec, cbase, cbase)


def _postmix_kernel(ya_ref, yb_ref, ga_ref, gb_ref, x_ref, g1_ref, wpa_ref, wpb_ref, wo_ref,
                    g_ref, sh_ref, sc_ref, wr_ref, rb_ref, ustrict_ref, lstrict_ref,
                    x1_ref, h2_ref, posk_ref, wk_ref, cnt_ref):
    ua = _dot(ya_ref[0], wpa_ref[...])
    ub = _dot(yb_ref[0], wpb_ref[...])
    m = ga_ref[0].astype(F32) * ua + gb_ref[0].astype(F32) * ub
    mo = _dot(m.astype(BF16), wo_ref[...])
    x1 = x_ref[0] + g1_ref[0] * mo
    x1_ref[0] = x1
    h2 = _modnorm(x1, g_ref[...], sc_ref[0], sh_ref[0]).astype(BF16)
    h2_ref[0] = h2
    for j in range(h2.shape[0] // TM_MOE):
        posk, wk, cnt = _route(h2[j * TM_MOE:(j + 1) * TM_MOE], wr_ref[...], rb_ref[...],
                               ustrict_ref[...], lstrict_ref[...])
        posk_ref[j] = posk
        wk_ref[j] = wk
        cnt_ref[j] = cnt


def _postmix(ya, yb, ga, gb, x, g1, wpa, wpb, wo, g, sh, sc, wr_t, rb, ustrict, lstrict):
    B, S, D = x.shape
    tm = TM_PRE
    sub_tiles = tm // TM_MOE
    nt = B * S // TM_MOE
    const = lambda b, s: (0, 0)
    tok = lambda w: pl.BlockSpec((1, tm, w), lambda b, s: (b, s, 0))
    row = pl.BlockSpec((1, 1, D), lambda b, s: (b, 0, 0))
    tiles = lambda r, c: pl.BlockSpec((sub_tiles, r, c), lambda b, s: (b * (S // tm) + s, 0, 0))
    return pl.pallas_call(
        _postmix_kernel,
        grid=(B, S // tm),
        in_specs=[tok(WIDTH), tok(WIDTH), tok(D), tok(D), tok(D), row,
                  pl.BlockSpec(wpa.shape, const), pl.BlockSpec(wpb.shape, const),
                  pl.BlockSpec(wo.shape, const), pl.BlockSpec((1, D), const), row, row,
                  pl.BlockSpec(wr_t.shape, const), pl.BlockSpec(rb.shape, const),
                  pl.BlockSpec(ustrict.shape, const), pl.BlockSpec(lstrict.shape, const)],
        out_specs=[tok(D), tok(D), tiles(TOP_K, TM_MOE), tiles(TOP_K, TM_MOE), tiles(N_EXPERTS, LANES)],
        out_shape=[jax.ShapeDtypeStruct((B, S, D), F32), jax.ShapeDtypeStruct((B, S, D), BF16),
                   jax.ShapeDtypeStruct((nt, TOP_K, TM_MOE), F32),
                   jax.ShapeDtypeStruct((nt, TOP_K, TM_MOE), F32),
                   jax.ShapeDtypeStruct((nt, N_EXPERTS, LANES), F32)],
        compiler_params=_cparams(("arbitrary", "arbitrary")),
        name="postmix",
    )(ya, yb, ga, gb, x, g1, wpa, wpb, wo, g, sh, sc, wr_t, rb, ustrict, lstrict)


def _route(h, wr, rb, ustrict, lstrict):
    tm = h.shape[0]
    per_group = N_EXPERTS // N_GROUPS
    logits = _dot_nt(wr, h)
    scores = _sigmoid(logits)
    biased = scores + rb
    sub8 = lax.broadcasted_iota(jnp.int32, (per_group, tm), 0).astype(F32)
    neg = -jnp.inf

    grp_rows = []
    for g in range(N_GROUPS):
        a = biased[g * per_group:(g + 1) * per_group]
        m1 = jnp.max(a, axis=0, keepdims=True)
        i1 = jnp.min(jnp.where(a == m1, sub8, float(per_group)), axis=0, keepdims=True)
        m2 = jnp.max(jnp.where(sub8 == i1, neg, a), axis=0, keepdims=True)
        grp_rows.append(m1 + m2)
    grp = jnp.concatenate(grp_rows, axis=0)

    gsub = lax.broadcasted_iota(jnp.int32, (N_GROUPS, tm), 0).astype(F32)
    gmask = jnp.zeros((N_GROUPS, tm), F32)
    for _ in range(TOPK_GROUPS):
        mx = jnp.max(grp, axis=0, keepdims=True)
        gi = jnp.min(jnp.where(grp == mx, gsub, float(N_GROUPS)), axis=0, keepdims=True)
        sel = gsub == gi
        gmask = jnp.where(sel, 1.0, gmask)
        grp = jnp.where(sel, neg, grp)
    masked = jnp.concatenate(
        [jnp.where(gmask[g:g + 1] > 0.5, biased[g * per_group:(g + 1) * per_group], neg)
         for g in range(N_GROUPS)], axis=0)

    esub = lax.broadcasted_iota(jnp.int32, (N_EXPERTS, tm), 0).astype(F32)
    sels, ws = [], []
    for _ in range(TOP_K):
        mx = jnp.max(masked, axis=0, keepdims=True)
        ei = jnp.min(jnp.where(masked == mx, esub, float(N_EXPERTS)), axis=0, keepdims=True)
        sel = esub == ei
        sels.append(sel)
        ws.append(jnp.sum(jnp.where(sel, scores, 0.0), axis=0, keepdims=True))
        masked = jnp.where(sel, neg, masked)
    wsum = ws[0]
    for w in ws[1:]:
        wsum = wsum + w

    selmask = jnp.zeros((N_EXPERTS, tm), F32)
    for sel in sels:
        selmask = selmask + jnp.where(sel, 1.0, 0.0)
    rank = _dot(selmask.astype(BF16), ustrict)
    counts = jnp.sum(selmask, axis=1, keepdims=True)
    padded = jnp.ceil(counts * (1.0 / RUN_ALIGN)) * RUN_ALIGN
    padded_b = jnp.broadcast_to(padded, (N_EXPERTS, LANES))
    base = _dot(lstrict, padded_b.astype(BF16))
    pos = base[:, 0:1] + rank

    posk = jnp.concatenate([jnp.sum(jnp.where(sel, pos, 0.0), axis=0, keepdims=True) for sel in sels], axis=0)
    wk = jnp.concatenate([(w / wsum) * ROUTED_SCALE for w in ws], axis=0)
    return posk, wk, padded_b


def _pack_halves(v):
    half = v.shape[1] // 2
    lo = lax.bitcast_convert_type(v[:, :half], U32)
    hi = lax.bitcast_convert_type(v[:, half:], U32)
    return (lo >> 16) | (hi & U32(0xFFFF0000))


def _unpack_halves(u):
    lo = lax.bitcast_convert_type(u << 16, F32).astype(BF16)
    hi = lax.bitcast_convert_type(u & U32(0xFFFF0000), F32).astype(BF16)
    return lo, hi


def _rows_copy(m, src_ref, src0, dst_ref, dst0, sem):
    rows = m * RUN_ALIGN
    return pltpu.make_async_copy(src_ref.at[pl.ds(pl.multiple_of(src0, RUN_ALIGN), rows)],
                                 dst_ref.at[pl.ds(pl.multiple_of(dst0, RUN_ALIGN), rows)], sem)


def _start_rows(m, src_ref, src0, dst_ref, dst0, sem):
    @pl.when(m > 0)
    def _():
        _rows_copy(m, src_ref, src0, dst_ref, dst0, sem).start()


def _wait_rows(m, src_ref, dst_ref, sem):
    @pl.when(m > 0)
    def _():
        _rows_copy(m, src_ref, 0, dst_ref, 0, sem).wait()


def _dispatch_kernel(off_ref, m_ref, base_ref, tot_ref, gapoff_ref, gapm_ref, h_ref, posk_ref, xs_hbm,
                     xs_ref, zero_ref, sem):
    i = pl.program_id(0)
    tm = h_ref.shape[0]
    n_chunks = K_SORT // MXU_DIM

    @pl.when(i == 0)
    def _():
        zero_ref[...] = jnp.zeros_like(zero_ref)

        def gap(e, carry):
            _start_rows(gapm_ref[e], zero_ref, 0, xs_hbm, gapoff_ref[e], sem)
            return carry
        lax.fori_loop(0, N_EXPERTS, gap, 0)

        def gap_wait(e, carry):
            _wait_rows(gapm_ref[e], zero_ref, xs_hbm, sem)
            return carry
        lax.fori_loop(0, N_EXPERTS, gap_wait, 0)

    slot = i % 2
    xs_slot = xs_ref.at[slot]
    h = h_ref[...]
    posk = posk_ref[0]
    rows = lax.broadcasted_iota(jnp.int32, (MXU_DIM, tm), 0).astype(F32).astype(BF16)
    one = jnp.ones((MXU_DIM, tm), BF16)
    def sort_chunks(lo, hi):
        for c in range(lo, hi):
            rel = posk - float(c * MXU_DIM)
            p = jnp.zeros((MXU_DIM, tm), BF16)
            for k in range(TOP_K):
                p = jnp.where(rows == rel[k:k + 1, :].astype(BF16), one, p)
            xs_slot[c * MXU_DIM:(c + 1) * MXU_DIM, :] = _pack_halves(_dot(p, h))

    sort_chunks(0, K_SORT_COMMON // MXU_DIM)

    @pl.when(tot_ref[i] * RUN_ALIGN > K_SORT_COMMON)
    def _():
        sort_chunks(K_SORT_COMMON // MXU_DIM, n_chunks)

    @pl.when(i > 0)
    def _():
        _wait_rows(tot_ref[i - 1], xs_ref.at[1 - slot], xs_hbm, sem)

    def run(e, carry):
        idx = i * N_EXPERTS + e
        _start_rows(m_ref[idx], xs_slot, base_ref[idx], xs_hbm, off_ref[idx], sem)
        return carry
    lax.fori_loop(0, N_EXPERTS, run, 0)

    @pl.when(i == pl.num_programs(0) - 1)
    def _():
        _wait_rows(tot_ref[i], xs_slot, xs_hbm, sem)


def _dispatch(off, run_len, base, tot, gapoff, gapm, h2, posk, n_rows):
    T, D = h2.shape
    tm = TM_MOE
    grid_spec = pltpu.PrefetchScalarGridSpec(
        num_scalar_prefetch=6,
        grid=(T // tm,),
        in_specs=[pl.BlockSpec((tm, D), lambda i, *_: (i, 0)),
                  pl.BlockSpec((1, TOP_K, tm), lambda i, *_: (i, 0, 0))],
        out_specs=pl.BlockSpec(memory_space=pl.ANY),
        scratch_shapes=[pltpu.VMEM((2, K_SORT, PACK_W), U32),
                        pltpu.VMEM((BLOCK_ROWS, PACK_W), U32),
                        pltpu.SemaphoreType.DMA],
    )
    return pl.pallas_call(
        _dispatch_kernel,
        grid_spec=grid_spec,
        out_shape=jax.ShapeDtypeStruct((n_rows, PACK_W), U32),
        compiler_params=_cparams(("arbitrary",)),
        name="dispatch",
    )(off, run_len, base, tot, gapoff, gapm, h2, posk)


X_RING = 3
Y_RING = 2


def _expert_kernel(bexp_ref, bstart_ref, bhalf_ref, nvalid_ref, xs_hbm, wg_ref, wu_ref, wd_ref, ys_hbm,
                   x_buf, y_buf, wgu_bf, wd_bf, x_sem, y_sem):
    i = pl.program_id(0)
    nvalid = nvalid_ref[0]

    def rows(b, n):
        return pl.ds(pl.multiple_of(bstart_ref[b], HALF_ROWS), n)

    def load(b):
        slot = b % X_RING
        return pltpu.make_async_copy(xs_hbm.at[rows(b, BLOCK_ROWS)], x_buf.at[slot], x_sem.at[slot])

    def store(b, start):
        slot = b % Y_RING
        for flag, n in ((1, HALF_ROWS), (0, BLOCK_ROWS)):
            @pl.when(bhalf_ref[b] == flag)
            def _(n=n):
                cp = pltpu.make_async_copy(y_buf.at[slot].at[pl.ds(0, n)], ys_hbm.at[rows(b, n)], y_sem.at[slot])
                if start:
                    cp.start()
                else:
                    cp.wait()

    @pl.when(i == 0)
    def _():
        for b in range(X_RING - 1):
            @pl.when(b < nvalid)
            def _(b=b):
                load(b).start()

    @pl.when(i < nvalid)
    def _():
        @pl.when(i + X_RING - 1 < nvalid)
        def _():
            load(i + X_RING - 1).start()

        @pl.when((i == 0) | (bexp_ref[i] != bexp_ref[jnp.maximum(i - 1, 0)]))
        def _():
            wgu_bf[:, :EXPERT_FF] = wg_ref[...].astype(BF16)
            wgu_bf[:, EXPERT_FF:] = wu_ref[...].astype(BF16)
            wd_bf[...] = wd_ref[...].astype(BF16)

        load(i).wait()

        @pl.when(i >= Y_RING)
        def _():
            store(i - Y_RING, start=False)

        for flag, n in ((1, HALF_ROWS), (0, BLOCK_ROWS)):
            @pl.when(bhalf_ref[i] == flag)
            def _(n=n):
                x_lo, x_hi = _unpack_halves(x_buf[i % X_RING, :n, :])
                gu = _dot(x_lo, wgu_bf[:PACK_W, :]) + _dot(x_hi, wgu_bf[PACK_W:, :])
                g = gu[:, :EXPERT_FF]
                a = (g * _sigmoid(g) * gu[:, EXPERT_FF:]).astype(BF16)
                y = _dot(a, wd_bf[...]).astype(BF16).astype(F32)
                y_buf[i % Y_RING, :n, :] = _pack_halves(y)

        store(i, start=True)

        @pl.when(i == nvalid - 1)
        def _():
            for back in range(Y_RING):
                @pl.when(i - back >= 0)
                def _(back=back):
                    store(i - back, start=False)


def _experts(bexp, bstart, bhalf, nvalid, xs, wg, wu, wd, layer):
    n_rows, D = xs.shape[0], D_MODEL
    nb = bexp.shape[0]

    wspec = lambda r, c: pl.BlockSpec((None, None, r, c),
                                      lambda i, be, bs, bh, nv: (layer, be[jnp.minimum(i, nv[0] - 1)], 0, 0))
    grid_spec = pltpu.PrefetchScalarGridSpec(
        num_scalar_prefetch=4,
        grid=(nb,),
        in_specs=[pl.BlockSpec(memory_space=pl.ANY),
                  wspec(D, EXPERT_FF), wspec(D, EXPERT_FF), wspec(EXPERT_FF, D)],
        out_specs=pl.BlockSpec(memory_space=pl.ANY),
        scratch_shapes=[pltpu.VMEM((X_RING, BLOCK_ROWS, PACK_W), U32), pltpu.VMEM((Y_RING, BLOCK_ROWS, PACK_W), U32),
                        pltpu.VMEM((D, 2 * EXPERT_FF), BF16), pltpu.VMEM((EXPERT_FF, D), BF16),
                        pltpu.SemaphoreType.DMA((X_RING,)), pltpu.SemaphoreType.DMA((Y_RING,))],
    )
    return pl.pallas_call(
        _expert_kernel,
        grid_spec=grid_spec,
        out_shape=jax.ShapeDtypeStruct((n_rows, PACK_W), U32),
        compiler_params=_cparams(("arbitrary",)),
        name="experts",
    )(bexp, bstart, bhalf, nvalid, xs, wg, wu, wd)


def _combine_kernel(off_ref, m_ref, base_ref, tot_ref, ys_hbm, posk_ref, wk_ref, h_ref, x_ref, g2_ref,
                    wsgu_ref, wsd_ref, o_ref, ys_ref, pw_ref, sem):
    i = pl.program_id(0)
    tm = h_ref.shape[0]
    slot = i % 2

    def fetch(tile, buf_slot):
        def run(e, carry):
            idx = tile * N_EXPERTS + e
            _start_rows(m_ref[idx], ys_hbm, off_ref[idx], ys_ref.at[buf_slot], base_ref[idx],
                        sem.at[buf_slot])
            return carry
        lax.fori_loop(0, N_EXPERTS, run, 0)

    @pl.when(i == 0)
    def _():
        ys_ref[...] = jnp.zeros_like(ys_ref)
        fetch(i, slot)

    @pl.when(i + 1 < pl.num_programs(0))
    def _():
        fetch(i + 1, 1 - slot)

    h = h_ref[...]
    gu = _dot(h, wsgu_ref[...])
    g = gu[:, :EXPERT_FF]
    shared = _dot((g * _sigmoid(g) * gu[:, EXPERT_FF:]).astype(BF16), wsd_ref[...])

    posk = posk_ref[0]
    wk = wk_ref[0]
    rows = lax.broadcasted_iota(jnp.int32, (MXU_DIM, tm), 0).astype(F32).astype(BF16)

    def weight_chunks(lo, hi):
        for c in range(lo, hi):
            rel = posk - float(c * MXU_DIM)
            pw = jnp.zeros((MXU_DIM, tm), BF16)
            for k in range(TOP_K):
                wrow = jnp.broadcast_to(wk[k:k + 1, :], (MXU_DIM, tm)).astype(BF16)
                pw = jnp.where(rows == rel[k:k + 1, :].astype(BF16), wrow, pw)
            pw_ref[c * MXU_DIM:(c + 1) * MXU_DIM, :] = pw

    def unsort(lo, hi):
        pw = pw_ref[lo:hi, :]
        halves = [lax.dot_general(pw, y, (((0,), (0,)), ((), ())), preferred_element_type=F32)
                  for y in _unpack_halves(ys_ref[slot, lo:hi, :])]
        return jnp.concatenate(halves, axis=1)

    weight_chunks(0, K_SORT_COMMON // MXU_DIM)
    _wait_rows(tot_ref[i], ys_hbm, ys_ref.at[slot], sem.at[slot])
    o_ref[...] = x_ref[...] + g2_ref[0] * (unsort(0, K_SORT_COMMON) + shared)

    @pl.when(tot_ref[i] * RUN_ALIGN > K_SORT_COMMON)
    def _():
        weight_chunks(K_SORT_COMMON // MXU_DIM, K_SORT // MXU_DIM)
        o_ref[...] = o_ref[...] + g2_ref[0] * unsort(K_SORT_COMMON, K_SORT)


def _combine(off, run_len, base, tot, ys, posk, wk, h2, x1, g2, wsgu, wsd, tiles_per_batch):
    T, D = h2.shape
    tm = TM_MOE
    const = lambda i, *_: (0, 0)
    tok = lambda w: pl.BlockSpec((tm, w), lambda i, *_: (i, 0))
    sel = pl.BlockSpec((1, TOP_K, tm), lambda i, *_: (i, 0, 0))
    grid_spec = pltpu.PrefetchScalarGridSpec(
        num_scalar_prefetch=4,
        grid=(T // tm,),
        in_specs=[pl.BlockSpec(memory_space=pl.ANY), sel, sel, tok(D), tok(D),
                  pl.BlockSpec((1, 1, D), lambda i, *_: (i // tiles_per_batch, 0, 0)),
                  pl.BlockSpec(wsgu.shape, const), pl.BlockSpec(wsd.shape, const)],
        out_specs=tok(D),
        scratch_shapes=[pltpu.VMEM((2, K_SORT, PACK_W), U32),
                        pltpu.VMEM((K_SORT, tm), BF16),
                        pltpu.SemaphoreType.DMA((2,))],
    )
    return pl.pallas_call(
        _combine_kernel,
        grid_spec=grid_spec,
        out_shape=jax.ShapeDtypeStruct((T, D), F32),
        compiler_params=_cparams(("arbitrary",)),
        name="combine",
    )(off, run_len, base, tot, ys, posk, wk, h2, x1, g2, wsgu, wsd)


def _moe(h2, x1, g2, posk, wk, cnt, wg, wu, wd, layer, wsgu, wsd, tiles_per_batch):
    T, D = h2.shape
    nt = T // TM_MOE

    pad = cnt[:, :, 0].astype(jnp.int32)
    base = jnp.cumsum(pad, axis=1) - pad
    tile_off = jnp.cumsum(pad, axis=0) - pad
    total = jnp.sum(pad, axis=0)
    region = ((total + HALF_ROWS - 1) // HALF_ROWS) * HALF_ROWS
    rend = jnp.cumsum(region)
    rstart = rend - region
    off = rstart[None, :] + tile_off
    nblk = (region + BLOCK_ROWS - 1) // BLOCK_ROWS
    bend = jnp.cumsum(nblk)
    nb_max = -(-(TOP_K * T + nt * N_EXPERTS * (RUN_ALIGN - 1)) // BLOCK_ROWS) + N_EXPERTS
    b = jnp.arange(nb_max, dtype=jnp.int32)
    bexp = jnp.minimum(jnp.sum((bend[None, :] <= b[:, None]).astype(jnp.int32), axis=1), N_EXPERTS - 1)
    mine = (bexp[:, None] == jnp.arange(N_EXPERTS, dtype=jnp.int32)[None, :]).astype(jnp.int32)
    pick = lambda per_expert: jnp.sum(mine * per_expert[None, :], axis=1)
    bstart = pick(rstart - (bend - nblk) * BLOCK_ROWS) + b * BLOCK_ROWS
    bhalf = (b == pick(bend) - 1) & (pick(region % BLOCK_ROWS) == HALF_ROWS)
    nvalid = bend[-1:].astype(jnp.int32)
    flat = lambda a: a.reshape(-1).astype(jnp.int32)

    run_len = flat(pad // RUN_ALIGN)
    tot = flat(jnp.sum(pad, axis=1) // RUN_ALIGN)
    n_rows = nb_max * BLOCK_ROWS + HALF_ROWS
    xs = _dispatch(flat(off), run_len, flat(base), tot, flat(rstart + total),
                   flat((region - total) // RUN_ALIGN), h2, posk, n_rows)
    ys = _experts(flat(bexp), flat(bstart), flat(bhalf), nvalid, xs, wg, wu, wd, layer)
    return _combine(flat(off), run_len, flat(base), tot, ys, posk, wk, h2, x1, g2, wsgu, wsd, tiles_per_batch)


def kernel(x, c, ada_w, ada_b, mix_norm_g, w_in, b_fgate, qn_a, kn_a, qn_b, kn_b, rel_bias, w_proj_a,
           w_proj_b, w_out, ffn_norm_g, w_router, router_bias, w_gate_e, w_up_e, w_down_e, w_gate_s,
           w_up_s, w_down_s):
    B, S, D = x.shape
    L = ada_w.shape[0]
    T = B * S
    assert D == D_MODEL and S % TM_PRE == 0 and TQ_FOX == TM_PRE and S % TM_MOE == 0 and S % (BAND_TILES * TQ_BAND) == 0

    mod = _adaln(c, ada_w, ada_b).reshape(L, B, 6, 1, D)

    hid = np.arange(MXU_DIM) // HEAD_DIM
    bd = jnp.asarray(np.where(hid[:, None] == hid[None, :], 1.0 / HEAD_DIM, 0.0), BF16)
    r = np.arange(TM_PRE)
    tri = jnp.asarray(r[:, None] <= r[None, :], BF16)
    r = np.arange(TM_MOE)
    ustrict = jnp.asarray(r[:, None] < r[None, :], BF16)
    r = np.arange(N_EXPERTS)
    lstrict = jnp.asarray(r[None, :] < r[:, None], BF16)

    q_scale = 1.0 / math.sqrt(HEAD_DIM)
    for l in range(L):
        sh1, sc1, g1, sh2, sc2, g2 = (mod[l, :, j] for j in range(6))
        w = w_in[l]
        cols = lambda c: w[:, c * WIDTH:(c + 1) * WIDTH]
        wqk = jnp.concatenate([cols(0), cols(1), cols(3), cols(4)], axis=1).astype(BF16)
        wvt = jnp.concatenate([cols(2), cols(5)], axis=1).T.astype(BF16)
        wf = jnp.pad(w[:, 6 * WIDTH:6 * WIDTH + N_HEADS].T.astype(BF16), ((0, N_HEADS), (0, 0)))
        wg = w[:, 6 * WIDTH + N_HEADS:].astype(BF16)
        gains = jnp.stack([qn_a[l] * (q_scale * LOG2E), kn_a[l], qn_b[l] * (q_scale * LOG2E), kn_b[l]])
        hn = jnp.pad(jnp.tile(gains, (1, N_HEADS)), ((0, 4), (0, 0)))

        qk, vt, dec, cbase, ga, gb = _premix(x, sh1, sc1, mix_norm_g[l][None], wqk, wvt, wf, wg,
                                             b_fgate[l][:, None], hn, bd, tri)
        ya = _band_attention(qk, vt, _band_bias(rel_bias[l]))
        yb = _fox_attention(qk, vt, dec, cbase)
        x1, h2, posk, wk, cnt = _postmix(
            ya, yb, ga, gb, x, g1, w_proj_a[l].astype(BF16), w_proj_b[l].astype(BF16),
            w_out[l].astype(BF16), ffn_norm_g[l][None], sh2, sc2, w_router[l].T.astype(BF16),
            router_bias[l][:, None], ustrict, lstrict)

        wsgu = jnp.concatenate([w_gate_s[l], w_up_s[l]], axis=-1).astype(BF16)
        x = _moe(h2.reshape(T, D), x1.reshape(T, D), g2, posk, wk, cnt, w_gate_e, w_up_e, w_down_e, l,
                 wsgu, w_down_s[l].astype(BF16), S // TM_MOE).reshape(B, S, D)
    return x
```

```python
import math

import jax
import jax.numpy as jnp
import numpy as np
from jax import lax
from jax.experimental import pallas as pl
from jax.experimental.pallas import tpu as pltpu

F32 = jnp.float32
BF16 = jnp.bfloat16

D_MODEL = 1024
HEAD_DIM = 64
N_HEADS = 8
WIDTH = N_HEADS * HEAD_DIM
CHUNK = 64
LOOKBACK_CHUNKS = 8
REL_CLIP = 128
N_EXPERTS = 64
TOP_K = 8
N_GROUPS = 8
TOPK_GROUPS = 4
EXPERT_FF = 256
ROUTED_SCALE = 2.5
EPS = 1e-6
NEG_INF = -1e30
LOG2E = math.log2(math.e)

LANES = 128
MXU_DIM = 256
VMEM_LIMIT = 56 * 1024 * 1024

TM_PRE = 512
TQ_BAND = 256
BAND_WIN = TQ_BAND + LOOKBACK_CHUNKS * CHUNK
BAND_TILES = LOOKBACK_CHUNKS * CHUNK // TQ_BAND
TQ_FOX = 512
PAIRS_PER_STEP = 4
PV_ROWS = HEAD_DIM + 16
TM_MOE = 256
RUN_ALIGN = 8
PACK_W = D_MODEL // 2
U32 = jnp.uint32
K_SORT = ((TOP_K * TM_MOE + N_EXPERTS * (RUN_ALIGN - 1) + MXU_DIM - 1) // MXU_DIM) * MXU_DIM
K_SORT_COMMON = 2304
BLOCK_ROWS = 1024
BLOCK_PARTS = 4
PART_ROWS = BLOCK_ROWS // BLOCK_PARTS


def _dot(a, b):
    return jnp.dot(a, b, preferred_element_type=F32)


def _dot_nt(a, b):
    return lax.dot_general(a, b, (((1,), (1,)), ((), ())), preferred_element_type=F32)


def _sigmoid(v):
    return 1.0 / (1.0 + jnp.exp(-v))


def _cparams(sem):
    return pltpu.CompilerParams(dimension_semantics=sem, vmem_limit_bytes=VMEM_LIMIT)


def _adaln_kernel(c_ref, w_ref, b_ref, o_ref):
    c = c_ref[...]
    ca = c * _sigmoid(c)
    o_ref[0] = jnp.dot(ca, w_ref[0], preferred_element_type=F32,
                       precision=lax.Precision.HIGHEST) + b_ref[0]


def _adaln(c, ada_w, ada_b):
    L, D, N = ada_w.shape
    B = c.shape[0]
    tn = N // 4
    return pl.pallas_call(
        _adaln_kernel,
        grid=(L, N // tn),
        in_specs=[pl.BlockSpec((B, D), lambda l, j: (0, 0)),
                  pl.BlockSpec((1, D, tn), lambda l, j: (l, 0, j)),
                  pl.BlockSpec((1, 1, tn), lambda l, j: (l, 0, j))],
        out_specs=pl.BlockSpec((1, B, tn), lambda l, j: (l, 0, j)),
        out_shape=jax.ShapeDtypeStruct((L, B, N), F32),
        compiler_params=_cparams(("arbitrary", "arbitrary")),
        name="adaln",
    )(c, ada_w, ada_b.reshape(L, 1, N))


def _modnorm(x, g, sc, sh):
    ms = jnp.mean(x * x, axis=-1, keepdims=True)
    y = x * lax.rsqrt(ms + EPS) * g
    return y * (1.0 + sc) + sh


def _premix_kernel(x_ref, sh_ref, sc_ref, g_ref, wqk_ref, wvt_ref, wf_ref, wg_ref, bf_ref, hn_ref, bd_ref,
                   tri_ref, qk_ref, vt_ref, dec_ref, cbase_ref, ga_ref, gb_ref, carry_ref):
    si = pl.program_id(1)
    tm = x_ref.shape[1]
    h = _modnorm(x_ref[0], g_ref[...], sc_ref[0], sh_ref[0]).astype(BF16)

    for c in range(4):
        z = _dot(h, wqk_ref[:, c * WIDTH:(c + 1) * WIDTH])
        sq = (z * z).astype(BF16)
        ms = jnp.concatenate(
            [_dot(sq[:, j * MXU_DIM:(j + 1) * MXU_DIM], bd_ref[...]) for j in range(WIDTH // MXU_DIM)],
            axis=1)
        z = z * lax.rsqrt(ms + EPS) * hn_ref[c:c + 1, :]
        qk_ref[c, 0] = z.astype(BF16)

    for c in range(2):
        vt = _dot_nt(wvt_ref[c * WIDTH:(c + 1) * WIDTH, :], h).astype(BF16)
        vt_ref[0, c * (N_HEADS // 2):(c + 1) * (N_HEADS // 2)] = vt.reshape(N_HEADS // 2, LANES, tm)

    fr = _dot_nt(wf_ref[...], h)[:N_HEADS]
    xg = fr + bf_ref[...]
    logf = jnp.minimum(xg, 0.0) - jnp.log(1.0 + jnp.exp(-jnp.abs(xg)))
    hi = logf.astype(BF16).astype(F32)
    r1 = logf - hi
    mid = r1.astype(BF16).astype(F32)
    lo = r1 - mid
    parts = jnp.concatenate([hi, mid, lo, jnp.zeros_like(hi)], axis=0)
    cs3 = _dot(parts.astype(BF16), tri_ref[...])
    cs = cs3[0:8] + cs3[8:16] + cs3[16:24]

    @pl.when(si == 0)
    def _():
        carry_ref[...] = jnp.zeros_like(carry_ref)

    drel = cs * (-LOG2E)
    dhi = drel.astype(BF16).astype(F32)
    dlo = drel - dhi
    dec = jnp.concatenate([dhi, dlo, jnp.zeros((LANES - 2 * N_HEADS, tm), F32)], axis=0)
    dec_ref[0] = dec.T.astype(BF16)
    carry = carry_ref[:, 0:1]
    cbase_ref[0, 0] = jnp.broadcast_to(carry * (-LOG2E), (N_HEADS, tm))
    carry_ref[...] = jnp.broadcast_to(carry + cs[:, tm - 1:tm], carry_ref.shape)

    for c in range(4):
        zg = _dot(h, wg_ref[:, c * WIDTH:(c + 1) * WIDTH])
        sg = _sigmoid(zg).astype(BF16)
        if c < 2:
            ga_ref[0, :, c * WIDTH:(c + 1) * WIDTH] = sg
        else:
            gb_ref[0, :, (c - 2) * WIDTH:(c - 1) * WIDTH] = sg


def _premix(x, sh, sc, g, wqk, wvt, wf, wg, bf, hn, bd, tri):
    B, S, D = x.shape
    tm = TM_PRE
    const = lambda b, s: (0, 0)
    return pl.pallas_call(
        _premix_kernel,
        grid=(B, S // tm),
        in_specs=[pl.BlockSpec((1, tm, D), lambda b, s: (b, s, 0)),
                  pl.BlockSpec((1, 1, D), lambda b, s: (b, 0, 0)),
                  pl.BlockSpec((1, 1, D), lambda b, s: (b, 0, 0)),
                  pl.BlockSpec((1, D), const),
                  pl.BlockSpec(wqk.shape, const),
                  pl.BlockSpec(wvt.shape, const),
                  pl.BlockSpec(wf.shape, const),
                  pl.BlockSpec(wg.shape, const),
                  pl.BlockSpec(bf.shape, const),
                  pl.BlockSpec(hn.shape, const),
                  pl.BlockSpec(bd.shape, const),
                  pl.BlockSpec(tri.shape, const)],
        out_specs=[pl.BlockSpec((4, 1, tm, WIDTH), lambda b, s: (0, b, s, 0)),
                   pl.BlockSpec((1, N_HEADS, LANES, tm), lambda b, s: (b, 0, 0, s)),
                   pl.BlockSpec((1, tm, LANES), lambda b, s: (b, s, 0)),
                   pl.BlockSpec((1, 1, N_HEADS, tm), lambda b, s: (b, s, 0, 0)),
                   pl.BlockSpec((1, tm, D), lambda b, s: (b, s, 0)),
                   pl.BlockSpec((1, tm, D), lambda b, s: (b, s, 0))],
        out_shape=[jax.ShapeDtypeStruct((4, B, S, WIDTH), BF16),
                   jax.ShapeDtypeStruct((B, N_HEADS, LANES, S), BF16),
                   jax.ShapeDtypeStruct((B, S, LANES), BF16),
                   jax.ShapeDtypeStruct((B, S // tm, N_HEADS, tm), F32),
                   jax.ShapeDtypeStruct((B, S, D), BF16),
                   jax.ShapeDtypeStruct((B, S, D), BF16)],
        scratch_shapes=[pltpu.VMEM((N_HEADS, LANES), F32)],
        compiler_params=_cparams(("arbitrary", "arbitrary")),
        name="premix",
    )(x, sh, sc, g, wqk, wvt, wf, wg, bf, hn, bd, tri)


def _values_with_ones(vt, hh):
    ones = jnp.ones((PV_ROWS - HEAD_DIM, vt.shape[1]), vt.dtype)
    return jnp.concatenate([vt[hh * HEAD_DIM:(hh + 1) * HEAD_DIM, :], ones], axis=0)


def _pair_finish_t(acc0, acc1):
    out_t = jnp.concatenate([acc[:HEAD_DIM] / acc[HEAD_DIM:HEAD_DIM + 1, :] for acc in (acc0, acc1)], axis=0)
    return out_t.T


def _band_kernel(q_ref, kprev_ref, kcur_ref, vprev_ref, vcur_ref, bias_ref, o_ref):
    tq = TQ_BAND
    step_start = pl.program_id(2) * (BAND_TILES * tq)
    lane = lax.broadcasted_iota(jnp.int32, (1, LANES), 1)
    krow = lax.broadcasted_iota(jnp.int32, (BAND_WIN, 1), 0)
    pieces = [(ref, j) for ref in (kprev_ref, kcur_ref) for j in range(BAND_TILES)]
    vpieces = [(ref, j) for ref in (vprev_ref, vcur_ref) for j in range(BAND_TILES)]
    units = [(u, pp, hh) for u in range(BAND_TILES) for pp in range(PAIRS_PER_STEP) for hh in range(2)]
    scores = []
    for u, pp, hh in units:
        q = q_ref[u * tq:(u + 1) * tq, pp * LANES:(pp + 1) * LANES]
        qm = jnp.where((lane // HEAD_DIM) == hh, q, jnp.zeros_like(q))
        scores.append(jnp.concatenate(
            [_dot_nt(ref[j * tq:(j + 1) * tq, pp * LANES:(pp + 1) * LANES], qm)
             for ref, j in pieces[u:u + 3]], axis=0))
    accs = []
    for n, (u, pp, hh) in enumerate(units):
        kvalid = (krow + (step_start + u * tq - LOOKBACK_CHUNKS * CHUNK)) >= 0
        s = jnp.where(kvalid, scores[n] + bias_ref[2 * pp + hh], NEG_INF)
        m = jnp.max(s, axis=0, keepdims=True)
        p = jnp.exp2(s - m).astype(BF16)
        acc = None
        for c, (ref, j) in enumerate(vpieces[u:u + 3]):
            t = _dot(_values_with_ones(ref[pp, :, j * tq:(j + 1) * tq], hh), p[c * tq:(c + 1) * tq, :])
            acc = t if acc is None else acc + t
        accs.append(acc)
    for n in range(0, len(units), 2):
        u, pp, _ = units[n]
        o_ref[u * tq:(u + 1) * tq, pp * LANES:(pp + 1) * LANES] = _pair_finish_t(
            accs[n], accs[n + 1]).astype(o_ref.dtype)


def _band_attention(qk, vt, bias_t):
    _, B, S, _ = qk.shape
    rows = BAND_TILES * TQ_BAND
    groups = N_HEADS // 2 // PAIRS_PER_STEP
    gw = PAIRS_PER_STEP * LANES

    def k_spec(back):
        return pl.BlockSpec((None, None, rows, gw),
                            lambda b, g, qi: (1, b, jnp.maximum(qi - back, 0), g))

    def v_spec(back):
        return pl.BlockSpec((None, PAIRS_PER_STEP, LANES, rows),
                            lambda b, g, qi: (b, g, 0, jnp.maximum(qi - back, 0)))

    return pl.pallas_call(
        _band_kernel,
        grid=(B, groups, S // rows),
        in_specs=[pl.BlockSpec((None, None, rows, gw), lambda b, g, qi: (0, b, qi, g)),
                  k_spec(1), k_spec(0), v_spec(1), v_spec(0),
                  pl.BlockSpec((2 * PAIRS_PER_STEP, BAND_WIN, TQ_BAND), lambda b, g, qi: (g, 0, 0))],
        out_specs=pl.BlockSpec((None, rows, gw), lambda b, g, qi: (b, qi, g)),
        out_shape=jax.ShapeDtypeStruct((B, S, WIDTH), BF16),
        compiler_params=_cparams(("arbitrary", "arbitrary", "arbitrary")),
        name="band_attn",
    )(qk, qk, qk, vt, vt, bias_t)


def _band_bias(rel_table):
    pad = LOOKBACK_CHUNKS * CHUNK
    i = np.arange(TQ_BAND)[:, None]
    j = np.arange(BAND_WIN)[None, :]
    period = 1024
    assert BAND_WIN + TQ_BAND <= period
    m = np.arange(period)
    rel = np.where(m <= BAND_WIN, pad - m, pad + period - m)
    g = rel_table[:, np.clip(rel, -REL_CLIP, REL_CLIP) + REL_CLIP].astype(F32)
    bias = jnp.tile(g, (1, TQ_BAND))[:, :TQ_BAND * (period - 1)].reshape(-1, TQ_BAND, period - 1)[:, :, :BAND_WIN]
    c0 = (i // CHUNK) * CHUNK
    inband = (j >= c0) & (j < c0 + pad + CHUNK)
    return jnp.where(inband[None], bias * LOG2E, NEG_INF).transpose(0, 2, 1)


FOX_DIAG, FOX_TWO_PAST, FOX_PAST_DIAG = 0, 1, 2


def _fox_kernel(qi_tab, ka_tab, kb_tab, kind_tab, first_tab, q_ref, ka_ref, kb_ref, vta_ref, vtb_ref,
                deca_ref, decb_ref, cba_ref, cbb_ref, o_ref, acc_ref, m_ref):
    grp = pl.program_id(1)
    t = pl.program_id(2)
    kind = kind_tab[t]
    tq, tk = q_ref.shape[0], ka_ref.shape[0]
    lane2 = lax.broadcasted_iota(jnp.int32, (1, 2 * LANES), 1)
    heads = [(pp, hh) for pp in range(PAIRS_PER_STEP) for hh in range(2)]
    tile_a = (ka_ref, vta_ref, deca_ref, cba_ref)
    tile_b = (kb_ref, vtb_ref, decb_ref, cbb_ref)

    @pl.when(first_tab[t] == 1)
    def _():
        acc_ref[...] = jnp.zeros_like(acc_ref)
        m_ref[...] = jnp.full_like(m_ref, NEG_INF)

    def step(tiles):
        ones = jnp.where(lax.broadcasted_iota(jnp.int32, (tq, LANES), 1) < 2 * N_HEADS, 1.0, 0.0).astype(BF16)
        causal = (lax.broadcasted_iota(jnp.int32, (tk, tq), 0)
                  <= lax.broadcasted_iota(jnp.int32, (tk, tq), 1))
        scores = []
        for pp, hh in heads:
            head = 2 * (PAIRS_PER_STEP * grp + pp) + hh
            q_aug = jnp.concatenate([q_ref[:, pp * LANES:(pp + 1) * LANES], ones], axis=1)
            use = (((lane2 // HEAD_DIM) == hh) | (lane2 == LANES + head) | (lane2 == LANES + N_HEADS + head))
            qm = jnp.where(use, q_aug, jnp.zeros_like(q_aug))
            per_tile = []
            for (k_ref, _, dec_ref, _), diagonal in tiles:
                k_aug = jnp.concatenate([k_ref[:, pp * LANES:(pp + 1) * LANES], dec_ref[...]], axis=1)
                s = _dot_nt(k_aug, qm)
                per_tile.append(jnp.where(causal, s, NEG_INF) if diagonal else s)
            scores.append(per_tile)
        for n, (pp, hh) in enumerate(heads):
            head = 2 * (PAIRS_PER_STEP * grp + pp) + hh
            bases = [cb_ref[pl.ds(head, 1), :] for (_, _, _, cb_ref), _ in tiles]
            m_old = m_ref[n]
            m_new = m_old
            for s, base in zip(scores[n], bases):
                m_new = jnp.maximum(m_new, jnp.max(s, axis=0, keepdims=True) + base)
            acc = acc_ref[n] * jnp.exp2(m_old - m_new)
            for s, base, ((_, vt_ref, _, _), _) in zip(scores[n], bases, tiles):
                p = jnp.exp2(s - (m_new - base)).astype(BF16)
                acc = acc + _dot(_values_with_ones(vt_ref[pp], hh), p)
            acc_ref[n] = acc
            m_ref[n] = m_new

    def finish():
        for pp in range(PAIRS_PER_STEP):
            o_ref[:, pp * LANES:(pp + 1) * LANES] = _pair_finish_t(
                acc_ref[2 * pp], acc_ref[2 * pp + 1]).astype(o_ref.dtype)

    @pl.when(kind == FOX_TWO_PAST)
    def _():
        step([(tile_a, False), (tile_b, False)])

    @pl.when(kind == FOX_PAST_DIAG)
    def _():
        step([(tile_a, False), (tile_b, True)])
        finish()

    @pl.when(kind == FOX_DIAG)
    def _():
        step([(tile_a, True)])
        finish()


def _fox_attention(qkv, vt, dec, cbase):
    _, B, S, _ = qkv.shape
    tq = TQ_FOX
    nq = S // tq
    groups = N_HEADS // 2 // PAIRS_PER_STEP
    gw = PAIRS_PER_STEP * LANES
    steps = []
    for i in range(nq):
        past = list(range(i))
        first = 1
        while len(past) >= 2:
            steps.append((i, past[0], past[1], FOX_TWO_PAST, first))
            past, first = past[2:], 0
        if past:
            steps.append((i, past[0], i, FOX_PAST_DIAG, first))
        else:
            steps.append((i, i, i, FOX_DIAG, first))
    tabs = [jnp.array([s[c] for s in steps], jnp.int32) for c in range(5)]

    def key_specs(col):
        tile = lambda *a: a[3 + col][a[2]]
        return [pl.BlockSpec((None, None, tq, gw), lambda b, g, t, *tb: (3, b, tile(b, g, t, *tb), g)),
                pl.BlockSpec((None, PAIRS_PER_STEP, LANES, tq),
                             lambda b, g, t, *tb: (b, groups + g, 0, tile(b, g, t, *tb))),
                pl.BlockSpec((None, tq, LANES), lambda b, g, t, *tb: (b, tile(b, g, t, *tb), 0)),
                pl.BlockSpec((None, None, N_HEADS, tq), lambda b, g, t, *tb: (b, tile(b, g, t, *tb), 0, 0))]

    ka, va, da, ca = key_specs(1)
    kb, vb, db, cb = key_specs(2)
    grid_spec = pltpu.PrefetchScalarGridSpec(
        num_scalar_prefetch=5,
        grid=(B, groups, len(steps)),
        in_specs=[pl.BlockSpec((None, None, tq, gw), lambda b, g, t, *tb: (2, b, tb[0][t], g)),
                  ka, kb, va, vb, da, db, ca, cb],
        out_specs=pl.BlockSpec((None, tq, gw), lambda b, g, t, *tb: (b, tb[0][t], g)),
        scratch_shapes=[pltpu.VMEM((2 * PAIRS_PER_STEP, PV_ROWS, tq), F32),
                        pltpu.VMEM((2 * PAIRS_PER_STEP, 1, tq), F32)],
    )
    return pl.pallas_call(
        _fox_kernel,
        grid_spec=grid_spec,
        out_shape=jax.ShapeDtypeStruct((B, S, WIDTH), BF16),
        compiler_params=_cparams(("arbitrary", "arbitrary", "arbitrary")),
        name="fox_attn",
    )(*tabs, qkv, qkv, qkv, vt, vt, dec, dec, cbase, cbase)


def _postmix_kernel(ya_ref, yb_ref, ga_ref, gb_ref, x_ref, g1_ref, wpa_ref, wpb_ref, wo_ref,
                    g_ref, sh_ref, sc_ref, wr_ref, rb_ref, ustrict_ref, lstrict_ref,
                    x1_ref, h2_ref, posk_ref, wk_ref, cnt_ref):
    ua = _dot(ya_ref[0], wpa_ref[...])
    ub = _dot(yb_ref[0], wpb_ref[...])
    m = ga_ref[0].astype(F32) * ua + gb_ref[0].astype(F32) * ub
    mo = _dot(m.astype(BF16), wo_ref[...])
    x1 = x_ref[0] + g1_ref[0] * mo
    x1_ref[0] = x1
    h2 = _modnorm(x1, g_ref[...], sc_ref[0], sh_ref[0]).astype(BF16)
    h2_ref[0] = h2
    for j in range(h2.shape[0] // TM_MOE):
        posk, wk, cnt = _route(h2[j * TM_MOE:(j + 1) * TM_MOE], wr_ref[...], rb_ref[...],
                               ustrict_ref[...], lstrict_ref[...])
        posk_ref[j] = posk
        wk_ref[j] = wk
        cnt_ref[j] = cnt


def _postmix(ya, yb, ga, gb, x, g1, wpa, wpb, wo, g, sh, sc, wr_t, rb, ustrict, lstrict):
    B, S, D = x.shape
    tm = TM_PRE
    sub_tiles = tm // TM_MOE
    nt = B * S // TM_MOE
    const = lambda b, s: (0, 0)
    tok = lambda w: pl.BlockSpec((1, tm, w), lambda b, s: (b, s, 0))
    row = pl.BlockSpec((1, 1, D), lambda b, s: (b, 0, 0))
    tiles = lambda r, c: pl.BlockSpec((sub_tiles, r, c), lambda b, s: (b * (S // tm) + s, 0, 0))
    return pl.pallas_call(
        _postmix_kernel,
        grid=(B, S // tm),
        in_specs=[tok(WIDTH), tok(WIDTH), tok(D), tok(D), tok(D), row,
                  pl.BlockSpec(wpa.shape, const), pl.BlockSpec(wpb.shape, const),
                  pl.BlockSpec(wo.shape, const), pl.BlockSpec((1, D), const), row, row,
                  pl.BlockSpec(wr_t.shape, const), pl.BlockSpec(rb.shape, const),
                  pl.BlockSpec(ustrict.shape, const), pl.BlockSpec(lstrict.shape, const)],
        out_specs=[tok(D), tok(D), tiles(TOP_K, TM_MOE), tiles(TOP_K, TM_MOE), tiles(N_EXPERTS, LANES)],
        out_shape=[jax.ShapeDtypeStruct((B, S, D), F32), jax.ShapeDtypeStruct((B, S, D), BF16),
                   jax.ShapeDtypeStruct((nt, TOP_K, TM_MOE), F32),
                   jax.ShapeDtypeStruct((nt, TOP_K, TM_MOE), F32),
                   jax.ShapeDtypeStruct((nt, N_EXPERTS, LANES), F32)],
        compiler_params=_cparams(("arbitrary", "arbitrary")),
        name="postmix",
    )(ya, yb, ga, gb, x, g1, wpa, wpb, wo, g, sh, sc, wr_t, rb, ustrict, lstrict)


def _route(h, wr, rb, ustrict, lstrict):
    tm = h.shape[0]
    per_group = N_EXPERTS // N_GROUPS
    logits = _dot_nt(wr, h)
    scores = _sigmoid(logits)
    biased = scores + rb
    sub8 = lax.broadcasted_iota(jnp.int32, (per_group, tm), 0).astype(F32)
    neg = -jnp.inf

    grp_rows = []
    for g in range(N_GROUPS):
        a = biased[g * per_group:(g + 1) * per_group]
        m1 = jnp.max(a, axis=0, keepdims=True)
        i1 = jnp.min(jnp.where(a == m1, sub8, float(per_group)), axis=0, keepdims=True)
        m2 = jnp.max(jnp.where(sub8 == i1, neg, a), axis=0, keepdims=True)
        grp_rows.append(m1 + m2)
    grp = jnp.concatenate(grp_rows, axis=0)

    gsub = lax.broadcasted_iota(jnp.int32, (N_GROUPS, tm), 0).astype(F32)
    gmask = jnp.zeros((N_GROUPS, tm), F32)
    for _ in range(TOPK_GROUPS):
        mx = jnp.max(grp, axis=0, keepdims=True)
        gi = jnp.min(jnp.where(grp == mx, gsub, float(N_GROUPS)), axis=0, keepdims=True)
        sel = gsub == gi
        gmask = jnp.where(sel, 1.0, gmask)
        grp = jnp.where(sel, neg, grp)
    masked = jnp.concatenate(
        [jnp.where(gmask[g:g + 1] > 0.5, biased[g * per_group:(g + 1) * per_group], neg)
         for g in range(N_GROUPS)], axis=0)

    esub = lax.broadcasted_iota(jnp.int32, (N_EXPERTS, tm), 0).astype(F32)
    sels, ws = [], []
    for _ in range(TOP_K):
        mx = jnp.max(masked, axis=0, keepdims=True)
        ei = jnp.min(jnp.where(masked == mx, esub, float(N_EXPERTS)), axis=0, keepdims=True)
        sel = esub == ei
        sels.append(sel)
        ws.append(jnp.sum(jnp.where(sel, scores, 0.0), axis=0, keepdims=True))
        masked = jnp.where(sel, neg, masked)
    wsum = ws[0]
    for w in ws[1:]:
        wsum = wsum + w

    selmask = jnp.zeros((N_EXPERTS, tm), F32)
    for sel in sels:
        selmask = selmask + jnp.where(sel, 1.0, 0.0)
    rank = _dot(selmask.astype(BF16), ustrict)
    counts = jnp.sum(selmask, axis=1, keepdims=True)
    padded = jnp.ceil(counts * (1.0 / RUN_ALIGN)) * RUN_ALIGN
    padded_b = jnp.broadcast_to(padded, (N_EXPERTS, LANES))
    base = _dot(lstrict, padded_b.astype(BF16))
    pos = base[:, 0:1] + rank

    posk = jnp.concatenate([jnp.sum(jnp.where(sel, pos, 0.0), axis=0, keepdims=True) for sel in sels], axis=0)
    wk = jnp.concatenate([(w / wsum) * ROUTED_SCALE for w in ws], axis=0)
    return posk, wk, padded_b


def _pack_halves(v):
    half = v.shape[1] // 2
    lo = lax.bitcast_convert_type(v[:, :half], U32)
    hi = lax.bitcast_convert_type(v[:, half:], U32)
    return (lo >> 16) | (hi & U32(0xFFFF0000))


def _unpack_halves(u):
    lo = lax.bitcast_convert_type(u << 16, F32).astype(BF16)
    hi = lax.bitcast_convert_type(u & U32(0xFFFF0000), F32).astype(BF16)
    return lo, hi


def _rows_copy(m, src_ref, src0, dst_ref, dst0, sem):
    rows = m * RUN_ALIGN
    return pltpu.make_async_copy(src_ref.at[pl.ds(pl.multiple_of(src0, RUN_ALIGN), rows)],
                                 dst_ref.at[pl.ds(pl.multiple_of(dst0, RUN_ALIGN), rows)], sem)


def _start_rows(m, src_ref, src0, dst_ref, dst0, sem):
    @pl.when(m > 0)
    def _():
        _rows_copy(m, src_ref, src0, dst_ref, dst0, sem).start()


def _wait_rows(m, src_ref, dst_ref, sem):
    @pl.when(m > 0)
    def _():
        _rows_copy(m, src_ref, 0, dst_ref, 0, sem).wait()


def _dispatch_kernel(off_ref, m_ref, base_ref, tot_ref, gapoff_ref, gapm_ref, h_ref, posk_ref, xs_hbm,
                     xs_ref, zero_ref, sem):
    i = pl.program_id(0)
    tm = h_ref.shape[0]
    n_chunks = K_SORT // MXU_DIM

    @pl.when(i == 0)
    def _():
        zero_ref[...] = jnp.zeros_like(zero_ref)

        def gap(e, carry):
            _start_rows(gapm_ref[e], zero_ref, 0, xs_hbm, gapoff_ref[e], sem)
            return carry
        lax.fori_loop(0, N_EXPERTS, gap, 0)

        def gap_wait(e, carry):
            _wait_rows(gapm_ref[e], zero_ref, xs_hbm, sem)
            return carry
        lax.fori_loop(0, N_EXPERTS, gap_wait, 0)

    slot = i % 2
    xs_slot = xs_ref.at[slot]
    h = h_ref[...]
    posk = posk_ref[0]
    rows = lax.broadcasted_iota(jnp.int32, (MXU_DIM, tm), 0).astype(F32).astype(BF16)
    one = jnp.ones((MXU_DIM, tm), BF16)
    def sort_chunks(lo, hi):
        for c in range(lo, hi):
            rel = posk - float(c * MXU_DIM)
            p = jnp.zeros((MXU_DIM, tm), BF16)
            for k in range(TOP_K):
                p = jnp.where(rows == rel[k:k + 1, :].astype(BF16), one, p)
            xs_slot[c * MXU_DIM:(c + 1) * MXU_DIM, :] = _pack_halves(_dot(p, h))

    sort_chunks(0, K_SORT_COMMON // MXU_DIM)

    @pl.when(tot_ref[i] * RUN_ALIGN > K_SORT_COMMON)
    def _():
        sort_chunks(K_SORT_COMMON // MXU_DIM, n_chunks)

    @pl.when(i > 0)
    def _():
        _wait_rows(tot_ref[i - 1], xs_ref.at[1 - slot], xs_hbm, sem)

    def run(e, carry):
        idx = i * N_EXPERTS + e
        _start_rows(m_ref[idx], xs_slot, base_ref[idx], xs_hbm, off_ref[idx], sem)
        return carry
    lax.fori_loop(0, N_EXPERTS, run, 0)

    @pl.when(i == pl.num_programs(0) - 1)
    def _():
        _wait_rows(tot_ref[i], xs_slot, xs_hbm, sem)


def _dispatch(off, run_len, base, tot, gapoff, gapm, h2, posk, n_rows):
    T, D = h2.shape
    tm = TM_MOE
    grid_spec = pltpu.PrefetchScalarGridSpec(
        num_scalar_prefetch=6,
        grid=(T // tm,),
        in_specs=[pl.BlockSpec((tm, D), lambda i, *_: (i, 0)),
                  pl.BlockSpec((1, TOP_K, tm), lambda i, *_: (i, 0, 0))],
        out_specs=pl.BlockSpec(memory_space=pl.ANY),
        scratch_shapes=[pltpu.VMEM((2, K_SORT, PACK_W), U32),
                        pltpu.VMEM((BLOCK_ROWS, PACK_W), U32),
                        pltpu.SemaphoreType.DMA],
    )
    return pl.pallas_call(
        _dispatch_kernel,
        grid_spec=grid_spec,
        out_shape=jax.ShapeDtypeStruct((n_rows, PACK_W), U32),
        compiler_params=_cparams(("arbitrary",)),
        name="dispatch",
    )(off, run_len, base, tot, gapoff, gapm, h2, posk)


X_RING = 3
Y_RING = 2


def _expert_kernel(bexp_ref, bstart_ref, bparts_ref, nvalid_ref, xs_hbm, wg_ref, wu_ref, wd_ref, ys_hbm,
                   x_buf, y_buf, wgu_bf, wd_bf, x_sem, y_sem):
    i = pl.program_id(0)
    nvalid = nvalid_ref[0]
    part_sizes = [(q, q * PART_ROWS) for q in range(1, BLOCK_PARTS + 1)]

    def rows(b, n):
        return pl.ds(pl.multiple_of(bstart_ref[b], PART_ROWS), n)

    def load(b):
        slot = b % X_RING
        return pltpu.make_async_copy(xs_hbm.at[rows(b, BLOCK_ROWS)], x_buf.at[slot], x_sem.at[slot])

    def store(b, start):
        slot = b % Y_RING
        for q, n in part_sizes:
            @pl.when(bparts_ref[b] == q)
            def _(n=n):
                cp = pltpu.make_async_copy(y_buf.at[slot].at[pl.ds(0, n)], ys_hbm.at[rows(b, n)], y_sem.at[slot])
                if start:
                    cp.start()
                else:
                    cp.wait()

    @pl.when(i == 0)
    def _():
        for b in range(X_RING - 1):
            @pl.when(b < nvalid)
            def _(b=b):
                load(b).start()

    @pl.when(i < nvalid)
    def _():
        @pl.when(i + X_RING - 1 < nvalid)
        def _():
            load(i + X_RING - 1).start()

        @pl.when((i == 0) | (bexp_ref[i] != bexp_ref[jnp.maximum(i - 1, 0)]))
        def _():
            wgu_bf[:, :EXPERT_FF] = wg_ref[...].astype(BF16)
            wgu_bf[:, EXPERT_FF:] = wu_ref[...].astype(BF16)
            wd_bf[...] = wd_ref[...].astype(BF16)

        load(i).wait()

        @pl.when(i >= Y_RING)
        def _():
            store(i - Y_RING, start=False)

        for q, n in part_sizes:
            @pl.when(bparts_ref[i] == q)
            def _(n=n):
                x_lo, x_hi = _unpack_halves(x_buf[i % X_RING, :n, :])
                gu = _dot(x_lo, wgu_bf[:PACK_W, :]) + _dot(x_hi, wgu_bf[PACK_W:, :])
                g = gu[:, :EXPERT_FF]
                a = (g * _sigmoid(g) * gu[:, EXPERT_FF:]).astype(BF16)
                y = _dot(a, wd_bf[...]).astype(BF16).astype(F32)
                y_buf[i % Y_RING, :n, :] = _pack_halves(y)

        store(i, start=True)

        @pl.when(i == nvalid - 1)
        def _():
            for back in range(Y_RING):
                @pl.when(i - back >= 0)
                def _(back=back):
                    store(i - back, start=False)


def _experts(bexp, bstart, bparts, nvalid, xs, wg, wu, wd, layer):
    n_rows, D = xs.shape[0], D_MODEL
    nb = bexp.shape[0]

    wspec = lambda r, c: pl.BlockSpec((None, None, r, c),
                                      lambda i, be, bs, bh, nv: (layer, be[jnp.minimum(i, nv[0] - 1)], 0, 0))
    grid_spec = pltpu.PrefetchScalarGridSpec(
        num_scalar_prefetch=4,
        grid=(nb,),
        in_specs=[pl.BlockSpec(memory_space=pl.ANY),
                  wspec(D, EXPERT_FF), wspec(D, EXPERT_FF), wspec(EXPERT_FF, D)],
        out_specs=pl.BlockSpec(memory_space=pl.ANY),
        scratch_shapes=[pltpu.VMEM((X_RING, BLOCK_ROWS, PACK_W), U32), pltpu.VMEM((Y_RING, BLOCK_ROWS, PACK_W), U32),
                        pltpu.VMEM((D, 2 * EXPERT_FF), BF16), pltpu.VMEM((EXPERT_FF, D), BF16),
                        pltpu.SemaphoreType.DMA((X_RING,)), pltpu.SemaphoreType.DMA((Y_RING,))],
    )
    return pl.pallas_call(
        _expert_kernel,
        grid_spec=grid_spec,
        out_shape=jax.ShapeDtypeStruct((n_rows, PACK_W), U32),
        compiler_params=_cparams(("arbitrary",)),
        name="experts",
    )(bexp, bstart, bparts, nvalid, xs, wg, wu, wd)


def _combine_kernel(off_ref, m_ref, base_ref, tot_ref, ys_hbm, posk_ref, wk_ref, h_ref, x_ref, g2_ref,
                    wsgu_ref, wsd_ref, o_ref, ys_ref, pw_ref, sem):
    i = pl.program_id(0)
    tm = h_ref.shape[0]
    slot = i % 2

    def fetch(tile, buf_slot):
        def run(e, carry):
            idx = tile * N_EXPERTS + e
            _start_rows(m_ref[idx], ys_hbm, off_ref[idx], ys_ref.at[buf_slot], base_ref[idx],
                        sem.at[buf_slot])
            return carry
        lax.fori_loop(0, N_EXPERTS, run, 0)

    @pl.when(i == 0)
    def _():
        ys_ref[...] = jnp.zeros_like(ys_ref)
        fetch(i, slot)

    @pl.when(i + 1 < pl.num_programs(0))
    def _():
        fetch(i + 1, 1 - slot)

    h = h_ref[...]
    gu = _dot(h, wsgu_ref[...])
    g = gu[:, :EXPERT_FF]
    shared = _dot((g * _sigmoid(g) * gu[:, EXPERT_FF:]).astype(BF16), wsd_ref[...])

    posk = posk_ref[0]
    wk = wk_ref[0]
    rows = lax.broadcasted_iota(jnp.int32, (MXU_DIM, tm), 0).astype(F32).astype(BF16)

    def weight_chunks(lo, hi):
        for c in range(lo, hi):
            rel = posk - float(c * MXU_DIM)
            pw = jnp.zeros((MXU_DIM, tm), BF16)
            for k in range(TOP_K):
                wrow = jnp.broadcast_to(wk[k:k + 1, :], (MXU_DIM, tm)).astype(BF16)
                pw = jnp.where(rows == rel[k:k + 1, :].astype(BF16), wrow, pw)
            pw_ref[c * MXU_DIM:(c + 1) * MXU_DIM, :] = pw

    def unsort(lo, hi):
        pw = pw_ref[lo:hi, :]
        halves = [lax.dot_general(pw, y, (((0,), (0,)), ((), ())), preferred_element_type=F32)
                  for y in _unpack_halves(ys_ref[slot, lo:hi, :])]
        return jnp.concatenate(halves, axis=1)

    weight_chunks(0, K_SORT_COMMON // MXU_DIM)
    _wait_rows(tot_ref[i], ys_hbm, ys_ref.at[slot], sem.at[slot])
    o_ref[...] = x_ref[...] + g2_ref[0] * (unsort(0, K_SORT_COMMON) + shared)

    @pl.when(tot_ref[i] * RUN_ALIGN > K_SORT_COMMON)
    def _():
        weight_chunks(K_SORT_COMMON // MXU_DIM, K_SORT // MXU_DIM)
        o_ref[...] = o_ref[...] + g2_ref[0] * unsort(K_SORT_COMMON, K_SORT)


def _combine(off, run_len, base, tot, ys, posk, wk, h2, x1, g2, wsgu, wsd, tiles_per_batch):
    T, D = h2.shape
    tm = TM_MOE
    const = lambda i, *_: (0, 0)
    tok = lambda w: pl.BlockSpec((tm, w), lambda i, *_: (i, 0))
    sel = pl.BlockSpec((1, TOP_K, tm), lambda i, *_: (i, 0, 0))
    grid_spec = pltpu.PrefetchScalarGridSpec(
        num_scalar_prefetch=4,
        grid=(T // tm,),
        in_specs=[pl.BlockSpec(memory_space=pl.ANY), sel, sel, tok(D), tok(D),
                  pl.BlockSpec((1, 1, D), lambda i, *_: (i // tiles_per_batch, 0, 0)),
                  pl.BlockSpec(wsgu.shape, const), pl.BlockSpec(wsd.shape, const)],
        out_specs=tok(D),
        scratch_shapes=[pltpu.VMEM((2, K_SORT, PACK_W), U32),
                        pltpu.VMEM((K_SORT, tm), BF16),
                        pltpu.SemaphoreType.DMA((2,))],
    )
    return pl.pallas_call(
        _combine_kernel,
        grid_spec=grid_spec,
        out_shape=jax.ShapeDtypeStruct((T, D), F32),
        compiler_params=_cparams(("arbitrary",)),
        name="combine",
    )(off, run_len, base, tot, ys, posk, wk, h2, x1, g2, wsgu, wsd)


def _moe(h2, x1, g2, posk, wk, cnt, wg, wu, wd, layer, wsgu, wsd, tiles_per_batch):
    T, D = h2.shape
    nt = T // TM_MOE

    pad = cnt[:, :, 0].astype(jnp.int32)
    base = jnp.cumsum(pad, axis=1) - pad
    tile_off = jnp.cumsum(pad, axis=0) - pad
    total = jnp.sum(pad, axis=0)
    region = ((total + PART_ROWS - 1) // PART_ROWS) * PART_ROWS
    rend = jnp.cumsum(region)
    rstart = rend - region
    off = rstart[None, :] + tile_off
    nblk = (region + BLOCK_ROWS - 1) // BLOCK_ROWS
    bend = jnp.cumsum(nblk)
    nb_max = -(-(TOP_K * T + nt * N_EXPERTS * (RUN_ALIGN - 1)) // BLOCK_ROWS) + N_EXPERTS
    b = jnp.arange(nb_max, dtype=jnp.int32)
    bexp = jnp.minimum(jnp.sum((bend[None, :] <= b[:, None]).astype(jnp.int32), axis=1), N_EXPERTS - 1)
    mine = (bexp[:, None] == jnp.arange(N_EXPERTS, dtype=jnp.int32)[None, :]).astype(jnp.int32)
    pick = lambda per_expert: jnp.sum(mine * per_expert[None, :], axis=1)
    bstart = pick(rstart - (bend - nblk) * BLOCK_ROWS) + b * BLOCK_ROWS
    last_parts = (region - (nblk - 1) * BLOCK_ROWS) // PART_ROWS
    bparts = jnp.where(b == pick(bend) - 1, pick(last_parts), BLOCK_PARTS)
    nvalid = bend[-1:].astype(jnp.int32)
    flat = lambda a: a.reshape(-1).astype(jnp.int32)

    run_len = flat(pad // RUN_ALIGN)
    tot = flat(jnp.sum(pad, axis=1) // RUN_ALIGN)
    n_rows = nb_max * BLOCK_ROWS + BLOCK_ROWS - PART_ROWS
    xs = _dispatch(flat(off), run_len, flat(base), tot, flat(rstart + total),
                   flat((region - total) // RUN_ALIGN), h2, posk, n_rows)
    ys = _experts(flat(bexp), flat(bstart), flat(bparts), nvalid, xs, wg, wu, wd, layer)
    return _combine(flat(off), run_len, flat(base), tot, ys, posk, wk, h2, x1, g2, wsgu, wsd, tiles_per_batch)


def kernel(x, c, ada_w, ada_b, mix_norm_g, w_in, b_fgate, qn_a, kn_a, qn_b, kn_b, rel_bias, w_proj_a,
           w_proj_b, w_out, ffn_norm_g, w_router, router_bias, w_gate_e, w_up_e, w_down_e, w_gate_s,
           w_up_s, w_down_s):
    B, S, D = x.shape
    L = ada_w.shape[0]
    T = B * S
    assert D == D_MODEL and S % TM_PRE == 0 and TQ_FOX == TM_PRE and S % TM_MOE == 0 and S % (BAND_TILES * TQ_BAND) == 0

    mod = _adaln(c, ada_w, ada_b).reshape(L, B, 6, 1, D)

    hid = np.arange(MXU_DIM) // HEAD_DIM
    bd = jnp.asarray(np.where(hid[:, None] == hid[None, :], 1.0 / HEAD_DIM, 0.0), BF16)
    r = np.arange(TM_PRE)
    tri = jnp.asarray(r[:, None] <= r[None, :], BF16)
    r = np.arange(TM_MOE)
    ustrict = jnp.asarray(r[:, None] < r[None, :], BF16)
    r = np.arange(N_EXPERTS)
    lstrict = jnp.asarray(r[None, :] < r[:, None], BF16)

    q_scale = 1.0 / math.sqrt(HEAD_DIM)
    for l in range(L):
        sh1, sc1, g1, sh2, sc2, g2 = (mod[l, :, j] for j in range(6))
        w = w_in[l]
        cols = lambda c: w[:, c * WIDTH:(c + 1) * WIDTH]
        wqk = jnp.concatenate([cols(0), cols(1), cols(3), cols(4)], axis=1).astype(BF16)
        wvt = jnp.concatenate([cols(2), cols(5)], axis=1).T.astype(BF16)
        wf = jnp.pad(w[:, 6 * WIDTH:6 * WIDTH + N_HEADS].T.astype(BF16), ((0, N_HEADS), (0, 0)))
        wg = w[:, 6 * WIDTH + N_HEADS:].astype(BF16)
        gains = jnp.stack([qn_a[l] * (q_scale * LOG2E), kn_a[l], qn_b[l] * (q_scale * LOG2E), kn_b[l]])
        hn = jnp.pad(jnp.tile(gains, (1, N_HEADS)), ((0, 4), (0, 0)))

        qk, vt, dec, cbase, ga, gb = _premix(x, sh1, sc1, mix_norm_g[l][None], wqk, wvt, wf, wg,
                                             b_fgate[l][:, None], hn, bd, tri)
        ya = _band_attention(qk, vt, _band_bias(rel_bias[l]))
        yb = _fox_attention(qk, vt, dec, cbase)
        x1, h2, posk, wk, cnt = _postmix(
            ya, yb, ga, gb, x, g1, w_proj_a[l].astype(BF16), w_proj_b[l].astype(BF16),
            w_out[l].astype(BF16), ffn_norm_g[l][None], sh2, sc2, w_router[l].T.astype(BF16),
            router_bias[l][:, None], ustrict, lstrict)

        wsgu = jnp.concatenate([w_gate_s[l], w_up_s[l]], axis=-1).astype(BF16)
        x = _moe(h2.reshape(T, D), x1.reshape(T, D), g2, posk, wk, cnt, w_gate_e, w_up_e, w_down_e, l,
                 wsgu, w_down_s[l].astype(BF16), S // TM_MOE).reshape(B, S, D)
    return x
```

```python
import math

import jax
import jax.numpy as jnp
import numpy as np
from jax import lax
from jax.experimental import pallas as pl
from jax.experimental.pallas import tpu as pltpu

F32 = jnp.float32
BF16 = jnp.bfloat16

D_MODEL = 1024
HEAD_DIM = 64
N_HEADS = 8
WIDTH = N_HEADS * HEAD_DIM
CHUNK = 64
LOOKBACK_CHUNKS = 8
REL_CLIP = 128
N_EXPERTS = 64
TOP_K = 8
N_GROUPS = 8
TOPK_GROUPS = 4
EXPERT_FF = 256
ROUTED_SCALE = 2.5
EPS = 1e-6
NEG_INF = -1e30
LOG2E = math.log2(math.e)

LANES = 128
MXU_DIM = 256
VMEM_LIMIT = 56 * 1024 * 1024

TM_PRE = 512
TQ_BAND = 256
BAND_WIN = TQ_BAND + LOOKBACK_CHUNKS * CHUNK
BAND_TILES = LOOKBACK_CHUNKS * CHUNK // TQ_BAND
TQ_FOX = 512
PAIRS_PER_STEP = 4
PV_ROWS = HEAD_DIM + 16
TM_MOE = 256
RUN_ALIGN = 8
PACK_W = D_MODEL // 2
U32 = jnp.uint32
K_SORT = ((TOP_K * TM_MOE + N_EXPERTS * (RUN_ALIGN - 1) + MXU_DIM - 1) // MXU_DIM) * MXU_DIM
K_SORT_COMMON = 2304
BLOCK_ROWS = 1024
HALF_ROWS = BLOCK_ROWS // 2


def _dot(a, b):
    return jnp.dot(a, b, preferred_element_type=F32)


def _dot_nt(a, b):
    return lax.dot_general(a, b, (((1,), (1,)), ((), ())), preferred_element_type=F32)


def _sigmoid(v):
    return 1.0 / (1.0 + jnp.exp(-v))


def _cparams(sem):
    return pltpu.CompilerParams(dimension_semantics=sem, vmem_limit_bytes=VMEM_LIMIT)


def _adaln_kernel(c_ref, w_ref, b_ref, o_ref):
    c = c_ref[...]
    ca = c * _sigmoid(c)
    o_ref[0] = jnp.dot(ca, w_ref[0], preferred_element_type=F32,
                       precision=lax.Precision.HIGHEST) + b_ref[0]


def _adaln(c, ada_w, ada_b):
    L, D, N = ada_w.shape
    B = c.shape[0]
    tn = N // 4
    return pl.pallas_call(
        _adaln_kernel,
        grid=(L, N // tn),
        in_specs=[pl.BlockSpec((B, D), lambda l, j: (0, 0)),
                  pl.BlockSpec((1, D, tn), lambda l, j: (l, 0, j)),
                  pl.BlockSpec((1, 1, tn), lambda l, j: (l, 0, j))],
        out_specs=pl.BlockSpec((1, B, tn), lambda l, j: (l, 0, j)),
        out_shape=jax.ShapeDtypeStruct((L, B, N), F32),
        compiler_params=_cparams(("arbitrary", "arbitrary")),
        name="adaln",
    )(c, ada_w, ada_b.reshape(L, 1, N))


def _modnorm(x, g, sc, sh):
    ms = jnp.mean(x * x, axis=-1, keepdims=True)
    y = x * lax.rsqrt(ms + EPS) * g
    return y * (1.0 + sc) + sh


def _premix_kernel(x_ref, sh_ref, sc_ref, g_ref, wqk_ref, wvt_ref, wf_ref, wg_ref, bf_ref, hn_ref, bd_ref,
                   tri_ref, qk_ref, vt_ref, dec_ref, cbase_ref, ga_ref, gb_ref, carry_ref):
    si = pl.program_id(1)
    tm = x_ref.shape[1]
    h = _modnorm(x_ref[0], g_ref[...], sc_ref[0], sh_ref[0]).astype(BF16)

    for c in range(4):
        z = _dot(h, wqk_ref[:, c * WIDTH:(c + 1) * WIDTH])
        sq = (z * z).astype(BF16)
        ms = jnp.concatenate(
            [_dot(sq[:, j * MXU_DIM:(j + 1) * MXU_DIM], bd_ref[...]) for j in range(WIDTH // MXU_DIM)],
            axis=1)
        z = z * lax.rsqrt(ms + EPS) * hn_ref[c:c + 1, :]
        qk_ref[c, 0] = z.astype(BF16)

    for c in range(2):
        vt = _dot_nt(wvt_ref[c * WIDTH:(c + 1) * WIDTH, :], h).astype(BF16)
        vt_ref[0, c * (N_HEADS // 2):(c + 1) * (N_HEADS // 2)] = vt.reshape(N_HEADS // 2, LANES, tm)

    fr = _dot_nt(wf_ref[...], h)[:N_HEADS]
    xg = fr + bf_ref[...]
    logf = jnp.minimum(xg, 0.0) - jnp.log(1.0 + jnp.exp(-jnp.abs(xg)))
    hi = logf.astype(BF16).astype(F32)
    r1 = logf - hi
    mid = r1.astype(BF16).astype(F32)
    lo = r1 - mid
    parts = jnp.concatenate([hi, mid, lo, jnp.zeros_like(hi)], axis=0)
    cs3 = _dot(parts.astype(BF16), tri_ref[...])
    cs = cs3[0:8] + cs3[8:16] + cs3[16:24]

    @pl.when(si == 0)
    def _():
        carry_ref[...] = jnp.zeros_like(carry_ref)

    drel = cs * (-LOG2E)
    dhi = drel.astype(BF16).astype(F32)
    dlo = drel - dhi
    dec = jnp.concatenate([dhi, dlo, jnp.zeros((LANES - 2 * N_HEADS, tm), F32)], axis=0)
    dec_ref[0] = dec.T.astype(BF16)
    carry = carry_ref[:, 0:1]
    cbase_ref[0, 0] = jnp.broadcast_to(carry * (-LOG2E), (N_HEADS, tm))
    carry_ref[...] = jnp.broadcast_to(carry + cs[:, tm - 1:tm], carry_ref.shape)

    for c in range(4):
        zg = _dot(h, wg_ref[:, c * WIDTH:(c + 1) * WIDTH])
        sg = _sigmoid(zg).astype(BF16)
        if c < 2:
            ga_ref[0, :, c * WIDTH:(c + 1) * WIDTH] = sg
        else:
            gb_ref[0, :, (c - 2) * WIDTH:(c - 1) * WIDTH] = sg


def _premix(x, sh, sc, g, wqk, wvt, wf, wg, bf, hn, bd, tri):
    B, S, D = x.shape
    tm = TM_PRE
    const = lambda b, s: (0, 0)
    return pl.pallas_call(
        _premix_kernel,
        grid=(B, S // tm),
        in_specs=[pl.BlockSpec((1, tm, D), lambda b, s: (b, s, 0)),
                  pl.BlockSpec((1, 1, D), lambda b, s: (b, 0, 0)),
                  pl.BlockSpec((1, 1, D), lambda b, s: (b, 0, 0)),
                  pl.BlockSpec((1, D), const),
                  pl.BlockSpec(wqk.shape, const),
                  pl.BlockSpec(wvt.shape, const),
                  pl.BlockSpec(wf.shape, const),
                  pl.BlockSpec(wg.shape, const),
                  pl.BlockSpec(bf.shape, const),
                  pl.BlockSpec(hn.shape, const),
                  pl.BlockSpec(bd.shape, const),
                  pl.BlockSpec(tri.shape, const)],
        out_specs=[pl.BlockSpec((4, 1, tm, WIDTH), lambda b, s: (0, b, s, 0)),
                   pl.BlockSpec((1, N_HEADS, LANES, tm), lambda b, s: (b, 0, 0, s)),
                   pl.BlockSpec((1, tm, LANES), lambda b, s: (b, s, 0)),
                   pl.BlockSpec((1, 1, N_HEADS, tm), lambda b, s: (b, s, 0, 0)),
                   pl.BlockSpec((1, tm, D), lambda b, s: (b, s, 0)),
                   pl.BlockSpec((1, tm, D), lambda b, s: (b, s, 0))],
        out_shape=[jax.ShapeDtypeStruct((4, B, S, WIDTH), BF16),
                   jax.ShapeDtypeStruct((B, N_HEADS, LANES, S), BF16),
                   jax.ShapeDtypeStruct((B, S, LANES), BF16),
                   jax.ShapeDtypeStruct((B, S // tm, N_HEADS, tm), F32),
                   jax.ShapeDtypeStruct((B, S, D), BF16),
                   jax.ShapeDtypeStruct((B, S, D), BF16)],
        scratch_shapes=[pltpu.VMEM((N_HEADS, LANES), F32)],
        compiler_params=_cparams(("arbitrary", "arbitrary")),
        name="premix",
    )(x, sh, sc, g, wqk, wvt, wf, wg, bf, hn, bd, tri)


def _values_with_ones(vt, hh):
    ones = jnp.ones((PV_ROWS - HEAD_DIM, vt.shape[1]), vt.dtype)
    return jnp.concatenate([vt[hh * HEAD_DIM:(hh + 1) * HEAD_DIM, :], ones], axis=0)


def _pair_finish_t(acc0, acc1):
    out_t = jnp.concatenate([acc[:HEAD_DIM] / acc[HEAD_DIM:HEAD_DIM + 1, :] for acc in (acc0, acc1)], axis=0)
    return out_t.T


def _band_kernel(q_ref, kprev_ref, kcur_ref, vprev_ref, vcur_ref, bias_ref, o_ref):
    tq = TQ_BAND
    step = pl.program_id(2)
    lane = lax.broadcasted_iota(jnp.int32, (1, LANES), 1)
    krow = lax.broadcasted_iota(jnp.int32, (BAND_WIN, 1), 0)
    pieces = [(ref, j) for ref in (kprev_ref, kcur_ref) for j in range(BAND_TILES)]
    vpieces = [(ref, j) for ref in (vprev_ref, vcur_ref) for j in range(BAND_TILES)]
    units = [(u, pp, hh) for u in range(BAND_TILES) for pp in range(PAIRS_PER_STEP) for hh in range(2)]

    def attend(first_step):
        scores = []
        for u, pp, hh in units:
            q = q_ref[u * tq:(u + 1) * tq, pp * LANES:(pp + 1) * LANES]
            qm = jnp.where((lane // HEAD_DIM) == hh, q, jnp.zeros_like(q))
            scores.append(jnp.concatenate(
                [_dot_nt(ref[j * tq:(j + 1) * tq, pp * LANES:(pp + 1) * LANES], qm)
                 for ref, j in pieces[u:u + 3]], axis=0))
        accs = []
        for n, (u, pp, hh) in enumerate(units):
            s = scores[n] + bias_ref[2 * pp + hh]
            if first_step:
                s = jnp.where(krow + (u * tq - LOOKBACK_CHUNKS * CHUNK) >= 0, s, NEG_INF)
            m = jnp.max(s, axis=0, keepdims=True)
            p = jnp.exp2(s - m).astype(BF16)
            acc = None
            for c, (ref, j) in enumerate(vpieces[u:u + 3]):
                t = _dot(_values_with_ones(ref[pp, :, j * tq:(j + 1) * tq], hh), p[c * tq:(c + 1) * tq, :])
                acc = t if acc is None else acc + t
            accs.append(acc)
        for n in range(0, len(units), 2):
            u, pp, _ = units[n]
            o_ref[u * tq:(u + 1) * tq, pp * LANES:(pp + 1) * LANES] = _pair_finish_t(
                accs[n], accs[n + 1]).astype(o_ref.dtype)

    @pl.when(step == 0)
    def _():
        attend(True)

    @pl.when(step > 0)
    def _():
        attend(False)


def _band_attention(qk, vt, bias_t):
    _, B, S, _ = qk.shape
    rows = BAND_TILES * TQ_BAND
    groups = N_HEADS // 2 // PAIRS_PER_STEP
    gw = PAIRS_PER_STEP * LANES

    def k_spec(back):
        return pl.BlockSpec((None, None, rows, gw),
                            lambda b, g, qi: (1, b, jnp.maximum(qi - back, 0), g))

    def v_spec(back):
        return pl.BlockSpec((None, PAIRS_PER_STEP, LANES, rows),
                            lambda b, g, qi: (b, g, 0, jnp.maximum(qi - back, 0)))

    return pl.pallas_call(
        _band_kernel,
        grid=(B, groups, S // rows),
        in_specs=[pl.BlockSpec((None, None, rows, gw), lambda b, g, qi: (0, b, qi, g)),
                  k_spec(1), k_spec(0), v_spec(1), v_spec(0),
                  pl.BlockSpec((2 * PAIRS_PER_STEP, BAND_WIN, TQ_BAND), lambda b, g, qi: (g, 0, 0))],
        out_specs=pl.BlockSpec((None, rows, gw), lambda b, g, qi: (b, qi, g)),
        out_shape=jax.ShapeDtypeStruct((B, S, WIDTH), BF16),
        compiler_params=_cparams(("arbitrary", "arbitrary", "arbitrary")),
        name="band_attn",
    )(qk, qk, qk, vt, vt, bias_t)


def _band_bias(rel_table):
    pad = LOOKBACK_CHUNKS * CHUNK
    i = np.arange(TQ_BAND)[:, None]
    j = np.arange(BAND_WIN)[None, :]
    period = 1024
    assert BAND_WIN + TQ_BAND <= period
    m = np.arange(period)
    rel = np.where(m <= BAND_WIN, pad - m, pad + period - m)
    g = rel_table[:, np.clip(rel, -REL_CLIP, REL_CLIP) + REL_CLIP].astype(F32)
    bias = jnp.tile(g, (1, TQ_BAND))[:, :TQ_BAND * (period - 1)].reshape(-1, TQ_BAND, period - 1)[:, :, :BAND_WIN]
    c0 = (i // CHUNK) * CHUNK
    inband = (j >= c0) & (j < c0 + pad + CHUNK)
    return jnp.where(inband[None], bias * LOG2E, NEG_INF).transpose(0, 2, 1)


FOX_DIAG, FOX_TWO_PAST, FOX_PAST_DIAG = 0, 1, 2


def _fox_kernel(qi_tab, ka_tab, kb_tab, kind_tab, first_tab, q_ref, ka_ref, kb_ref, vta_ref, vtb_ref,
                deca_ref, decb_ref, cba_ref, cbb_ref, o_ref, acc_ref, m_ref):
    grp = pl.program_id(1)
    t = pl.program_id(2)
    kind = kind_tab[t]
    tq, tk = q_ref.shape[0], ka_ref.shape[0]
    lane2 = lax.broadcasted_iota(jnp.int32, (1, 2 * LANES), 1)
    heads = [(pp, hh) for pp in range(PAIRS_PER_STEP) for hh in range(2)]
    tile_a = (ka_ref, vta_ref, deca_ref, cba_ref)
    tile_b = (kb_ref, vtb_ref, decb_ref, cbb_ref)

    @pl.when(first_tab[t] == 1)
    def _():
        acc_ref[...] = jnp.zeros_like(acc_ref)
        m_ref[...] = jnp.full_like(m_ref, NEG_INF)

    def step(tiles):
        ones = jnp.where(lax.broadcasted_iota(jnp.int32, (tq, LANES), 1) < 2 * N_HEADS, 1.0, 0.0).astype(BF16)
        causal = (lax.broadcasted_iota(jnp.int32, (tk, tq), 0)
                  <= lax.broadcasted_iota(jnp.int32, (tk, tq), 1))
        scores = []
        for pp, hh in heads:
            head = 2 * (PAIRS_PER_STEP * grp + pp) + hh
            q_aug = jnp.concatenate([q_ref[:, pp * LANES:(pp + 1) * LANES], ones], axis=1)
            use = (((lane2 // HEAD_DIM) == hh) | (lane2 == LANES + head) | (lane2 == LANES + N_HEADS + head))
            qm = jnp.where(use, q_aug, jnp.zeros_like(q_aug))
            per_tile = []
            for (k_ref, _, dec_ref, _), diagonal in tiles:
                k_aug = jnp.concatenate([k_ref[:, pp * LANES:(pp + 1) * LANES], dec_ref[...]], axis=1)
                s = _dot_nt(k_aug, qm)
                per_tile.append(jnp.where(causal, s, NEG_INF) if diagonal else s)
            scores.append(per_tile)
        for n, (pp, hh) in enumerate(heads):
            head = 2 * (PAIRS_PER_STEP * grp + pp) + hh
            bases = [cb_ref[pl.ds(head, 1), :] for (_, _, _, cb_ref), _ in tiles]
            m_old = m_ref[n]
            m_new = m_old
            for s, base in zip(scores[n], bases):
                m_new = jnp.maximum(m_new, jnp.max(s, axis=0, keepdims=True) + base)
            acc = acc_ref[n] * jnp.exp2(m_old - m_new)
            for s, base, ((_, vt_ref, _, _), _) in zip(scores[n], bases, tiles):
                p = jnp.exp2(s - (m_new - base)).astype(BF16)
                acc = acc + _dot(_values_with_ones(vt_ref[pp], hh), p)
            acc_ref[n] = acc
            m_ref[n] = m_new

    def finish():
        for pp in range(PAIRS_PER_STEP):
            o_ref[:, pp * LANES:(pp + 1) * LANES] = _pair_finish_t(
                acc_ref[2 * pp], acc_ref[2 * pp + 1]).astype(o_ref.dtype)

    @pl.when(kind == FOX_TWO_PAST)
    def _():
        step([(tile_a, False), (tile_b, False)])

    @pl.when(kind == FOX_PAST_DIAG)
    def _():
        step([(tile_a, False), (tile_b, True)])
        finish()

    @pl.when(kind == FOX_DIAG)
    def _():
        step([(tile_a, True)])
        finish()


def _fox_attention(qkv, vt, dec, cbase):
    _, B, S, _ = qkv.shape
    tq = TQ_FOX
    nq = S // tq
    groups = N_HEADS // 2 // PAIRS_PER_STEP
    gw = PAIRS_PER_STEP * LANES
    steps = []
    for i in range(nq):
        past = list(range(i))
        first = 1
        while len(past) >= 2:
            steps.append((i, past[0], past[1], FOX_TWO_PAST, first))
            past, first = past[2:], 0
        if past:
            steps.append((i, past[0], i, FOX_PAST_DIAG, first))
        else:
            steps.append((i, i, i, FOX_DIAG, first))
    tabs = [jnp.array([s[c] for s in steps], jnp.int32) for c in range(5)]

    def key_specs(col):
        tile = lambda *a: a[3 + col][a[2]]
        return [pl.BlockSpec((None, None, tq, gw), lambda b, g, t, *tb: (3, b, tile(b, g, t, *tb), g)),
                pl.BlockSpec((None, PAIRS_PER_STEP, LANES, tq),
                             lambda b, g, t, *tb: (b, groups + g, 0, tile(b, g, t, *tb))),
                pl.BlockSpec((None, tq, LANES), lambda b, g, t, *tb: (b, tile(b, g, t, *tb), 0)),
                pl.BlockSpec((None, None, N_HEADS, tq), lambda b, g, t, *tb: (b, tile(b, g, t, *tb), 0, 0))]

    ka, va, da, ca = key_specs(1)
    kb, vb, db, cb = key_specs(2)
    grid_spec = pltpu.PrefetchScalarGridSpec(
        num_scalar_prefetch=5,
        grid=(B, groups, len(steps)),
        in_specs=[pl.BlockSpec((None, None, tq, gw), lambda b, g, t, *tb: (2, b, tb[0][t], g)),
                  ka, kb, va, vb, da, db, ca, cb],
        out_specs=pl.BlockSpec((None, tq, gw), lambda b, g, t, *tb: (b, tb[0][t], g)),
        scratch_shapes=[pltpu.VMEM((2 * PAIRS_PER_STEP, PV_ROWS, tq), F32),
                        pltpu.VMEM((2 * PAIRS_PER_STEP, 1, tq), F32)],
    )
    return pl.pallas_call(
        _fox_kernel,
        grid_spec=grid_spec,
        out_shape=jax.ShapeDtypeStruct((B, S, WIDTH), BF16),
        compiler_params=_cparams(("arbitrary", "arbitrary", "arbitrary")),
        name="fox_attn",
    )(*tabs, qkv, qkv, qkv, vt, vt, dec, dec, cbase, cbase)


def _postmix_kernel(ya_ref, yb_ref, ga_ref, gb_ref, x_ref, g1_ref, wpa_ref, wpb_ref, wo_ref,
                    g_ref, sh_ref, sc_ref, wr_ref, rb_ref, ustrict_ref, lstrict_ref,
                    x1_ref, h2_ref, posk_ref, wk_ref, cnt_ref):
    ua = _dot(ya_ref[0], wpa_ref[...])
    ub = _dot(yb_ref[0], wpb_ref[...])
    m = ga_ref[0].astype(F32) * ua + gb_ref[0].astype(F32) * ub
    mo = _dot(m.astype(BF16), wo_ref[...])
    x1 = x_ref[0] + g1_ref[0] * mo
    x1_ref[0] = x1
    h2 = _modnorm(x1, g_ref[...], sc_ref[0], sh_ref[0]).astype(BF16)
    h2_ref[0] = h2
    for j in range(h2.shape[0] // TM_MOE):
        posk, wk, cnt = _route(h2[j * TM_MOE:(j + 1) * TM_MOE], wr_ref[...], rb_ref[...],
                               ustrict_ref[...], lstrict_ref[...])
        posk_ref[j] = posk
        wk_ref[j] = wk
        cnt_ref[j] = cnt


def _postmix(ya, yb, ga, gb, x, g1, wpa, wpb, wo, g, sh, sc, wr_t, rb, ustrict, lstrict):
    B, S, D = x.shape
    tm = TM_PRE
    sub_tiles = tm // TM_MOE
    nt = B * S // TM_MOE
    const = lambda b, s: (0, 0)
    tok = lambda w: pl.BlockSpec((1, tm, w), lambda b, s: (b, s, 0))
    row = pl.BlockSpec((1, 1, D), lambda b, s: (b, 0, 0))
    tiles = lambda r, c: pl.BlockSpec((sub_tiles, r, c), lambda b, s: (b * (S // tm) + s, 0, 0))
    return pl.pallas_call(
        _postmix_kernel,
        grid=(B, S // tm),
        in_specs=[tok(WIDTH), tok(WIDTH), tok(D), tok(D), tok(D), row,
                  pl.BlockSpec(wpa.shape, const), pl.BlockSpec(wpb.shape, const),
                  pl.BlockSpec(wo.shape, const), pl.BlockSpec((1, D), const), row, row,
                  pl.BlockSpec(wr_t.shape, const), pl.BlockSpec(rb.shape, const),
                  pl.BlockSpec(ustrict.shape, const), pl.BlockSpec(lstrict.shape, const)],
        out_specs=[tok(D), tok(D), tiles(TOP_K, TM_MOE), tiles(TOP_K, TM_MOE), tiles(N_EXPERTS, LANES)],
        out_shape=[jax.ShapeDtypeStruct((B, S, D), F32), jax.ShapeDtypeStruct((B, S, D), BF16),
                   jax.ShapeDtypeStruct((nt, TOP_K, TM_MOE), F32),
                   jax.ShapeDtypeStruct((nt, TOP_K, TM_MOE), F32),
                   jax.ShapeDtypeStruct((nt, N_EXPERTS, LANES), F32)],
        compiler_params=_cparams(("arbitrary", "arbitrary")),
        name="postmix",
    )(ya, yb, ga, gb, x, g1, wpa, wpb, wo, g, sh, sc, wr_t, rb, ustrict, lstrict)


def _route(h, wr, rb, ustrict, lstrict):
    tm = h.shape[0]
    per_group = N_EXPERTS // N_GROUPS
    logits = _dot_nt(wr, h)
    scores = _sigmoid(logits)
    biased = scores + rb
    sub8 = lax.broadcasted_iota(jnp.int32, (per_group, tm), 0).astype(F32)
    neg = -jnp.inf

    grp_rows = []
    for g in range(N_GROUPS):
        a = biased[g * per_group:(g + 1) * per_group]
        m1 = jnp.max(a, axis=0, keepdims=True)
        i1 = jnp.min(jnp.where(a == m1, sub8, float(per_group)), axis=0, keepdims=True)
        m2 = jnp.max(jnp.where(sub8 == i1, neg, a), axis=0, keepdims=True)
        grp_rows.append(m1 + m2)
    grp = jnp.concatenate(grp_rows, axis=0)

    gsub = lax.broadcasted_iota(jnp.int32, (N_GROUPS, tm), 0).astype(F32)
    gmask = jnp.zeros((N_GROUPS, tm), F32)
    for _ in range(TOPK_GROUPS):
        mx = jnp.max(grp, axis=0, keepdims=True)
        gi = jnp.min(jnp.where(grp == mx, gsub, float(N_GROUPS)), axis=0, keepdims=True)
        sel = gsub == gi
        gmask = jnp.where(sel, 1.0, gmask)
        grp = jnp.where(sel, neg, grp)
    masked = jnp.concatenate(
        [jnp.where(gmask[g:g + 1] > 0.5, biased[g * per_group:(g + 1) * per_group], neg)
         for g in range(N_GROUPS)], axis=0)

    esub = lax.broadcasted_iota(jnp.int32, (N_EXPERTS, tm), 0).astype(F32)
    sels, ws = [], []
    for _ in range(TOP_K):
        mx = jnp.max(masked, axis=0, keepdims=True)
        ei = jnp.min(jnp.where(masked == mx, esub, float(N_EXPERTS)), axis=0, keepdims=True)
        sel = esub == ei
        sels.append(sel)
        ws.append(jnp.sum(jnp.where(sel, scores, 0.0), axis=0, keepdims=True))
        masked = jnp.where(sel, neg, masked)
    wsum = ws[0]
    for w in ws[1:]:
        wsum = wsum + w

    selmask = jnp.zeros((N_EXPERTS, tm), F32)
    for sel in sels:
        selmask = selmask + jnp.where(sel, 1.0, 0.0)
    rank = _dot(selmask.astype(BF16), ustrict)
    counts = jnp.sum(selmask, axis=1, keepdims=True)
    padded = jnp.ceil(counts * (1.0 / RUN_ALIGN)) * RUN_ALIGN
    padded_b = jnp.broadcast_to(padded, (N_EXPERTS, LANES))
    base = _dot(lstrict, padded_b.astype(BF16))
    pos = base[:, 0:1] + rank

    posk = jnp.concatenate([jnp.sum(jnp.where(sel, pos, 0.0), axis=0, keepdims=True) for sel in sels], axis=0)
    wk = jnp.concatenate([(w / wsum) * ROUTED_SCALE for w in ws], axis=0)
    return posk, wk, padded_b


def _pack_halves(v):
    half = v.shape[1] // 2
    lo = lax.bitcast_convert_type(v[:, :half], U32)
    hi = lax.bitcast_convert_type(v[:, half:], U32)
    return (lo >> 16) | (hi & U32(0xFFFF0000))


def _unpack_halves(u):
    lo = lax.bitcast_convert_type(u << 16, F32).astype(BF16)
    hi = lax.bitcast_convert_type(u & U32(0xFFFF0000), F32).astype(BF16)
    return lo, hi


def _rows_copy(m, src_ref, src0, dst_ref, dst0, sem):
    rows = m * RUN_ALIGN
    return pltpu.make_async_copy(src_ref.at[pl.ds(pl.multiple_of(src0, RUN_ALIGN), rows)],
                                 dst_ref.at[pl.ds(pl.multiple_of(dst0, RUN_ALIGN), rows)], sem)


def _start_rows(m, src_ref, src0, dst_ref, dst0, sem):
    @pl.when(m > 0)
    def _():
        _rows_copy(m, src_ref, src0, dst_ref, dst0, sem).start()


def _wait_rows(m, src_ref, dst_ref, sem):
    @pl.when(m > 0)
    def _():
        _rows_copy(m, src_ref, 0, dst_ref, 0, sem).wait()


def _dispatch_kernel(off_ref, m_ref, base_ref, tot_ref, gapoff_ref, gapm_ref, h_ref, posk_ref, xs_hbm,
                     xs_ref, zero_ref, sem):
    i = pl.program_id(0)
    tm = h_ref.shape[0]
    n_chunks = K_SORT // MXU_DIM

    @pl.when(i == 0)
    def _():
        zero_ref[...] = jnp.zeros_like(zero_ref)

        def gap(e, carry):
            _start_rows(gapm_ref[e], zero_ref, 0, xs_hbm, gapoff_ref[e], sem)
            return carry
        lax.fori_loop(0, N_EXPERTS, gap, 0)

        def gap_wait(e, carry):
            _wait_rows(gapm_ref[e], zero_ref, xs_hbm, sem)
            return carry
        lax.fori_loop(0, N_EXPERTS, gap_wait, 0)

    slot = i % 2
    xs_slot = xs_ref.at[slot]
    h = h_ref[...]
    posk = posk_ref[0]
    rows = lax.broadcasted_iota(jnp.int32, (MXU_DIM, tm), 0).astype(F32).astype(BF16)
    one = jnp.ones((MXU_DIM, tm), BF16)
    def sort_chunks(lo, hi):
        for c in range(lo, hi):
            rel = posk - float(c * MXU_DIM)
            p = jnp.zeros((MXU_DIM, tm), BF16)
            for k in range(TOP_K):
                p = jnp.where(rows == rel[k:k + 1, :].astype(BF16), one, p)
            xs_slot[c * MXU_DIM:(c + 1) * MXU_DIM, :] = _pack_halves(_dot(p, h))

    sort_chunks(0, K_SORT_COMMON // MXU_DIM)

    @pl.when(tot_ref[i] * RUN_ALIGN > K_SORT_COMMON)
    def _():
        sort_chunks(K_SORT_COMMON // MXU_DIM, n_chunks)

    @pl.when(i > 0)
    def _():
        _wait_rows(tot_ref[i - 1], xs_ref.at[1 - slot], xs_hbm, sem)

    def run(e, carry):
        idx = i * N_EXPERTS + e
        _start_rows(m_ref[idx], xs_slot, base_ref[idx], xs_hbm, off_ref[idx], sem)
        return carry
    lax.fori_loop(0, N_EXPERTS, run, 0)

    @pl.when(i == pl.num_programs(0) - 1)
    def _():
        _wait_rows(tot_ref[i], xs_slot, xs_hbm, sem)


def _dispatch(off, run_len, base, tot, gapoff, gapm, h2, posk, n_rows):
    T, D = h2.shape
    tm = TM_MOE
    grid_spec = pltpu.PrefetchScalarGridSpec(
        num_scalar_prefetch=6,
        grid=(T // tm,),
        in_specs=[pl.BlockSpec((tm, D), lambda i, *_: (i, 0)),
                  pl.BlockSpec((1, TOP_K, tm), lambda i, *_: (i, 0, 0))],
        out_specs=pl.BlockSpec(memory_space=pl.ANY),
        scratch_shapes=[pltpu.VMEM((2, K_SORT, PACK_W), U32),
                        pltpu.VMEM((BLOCK_ROWS, PACK_W), U32),
                        pltpu.SemaphoreType.DMA],
    )
    return pl.pallas_call(
        _dispatch_kernel,
        grid_spec=grid_spec,
        out_shape=jax.ShapeDtypeStruct((n_rows, PACK_W), U32),
        compiler_params=_cparams(("arbitrary",)),
        name="dispatch",
    )(off, run_len, base, tot, gapoff, gapm, h2, posk)


X_RING = 3
Y_RING = 2


def _expert_kernel(bexp_ref, bstart_ref, bhalf_ref, nvalid_ref, xs_hbm, wg_ref, wu_ref, wd_ref, ys_hbm,
                   x_buf, y_buf, wgu_bf, wd_bf, x_sem, y_sem):
    i = pl.program_id(0)
    nvalid = nvalid_ref[0]

    def rows(b, n):
        return pl.ds(pl.multiple_of(bstart_ref[b], HALF_ROWS), n)

    def load(b):
        slot = b % X_RING
        return pltpu.make_async_copy(xs_hbm.at[rows(b, BLOCK_ROWS)], x_buf.at[slot], x_sem.at[slot])

    def store(b, start):
        slot = b % Y_RING
        for flag, n in ((1, HALF_ROWS), (0, BLOCK_ROWS)):
            @pl.when(bhalf_ref[b] == flag)
            def _(n=n):
                cp = pltpu.make_async_copy(y_buf.at[slot].at[pl.ds(0, n)], ys_hbm.at[rows(b, n)], y_sem.at[slot])
                if start:
                    cp.start()
                else:
                    cp.wait()

    @pl.when(i == 0)
    def _():
        for b in range(X_RING - 1):
            @pl.when(b < nvalid)
            def _(b=b):
                load(b).start()

    @pl.when(i < nvalid)
    def _():
        @pl.when(i + X_RING - 1 < nvalid)
        def _():
            load(i + X_RING - 1).start()

        @pl.when((i == 0) | (bexp_ref[i] != bexp_ref[jnp.maximum(i - 1, 0)]))
        def _():
            wgu_bf[:, :EXPERT_FF] = wg_ref[...].astype(BF16)
            wgu_bf[:, EXPERT_FF:] = wu_ref[...].astype(BF16)
            wd_bf[...] = wd_ref[...].astype(BF16)

        load(i).wait()

        @pl.when(i >= Y_RING)
        def _():
            store(i - Y_RING, start=False)

        for flag, n in ((1, HALF_ROWS), (0, BLOCK_ROWS)):
            @pl.when(bhalf_ref[i] == flag)
            def _(n=n):
                x_lo, x_hi = _unpack_halves(x_buf[i % X_RING, :n, :])
                gu = _dot(x_lo, wgu_bf[:PACK_W, :]) + _dot(x_hi, wgu_bf[PACK_W:, :])
                g = gu[:, :EXPERT_FF]
                a = (g * _sigmoid(g) * gu[:, EXPERT_FF:]).astype(BF16)
                y = _dot(a, wd_bf[...]).astype(BF16).astype(F32)
                y_buf[i % Y_RING, :n, :] = _pack_halves(y)

        store(i, start=True)

        @pl.when(i == nvalid - 1)
        def _():
            for back in range(Y_RING):
                @pl.when(i - back >= 0)
                def _(back=back):
                    store(i - back, start=False)


def _experts(bexp, bstart, bhalf, nvalid, xs, wg, wu, wd, layer):
    n_rows, D = xs.shape[0], D_MODEL
    nb = bexp.shape[0]

    wspec = lambda r, c: pl.BlockSpec((None, None, r, c),
                                      lambda i, be, bs, bh, nv: (layer, be[jnp.minimum(i, nv[0] - 1)], 0, 0))
    grid_spec = pltpu.PrefetchScalarGridSpec(
        num_scalar_prefetch=4,
        grid=(nb,),
        in_specs=[pl.BlockSpec(memory_space=pl.ANY),
                  wspec(D, EXPERT_FF), wspec(D, EXPERT_FF), wspec(EXPERT_FF, D)],
        out_specs=pl.BlockSpec(memory_space=pl.ANY),
        scratch_shapes=[pltpu.VMEM((X_RING, BLOCK_ROWS, PACK_W), U32), pltpu.VMEM((Y_RING, BLOCK_ROWS, PACK_W), U32),
                        pltpu.VMEM((D, 2 * EXPERT_FF), BF16), pltpu.VMEM((EXPERT_FF, D), BF16),
                        pltpu.SemaphoreType.DMA((X_RING,)), pltpu.SemaphoreType.DMA((Y_RING,))],
    )
    return pl.pallas_call(
        _expert_kernel,
        grid_spec=grid_spec,
        out_shape=jax.ShapeDtypeStruct((n_rows, PACK_W), U32),
        compiler_params=_cparams(("arbitrary",)),
        name="experts",
    )(bexp, bstart, bhalf, nvalid, xs, wg, wu, wd)


def _combine_kernel(off_ref, m_ref, base_ref, tot_ref, ys_hbm, posk_ref, wk_ref, h_ref, x_ref, g2_ref,
                    wsgu_ref, wsd_ref, o_ref, ys_ref, pw_ref, sem):
    i = pl.program_id(0)
    tm = h_ref.shape[0]
    slot = i % 2

    def fetch(tile, buf_slot):
        def run(e, carry):
            idx = tile * N_EXPERTS + e
            _start_rows(m_ref[idx], ys_hbm, off_ref[idx], ys_ref.at[buf_slot], base_ref[idx],
                        sem.at[buf_slot])
            return carry
        lax.fori_loop(0, N_EXPERTS, run, 0)

    @pl.when(i == 0)
    def _():
        ys_ref[...] = jnp.zeros_like(ys_ref)
        fetch(i, slot)

    @pl.when(i + 1 < pl.num_programs(0))
    def _():
        fetch(i + 1, 1 - slot)

    h = h_ref[...]
    gu = _dot(h, wsgu_ref[...])
    g = gu[:, :EXPERT_FF]
    shared = _dot((g * _sigmoid(g) * gu[:, EXPERT_FF:]).astype(BF16), wsd_ref[...])

    posk = posk_ref[0]
    wk = wk_ref[0]
    rows = lax.broadcasted_iota(jnp.int32, (MXU_DIM, tm), 0).astype(F32).astype(BF16)

    def weight_chunks(lo, hi):
        for c in range(lo, hi):
            rel = posk - float(c * MXU_DIM)
            pw = jnp.zeros((MXU_DIM, tm), BF16)
            for k in range(TOP_K):
                wrow = jnp.broadcast_to(wk[k:k + 1, :], (MXU_DIM, tm)).astype(BF16)
                pw = jnp.where(rows == rel[k:k + 1, :].astype(BF16), wrow, pw)
            pw_ref[c * MXU_DIM:(c + 1) * MXU_DIM, :] = pw

    def unsort(lo, hi):
        pw = pw_ref[lo:hi, :]
        halves = [lax.dot_general(pw, y, (((0,), (0,)), ((), ())), preferred_element_type=F32)
                  for y in _unpack_halves(ys_ref[slot, lo:hi, :])]
        return jnp.concatenate(halves, axis=1)

    weight_chunks(0, K_SORT_COMMON // MXU_DIM)
    _wait_rows(tot_ref[i], ys_hbm, ys_ref.at[slot], sem.at[slot])
    o_ref[...] = x_ref[...] + g2_ref[0] * (unsort(0, K_SORT_COMMON) + shared)

    @pl.when(tot_ref[i] * RUN_ALIGN > K_SORT_COMMON)
    def _():
        weight_chunks(K_SORT_COMMON // MXU_DIM, K_SORT // MXU_DIM)
        o_ref[...] = o_ref[...] + g2_ref[0] * unsort(K_SORT_COMMON, K_SORT)


def _combine(off, run_len, base, tot, ys, posk, wk, h2, x1, g2, wsgu, wsd, tiles_per_batch):
    T, D = h2.shape
    tm = TM_MOE
    const = lambda i, *_: (0, 0)
    tok = lambda w: pl.BlockSpec((tm, w), lambda i, *_: (i, 0))
    sel = pl.BlockSpec((1, TOP_K, tm), lambda i, *_: (i, 0, 0))
    grid_spec = pltpu.PrefetchScalarGridSpec(
        num_scalar_prefetch=4,
        grid=(T // tm,),
        in_specs=[pl.BlockSpec(memory_space=pl.ANY), sel, sel, tok(D), tok(D),
                  pl.BlockSpec((1, 1, D), lambda i, *_: (i // tiles_per_batch, 0, 0)),
                  pl.BlockSpec(wsgu.shape, const), pl.BlockSpec(wsd.shape, const)],
        out_specs=tok(D),
        scratch_shapes=[pltpu.VMEM((2, K_SORT, PACK_W), U32),
                        pltpu.VMEM((K_SORT, tm), BF16),
                        pltpu.SemaphoreType.DMA((2,))],
    )
    return pl.pallas_call(
        _combine_kernel,
        grid_spec=grid_spec,
        out_shape=jax.ShapeDtypeStruct((T, D), F32),
        compiler_params=_cparams(("arbitrary",)),
        name="combine",
    )(off, run_len, base, tot, ys, posk, wk, h2, x1, g2, wsgu, wsd)


def _moe(h2, x1, g2, posk, wk, cnt, wg, wu, wd, layer, wsgu, wsd, tiles_per_batch):
    T, D = h2.shape
    nt = T // TM_MOE

    pad = cnt[:, :, 0].astype(jnp.int32)
    base = jnp.cumsum(pad, axis=1) - pad
    tile_off = jnp.cumsum(pad, axis=0) - pad
    total = jnp.sum(pad, axis=0)
    region = ((total + HALF_ROWS - 1) // HALF_ROWS) * HALF_ROWS
    rend = jnp.cumsum(region)
    rstart = rend - region
    off = rstart[None, :] + tile_off
    nblk = (region + BLOCK_ROWS - 1) // BLOCK_ROWS
    bend = jnp.cumsum(nblk)
    nb_max = -(-(TOP_K * T + nt * N_EXPERTS * (RUN_ALIGN - 1)) // BLOCK_ROWS) + N_EXPERTS
    b = jnp.arange(nb_max, dtype=jnp.int32)
    bexp = jnp.minimum(jnp.sum((bend[None, :] <= b[:, None]).astype(jnp.int32), axis=1), N_EXPERTS - 1)
    mine = (bexp[:, None] == jnp.arange(N_EXPERTS, dtype=jnp.int32)[None, :]).astype(jnp.int32)
    pick = lambda per_expert: jnp.sum(mine * per_expert[None, :], axis=1)
    bstart = pick(rstart - (bend - nblk) * BLOCK_ROWS) + b * BLOCK_ROWS
    bhalf = (b == pick(bend) - 1) & (pick(region % BLOCK_ROWS) == HALF_ROWS)
    nvalid = bend[-1:].astype(jnp.int32)
    flat = lambda a: a.reshape(-1).astype(jnp.int32)

    run_len = flat(pad // RUN_ALIGN)
    tot = flat(jnp.sum(pad, axis=1) // RUN_ALIGN)
    n_rows = nb_max * BLOCK_ROWS + HALF_ROWS
    xs = _dispatch(flat(off), run_len, flat(base), tot, flat(rstart + total),
                   flat((region - total) // RUN_ALIGN), h2, posk, n_rows)
    ys = _experts(flat(bexp), flat(bstart), flat(bhalf), nvalid, xs, wg, wu, wd, layer)
    return _combine(flat(off), run_len, flat(base), tot, ys, posk, wk, h2, x1, g2, wsgu, wsd, tiles_per_batch)


def kernel(x, c, ada_w, ada_b, mix_norm_g, w_in, b_fgate, qn_a, kn_a, qn_b, kn_b, rel_bias, w_proj_a,
           w_proj_b, w_out, ffn_norm_g, w_router, router_bias, w_gate_e, w_up_e, w_down_e, w_gate_s,
           w_up_s, w_down_s):
    B, S, D = x.shape
    L = ada_w.shape[0]
    T = B * S
    assert D == D_MODEL and S % TM_PRE == 0 and TQ_FOX == TM_PRE and S % TM_MOE == 0 and S % (BAND_TILES * TQ_BAND) == 0

    mod = _adaln(c, ada_w, ada_b).reshape(L, B, 6, 1, D)

    hid = np.arange(MXU_DIM) // HEAD_DIM
    bd = jnp.asarray(np.where(hid[:, None] == hid[None, :], 1.0 / HEAD_DIM, 0.0), BF16)
    r = np.arange(TM_PRE)
    tri = jnp.asarray(r[:, None] <= r[None, :], BF16)
    r = np.arange(TM_MOE)
    ustrict = jnp.asarray(r[:, None] < r[None, :], BF16)
    r = np.arange(N_EXPERTS)
    lstrict = jnp.asarray(r[None, :] < r[:, None], BF16)

    q_scale = 1.0 / math.sqrt(HEAD_DIM)
    for l in range(L):
        sh1, sc1, g1, sh2, sc2, g2 = (mod[l, :, j] for j in range(6))
        w = w_in[l]
        cols = lambda c: w[:, c * WIDTH:(c + 1) * WIDTH]
        wqk = jnp.concatenate([cols(0), cols(1), cols(3), cols(4)], axis=1).astype(BF16)
        wvt = jnp.concatenate([cols(2), cols(5)], axis=1).T.astype(BF16)
        wf = jnp.pad(w[:, 6 * WIDTH:6 * WIDTH + N_HEADS].T.astype(BF16), ((0, N_HEADS), (0, 0)))
        wg = w[:, 6 * WIDTH + N_HEADS:].astype(BF16)
        gains = jnp.stack([qn_a[l] * (q_scale * LOG2E), kn_a[l], qn_b[l] * (q_scale * LOG2E), kn_b[l]])
        hn = jnp.pad(jnp.tile(gains, (1, N_HEADS)), ((0, 4), (0, 0)))

        qk, vt, dec, cbase, ga, gb = _premix(x, sh1, sc1, mix_norm_g[l][None], wqk, wvt, wf, wg,
                                             b_fgate[l][:, None], hn, bd, tri)
        ya = _band_attention(qk, vt, _band_bias(rel_bias[l]))
        yb = _fox_attention(qk, vt, dec, cbase)
        x1, h2, posk, wk, cnt = _postmix(
            ya, yb, ga, gb, x, g1, w_proj_a[l].astype(BF16), w_proj_b[l].astype(BF16),
            w_out[l].astype(BF16), ffn_norm_g[l][None], sh2, sc2, w_router[l].T.astype(BF16),
            router_bias[l][:, None], ustrict, lstrict)

        wsgu = jnp.concatenate([w_gate_s[l], w_up_s[l]], axis=-1).astype(BF16)
        x = _moe(h2.reshape(T, D), x1.reshape(T, D), g2, posk, wk, cnt, w_gate_e, w_up_e, w_down_e, l,
                 wsgu, w_down_s[l].astype(BF16), S // TM_MOE).reshape(B, S, D)
    return x
```
